```python
import jax, jax.numpy as jnp
from jax import lax
import numpy as np

D_MODEL = 2048
BATCH = 8
SEQ = 8192
DEPTH = 2

N_MIXERS = 2
HEAD_DIM = 128
EPS = 1e-6
FOX_HEADS = D_MODEL // HEAD_DIM
FOX_BLOCK = 128
DIL_PATTERNS = ((128, 1), (512, 4), (2048, 16))
N_GROUPS = len(DIL_PATTERNS)
DIL_SPAN = 128
DIL_HEADS = D_MODEL // (2 * HEAD_DIM)
DIL_V_DIM = D_MODEL // DIL_HEADS
ALIBI_MAX_EXP = 8.0
D_FF = 4 * D_MODEL
N_FOX_LAYERS = (DEPTH + 1) // 2
N_DIL_LAYERS = DEPTH // 2

kernel_name = "fox_dilated_hybrid_trunk"


def rms_norm(x, g):
    xf = x.astype(jnp.float32)
    y = xf * lax.rsqrt(jnp.mean(xf * xf, axis=-1, keepdims=True) + EPS)
    return (y * g.astype(jnp.float32)).astype(x.dtype)


def sq_relu_mlp(h, w_up, w_down):
    a = jax.nn.relu(h @ w_up)
    return (a * a) @ w_down


def fox_attention(h, w_in, b_f, q_gain, k_gain, w_out):
    B, S, _ = h.shape
    H, dh = FOX_HEADS, HEAD_DIM
    proj = h @ w_in
    q, k, v, f = jnp.split(proj, [H * dh, 2 * H * dh, 3 * H * dh], axis=-1)
    q = rms_norm(q.reshape(B, S, H, dh), q_gain)
    k = rms_norm(k.reshape(B, S, H, dh), k_gain)
    v = v.reshape(B, S, H, dh)
    log_f = jax.nn.log_sigmoid((f + b_f).astype(jnp.float32))
    c = jnp.cumsum(log_f, axis=1)
    c_keys = jnp.transpose(c, (0, 2, 1))
    nb = S // FOX_BLOCK
    qb = jnp.moveaxis(q.reshape(B, nb, FOX_BLOCK, H, dh), 1, 0)
    cb = jnp.moveaxis(c.reshape(B, nb, FOX_BLOCK, H), 1, 0)
    kpos = jnp.arange(S)
    scale = dh ** -0.5

    def one_block(args):
        i, q_i, c_i = args
        s = jnp.einsum('bqhd,bkhd->bhqk', q_i, k, preferred_element_type=jnp.float32) * scale
        s = s + jnp.transpose(c_i, (0, 2, 1))[..., None] - c_keys[:, :, None, :]
        qpos = i * FOX_BLOCK + jnp.arange(FOX_BLOCK)
        causal = kpos[None, :] <= qpos[:, None]
        s = jnp.where(causal, s, -jnp.inf)
        p = jax.nn.softmax(s, axis=-1)
        return jnp.einsum('bhqk,bkhd->bqhd', p.astype(v.dtype), v)

    o = lax.map(one_block, (jnp.arange(nb), qb, cb))
    o = jnp.moveaxis(o, 0, 1).reshape(B, S, H * dh)
    return o @ w_out


def dilated_group(q, k, v, slopes, r):
    B, S, H, dh = q.shape
    L = S // r
    nb = -(-L // DIL_SPAN)
    Lp = nb * DIL_SPAN

    def to_blocks(t):
        t = t.reshape((B, L, r) + t.shape[2:])
        t = jnp.moveaxis(t, 2, 1)
        t = jnp.pad(t, [(0, 0), (0, 0), (0, Lp - L)] + [(0, 0)] * (t.ndim - 3))
        return t.reshape((B, r, nb, DIL_SPAN) + t.shape[3:])

    def with_prev(t):
        prev = jnp.pad(t, [(0, 0), (0, 0), (1, 0)] + [(0, 0)] * (t.ndim - 3))[:, :, :-1]
        return jnp.concatenate([prev, t], axis=3)

    def from_blocks(t):
        t = t.reshape((B, r, Lp) + t.shape[4:])[:, :, :L]
        t = jnp.moveaxis(t, 1, 2)
        return t.reshape((B, S) + t.shape[3:])

    qb = to_blocks(q)
    kw = with_prev(to_blocks(k))
    vw = with_prev(to_blocks(v))
    s = jnp.einsum('brnqhd,brnkhd->brnhqk', qb, kw, preferred_element_type=jnp.float32) * dh ** -0.5
    qi = jnp.arange(DIL_SPAN)[:, None]
    kj = jnp.arange(2 * DIL_SPAN)[None, :]
    delta = qi + DIL_SPAN - kj
    blk = jnp.arange(nb)[:, None, None]
    valid = (delta >= 0) & (delta <= DIL_SPAN) & ((blk > 0) | (kj >= DIL_SPAN))
    alibi = -slopes.astype(jnp.float32)[:, None, None] * (delta * r).astype(jnp.float32)
    s = jnp.where(valid[None, None, :, None], s + alibi, -jnp.inf)
    m = jnp.max(s, axis=-1)
    p = jnp.exp(s - m[..., None])
    den = jnp.sum(p, axis=-1)
    num = jnp.einsum('brnhqk,brnkhd->brnqhd', p, vw.astype(jnp.float32))
    return (from_blocks(jnp.swapaxes(m, 3, 4)),
            from_blocks(jnp.swapaxes(den, 3, 4)),
            from_blocks(num))


def dilated_attention(h, w_in, q_gain, k_gain, w_out):
    B, S, _ = h.shape
    G, H, dh, dv = N_GROUPS, DIL_HEADS, HEAD_DIM, DIL_V_DIM
    proj = h @ w_in
    q, k, v = jnp.split(proj, [G * H * dh, 2 * G * H * dh], axis=-1)
    q = rms_norm(q.reshape(B, S, G, H, dh), q_gain[:, None, :])
    k = rms_norm(k.reshape(B, S, G, H, dh), k_gain[:, None, :])
    v = v.reshape(B, S, H, dv)
    slopes = jnp.exp2(-ALIBI_MAX_EXP * jnp.arange(1, G * H + 1, dtype=jnp.float32) / (G * H)).reshape(G, H)
    ms, dens, nums = [], [], []
    for g, (window, r) in enumerate(DIL_PATTERNS):
        m_g, den_g, num_g = dilated_group(q[:, :, g], k[:, :, g], v, slopes[g], r)
        ms.append(m_g); dens.append(den_g); nums.append(num_g)
    m_all = jnp.stack(ms, 0)
    w = jnp.exp(m_all - jnp.max(m_all, axis=0, keepdims=True))
    den = jnp.sum(w * jnp.stack(dens, 0), axis=0)
    num = jnp.sum(w[..., None] * jnp.stack(nums, 0), axis=0)
    o = (num / den[..., None]).reshape(B, S, H * dv).astype(h.dtype)
    return o @ w_out


def _fwd_setup_inputs(seed: int = 0) -> dict:
    key = jax.random.key(seed)
    ks = jax.random.split(key, 16)
    D = D_MODEL
    nrm = lambda k, shape, fan_in: jax.random.normal(k, shape, jnp.float32) * fan_in ** -0.5
    x = jax.random.normal(ks[0], (BATCH, SEQ, D), jnp.float32)
    fox_qkv = nrm(ks[1], (N_FOX_LAYERS, D, 3 * FOX_HEADS * HEAD_DIM), D)
    fox_fg = 0.1 * nrm(ks[2], (N_FOX_LAYERS, D, FOX_HEADS), D)
    fox_w_in = jnp.concatenate([fox_qkv, fox_fg], axis=-1)
    fox_b_f = 3.0 + 0.1 * jax.random.normal(ks[3], (N_FOX_LAYERS, FOX_HEADS), jnp.float32)
    fox_q_gain = 1.0 + 0.02 * jax.random.normal(ks[4], (N_FOX_LAYERS, HEAD_DIM), jnp.float32)
    fox_k_gain = 1.0 + 0.02 * jax.random.normal(ks[5], (N_FOX_LAYERS, HEAD_DIM), jnp.float32)
    fox_w_out = nrm(ks[6], (N_FOX_LAYERS, FOX_HEADS * HEAD_DIM, D), FOX_HEADS * HEAD_DIM)
    dil_cols = 2 * N_GROUPS * DIL_HEADS * HEAD_DIM + DIL_HEADS * DIL_V_DIM
    dil_w_in = nrm(ks[7], (N_DIL_LAYERS, D, dil_cols), D)
    dil_q_gain = 1.0 + 0.02 * jax.random.normal(ks[8], (N_DIL_LAYERS, N_GROUPS, HEAD_DIM), jnp.float32)
    dil_k_gain = 1.0 + 0.02 * jax.random.normal(ks[9], (N_DIL_LAYERS, N_GROUPS, HEAD_DIM), jnp.float32)
    dil_w_out = nrm(ks[10], (N_DIL_LAYERS, DIL_HEADS * DIL_V_DIM, D), DIL_HEADS * DIL_V_DIM)
    mix_norm_g = 1.0 + 0.02 * jax.random.normal(ks[11], (DEPTH, D), jnp.float32)
    mlp_norm_g = 1.0 + 0.02 * jax.random.normal(ks[12], (DEPTH, D), jnp.float32)
    mlp_w_up = nrm(ks[13], (DEPTH, D, D_FF), D)
    mlp_w_down = nrm(ks[14], (DEPTH, D_FF, D), D_FF)
    return {"x": x, "fox_w_in": fox_w_in, "fox_b_f": fox_b_f, "fox_q_gain": fox_q_gain,
            "fox_k_gain": fox_k_gain, "fox_w_out": fox_w_out, "dil_w_in": dil_w_in,
            "dil_q_gain": dil_q_gain, "dil_k_gain": dil_k_gain, "dil_w_out": dil_w_out,
            "mix_norm_g": mix_norm_g, "mlp_norm_g": mlp_norm_g,
            "mlp_w_up": mlp_w_up, "mlp_w_down": mlp_w_down}


def _fwd_reference(x, fox_w_in, fox_b_f, fox_q_gain, fox_k_gain, fox_w_out, dil_w_in,
              dil_q_gain, dil_k_gain, dil_w_out, mix_norm_g, mlp_norm_g, mlp_w_up, mlp_w_down):
    for i in range(DEPTH):
        j = i // N_MIXERS
        h = rms_norm(x, mix_norm_g[i])
        if i % N_MIXERS == 0:
            mix = fox_attention(h, fox_w_in[j], fox_b_f[j], fox_q_gain[j], fox_k_gain[j], fox_w_out[j])
        else:
            mix = dilated_attention(h, dil_w_in[j], dil_q_gain[j], dil_k_gain[j], dil_w_out[j])
        x = x + mix.astype(x.dtype)
        h = rms_norm(x, mlp_norm_g[i])
        x = x + sq_relu_mlp(h, mlp_w_up[i], mlp_w_down[i]).astype(x.dtype)
    return x


import jax as _jax
import jax.numpy as _jnp

TWIN_FORMAT = 'train_step'
FWD_PARAMS = ['x', 'fox_w_in', 'fox_b_f', 'fox_q_gain', 'fox_k_gain', 'fox_w_out', 'dil_w_in', 'dil_q_gain', 'dil_k_gain', 'dil_w_out', 'mix_norm_g', 'mlp_norm_g', 'mlp_w_up', 'mlp_w_down']
TWIN_WEIGHTS = ['fox_w_in', 'fox_b_f', 'fox_q_gain', 'fox_k_gain', 'fox_w_out', 'dil_w_in', 'dil_q_gain', 'dil_k_gain', 'dil_w_out', 'mix_norm_g', 'mlp_norm_g', 'mlp_w_up', 'mlp_w_down']
TWIN_DIFF_INPUT = 'x'
TWIN_INPUTS = ['x', 'fox_w_in', 'fox_b_f', 'fox_q_gain', 'fox_k_gain', 'fox_w_out', 'dil_w_in', 'dil_q_gain', 'dil_k_gain', 'dil_w_out', 'mix_norm_g', 'mlp_norm_g', 'mlp_w_up', 'mlp_w_down', 'loss_target', 'm_fox_w_in', 'm_fox_b_f', 'm_fox_q_gain', 'm_fox_k_gain', 'm_fox_w_out', 'm_dil_w_in', 'm_dil_q_gain', 'm_dil_k_gain', 'm_dil_w_out', 'm_mix_norm_g', 'm_mlp_norm_g', 'm_mlp_w_up', 'm_mlp_w_down', 'v_fox_w_in', 'v_fox_b_f', 'v_fox_q_gain', 'v_fox_k_gain', 'v_fox_w_out', 'v_dil_w_in', 'v_dil_q_gain', 'v_dil_k_gain', 'v_dil_w_out', 'v_mix_norm_g', 'v_mlp_norm_g', 'v_mlp_w_up', 'v_mlp_w_down']
TWIN_OUTPUTS = ['loss', 'grad_x', 'grad_fox_w_in', 'grad_fox_b_f', 'grad_fox_q_gain', 'grad_fox_k_gain', 'grad_fox_w_out', 'grad_dil_w_in', 'grad_dil_q_gain', 'grad_dil_k_gain', 'grad_dil_w_out', 'grad_mix_norm_g', 'grad_mlp_norm_g', 'grad_mlp_w_up', 'grad_mlp_w_down', 'delta_fox_w_in', 'delta_fox_b_f', 'delta_fox_q_gain', 'delta_fox_k_gain', 'delta_fox_w_out', 'delta_dil_w_in', 'delta_dil_q_gain', 'delta_dil_k_gain', 'delta_dil_w_out', 'delta_mix_norm_g', 'delta_mlp_norm_g', 'delta_mlp_w_up', 'delta_mlp_w_down', 'new_m_fox_w_in', 'new_m_fox_b_f', 'new_m_fox_q_gain', 'new_m_fox_k_gain', 'new_m_fox_w_out', 'new_m_dil_w_in', 'new_m_dil_q_gain', 'new_m_dil_k_gain', 'new_m_dil_w_out', 'new_m_mix_norm_g', 'new_m_mlp_norm_g', 'new_m_mlp_w_up', 'new_m_mlp_w_down', 'new_v_fox_w_in', 'new_v_fox_b_f', 'new_v_fox_q_gain', 'new_v_fox_k_gain', 'new_v_fox_w_out', 'new_v_dil_w_in', 'new_v_dil_q_gain', 'new_v_dil_k_gain', 'new_v_dil_w_out', 'new_v_mix_norm_g', 'new_v_mlp_norm_g', 'new_v_mlp_w_up', 'new_v_mlp_w_down']
TWIN_LEAF_KINDS = {'loss': 'loss', 'grad_x': 'grad_x', 'grad_fox_w_in': 'grad_w', 'grad_fox_b_f': 'grad_w', 'grad_fox_q_gain': 'grad_w', 'grad_fox_k_gain': 'grad_w', 'grad_fox_w_out': 'grad_w', 'grad_dil_w_in': 'grad_w', 'grad_dil_q_gain': 'grad_w', 'grad_dil_k_gain': 'grad_w', 'grad_dil_w_out': 'grad_w', 'grad_mix_norm_g': 'grad_w', 'grad_mlp_norm_g': 'grad_w', 'grad_mlp_w_up': 'grad_w', 'grad_mlp_w_down': 'grad_w', 'delta_fox_w_in': 'delta_w', 'delta_fox_b_f': 'delta_w', 'delta_fox_q_gain': 'delta_w', 'delta_fox_k_gain': 'delta_w', 'delta_fox_w_out': 'delta_w', 'delta_dil_w_in': 'delta_w', 'delta_dil_q_gain': 'delta_w', 'delta_dil_k_gain': 'delta_w', 'delta_dil_w_out': 'delta_w', 'delta_mix_norm_g': 'delta_w', 'delta_mlp_norm_g': 'delta_w', 'delta_mlp_w_up': 'delta_w', 'delta_mlp_w_down': 'delta_w', 'new_m_fox_w_in': 'new_m', 'new_m_fox_b_f': 'new_m', 'new_m_fox_q_gain': 'new_m', 'new_m_fox_k_gain': 'new_m', 'new_m_fox_w_out': 'new_m', 'new_m_dil_w_in': 'new_m', 'new_m_dil_q_gain': 'new_m', 'new_m_dil_k_gain': 'new_m', 'new_m_dil_w_out': 'new_m', 'new_m_mix_norm_g': 'new_m', 'new_m_mlp_norm_g': 'new_m', 'new_m_mlp_w_up': 'new_m', 'new_m_mlp_w_down': 'new_m', 'new_v_fox_w_in': 'new_v', 'new_v_fox_b_f': 'new_v', 'new_v_fox_q_gain': 'new_v', 'new_v_fox_k_gain': 'new_v', 'new_v_fox_w_out': 'new_v', 'new_v_dil_w_in': 'new_v', 'new_v_dil_q_gain': 'new_v', 'new_v_dil_k_gain': 'new_v', 'new_v_dil_w_out': 'new_v', 'new_v_mix_norm_g': 'new_v', 'new_v_mlp_norm_g': 'new_v', 'new_v_mlp_w_up': 'new_v', 'new_v_mlp_w_down': 'new_v'}


def _forward(args):
    return _fwd_reference(*[args[k] for k in FWD_PARAMS])


def _output_shape():
    def fwd():
        inp = _fwd_setup_inputs(0)
        return _fwd_reference(*[inp[k] for k in FWD_PARAMS])
    out = _jax.eval_shape(fwd)
    return out.shape, out.dtype

N_MICROBATCH = 1
ADAM_LR = 0.001
ADAM_B1 = 0.9
ADAM_B2 = 0.999
ADAM_EPS = 1e-08
ADAM_WD = 0.01
ADAM_STEP = 10
PER_EXAMPLE_BATCH_AXIS = {'x': 0, 'loss_target': 0}
SHARED_INPUTS = []
_WEIGHT_DTYPES = {'fox_w_in': _jnp.float32, 'fox_b_f': _jnp.float32, 'fox_q_gain': _jnp.float32, 'fox_k_gain': _jnp.float32, 'fox_w_out': _jnp.float32, 'dil_w_in': _jnp.float32, 'dil_q_gain': _jnp.float32, 'dil_k_gain': _jnp.float32, 'dil_w_out': _jnp.float32, 'mix_norm_g': _jnp.float32, 'mlp_norm_g': _jnp.float32, 'mlp_w_up': _jnp.float32, 'mlp_w_down': _jnp.float32}
MOMENT_SCALE = {'fox_w_in': 2.717445e-01, 'fox_b_f': 7.818913e+01, 'fox_q_gain': 1.546319e+01, 'fox_k_gain': 1.542624e+01, 'fox_w_out': 3.713210e-01, 'dil_w_in': 5.725587e+00, 'dil_q_gain': 4.703635e+00, 'dil_k_gain': 4.708813e+00, 'dil_w_out': 1.109374e+01, 'mix_norm_g': 9.435680e+00, 'mlp_norm_g': 9.875875e+01, 'mlp_w_up': 4.868730e+00, 'mlp_w_down': 1.931145e+01}


def _to_microbatches(a, axis):
    t = _jnp.moveaxis(a, axis, 0)
    t = t.reshape((N_MICROBATCH, t.shape[0] // N_MICROBATCH) + t.shape[1:])
    return _jnp.moveaxis(t, 1, axis + 1)


def setup_inputs(seed: int = 0) -> dict:
    inp = _fwd_setup_inputs(seed)
    key = _jax.random.fold_in(_jax.random.key(seed), 7919)
    shape, _ = _output_shape()
    out = dict(inp)
    out["loss_target"] = _jax.random.normal(_jax.random.fold_in(key, 0), shape, _jnp.float32)
    for i, name in enumerate(TWIN_WEIGHTS):
        w = inp[name].astype(_jnp.float32)
        if MOMENT_SCALE is None:
            s = _jnp.sqrt(_jnp.mean(_jnp.square(w)) + 1e-30)
        else:
            s = MOMENT_SCALE[name]
        km, kv = _jax.random.split(_jax.random.fold_in(key, i + 1))
        out[name] = w
        out["m_" + name] = s * _jax.random.normal(km, w.shape, _jnp.float32)
        out["v_" + name] = (s * s) * _jax.random.uniform(kv, w.shape, _jnp.float32, 0.5, 1.5)
    if N_MICROBATCH > 1:
        for name, axis in PER_EXAMPLE_BATCH_AXIS.items():
            out[name] = _to_microbatches(out[name], axis)
    return {'x': out['x'], 'fox_w_in': out['fox_w_in'], 'fox_b_f': out['fox_b_f'], 'fox_q_gain': out['fox_q_gain'], 'fox_k_gain': out['fox_k_gain'], 'fox_w_out': out['fox_w_out'], 'dil_w_in': out['dil_w_in'], 'dil_q_gain': out['dil_q_gain'], 'dil_k_gain': out['dil_k_gain'], 'dil_w_out': out['dil_w_out'], 'mix_norm_g': out['mix_norm_g'], 'mlp_norm_g': out['mlp_norm_g'], 'mlp_w_up': out['mlp_w_up'], 'mlp_w_down': out['mlp_w_down'], 'loss_target': out['loss_target'], 'm_fox_w_in': out['m_fox_w_in'], 'm_fox_b_f': out['m_fox_b_f'], 'm_fox_q_gain': out['m_fox_q_gain'], 'm_fox_k_gain': out['m_fox_k_gain'], 'm_fox_w_out': out['m_fox_w_out'], 'm_dil_w_in': out['m_dil_w_in'], 'm_dil_q_gain': out['m_dil_q_gain'], 'm_dil_k_gain': out['m_dil_k_gain'], 'm_dil_w_out': out['m_dil_w_out'], 'm_mix_norm_g': out['m_mix_norm_g'], 'm_mlp_norm_g': out['m_mlp_norm_g'], 'm_mlp_w_up': out['m_mlp_w_up'], 'm_mlp_w_down': out['m_mlp_w_down'], 'v_fox_w_in': out['v_fox_w_in'], 'v_fox_b_f': out['v_fox_b_f'], 'v_fox_q_gain': out['v_fox_q_gain'], 'v_fox_k_gain': out['v_fox_k_gain'], 'v_fox_w_out': out['v_fox_w_out'], 'v_dil_w_in': out['v_dil_w_in'], 'v_dil_q_gain': out['v_dil_q_gain'], 'v_dil_k_gain': out['v_dil_k_gain'], 'v_dil_w_out': out['v_dil_w_out'], 'v_mix_norm_g': out['v_mix_norm_g'], 'v_mlp_norm_g': out['v_mlp_norm_g'], 'v_mlp_w_up': out['v_mlp_w_up'], 'v_mlp_w_down': out['v_mlp_w_down']}


def _loss(weights, diff, rest, loss_target):
    with _jax.named_scope("forward"):
        args = {**rest, TWIN_DIFF_INPUT: diff, **{k: w.astype(_WEIGHT_DTYPES[k]) for k, w in weights.items()}}
        y = _forward(args)
    with _jax.named_scope("loss_head"):
        err = _jnp.square(y.astype(_jnp.float32) - loss_target)
        return 0.5 * _jnp.sum(_jnp.mean(err, axis=-1)) if err.ndim else 0.5 * err


def _adamw(w, g, m, v):
    m = ADAM_B1 * m + (1.0 - ADAM_B1) * g
    v = ADAM_B2 * v + (1.0 - ADAM_B2) * _jnp.square(g)
    m_hat = m / (1.0 - ADAM_B1 ** ADAM_STEP)
    v_hat = v / (1.0 - ADAM_B2 ** ADAM_STEP)
    delta = -ADAM_LR * (m_hat / (_jnp.sqrt(v_hat) + ADAM_EPS) + ADAM_WD * w)
    return delta, m, v


def reference(x, fox_w_in, fox_b_f, fox_q_gain, fox_k_gain, fox_w_out, dil_w_in, dil_q_gain, dil_k_gain, dil_w_out, mix_norm_g, mlp_norm_g, mlp_w_up, mlp_w_down, loss_target, m_fox_w_in, m_fox_b_f, m_fox_q_gain, m_fox_k_gain, m_fox_w_out, m_dil_w_in, m_dil_q_gain, m_dil_k_gain, m_dil_w_out, m_mix_norm_g, m_mlp_norm_g, m_mlp_w_up, m_mlp_w_down, v_fox_w_in, v_fox_b_f, v_fox_q_gain, v_fox_k_gain, v_fox_w_out, v_dil_w_in, v_dil_q_gain, v_dil_k_gain, v_dil_w_out, v_mix_norm_g, v_mlp_norm_g, v_mlp_w_up, v_mlp_w_down):
    given = dict(x=x, fox_w_in=fox_w_in, fox_b_f=fox_b_f, fox_q_gain=fox_q_gain, fox_k_gain=fox_k_gain, fox_w_out=fox_w_out, dil_w_in=dil_w_in, dil_q_gain=dil_q_gain, dil_k_gain=dil_k_gain, dil_w_out=dil_w_out, mix_norm_g=mix_norm_g, mlp_norm_g=mlp_norm_g, mlp_w_up=mlp_w_up, mlp_w_down=mlp_w_down, loss_target=loss_target, m_fox_w_in=m_fox_w_in, m_fox_b_f=m_fox_b_f, m_fox_q_gain=m_fox_q_gain, m_fox_k_gain=m_fox_k_gain, m_fox_w_out=m_fox_w_out, m_dil_w_in=m_dil_w_in, m_dil_q_gain=m_dil_q_gain, m_dil_k_gain=m_dil_k_gain, m_dil_w_out=m_dil_w_out, m_mix_norm_g=m_mix_norm_g, m_mlp_norm_g=m_mlp_norm_g, m_mlp_w_up=m_mlp_w_up, m_mlp_w_down=m_mlp_w_down, v_fox_w_in=v_fox_w_in, v_fox_b_f=v_fox_b_f, v_fox_q_gain=v_fox_q_gain, v_fox_k_gain=v_fox_k_gain, v_fox_w_out=v_fox_w_out, v_dil_w_in=v_dil_w_in, v_dil_q_gain=v_dil_q_gain, v_dil_k_gain=v_dil_k_gain, v_dil_w_out=v_dil_w_out, v_mix_norm_g=v_mix_norm_g, v_mlp_norm_g=v_mlp_norm_g, v_mlp_w_up=v_mlp_w_up, v_mlp_w_down=v_mlp_w_down)
    weights = {n: given[n] for n in TWIN_WEIGHTS}
    shared = {n: given[n] for n in SHARED_INPUTS}
    per_example = {n: given[n] for n in ['x']}
    grad_fn = _jax.value_and_grad(_loss, argnums=(0, 1))

    def one_microbatch(ex, loss_target):
        ex = dict(ex)
        diff = ex.pop(TWIN_DIFF_INPUT)
        return grad_fn(weights, diff, {**shared, **ex}, loss_target)

    if N_MICROBATCH == 1:
        loss, (grad_w, grad_x) = one_microbatch(per_example, given["loss_target"])
    else:
        def body(carry, xs):
            loss_sum, grad_sum = carry
            l_k, (gw_k, gx_k) = one_microbatch(xs[0], xs[1])
            with _jax.named_scope("update"):
                return (loss_sum + l_k, _jax.tree.map(_jnp.add, grad_sum, gw_k)), gx_k

        init = (_jnp.zeros((), _jnp.float32), _jax.tree.map(_jnp.zeros_like, weights))
        (loss, grad_w), grad_x = _jax.lax.scan(body, init, (per_example, given["loss_target"]))
    with _jax.named_scope("update"):
        delta_w, new_m, new_v = {}, {}, {}
        for n in TWIN_WEIGHTS:
            delta_w[n], new_m[n], new_v[n] = _adamw(weights[n], grad_w[n], given["m_" + n], given["v_" + n])
    return (loss, grad_x, *[grad_w[n] for n in TWIN_WEIGHTS], *[delta_w[n] for n in TWIN_WEIGHTS],
            *[new_m[n] for n in TWIN_WEIGHTS], *[new_v[n] for n in TWIN_WEIGHTS])
```

```python
import functools

import numpy as np
import jax
import jax.numpy as jnp
from jax import lax
from jax.experimental import pallas as pl
from jax.experimental.pallas import tpu as pltpu

F32 = jnp.float32
BF16 = jnp.bfloat16

HEAD_DIM = 128
DIL_PATTERNS = ((128, 1), (512, 4), (2048, 16))
DIL_SPAN = 128
ALIBI_MAX_EXP = 8.0
EPS = 1e-6
MASKED = -1e30

ADAM_LR = 0.001
ADAM_B1 = 0.9
ADAM_B2 = 0.999
ADAM_EPS = 1e-08
ADAM_WD = 0.01
ADAM_STEP = 10

N_CHIPS = 4
VMEM_LIMIT_BYTES = 56 * 1024 * 1024
MESH = pl.DeviceIdType.MESH

NN = (((1,), (0,)), ((), ()))
NT = (((1,), (1,)), ((), ()))
TN = (((0,), (0,)), ((), ()))


def _params(*sem, **kw):
    return pltpu.CompilerParams(dimension_semantics=sem or None, vmem_limit_bytes=VMEM_LIMIT_BYTES, **kw)


def _dot(a, b, dims):
    return lax.dot_general(a, b, dims, preferred_element_type=F32)


def _tile(n, want):
    if n <= want:
        return n
    t = want - want % 128
    while n % t:
        t -= 128
    return t


def _mm(a, b, M, N, K, *, mode, name, out_dtypes, b_stack=0, out_stack=0, extras=(), epilogue=None,
        tm=1024, tn=1024, tk=512):
    per_b = per_o = None
    if b_stack:
        per_b = (K if mode == "nt" else N) // b_stack
    if out_stack:
        per_o = N // out_stack
    tm = _tile(M, tm)
    tn = _tile(min(x for x in (N, per_o, per_b if mode != "nt" else None) if x), tn)
    tk = _tile(min(x for x in (K, per_b if mode == "nt" else None) if x), tk)
    assert M % tm == 0 and N % tn == 0 and K % tk == 0, (name, M, N, K, tm, tn, tk)
    gk = K // tk
    if mode == "tn":
        a_spec = pl.BlockSpec((tk, tm), lambda i, j, k: (k, i))
    else:
        a_spec = pl.BlockSpec((tm, tk), lambda i, j, k: (i, k))
    if mode == "nt":
        if b_stack:
            npk = per_b // tk
            b_spec = pl.BlockSpec((None, tn, tk), lambda i, j, k: (k // npk, j, k % npk))
        else:
            b_spec = pl.BlockSpec((tn, tk), lambda i, j, k: (j, k))
    else:
        if b_stack:
            npj = per_b // tn
            b_spec = pl.BlockSpec((None, tk, tn), lambda i, j, k: (j // npj, k, j % npj))
        else:
            b_spec = pl.BlockSpec((tk, tn), lambda i, j, k: (k, j))
    if out_stack:
        npo = per_o // tn
        o_spec = pl.BlockSpec((None, tm, tn), lambda i, j, k: (j // npo, i, j % npo))
        o_shape = (out_stack, M, per_o)
    else:
        o_spec = pl.BlockSpec((tm, tn), lambda i, j, k: (i, j))
        o_shape = (M, N)
    e_spec = pl.BlockSpec((tm, tn), lambda i, j, k: (i, j))
    dims = {"nn": NN, "nt": NT, "tn": TN}[mode]
    ne, no = len(extras), len(out_dtypes)

    def body(a_ref, b_ref, *rest):
        ex, outs, acc = rest[:ne], rest[ne:ne + no], rest[ne + no]
        k = pl.program_id(2)

        @pl.when(k == 0)
        def _():
            acc[...] = jnp.zeros_like(acc)

        acc[...] += _dot(a_ref[...].astype(BF16), b_ref[...].astype(BF16), dims)

        @pl.when(k == gk - 1)
        def _():
            r = acc[...]
            res = epilogue(r, *[e[...] for e in ex]) if epilogue is not None else (r,)
            for o, v in zip(outs, res):
                o[...] = v.astype(o.dtype)

    outs = pl.pallas_call(
        body, name=name,
        grid=(M // tm, N // tn, gk),
        in_specs=[a_spec, b_spec] + [e_spec] * ne,
        out_specs=[o_spec] * no,
        out_shape=[jax.ShapeDtypeStruct(o_shape, d) for d in out_dtypes],
        scratch_shapes=[pltpu.VMEM((tm, tn), F32)],
        compiler_params=_params("parallel", "parallel", "arbitrary"),
    )(a, b, *extras)
    return outs[0] if no == 1 else outs


def _rms_fwd(x, g, name, tb=512):
    T, D = x.shape
    tb = min(tb, T)

    def body(x_ref, g_ref, o_ref):
        xv = x_ref[...]
        r = lax.rsqrt(jnp.mean(xv * xv, axis=-1, keepdims=True) + EPS)
        o_ref[...] = (xv * r * g_ref[...]).astype(BF16)

    return pl.pallas_call(
        body, name=name, grid=(T // tb,),
        in_specs=[pl.BlockSpec((tb, D), lambda i: (i, 0)), pl.BlockSpec((1, D), lambda i: (0, 0))],
        out_specs=pl.BlockSpec((tb, D), lambda i: (i, 0)),
        out_shape=jax.ShapeDtypeStruct((T, D), BF16),
        compiler_params=_params("parallel"),
    )(x, g)


def _rms_bwd(dy, x, g, dres, name, tb=256):
    T, D = x.shape
    tb = min(tb, T)

    def body(dy_ref, x_ref, g_ref, dres_ref, dx_ref, dxb_ref, dg_ref):
        i = pl.program_id(0)
        xv, dyv = x_ref[...], dy_ref[...]
        r = lax.rsqrt(jnp.mean(xv * xv, axis=-1, keepdims=True) + EPS)
        gy = dyv * g_ref[...]
        dx = r * gy - xv * (r * r * r) * jnp.mean(gy * xv, axis=-1, keepdims=True)
        tot = dres_ref[...] + dx
        dx_ref[...] = tot
        dxb_ref[...] = tot.astype(BF16)
        part = jnp.sum(dyv * (xv * r), axis=0, keepdims=True)

        @pl.when(i == 0)
        def _():
            dg_ref[...] = part

        @pl.when(i > 0)
        def _():
            dg_ref[...] += part

    row = pl.BlockSpec((tb, D), lambda i: (i, 0))
    vec = pl.BlockSpec((1, D), lambda i: (0, 0))
    return pl.pallas_call(
        body, name=name, grid=(T // tb,),
        in_specs=[row, row, vec, row],
        out_specs=[row, row, vec],
        out_shape=[jax.ShapeDtypeStruct((T, D), F32), jax.ShapeDtypeStruct((T, D), BF16),
                   jax.ShapeDtypeStruct((1, D), F32)],
        compiler_params=_params("arbitrary"),
    )(dy, x, g, dres)


def _loss_head(y, tgt, name, tb=256):
    T, D = y.shape
    tb = min(tb, T)

    def body(y_ref, t_ref, dy_ref, dyb_ref, loss_ref):
        i = pl.program_id(0)
        e = y_ref[...] - t_ref[...]
        d = e * (1.0 / D)
        dy_ref[...] = d
        dyb_ref[...] = d.astype(BF16)
        part = 0.5 * jnp.sum(jnp.sum(e * e, axis=1, keepdims=True) * (1.0 / D), axis=0, keepdims=True)

        @pl.when(i == 0)
        def _():
            loss_ref[...] = part

        @pl.when(i > 0)
        def _():
            loss_ref[...] += part

    row = pl.BlockSpec((tb, D), lambda i: (i, 0))
    return pl.pallas_call(
        body, name=name, grid=(T // tb,),
        in_specs=[row, row],
        out_specs=[row, row, pl.BlockSpec((1, 1), lambda i: (0, 0))],
        out_shape=[jax.ShapeDtypeStruct((T, D), F32), jax.ShapeDtypeStruct((T, D), BF16),
                   jax.ShapeDtypeStruct((1, 1), F32)],
        compiler_params=_params("arbitrary"),
    )(y, tgt)


def _head_rms(xh, g):
    r = lax.rsqrt(jnp.mean(xh * xh, axis=-1, keepdims=True) + EPS)
    return xh * r * g


def _qkv_prep(proj, gains, n_norm, gain_row, ch, name, tb=512):
    T, W = proj.shape
    tb = min(tb, T)
    nch = W // ch
    nh = ch // HEAD_DIM

    def body(p_ref, g_ref, o_ref):
        j = pl.program_id(0)

        @pl.when(j < n_norm)
        def _():
            g = g_ref[...]
            for h in range(nh):
                sl = slice(h * HEAD_DIM, (h + 1) * HEAD_DIM)
                o_ref[:, sl] = _head_rms(p_ref[:, sl], g).astype(BF16)

        @pl.when(j >= n_norm)
        def _():
            o_ref[...] = p_ref[...].astype(BF16)

    return pl.pallas_call(
        body, name=name, grid=(nch, T // tb),
        in_specs=[pl.BlockSpec((tb, ch), lambda j, i: (i, j)),
                  pl.BlockSpec((None, 1, HEAD_DIM), lambda j, i: (gain_row(j), 0, 0))],
        out_specs=pl.BlockSpec((tb, ch), lambda j, i: (i, j)),
        out_shape=jax.ShapeDtypeStruct((T, W), BF16),
        compiler_params=_params("parallel", "parallel"),
    )(proj, gains)


def _into(body, name, grid, in_specs, out_spec, out_shape, extra_out_specs, extra_out_shapes, buf, operands, sem):
    if buf is None:
        def kernel(*refs):
            body(*refs)
        ins, alias, ops = in_specs, {}, operands
    else:
        def kernel(_, *refs):
            body(*refs)
        ins = [pl.BlockSpec(memory_space=pl.ANY)] + in_specs
        alias, ops = {0: 0}, (buf,) + tuple(operands)
    return pl.pallas_call(
        kernel, name=name, grid=grid, in_specs=ins,
        out_specs=[out_spec] + extra_out_specs,
        out_shape=[out_shape] + extra_out_shapes,
        input_output_aliases=alias,
        compiler_params=_params(*sem),
    )(*ops)


def _head_rms_bwd_into(buf, W, d, proj, gain, off, ch, name, tb=256):
    T, wd = d.shape
    tb = min(tb, T)
    n = wd // ch
    nh = ch // HEAD_DIM

    def body(d_ref, p_ref, g_ref, o_ref, dg_ref):
        i = pl.program_id(1)
        g = g_ref[...]
        part = jnp.zeros((1, HEAD_DIM), F32)
        for h in range(nh):
            sl = slice(h * HEAD_DIM, (h + 1) * HEAD_DIM)
            xh, dy = p_ref[:, sl], d_ref[:, sl]
            r = lax.rsqrt(jnp.mean(xh * xh, axis=-1, keepdims=True) + EPS)
            gy = dy * g
            dx = r * gy - xh * (r * r * r) * jnp.mean(gy * xh, axis=-1, keepdims=True)
            o_ref[:, sl] = dx.astype(BF16)
            part = part + jnp.sum(dy * (xh * r), axis=0, keepdims=True)

        @pl.when(i == 0)
        def _():
            dg_ref[...] = part

        @pl.when(i > 0)
        def _():
            dg_ref[...] += part

    return _into(
        body, name, (n, T // tb),
        [pl.BlockSpec((tb, ch), lambda j, i: (i, j)), pl.BlockSpec((tb, ch), lambda j, i: (i, off + j)),
         pl.BlockSpec((1, HEAD_DIM), lambda j, i: (0, 0))],
        pl.BlockSpec((tb, ch), lambda j, i: (i, off + j)), jax.ShapeDtypeStruct((T, W), BF16),
        [pl.BlockSpec((None, 1, HEAD_DIM), lambda j, i: (j, 0, 0))], [jax.ShapeDtypeStruct((n, 1, HEAD_DIM), F32)],
        buf, (d, proj, gain), ("parallel", "arbitrary"))


def _sum_cast_into(buf, W, srcs, off, ch, name, tb=256):
    T, wd = srcs[0].shape
    tb = min(tb, T)
    n = wd // ch
    ns = len(srcs)

    def body(*refs):
        o_ref = refs[ns]
        tot = refs[0][...]
        for s in refs[1:ns]:
            tot = tot + s[...]
        o_ref[...] = tot.astype(BF16)

    out = _into(
        body, name, (n, T // tb),
        [pl.BlockSpec((tb, ch), lambda j, i: (i, j))] * ns,
        pl.BlockSpec((tb, ch), lambda j, i: (i, off + j)), jax.ShapeDtypeStruct((T, W), BF16),
        [], [], buf, tuple(srcs), ("parallel", "parallel"))
    return out[0]


def _tri(n, lower):
    r = lax.broadcasted_iota(jnp.int32, (n, n), 0)
    c = lax.broadcasted_iota(jnp.int32, (n, n), 1)
    return jnp.where((c <= r) if lower else (c >= r), 1.0, 0.0).astype(F32)


def _dot_exact(a, b):
    return lax.dot_general(a, b, NN, precision=lax.Precision.HIGHEST, preferred_element_type=F32)


def _log_sigmoid(z):
    return jnp.minimum(z, 0.0) - jnp.log(1.0 + jnp.exp(-jnp.abs(z)))


def _gate_fwd(f_raw, b_pad, hp, name, blk=256):
    T = f_raw.shape[0]
    blk = min(blk, T)

    def body(f_ref, b_ref, c_ref):
        tri = _tri(blk, True)
        carry = jnp.zeros((1, HEAD_DIM), F32)
        for j in range(T // blk):
            lf = _log_sigmoid(f_ref[j * blk:(j + 1) * blk, :] + b_ref[...])
            cb = _dot_exact(tri, lf) + carry
            carry = cb[blk - 1:blk, :]
            c_ref[:, j * blk:(j + 1) * blk] = cb.T[:hp, :]

    return pl.pallas_call(
        body, name=name,
        in_specs=[pl.BlockSpec(memory_space=pltpu.VMEM)] * 2,
        out_specs=pl.BlockSpec(memory_space=pltpu.VMEM),
        out_shape=jax.ShapeDtypeStruct((hp, T), F32),
        compiler_params=_params(),
    )(f_raw, b_pad)


def _gate_bwd(dc_rows, dc_cols, f_raw, b_pad, n_heads, hp, name, blk=256):
    T = f_raw.shape[0]
    blk = min(blk, T)

    def body(dc_ref, dcc_ref, f_ref, b_ref, dz_ref, db_ref):
        tri = _tri(blk, False)
        lane = lax.broadcasted_iota(jnp.int32, (blk, HEAD_DIM), 1)
        carry = jnp.zeros((1, HEAD_DIM), F32)
        db = jnp.zeros((1, HEAD_DIM), F32)
        for j in reversed(range(T // blk)):
            rows = dc_ref[:, j * blk:(j + 1) * blk]
            if hp < HEAD_DIM:
                rows = jnp.concatenate([rows, jnp.zeros((HEAD_DIM - hp, blk), F32)], axis=0)
            dlf = _dot_exact(tri, rows.T + dcc_ref[j * blk:(j + 1) * blk, :]) + carry
            carry = dlf[0:1, :]
            z = f_ref[j * blk:(j + 1) * blk, :] + b_ref[...]
            dz = jnp.where(lane < n_heads, dlf / (1.0 + jnp.exp(z)), 0.0)
            dz_ref[j * blk:(j + 1) * blk, :] = dz.astype(BF16)
            db = db + jnp.sum(dz, axis=0, keepdims=True)
        db_ref[...] = db

    return pl.pallas_call(
        body, name=name,
        in_specs=[pl.BlockSpec(memory_space=pltpu.VMEM)] * 4,
        out_specs=[pl.BlockSpec(memory_space=pltpu.VMEM)] * 2,
        out_shape=[jax.ShapeDtypeStruct((T, HEAD_DIM), BF16), jax.ShapeDtypeStruct((1, HEAD_DIM), F32)],
        compiler_params=_params(),
    )(dc_rows, dc_cols, f_raw, b_pad)


def _pairs(nb, key_major):
    if key_major:
        pairs = [(qi, ki) for ki in range(nb) for qi in range(ki, nb)]
    else:
        pairs = [(qi, ki) for qi in range(nb) for ki in range(qi + 1)]
    return (jnp.asarray(np.array([p[0] for p in pairs], np.int32)),
            jnp.asarray(np.array([p[1] for p in pairs], np.int32)))


def _causal_logits(q, k, ck, qi, ki, tb):
    s = _dot(q, k, NT) * (HEAD_DIM ** -0.5) - ck
    row = qi * tb + lax.broadcasted_iota(jnp.int32, (tb, tb), 0)
    col = ki * tb + lax.broadcasted_iota(jnp.int32, (tb, tb), 1)
    return jnp.where(col <= row, s, MASKED)


def _fox_fwd(qkv, ck, H, name, tb=512):
    T = qkv.shape[0]
    tb = min(tb, T)
    nb = T // tb
    qt, kt = _pairs(nb, False)

    def body(qt_ref, kt_ref, q_ref, k_ref, v_ref, ck_ref, o_ref, lse_ref, m_sc, l_sc, acc_sc):
        p_ = pl.program_id(1)
        qi, ki = qt_ref[p_], kt_ref[p_]

        @pl.when(ki == 0)
        def _():
            m_sc[...] = jnp.full_like(m_sc, MASKED)
            l_sc[...] = jnp.zeros_like(l_sc)
            acc_sc[...] = jnp.zeros_like(acc_sc)

        s = _causal_logits(q_ref[...], k_ref[...], ck_ref[...], qi, ki, tb)
        m_prev = m_sc[...]
        m_new = jnp.maximum(m_prev, jnp.max(s, axis=1, keepdims=True))
        alpha = jnp.exp(m_prev - m_new)
        p = jnp.exp(s - m_new[:, :1])
        l_sc[...] = alpha * l_sc[...] + jnp.sum(p, axis=1, keepdims=True)
        acc_sc[...] = alpha * acc_sc[...] + _dot(p.astype(BF16), v_ref[...], NN)
        m_sc[...] = m_new

        @pl.when(ki == qi)
        def _():
            o_ref[...] = (acc_sc[...] / l_sc[...]).astype(BF16)
            lse_ref[...] = m_sc[...] + jnp.log(l_sc[...])

    blk = lambda f: pl.BlockSpec((tb, HEAD_DIM), f)
    return pl.pallas_call(
        body, name=name,
        grid_spec=pltpu.PrefetchScalarGridSpec(
            num_scalar_prefetch=2, grid=(H, qt.shape[0]),
            in_specs=[blk(lambda h, p, qt, kt: (qt[p], h)),
                      blk(lambda h, p, qt, kt: (kt[p], H + h)),
                      blk(lambda h, p, qt, kt: (kt[p], 2 * H + h)),
                      pl.BlockSpec((None, 1, tb), lambda h, p, qt, kt: (h, 0, kt[p]))],
            out_specs=[blk(lambda h, p, qt, kt: (qt[p], h)), blk(lambda h, p, qt, kt: (qt[p], h))],
            scratch_shapes=[pltpu.VMEM((tb, HEAD_DIM), F32)] * 3),
        out_shape=[jax.ShapeDtypeStruct((T, H * HEAD_DIM), BF16), jax.ShapeDtypeStruct((T, H * HEAD_DIM), F32)],
        compiler_params=_params("parallel", "arbitrary"),
    )(qt, kt, qkv, qkv, qkv, ck)


def _row_dot(do, o, nh, width, name, lane_per_head, tb=256):
    T = do.shape[0]
    tb = min(tb, T)
    wout = HEAD_DIM if lane_per_head else nh * HEAD_DIM

    def body(do_ref, o_ref, d_ref):
        lane = lax.broadcasted_iota(jnp.int32, (tb, HEAD_DIM), 1)
        tile = jnp.zeros((tb, HEAD_DIM), F32)
        for h in range(nh):
            sl = slice(h * width, (h + 1) * width)
            d = jnp.sum(do_ref[:, sl].astype(F32) * o_ref[:, sl].astype(F32), axis=1, keepdims=True)
            if lane_per_head:
                tile = jnp.where(lane == h, d, tile)
            else:
                d_ref[:, h * HEAD_DIM:(h + 1) * HEAD_DIM] = jnp.broadcast_to(d, (tb, HEAD_DIM))
        if lane_per_head:
            d_ref[...] = tile

    row = pl.BlockSpec((tb, nh * width), lambda i: (i, 0))
    return pl.pallas_call(
        body, name=name, grid=(T // tb,),
        in_specs=[row, row], out_specs=pl.BlockSpec((tb, wout), lambda i: (i, 0)),
        out_shape=jax.ShapeDtypeStruct((T, wout), F32),
        compiler_params=_params("parallel"),
    )(do, o)


def _fox_bwd(qkv, do, ck, lse, dd, H, hp, name, tb=512):
    T = qkv.shape[0]
    tb = min(tb, T)
    nb = T // tb
    qt, kt = _pairs(nb, True)
    scale = HEAD_DIM ** -0.5

    def body(qt_ref, kt_ref, q_ref, k_ref, v_ref, do_ref, ck_ref, lse_ref, dd_ref, dq_ref, dk_ref, dv_ref, dc_ref, dcq_ref):
        p_ = pl.program_id(1)
        qi, ki = qt_ref[p_], kt_ref[p_]

        @pl.when(p_ == 0)
        def _():
            dq_ref[...] = jnp.zeros_like(dq_ref)
            dcq_ref[...] = jnp.zeros_like(dcq_ref)

        @pl.when(qi == ki)
        def _():
            dk_ref[...] = jnp.zeros_like(dk_ref)
            dv_ref[...] = jnp.zeros_like(dv_ref)
            dc_ref[...] = jnp.zeros_like(dc_ref)

        s = _causal_logits(q_ref[...], k_ref[...], ck_ref[...], qi, ki, tb)
        p = jnp.exp(s - lse_ref[:, :1])
        dv_ref[...] += _dot(p.astype(BF16), do_ref[...], TN)
        dp = _dot(do_ref[...], v_ref[...], NT)
        ds = p * (dp - dd_ref[:, :1])
        dc_ref[...] -= jnp.sum(ds, axis=0, keepdims=True)
        dsb = ds.astype(BF16)
        rows = pl.ds(pl.multiple_of(qi * tb, tb), tb)
        dq_ref[rows, :] += scale * _dot(dsb, k_ref[...], NN)
        dcq_ref[rows, :] += jnp.sum(ds, axis=1, keepdims=True)
        dk_ref[...] += scale * _dot(dsb, q_ref[...], TN)

    blk = lambda f: pl.BlockSpec((tb, HEAD_DIM), f)
    at_q = lambda h, p, qt, kt: (qt[p], h)
    at_k = lambda h, p, qt, kt: (kt[p], h)
    crow = pl.BlockSpec((None, 1, tb), lambda h, p, qt, kt: (h, 0, kt[p]))
    whole = pl.BlockSpec((T, HEAD_DIM), lambda h, p, qt, kt: (0, h))
    wide = jax.ShapeDtypeStruct((T, H * HEAD_DIM), F32)
    return pl.pallas_call(
        body, name=name,
        grid_spec=pltpu.PrefetchScalarGridSpec(
            num_scalar_prefetch=2, grid=(H, qt.shape[0]),
            in_specs=[blk(at_q),
                      blk(lambda h, p, qt, kt: (kt[p], H + h)),
                      blk(lambda h, p, qt, kt: (kt[p], 2 * H + h)),
                      blk(at_q), crow, blk(at_q), blk(at_q)],
            out_specs=[whole, blk(at_k), blk(at_k), crow, whole]),
        out_shape=[wide, wide, wide, jax.ShapeDtypeStruct((hp, 1, T), F32), wide],
        compiler_params=_params("parallel", "arbitrary"),
    )(qt, kt, qkv, qkv, qkv, do, ck, lse, dd)


def _lane_per_head(wide, H, name, tb=256):
    T = wide.shape[0]
    tb = min(tb, T)

    def body(w_ref, o_ref):
        lane = lax.broadcasted_iota(jnp.int32, (tb, HEAD_DIM), 1)
        tile = jnp.zeros((tb, HEAD_DIM), F32)
        for h in range(H):
            tile = jnp.where(lane == h, w_ref[:, h * HEAD_DIM:(h + 1) * HEAD_DIM], tile)
        o_ref[...] = tile

    return pl.pallas_call(
        body, name=name, grid=(T // tb,),
        in_specs=[pl.BlockSpec((tb, H * HEAD_DIM), lambda i: (i, 0))],
        out_specs=pl.BlockSpec((tb, HEAD_DIM), lambda i: (i, 0)),
        out_shape=jax.ShapeDtypeStruct((T, HEAD_DIM), F32),
        compiler_params=_params("parallel"),
    )(wide)


def _slopes(n_groups, nh):
    n = n_groups * nh
    s = np.exp2(-ALIBI_MAX_EXP * np.arange(1, n + 1, dtype=np.float32) / np.float32(n)).astype(np.float32)
    return s.reshape(n_groups, nh)


def _window_logits(qh, kh, slope_r, prev, has_prev):
    qi = lax.broadcasted_iota(jnp.int32, (DIL_SPAN, DIL_SPAN), 0)
    kl = lax.broadcasted_iota(jnp.int32, (DIL_SPAN, DIL_SPAN), 1)
    delta = qi - kl + (DIL_SPAN if prev else 0)
    s = _dot(qh, kh, NT) * (HEAD_DIM ** -0.5) - slope_r * delta.astype(F32)
    valid = ((kl >= qi) & has_prev) if prev else (kl <= qi)
    return jnp.where(valid, s, MASKED)


def _dil_views(T, r, G, nh, dv):
    L = T // r
    C, V = nh * HEAD_DIM, nh * dv
    return L, C, V, 2 * G * C + V


def _dil_fwd(qkv, g, r, G, nh, dv, slopes, name):
    T = qkv.shape[0]
    L, C, V, W = _dil_views(T, r, G, nh, dv)
    nblk = L // DIL_SPAN
    nc, nv = W // C, W // V
    view = qkv.reshape(L, r * W)

    def body(q_ref, kp_ref, kc_ref, vp_ref, vc_ref, num_ref, m_ref, den_ref):
        has_prev = pl.program_id(1) > 0
        lane = lax.broadcasted_iota(jnp.int32, (DIL_SPAN, HEAD_DIM), 1)
        m_tile = jnp.zeros((DIL_SPAN, HEAD_DIM), F32)
        den_tile = jnp.ones((DIL_SPAN, HEAD_DIM), F32)
        for h in range(nh):
            sl = slice(h * HEAD_DIM, (h + 1) * HEAD_DIM)
            vs = slice(h * dv, (h + 1) * dv)
            sr = float(slopes[h]) * r
            sc = _window_logits(q_ref[:, sl], kc_ref[:, sl], sr, False, has_prev)
            sp = _window_logits(q_ref[:, sl], kp_ref[:, sl], sr, True, has_prev)
            m = jnp.maximum(jnp.max(sc, axis=1, keepdims=True), jnp.max(sp, axis=1, keepdims=True))
            pc, pp = jnp.exp(sc - m), jnp.exp(sp - m)
            den = jnp.sum(pc, axis=1, keepdims=True) + jnp.sum(pp, axis=1, keepdims=True)
            num_ref[:, vs] = _dot(pc.astype(BF16), vc_ref[:, vs], NN) + _dot(pp.astype(BF16), vp_ref[:, vs], NN)
            m_tile = jnp.where(lane == h, m, m_tile)
            den_tile = jnp.where(lane == h, den, den_tile)
        m_ref[...] = m_tile
        den_ref[...] = den_tile

    prev = lambda i: jnp.maximum(i - 1, 0)
    stat = pl.BlockSpec((DIL_SPAN, HEAD_DIM), lambda b, i: (i, b))
    num, m, den = pl.pallas_call(
        body, name=name, grid=(r, nblk),
        in_specs=[pl.BlockSpec((DIL_SPAN, C), lambda b, i: (i, b * nc + g)),
                  pl.BlockSpec((DIL_SPAN, C), lambda b, i: (prev(i), b * nc + G + g)),
                  pl.BlockSpec((DIL_SPAN, C), lambda b, i: (i, b * nc + G + g)),
                  pl.BlockSpec((DIL_SPAN, V), lambda b, i: (prev(i), b * nv + nv - 1)),
                  pl.BlockSpec((DIL_SPAN, V), lambda b, i: (i, b * nv + nv - 1))],
        out_specs=[pl.BlockSpec((DIL_SPAN, V), lambda b, i: (i, b)), stat, stat],
        out_shape=[jax.ShapeDtypeStruct((L, r * V), F32), jax.ShapeDtypeStruct((L, r * HEAD_DIM), F32),
                   jax.ShapeDtypeStruct((L, r * HEAD_DIM), F32)],
        compiler_params=_params("parallel", "parallel"),
    )(view, view, view, view, view)
    return num.reshape(T, V), m.reshape(T, HEAD_DIM), den.reshape(T, HEAD_DIM)


def _dil_merge(nums, ms, dens, nh, dv, name, tb=256):
    T, V = nums[0].shape
    tb = min(tb, T)
    G = len(nums)

    def body(*refs):
        num_r, m_r, den_r = refs[:G], refs[G:2 * G], refs[2 * G:3 * G]
        o_ref, lse_ref = refs[3 * G], refs[3 * G + 1]
        mm = m_r[0][...]
        for g in range(1, G):
            mm = jnp.maximum(mm, m_r[g][...])
        w = [jnp.exp(m_r[g][...] - mm) for g in range(G)]
        den = w[0] * den_r[0][...]
        for g in range(1, G):
            den = den + w[g] * den_r[g][...]
        lse_ref[...] = mm + jnp.log(den)
        for h in range(nh):
            vs = slice(h * dv, (h + 1) * dv)
            num = w[0][:, h:h + 1] * num_r[0][:, vs]
            for g in range(1, G):
                num = num + w[g][:, h:h + 1] * num_r[g][:, vs]
            o_ref[:, vs] = (num / den[:, h:h + 1]).astype(BF16)

    wide = pl.BlockSpec((tb, V), lambda i: (i, 0))
    stat = pl.BlockSpec((tb, HEAD_DIM), lambda i: (i, 0))
    return pl.pallas_call(
        body, name=name, grid=(T // tb,),
        in_specs=[wide] * G + [stat] * (2 * G),
        out_specs=[wide, stat],
        out_shape=[jax.ShapeDtypeStruct((T, V), BF16), jax.ShapeDtypeStruct((T, HEAD_DIM), F32)],
        compiler_params=_params("parallel"),
    )(*nums, *ms, *dens)


def _dil_dq(qkv, do, lse, dd, g, r, G, nh, dv, slopes, name):
    T = qkv.shape[0]
    L, C, V, W = _dil_views(T, r, G, nh, dv)
    nblk = L // DIL_SPAN
    nc, nv = W // C, W // V
    view = qkv.reshape(L, r * W)
    scale = HEAD_DIM ** -0.5

    def body(q_ref, kp_ref, kc_ref, vp_ref, vc_ref, do_ref, lse_ref, dd_ref, dq_ref):
        has_prev = pl.program_id(1) > 0
        for h in range(nh):
            sl = slice(h * HEAD_DIM, (h + 1) * HEAD_DIM)
            vs = slice(h * dv, (h + 1) * dv)
            sr = float(slopes[h]) * r
            lse_h, dd_h = lse_ref[:, h:h + 1], dd_ref[:, h:h + 1]
            acc = jnp.zeros((DIL_SPAN, HEAD_DIM), F32)
            for k_ref, v_ref, is_prev in ((kc_ref, vc_ref, False), (kp_ref, vp_ref, True)):
                s = _window_logits(q_ref[:, sl], k_ref[:, sl], sr, is_prev, has_prev)
                p = jnp.exp(s - lse_h)
                dp = _dot(do_ref[:, vs], v_ref[:, vs], NT)
                ds = (p * (dp - dd_h)).astype(BF16)
                acc = acc + _dot(ds, k_ref[:, sl], NN)
            dq_ref[:, sl] = scale * acc

    prev = lambda i: jnp.maximum(i - 1, 0)
    stat = pl.BlockSpec((DIL_SPAN, HEAD_DIM), lambda b, i: (i, b))
    dq = pl.pallas_call(
        body, name=name, grid=(r, nblk),
        in_specs=[pl.BlockSpec((DIL_SPAN, C), lambda b, i: (i, b * nc + g)),
                  pl.BlockSpec((DIL_SPAN, C), lambda b, i: (prev(i), b * nc + G + g)),
                  pl.BlockSpec((DIL_SPAN, C), lambda b, i: (i, b * nc + G + g)),
                  pl.BlockSpec((DIL_SPAN, V), lambda b, i: (prev(i), b * nv + nv - 1)),
                  pl.BlockSpec((DIL_SPAN, V), lambda b, i: (i, b * nv + nv - 1)),
                  pl.BlockSpec((DIL_SPAN, V), lambda b, i: (i, b)), stat, stat],
        out_specs=pl.BlockSpec((DIL_SPAN, C), lambda b, i: (i, b)),
        out_shape=jax.ShapeDtypeStruct((L, r * C), F32),
        compiler_params=_params("parallel", "parallel"),
    )(view, view, view, view, view, do.reshape(L, r * V), lse.reshape(L, r * HEAD_DIM), dd.reshape(L, r * HEAD_DIM))
    return dq.reshape(T, C)


def _dil_dkv(qkv, do, lse, dd, g, r, G, nh, dv, slopes, name):
    T = qkv.shape[0]
    L, C, V, W = _dil_views(T, r, G, nh, dv)
    nblk = L // DIL_SPAN
    nc, nv = W // C, W // V
    view = qkv.reshape(L, r * W)
    scale = HEAD_DIM ** -0.5

    def body(k_ref, v_ref, qc_ref, qn_ref, doc_ref, don_ref, lsec_ref, lsen_ref, ddc_ref, ddn_ref, dk_ref, dv_ref):
        has_next = pl.program_id(1) < nblk - 1
        for h in range(nh):
            sl = slice(h * HEAD_DIM, (h + 1) * HEAD_DIM)
            vs = slice(h * dv, (h + 1) * dv)
            sr = float(slopes[h]) * r
            dk = jnp.zeros((DIL_SPAN, HEAD_DIM), F32)
            dvh = jnp.zeros((DIL_SPAN, dv), F32)
            for q_ref, do_ref, lse_ref, dd_ref, is_next in ((qc_ref, doc_ref, lsec_ref, ddc_ref, False),
                                                          (qn_ref, don_ref, lsen_ref, ddn_ref, True)):
                s = _window_logits(q_ref[:, sl], k_ref[:, sl], sr, is_next, has_next)
                p = jnp.exp(s - lse_ref[:, h:h + 1])
                dvh = dvh + _dot(p.astype(BF16), do_ref[:, vs], TN)
                dp = _dot(do_ref[:, vs], v_ref[:, vs], NT)
                ds = (p * (dp - dd_ref[:, h:h + 1])).astype(BF16)
                dk = dk + _dot(ds, q_ref[:, sl], TN)
            dk_ref[:, sl] = scale * dk
            dv_ref[:, vs] = dvh

    nxt = lambda i: jnp.minimum(i + 1, nblk - 1)
    stat_c = pl.BlockSpec((DIL_SPAN, HEAD_DIM), lambda b, i: (i, b))
    stat_n = pl.BlockSpec((DIL_SPAN, HEAD_DIM), lambda b, i: (nxt(i), b))
    do_v, lse_v, dd_v = do.reshape(L, r * V), lse.reshape(L, r * HEAD_DIM), dd.reshape(L, r * HEAD_DIM)
    dk, dvv = pl.pallas_call(
        body, name=name, grid=(r, nblk),
        in_specs=[pl.BlockSpec((DIL_SPAN, C), lambda b, i: (i, b * nc + G + g)),
                  pl.BlockSpec((DIL_SPAN, V), lambda b, i: (i, b * nv + nv - 1)),
                  pl.BlockSpec((DIL_SPAN, C), lambda b, i: (i, b * nc + g)),
                  pl.BlockSpec((DIL_SPAN, C), lambda b, i: (nxt(i), b * nc + g)),
                  pl.BlockSpec((DIL_SPAN, V), lambda b, i: (i, b)),
                  pl.BlockSpec((DIL_SPAN, V), lambda b, i: (nxt(i), b)),
                  stat_c, stat_n, stat_c, stat_n],
        out_specs=[pl.BlockSpec((DIL_SPAN, C), lambda b, i: (i, b)), pl.BlockSpec((DIL_SPAN, V), lambda b, i: (i, b))],
        out_shape=[jax.ShapeDtypeStruct((L, r * C), F32), jax.ShapeDtypeStruct((L, r * V), F32)],
        compiler_params=_params("parallel", "parallel"),
    )(view, view, view, view, do_v, do_v, lse_v, lse_v, dd_v, dd_v)
    return dk.reshape(T, C), dvv.reshape(T, V)


def _relu2(r):
    a = jnp.maximum(r, 0.0)
    return a, a * a


def _mlp_fwd(x, g, w_up, w_down, tag):
    T, D = x.shape
    F = w_down.shape[0]
    h = _rms_fwd(x, g, f"{tag}_norm")
    a, a2 = _mm(h, w_up, T, F, D, mode="nn", name=f"{tag}_up", b_stack=N_CHIPS, out_dtypes=(F32, BF16), epilogue=_relu2)
    y = _mm(a2, w_down, T, D, F, mode="nn", name=f"{tag}_down", out_dtypes=(F32,), extras=(x,),
            epilogue=lambda r, res: (res + r,))
    return y, (x, h, a, a2)


def _mlp_bwd(dy, dyb, saved, g, w_up, w_down, tag):
    x, h, a, a2 = saved
    T, D = x.shape
    F = w_down.shape[0]
    d_down = _mm(a2, dyb, F, D, T, mode="tn", name=f"{tag}_dwdown", out_dtypes=(F32,))
    du = _mm(dyb, w_down, T, F, D, mode="nt", name=f"{tag}_da", out_dtypes=(BF16,), extras=(a,),
             epilogue=lambda r, av: (r * (2.0 * av),))
    d_up = _mm(h, du, D, F, T, mode="tn", name=f"{tag}_dwup", out_stack=N_CHIPS, out_dtypes=(F32,))
    dh = _mm(du, w_up, T, D, F, mode="nt", name=f"{tag}_dh", b_stack=N_CHIPS, out_dtypes=(F32,))
    dx, dxb, dg = _rms_bwd(dh, x, g, dy, f"{tag}_dnorm")
    return dx, dxb, dg, d_up, d_down


def _fox_dims(D):
    H = D // HEAD_DIM
    return H, max(8, H), (H // 2) * HEAD_DIM


def _fox_layer_fwd(x, g, w_qkv, w_f, b_pad, gains, w_out):
    T, D = x.shape
    H, hp, ch = _fox_dims(D)
    h = _rms_fwd(x, g, "fox_norm")
    proj = _mm(h, w_qkv, T, 3 * D, D, mode="nn", name="fox_proj", out_dtypes=(F32,))
    f_raw = _mm(h, w_f, T, HEAD_DIM, D, mode="nn", name="fox_gate_proj", out_dtypes=(F32,))
    qkv = _qkv_prep(proj, gains, 4, lambda j: jnp.minimum(j // 2, 1), ch, "fox_qk_norm")
    ck = _gate_fwd(f_raw, b_pad, hp, "fox_gate").reshape(hp, 1, T)
    o, lse = _fox_fwd(qkv, ck, H, "fox_attn")
    y = _mm(o, w_out, T, D, D, mode="nn", name="fox_out", out_dtypes=(F32,), extras=(x,),
            epilogue=lambda r, res: (res + r,))
    return y, (x, h, proj, f_raw, qkv, ck, o, lse)


def _fox_layer_bwd(dy, dyb, saved, g, w_qkv, w_f, b_pad, gains, w_out):
    x, h, proj, f_raw, qkv, ck, o, lse = saved
    T, D = x.shape
    H, hp, ch = _fox_dims(D)
    d_out = _mm(o, dyb, D, D, T, mode="tn", name="fox_dwout", out_dtypes=(F32,))
    do = _mm(dyb, w_out, T, D, D, mode="nt", name="fox_do", out_dtypes=(BF16,))
    dd = _row_dot(do, o, H, HEAD_DIM, "fox_rowdot", False)
    dq, dk, dv, dck, dcq = _fox_bwd(qkv, do, ck, lse, dd, H, hp, "fox_attn_bwd")
    dcq = _lane_per_head(dcq, H, "fox_dc_query")
    dproj, dgq = _head_rms_bwd_into(None, 3 * D, dq, proj, gains[0], 0, ch, "fox_dq_norm")
    dproj, dgk = _head_rms_bwd_into(dproj, 3 * D, dk, proj, gains[1], 2, ch, "fox_dk_norm")
    dproj = _sum_cast_into(dproj, 3 * D, [dv], 4, ch, "fox_dv_cast")
    dz, db = _gate_bwd(dck.reshape(hp, T), dcq, f_raw, b_pad, H, hp, "fox_gate_bwd")
    d_qkv = _mm(h, dproj, D, 3 * D, T, mode="tn", name="fox_dwqkv", out_dtypes=(F32,))
    d_f = _mm(h, dz, D, HEAD_DIM, T, mode="tn", name="fox_dwgate", out_dtypes=(F32,))
    dh = _mm(dproj, w_qkv, T, D, 3 * D, mode="nt", name="fox_dh", out_dtypes=(F32,))
    dh = _mm(dz, w_f, T, D, HEAD_DIM, mode="nt", name="fox_dh_gate", out_dtypes=(F32,), extras=(dh,),
             epilogue=lambda r, e: (e + r,))
    dx, dxb, dg = _rms_bwd(dh, x, g, dy, "fox_dnorm")
    dgains = jnp.stack([dgq.sum(axis=0), dgk.sum(axis=0)])
    return dx, dxb, dg, d_qkv, d_f, db, dgains, d_out


def _dil_dims(D):
    nh = D // (2 * HEAD_DIM)
    return nh, D // nh, len(DIL_PATTERNS)


def _dil_layer_fwd(x, g, w_in, gains, w_out):
    T, D = x.shape
    nh, dv, G = _dil_dims(D)
    C = nh * HEAD_DIM
    W = 2 * G * C + nh * dv
    slopes = _slopes(G, nh)
    h = _rms_fwd(x, g, "dil_norm")
    proj = _mm(h, w_in, T, W, D, mode="nn", name="dil_proj", b_stack=N_CHIPS, out_dtypes=(F32,))
    qkv = _qkv_prep(proj, gains, 2 * G, lambda j: jnp.minimum(j, 2 * G - 1), C, "dil_qk_norm")
    parts = [_dil_fwd(qkv, gi, r, G, nh, dv, slopes[gi], f"dil_attn_g{gi}") for gi, (_, r) in enumerate(DIL_PATTERNS)]
    o, lse = _dil_merge([p[0] for p in parts], [p[1] for p in parts], [p[2] for p in parts], nh, dv, "dil_merge")
    y = _mm(o, w_out, T, D, D, mode="nn", name="dil_out", out_dtypes=(F32,), extras=(x,),
            epilogue=lambda r, res: (res + r,))
    return y, (x, h, proj, qkv, o, lse)


def _dil_layer_bwd(dy, dyb, saved, g, w_in, gains, w_out):
    x, h, proj, qkv, o, lse = saved
    T, D = x.shape
    nh, dv, G = _dil_dims(D)
    C = nh * HEAD_DIM
    W = 2 * G * C + nh * dv
    slopes = _slopes(G, nh)
    d_out = _mm(o, dyb, D, D, T, mode="tn", name="dil_dwout", out_dtypes=(F32,))
    do = _mm(dyb, w_out, T, D, D, mode="nt", name="dil_do", out_dtypes=(BF16,))
    dd = _row_dot(do, o, nh, dv, "dil_rowdot", True)
    dproj, dgs, dvs = None, [None] * (2 * G), []
    for gi, (_, r) in enumerate(DIL_PATTERNS):
        dq = _dil_dq(qkv, do, lse, dd, gi, r, G, nh, dv, slopes[gi], f"dil_dq_g{gi}")
        dk, dvg = _dil_dkv(qkv, do, lse, dd, gi, r, G, nh, dv, slopes[gi], f"dil_dkv_g{gi}")
        dvs.append(dvg)
        dproj, dgs[gi] = _head_rms_bwd_into(dproj, W, dq, proj, gains[gi], gi, C, f"dil_dq_norm_g{gi}")
        dproj, dgs[G + gi] = _head_rms_bwd_into(dproj, W, dk, proj, gains[G + gi], G + gi, C, f"dil_dk_norm_g{gi}")
    dproj = _sum_cast_into(dproj, W, dvs, 2 * G, C, "dil_dv_cast")
    d_in = _mm(h, dproj, D, W, T, mode="tn", name="dil_dwin", out_stack=N_CHIPS, out_dtypes=(F32,))
    dh = _mm(dproj, w_in, T, D, W, mode="nt", name="dil_dh", b_stack=N_CHIPS, out_dtypes=(F32,))
    dx, dxb, dg = _rms_bwd(dh, x, g, dy, "dil_dnorm")
    dgains = jnp.concatenate(dgs, axis=0)
    return dx, dxb, dg, d_in, dgains, d_out


def _local_step(x, tgt, w):
    y0, s_fox = _fox_layer_fwd(x, w["mix_g"][0], w["fox_qkv"], w["fox_f"], w["fox_b"], w["fox_gains"], w["fox_out"])
    y1, s_mlp0 = _mlp_fwd(y0, w["mlp_g"][0], w["up"][0], w["down"][0], "mlp0")
    y2, s_dil = _dil_layer_fwd(y1, w["mix_g"][1], w["dil_in"], w["dil_gains"], w["dil_out"])
    y3, s_mlp1 = _mlp_fwd(y2, w["mlp_g"][1], w["up"][1], w["down"][1], "mlp1")
    dy, dyb, loss = _loss_head(y3, tgt, "loss_head")
    g = {}
    dy, dyb, g_mlp1, up1, down1 = _mlp_bwd(dy, dyb, s_mlp1, w["mlp_g"][1], w["up"][1], w["down"][1], "mlp1")
    dy, dyb, g_mix1, g["dil_in"], g["dil_gains"], g["dil_out"] = _dil_layer_bwd(
        dy, dyb, s_dil, w["mix_g"][1], w["dil_in"], w["dil_gains"], w["dil_out"])
    dy, dyb, g_mlp0, up0, down0 = _mlp_bwd(dy, dyb, s_mlp0, w["mlp_g"][0], w["up"][0], w["down"][0], "mlp0")
    dy, dyb, g_mix0, g["fox_qkv"], g["fox_f"], g["fox_b"], g["fox_gains"], g["fox_out"] = _fox_layer_bwd(
        dy, dyb, s_fox, w["mix_g"][0], w["fox_qkv"], w["fox_f"], w["fox_b"], w["fox_gains"], w["fox_out"])
    g["mix_g"], g["mlp_g"] = (g_mix0, g_mix1), (g_mlp0, g_mlp1)
    g["up"], g["down"] = (up0, up1), (down0, down1)
    return loss[0, 0], dy, g


ANY = pl.BlockSpec(memory_space=pl.ANY)


def _place():
    x, y, c = lax.axis_index("x"), lax.axis_index("y"), lax.axis_index("c")
    chips = [(1 - x, y), (x, 1 - y), (1 - x, 1 - y)]
    return x, y, c, chips


def _remote(src, dst, send_sem, recv_sem, to):
    return pltpu.make_async_remote_copy(src_ref=src, dst_ref=dst, send_sem=send_sem, recv_sem=recv_sem,
                                        device_id=to, device_id_type=MESH)


def _gather_weights(shards):
    n = len(shards)

    def body(*refs):
        src, dst = refs[:n], refs[n:2 * n]
        send_sems, recv_sems, local_sems = refs[2 * n:]
        x, y, c, chips = _place()
        mine = 2 * x + y
        local = [pltpu.make_async_copy(src[t], dst[t].at[mine], local_sems.at[t]) for t in range(n)]
        for cp in local:
            cp.start()

        def half(t, slot, which):
            hr = shards[t].shape[0] // 2
            return dst[t].at[slot, pl.ds(which * hr, hr), :]

        def my_half(t):
            hr = shards[t].shape[0] // 2
            return src[t].at[pl.ds(c * hr, hr), :]

        sends = []
        for t in range(n):
            for j, (px, py) in enumerate(chips):
                cp = _remote(my_half(t), half(t, mine, c), send_sems.at[t, j], recv_sems.at[t, j], (px, py, c))
                cp.start()
                sends.append(cp)
        for j, (px, py) in enumerate(chips):
            for t in range(n):
                landed = half(t, 2 * px + py, c)
                _remote(landed, landed, send_sems.at[t, j], recv_sems.at[t, j], (px, py, c)).wait_recv()
                cp = _remote(landed, landed, send_sems.at[t, 3 + j], recv_sems.at[t, 3 + j], (x, y, 1 - c))
                cp.start()
                sends.append(cp)
        for j, (px, py) in enumerate(chips):
            for t in range(n):
                other = half(t, 2 * px + py, 1 - c)
                _remote(other, other, send_sems.at[t, 3 + j], recv_sems.at[t, 3 + j], (x, y, 1 - c)).wait_recv()
        for cp in sends:
            cp.wait_send()
        for cp in local:
            cp.wait()

    return pl.pallas_call(
        body, name="gather_weights",
        in_specs=[ANY] * n, out_specs=[ANY] * n,
        out_shape=[jax.ShapeDtypeStruct((N_CHIPS,) + s.shape, s.dtype) for s in shards],
        scratch_shapes=[pltpu.SemaphoreType.DMA((n, 6)), pltpu.SemaphoreType.DMA((n, 6)), pltpu.SemaphoreType.DMA((n,))],
        compiler_params=_params(has_side_effects=True),
    )(*shards)


def _pair_exchange(grads):
    n = len(grads)

    def body(*refs):
        src, dst = refs[:n], refs[n:2 * n]
        send_sems, recv_sems = refs[2 * n:]
        x, y, c, _ = _place()
        cps = []
        for t in range(n):
            hr = grads[t].shape[1] // 2
            cp = _remote(src[t].at[:, pl.ds((1 - c) * hr, hr), :], dst[t], send_sems.at[t], recv_sems.at[t], (x, y, 1 - c))
            cp.start()
            cps.append(cp)
        for cp in cps:
            cp.wait()

    return pl.pallas_call(
        body, name="grad_pair_exchange",
        in_specs=[ANY] * n, out_specs=[ANY] * n,
        out_shape=[jax.ShapeDtypeStruct((g.shape[0], g.shape[1] // 2, g.shape[2]), g.dtype) for g in grads],
        scratch_shapes=[pltpu.SemaphoreType.DMA((n,)), pltpu.SemaphoreType.DMA((n,))],
        compiler_params=_params(has_side_effects=True),
    )(*grads)


def _pair_add(g, got, cidx, name, tb=256):
    S, R, C = g.shape
    hr = R // 2
    tb = _rows_tile(hr, tb)
    nb = hr // tb

    def body(c_ref, a_ref, b_ref, o_ref):
        o_ref[...] = a_ref[...] + b_ref[...]

    return pl.pallas_call(
        body, name=name,
        grid_spec=pltpu.PrefetchScalarGridSpec(
            num_scalar_prefetch=1, grid=(S, nb),
            in_specs=[pl.BlockSpec((None, tb, C), lambda s, i, c: (s, c[0] * nb + i, 0)),
                      pl.BlockSpec((None, tb, C), lambda s, i, c: (s, i, 0))],
            out_specs=pl.BlockSpec((None, tb, C), lambda s, i, c: (s, i, 0))),
        out_shape=jax.ShapeDtypeStruct((S, hr, C), F32),
        compiler_params=_params("parallel", "parallel"),
    )(cidx, g, got)


def _rows_tile(n, want):
    t = min(n, want)
    while n % t or t % 8:
        t -= 8
    return t


def _chip_scatter(sums):
    n = len(sums)

    def body(*refs):
        src, dst = refs[:n], refs[n:2 * n]
        send_sems, recv_sems, local_sems = refs[2 * n:]
        x, y, c, chips = _place()
        mine = 2 * x + y
        local = [pltpu.make_async_copy(src[t].at[mine], dst[t].at[mine], local_sems.at[t]) for t in range(n)]
        for cp in local:
            cp.start()
        sends = []
        for t in range(n):
            for j, (px, py) in enumerate(chips):
                cp = _remote(src[t].at[2 * px + py], dst[t].at[mine], send_sems.at[t, j], recv_sems.at[t, j], (px, py, c))
                cp.start()
                sends.append(cp)
        for t in range(n):
            for j, (px, py) in enumerate(chips):
                slot = dst[t].at[2 * px + py]
                _remote(slot, slot, send_sems.at[t, j], recv_sems.at[t, j], (px, py, c)).wait_recv()
        for cp in sends:
            cp.wait_send()
        for cp in local:
            cp.wait()

    return pl.pallas_call(
        body, name="grad_chip_scatter",
        in_specs=[ANY] * n, out_specs=[ANY] * n,
        out_shape=[jax.ShapeDtypeStruct(s.shape, s.dtype) for s in sums],
        scratch_shapes=[pltpu.SemaphoreType.DMA((n, 3)), pltpu.SemaphoreType.DMA((n, 3)), pltpu.SemaphoreType.DMA((n,))],
        compiler_params=_params(has_side_effects=True),
    )(*sums)


def _chip_sum(parts, cidx, name, tb=256):
    S, hr, C = parts.shape
    tb = _rows_tile(hr, tb)
    nb = hr // tb

    def body(c_ref, *refs):
        o_ref = refs[S]
        tot = refs[0][...]
        for s in range(1, S):
            tot = tot + refs[s][...]
        o_ref[...] = tot

    return pl.pallas_call(
        body, name=name,
        grid_spec=pltpu.PrefetchScalarGridSpec(
            num_scalar_prefetch=1, grid=(nb,),
            in_specs=[pl.BlockSpec((None, tb, C), functools.partial(lambda s, i, c: (s, i, 0), s)) for s in range(S)],
            out_specs=pl.BlockSpec((tb, C), lambda i, c: (c[0] * nb + i, 0))),
        out_shape=jax.ShapeDtypeStruct((2 * hr, C), F32),
        compiler_params=_params("parallel"),
    )(cidx, *([parts] * S))


def _half_exchange(halves):
    n = len(halves)

    def body(*refs):
        dst = refs[n:2 * n]
        send_sems, recv_sems = refs[2 * n:]
        x, y, c, _ = _place()
        cps = []
        for t in range(n):
            hr = halves[t].shape[0] // 2
            rows = dst[t].at[pl.ds(c * hr, hr), :]
            cp = _remote(rows, rows, send_sems.at[t], recv_sems.at[t], (x, y, 1 - c))
            cp.start()
            cps.append(cp)
        for t, cp in enumerate(cps):
            cp.wait_send()
            hr = halves[t].shape[0] // 2
            other = dst[t].at[pl.ds((1 - c) * hr, hr), :]
            _remote(other, other, send_sems.at[t], recv_sems.at[t], (x, y, 1 - c)).wait_recv()

    return pl.pallas_call(
        body, name="grad_half_exchange",
        in_specs=[ANY] * n, out_specs=[ANY] * n,
        out_shape=[jax.ShapeDtypeStruct(h.shape, h.dtype) for h in halves],
        input_output_aliases={t: t for t in range(n)},
        scratch_shapes=[pltpu.SemaphoreType.DMA((n,)), pltpu.SemaphoreType.DMA((n,))],
        compiler_params=_params(has_side_effects=True),
    )(*halves)


def _adamw_math(w, g, m, v):
    m = ADAM_B1 * m + (1.0 - ADAM_B1) * g
    v = ADAM_B2 * v + (1.0 - ADAM_B2) * (g * g)
    m_hat = m / (1.0 - ADAM_B1 ** ADAM_STEP)
    v_hat = v / (1.0 - ADAM_B2 ** ADAM_STEP)
    delta = -ADAM_LR * (m_hat / (jnp.sqrt(v_hat) + ADAM_EPS) + ADAM_WD * w)
    return delta, m, v


def _adamw(w, g, m, v, name, tb=256):
    R, C = w.shape
    tb = _rows_tile(R, tb)

    def body(w_ref, g_ref, m_ref, v_ref, d_ref, mo_ref, vo_ref):
        d, mn, vn = _adamw_math(w_ref[...], g_ref[...], m_ref[...], v_ref[...])
        d_ref[...] = d
        mo_ref[...] = mn
        vo_ref[...] = vn

    row = pl.BlockSpec((tb, C), lambda i: (i, 0))
    return pl.pallas_call(
        body, name=name, grid=(R // tb,),
        in_specs=[row] * 4, out_specs=[row] * 3,
        out_shape=[jax.ShapeDtypeStruct((R, C), F32)] * 3,
        compiler_params=_params("parallel"),
    )(w, g, m, v)


N_DEV = 8


def _small_update(g, w, m, v):
    P = g.shape[0]

    def body(g_ref, w_ref, m_ref, v_ref, go_ref, d_ref, mo_ref, vo_ref, buf, send_sems, recv_sems):
        x, y, c, _ = _place()
        me = 4 * x + 2 * y + c
        buf[me] = g_ref[...]
        cps = []
        for k in range(1, N_DEV):
            fx, fy, fc = (k >> 2) & 1, (k >> 1) & 1, k & 1
            px = (1 - x) if fx else x
            py = (1 - y) if fy else y
            pc = (1 - c) if fc else c
            cp = _remote(g_ref, buf.at[me], send_sems.at[k - 1], recv_sems.at[k - 1], (px, py, pc))
            cp.start()
            cps.append((cp, 4 * px + 2 * py + pc))
        for k, (cp, peer) in enumerate(cps):
            _remote(g_ref, buf.at[peer], send_sems.at[k], recv_sems.at[k], (x, y, c)).wait_recv()
        for cp, _ in cps:
            cp.wait_send()
        tot = buf[0]
        for d in range(1, N_DEV):
            tot = tot + buf[d]
        go_ref[...] = tot
        dl, mn, vn = _adamw_math(w_ref[...], tot, m_ref[...], v_ref[...])
        d_ref[...] = dl
        mo_ref[...] = mn
        vo_ref[...] = vn

    vm = pl.BlockSpec(memory_space=pltpu.VMEM)
    return pl.pallas_call(
        body, name="small_params_update",
        in_specs=[vm] * 4, out_specs=[vm] * 4,
        out_shape=[jax.ShapeDtypeStruct((P, HEAD_DIM), F32)] * 4,
        scratch_shapes=[pltpu.VMEM((N_DEV, P, HEAD_DIM), F32), pltpu.SemaphoreType.DMA((N_DEV - 1,)),
                        pltpu.SemaphoreType.DMA((N_DEV - 1,))],
        compiler_params=_params(has_side_effects=True),
    )(g, w, m, v)


SMALL = ("fox_b_f", "fox_q_gain", "fox_k_gain", "dil_q_gain", "dil_k_gain", "mix_norm_g", "mlp_norm_g")
LARGE = ("fox_w_in", "fox_w_out", "dil_w_in", "dil_w_out", "mlp_w_up", "mlp_w_down")
WEIGHTS = ("fox_w_in", "fox_b_f", "fox_q_gain", "fox_k_gain", "fox_w_out", "dil_w_in", "dil_q_gain", "dil_k_gain",
           "dil_w_out", "mix_norm_g", "mlp_norm_g", "mlp_w_up", "mlp_w_down")


def _pack(parts):
    rows = []
    for a in parts:
        flat = a.reshape(-1)
        n = -(-flat.shape[0] // (8 * HEAD_DIM)) * (8 * HEAD_DIM)
        rows.append(jnp.pad(flat, (0, n - flat.shape[0])).reshape(-1, HEAD_DIM))
    return jnp.concatenate(rows, axis=0)


def _unpack(packed, like):
    out, r = [], 0
    for a in like:
        size = int(np.prod(a.shape))
        n = -(-size // (8 * HEAD_DIM)) * 8
        out.append(packed[r:r + n].reshape(-1)[:size].reshape(a.shape))
        r += n
    return out


def _pad_lanes(a):
    return jnp.pad(a, [(0, 0)] * (a.ndim - 1) + [(0, HEAD_DIM - a.shape[-1])])


def _as_shards(a):
    return a.reshape(N_CHIPS, a.shape[0] // N_CHIPS, a.shape[1])


def kernel(x, fox_w_in, fox_b_f, fox_q_gain, fox_k_gain, fox_w_out, dil_w_in, dil_q_gain, dil_k_gain, dil_w_out, mix_norm_g, mlp_norm_g, mlp_w_up, mlp_w_down, loss_target, m_fox_w_in, m_fox_b_f, m_fox_q_gain, m_fox_k_gain, m_fox_w_out, m_dil_w_in, m_dil_q_gain, m_dil_k_gain, m_dil_w_out, m_mix_norm_g, m_mlp_norm_g, m_mlp_w_up, m_mlp_w_down, v_fox_w_in, v_fox_b_f, v_fox_q_gain, v_fox_k_gain, v_fox_w_out, v_dil_w_in, v_dil_q_gain, v_dil_k_gain, v_dil_w_out, v_mix_norm_g, v_mlp_norm_g, v_mlp_w_up, v_mlp_w_down):
    wts = dict(fox_w_in=fox_w_in, fox_b_f=fox_b_f, fox_q_gain=fox_q_gain, fox_k_gain=fox_k_gain, fox_w_out=fox_w_out,
               dil_w_in=dil_w_in, dil_q_gain=dil_q_gain, dil_k_gain=dil_k_gain, dil_w_out=dil_w_out,
               mix_norm_g=mix_norm_g, mlp_norm_g=mlp_norm_g, mlp_w_up=mlp_w_up, mlp_w_down=mlp_w_down)
    mom1 = dict(fox_w_in=m_fox_w_in, fox_b_f=m_fox_b_f, fox_q_gain=m_fox_q_gain, fox_k_gain=m_fox_k_gain,
                fox_w_out=m_fox_w_out, dil_w_in=m_dil_w_in, dil_q_gain=m_dil_q_gain, dil_k_gain=m_dil_k_gain,
                dil_w_out=m_dil_w_out, mix_norm_g=m_mix_norm_g, mlp_norm_g=m_mlp_norm_g, mlp_w_up=m_mlp_w_up,
                mlp_w_down=m_mlp_w_down)
    mom2 = dict(fox_w_in=v_fox_w_in, fox_b_f=v_fox_b_f, fox_q_gain=v_fox_q_gain, fox_k_gain=v_fox_k_gain,
                fox_w_out=v_fox_w_out, dil_w_in=v_dil_w_in, dil_q_gain=v_dil_q_gain, dil_k_gain=v_dil_k_gain,
                dil_w_out=v_dil_w_out, mix_norm_g=v_mix_norm_g, mlp_norm_g=v_mlp_norm_g, mlp_w_up=v_mlp_w_up,
                mlp_w_down=v_mlp_w_down)
    T, D = x.shape[1], x.shape[2]
    H = D // HEAD_DIM
    cidx = lax.axis_index("c").astype(jnp.int32).reshape(1)

    def shards_of(d):
        return [d["fox_w_in"][0], d["fox_w_out"][0], d["dil_w_in"][0], d["dil_w_out"][0],
                d["mlp_w_up"][0], d["mlp_w_up"][1], d["mlp_w_down"][0], d["mlp_w_down"][1]]

    w_sh, m_sh, v_sh = shards_of(wts), shards_of(mom1), shards_of(mom2)
    full = _gather_weights([s.astype(BF16) for s in w_sh])
    fox_in = jnp.moveaxis(full[0], 0, 1).reshape(D, -1)
    w = dict(
        fox_qkv=fox_in[:, :3 * D], fox_f=_pad_lanes(fox_in[:, 3 * D:]), fox_b=_pad_lanes(fox_b_f),
        fox_gains=jnp.stack([fox_q_gain, fox_k_gain]), fox_out=full[1].reshape(D, D),
        dil_in=full[2], dil_gains=jnp.concatenate([dil_q_gain[0], dil_k_gain[0]])[:, None, :], dil_out=full[3].reshape(D, D),
        up=[full[4], full[5]], down=[full[6].reshape(-1, D), full[7].reshape(-1, D)],
        mix_g=[mix_norm_g[0:1], mix_norm_g[1:2]], mlp_g=[mlp_norm_g[0:1], mlp_norm_g[1:2]])

    loss, grad_x, g = _local_step(x.reshape(T, D), loss_target.reshape(T, D), w)
    loss = lax.psum(loss, ("x", "y", "c"))

    g_fox_in = jnp.concatenate([g["fox_qkv"], g["fox_f"][:, :H]], axis=1)
    g_fox_in = jnp.moveaxis(g_fox_in.reshape(D, N_CHIPS, -1), 1, 0)
    stacked = [g_fox_in, _as_shards(g["fox_out"]), g["dil_in"], _as_shards(g["dil_out"]),
               g["up"][0], g["up"][1], _as_shards(g["down"][0]), _as_shards(g["down"][1])]
    got = _pair_exchange(stacked)
    sums = [_pair_add(a, b, cidx, f"grad_pair_add_{t}") for t, (a, b) in enumerate(zip(stacked, got))]
    parts = _chip_scatter(sums)
    halves = [_chip_sum(p, cidx, f"grad_chip_sum_{t}") for t, p in enumerate(parts)]
    totals = _half_exchange(halves)
    upd = [_adamw(w_sh[t], totals[t], m_sh[t], v_sh[t], f"adamw_{t}") for t in range(len(totals))]

    def large(k):
        a = [totals[t] if k == 0 else upd[t][k - 1] for t in range(len(totals))]
        return dict(fox_w_in=a[0][None], fox_w_out=a[1][None], dil_w_in=a[2][None], dil_w_out=a[3][None],
                    mlp_w_up=jnp.stack([a[4], a[5]]), mlp_w_down=jnp.stack([a[6], a[7]]))

    small_like = [wts[n] for n in SMALL]
    g_small = [g["fox_b"][:, :H], g["fox_gains"][0], g["fox_gains"][1], g["dil_gains"][:3, 0][None], g["dil_gains"][3:, 0][None],
               jnp.concatenate(g["mix_g"]), jnp.concatenate(g["mlp_g"])]
    packed = _small_update(_pack(g_small), _pack(small_like), _pack([mom1[n] for n in SMALL]), _pack([mom2[n] for n in SMALL]))
    small = [dict(zip(SMALL, _unpack(p, small_like))) for p in packed]

    outs = [loss, grad_x.reshape(x.shape)]
    for k in range(4):
        big = large(k)
        outs += [big[n] if n in big else small[k][n] for n in WEIGHTS]
    return tuple(outs)
```

```python
import functools

import numpy as np
import jax
import jax.numpy as jnp
from jax import lax
from jax.experimental import pallas as pl
from jax.experimental.pallas import tpu as pltpu

F32 = jnp.float32
BF16 = jnp.bfloat16

HEAD_DIM = 128
DIL_PATTERNS = ((128, 1), (512, 4), (2048, 16))
DIL_SPAN = 128
ALIBI_MAX_EXP = 8.0
EPS = 1e-6
MASKED = -1e30

ADAM_LR = 0.001
ADAM_B1 = 0.9
ADAM_B2 = 0.999
ADAM_EPS = 1e-08
ADAM_WD = 0.01
ADAM_STEP = 10

N_CHIPS = 4
VMEM_LIMIT_BYTES = 56 * 1024 * 1024
MESH = pl.DeviceIdType.MESH

NN = (((1,), (0,)), ((), ()))
NT = (((1,), (1,)), ((), ()))
TN = (((0,), (0,)), ((), ()))


def _params(*sem, **kw):
    return pltpu.CompilerParams(dimension_semantics=sem or None, vmem_limit_bytes=VMEM_LIMIT_BYTES, **kw)


def _dot(a, b, dims):
    return lax.dot_general(a, b, dims, preferred_element_type=F32)


def _tile(n, want):
    if n <= want:
        return n
    t = want - want % 128
    while n % t:
        t -= 128
    return t


def _mm(a, b, M, N, K, *, mode, name, out_dtypes, b_stack=0, out_stack=0, extras=(), epilogue=None,
        tm=1024, tn=1024, tk=2048):
    per_b = per_o = None
    if b_stack:
        per_b = (K if mode == "nt" else N) // b_stack
    if out_stack:
        per_o = N // out_stack
    tm = _tile(M, tm)
    tn = _tile(min(x for x in (N, per_o, per_b if mode != "nt" else None) if x), tn)
    tk = _tile(min(x for x in (K, per_b if mode == "nt" else None) if x), tk)
    assert M % tm == 0 and N % tn == 0 and K % tk == 0, (name, M, N, K, tm, tn, tk)
    gk = K // tk
    if mode == "tn":
        a_spec = pl.BlockSpec((tk, tm), lambda i, j, k: (k, i))
    else:
        a_spec = pl.BlockSpec((tm, tk), lambda i, j, k: (i, k))
    if mode == "nt":
        if b_stack:
            npk = per_b // tk
            b_spec = pl.BlockSpec((None, tn, tk), lambda i, j, k: (k // npk, j, k % npk))
        else:
            b_spec = pl.BlockSpec((tn, tk), lambda i, j, k: (j, k))
    else:
        if b_stack:
            npj = per_b // tn
            b_spec = pl.BlockSpec((None, tk, tn), lambda i, j, k: (j // npj, k, j % npj))
        else:
            b_spec = pl.BlockSpec((tk, tn), lambda i, j, k: (k, j))
    if out_stack:
        npo = per_o // tn
        o_spec = pl.BlockSpec((None, tm, tn), lambda i, j, k: (j // npo, i, j % npo))
        o_shape = (out_stack, M, per_o)
    else:
        o_spec = pl.BlockSpec((tm, tn), lambda i, j, k: (i, j))
        o_shape = (M, N)
    e_spec = pl.BlockSpec((tm, tn), lambda i, j, k: (i, j))
    dims = {"nn": NN, "nt": NT, "tn": TN}[mode]
    ne, no = len(extras), len(out_dtypes)

    def body(a_ref, b_ref, *rest):
        ex, outs = rest[:ne], rest[ne:ne + no]
        k = pl.program_id(2)

        def product():
            return _dot(a_ref[...].astype(BF16), b_ref[...].astype(BF16), dims)

        def finish(r):
            res = epilogue(r, *[e[...] for e in ex]) if epilogue is not None else (r,)
            for o, v in zip(outs, res):
                o[...] = v.astype(o.dtype)

        if gk == 1:
            finish(product())
            return
        acc = rest[ne + no]

        @pl.when(k == 0)
        def _():
            acc[...] = product()

        @pl.when((k > 0) & (k < gk - 1))
        def _():
            acc[...] += product()

        @pl.when(k == gk - 1)
        def _():
            finish(acc[...] + product())

    outs = pl.pallas_call(
        body, name=name,
        grid=(M // tm, N // tn, gk),
        in_specs=[a_spec, b_spec] + [e_spec] * ne,
        out_specs=[o_spec] * no,
        out_shape=[jax.ShapeDtypeStruct(o_shape, d) for d in out_dtypes],
        scratch_shapes=[pltpu.VMEM((tm, tn), F32)] if gk > 1 else [],
        compiler_params=_params("parallel", "parallel", "arbitrary"),
    )(a, b, *extras)
    return outs[0] if no == 1 else outs


def _rms_fwd(x, g, name, tb=512):
    T, D = x.shape
    tb = min(tb, T)

    def body(x_ref, g_ref, o_ref):
        xv = x_ref[...]
        r = lax.rsqrt(jnp.mean(xv * xv, axis=-1, keepdims=True) + EPS)
        o_ref[...] = (xv * r * g_ref[...]).astype(BF16)

    return pl.pallas_call(
        body, name=name, grid=(T // tb,),
        in_specs=[pl.BlockSpec((tb, D), lambda i: (i, 0)), pl.BlockSpec((1, D), lambda i: (0, 0))],
        out_specs=pl.BlockSpec((tb, D), lambda i: (i, 0)),
        out_shape=jax.ShapeDtypeStruct((T, D), BF16),
        compiler_params=_params("parallel"),
    )(x, g)


def _rms_bwd(dy, x, g, dres, name, tb=256):
    T, D = x.shape
    tb = min(tb, T)

    def body(dy_ref, x_ref, g_ref, dres_ref, dx_ref, dxb_ref, dg_ref):
        i = pl.program_id(0)
        xv, dyv = x_ref[...], dy_ref[...]
        r = lax.rsqrt(jnp.mean(xv * xv, axis=-1, keepdims=True) + EPS)
        gy = dyv * g_ref[...]
        dx = r * gy - xv * (r * r * r) * jnp.mean(gy * xv, axis=-1, keepdims=True)
        tot = dres_ref[...] + dx
        dx_ref[...] = tot
        dxb_ref[...] = tot.astype(BF16)
        part = jnp.sum(dyv * (xv * r), axis=0, keepdims=True)

        @pl.when(i == 0)
        def _():
            dg_ref[...] = part

        @pl.when(i > 0)
        def _():
            dg_ref[...] += part

    row = pl.BlockSpec((tb, D), lambda i: (i, 0))
    vec = pl.BlockSpec((1, D), lambda i: (0, 0))
    return pl.pallas_call(
        body, name=name, grid=(T // tb,),
        in_specs=[row, row, vec, row],
        out_specs=[row, row, vec],
        out_shape=[jax.ShapeDtypeStruct((T, D), F32), jax.ShapeDtypeStruct((T, D), BF16),
                   jax.ShapeDtypeStruct((1, D), F32)],
        compiler_params=_params("arbitrary"),
    )(dy, x, g, dres)


def _loss_head(y, tgt, name, tb=256):
    T, D = y.shape
    tb = min(tb, T)

    def body(y_ref, t_ref, dy_ref, dyb_ref, loss_ref):
        i = pl.program_id(0)
        e = y_ref[...] - t_ref[...]
        d = e * (1.0 / D)
        dy_ref[...] = d
        dyb_ref[...] = d.astype(BF16)
        part = 0.5 * jnp.sum(jnp.sum(e * e, axis=1, keepdims=True) * (1.0 / D), axis=0, keepdims=True)

        @pl.when(i == 0)
        def _():
            loss_ref[...] = part

        @pl.when(i > 0)
        def _():
            loss_ref[...] += part

    row = pl.BlockSpec((tb, D), lambda i: (i, 0))
    return pl.pallas_call(
        body, name=name, grid=(T // tb,),
        in_specs=[row, row],
        out_specs=[row, row, pl.BlockSpec((1, 1), lambda i: (0, 0))],
        out_shape=[jax.ShapeDtypeStruct((T, D), F32), jax.ShapeDtypeStruct((T, D), BF16),
                   jax.ShapeDtypeStruct((1, 1), F32)],
        compiler_params=_params("arbitrary"),
    )(y, tgt)


def _head_rms(xh, g):
    r = lax.rsqrt(jnp.mean(xh * xh, axis=-1, keepdims=True) + EPS)
    return xh * r * g


def _qkv_prep(proj, gains, n_norm, gain_row, ch, name, n_scaled=0, post_scale=1.0, tb=512):
    T, W = proj.shape
    tb = min(tb, T)
    nch = W // ch
    nh = ch // HEAD_DIM

    def body(p_ref, g_ref, o_ref):
        j = pl.program_id(0)

        @pl.when(j < n_norm)
        def _():
            g = g_ref[...]
            if n_scaled:
                g = g * jnp.where(j < n_scaled, post_scale, 1.0)
            for h in range(nh):
                sl = slice(h * HEAD_DIM, (h + 1) * HEAD_DIM)
                o_ref[:, sl] = _head_rms(p_ref[:, sl], g).astype(BF16)

        @pl.when(j >= n_norm)
        def _():
            o_ref[...] = p_ref[...].astype(BF16)

    return pl.pallas_call(
        body, name=name, grid=(nch, T // tb),
        in_specs=[pl.BlockSpec((tb, ch), lambda j, i: (i, j)),
                  pl.BlockSpec((None, 1, HEAD_DIM), lambda j, i: (gain_row(j), 0, 0))],
        out_specs=pl.BlockSpec((tb, ch), lambda j, i: (i, j)),
        out_shape=jax.ShapeDtypeStruct((T, W), BF16),
        compiler_params=_params("parallel", "parallel"),
    )(proj, gains)


def _into(body, name, grid, in_specs, out_spec, out_shape, extra_out_specs, extra_out_shapes, buf, operands, sem):
    if buf is None:
        def kernel(*refs):
            body(*refs)
        ins, alias, ops = in_specs, {}, operands
    else:
        def kernel(_, *refs):
            body(*refs)
        ins = [pl.BlockSpec(memory_space=pl.ANY)] + in_specs
        alias, ops = {0: 0}, (buf,) + tuple(operands)
    return pl.pallas_call(
        kernel, name=name, grid=grid, in_specs=ins,
        out_specs=[out_spec] + extra_out_specs,
        out_shape=[out_shape] + extra_out_shapes,
        input_output_aliases=alias,
        compiler_params=_params(*sem),
    )(*ops)


def _head_rms_bwd_into(buf, W, d, proj, gain, off, ch, name, in_scale=1.0, tb=256):
    T, wd = d.shape
    tb = min(tb, T)
    n = wd // ch
    nh = ch // HEAD_DIM

    def body(d_ref, p_ref, g_ref, o_ref, dg_ref):
        i = pl.program_id(1)
        g = g_ref[...]
        part = jnp.zeros((1, HEAD_DIM), F32)
        for h in range(nh):
            sl = slice(h * HEAD_DIM, (h + 1) * HEAD_DIM)
            xh, dy = p_ref[:, sl], d_ref[:, sl]
            if in_scale != 1.0:
                dy = dy * in_scale
            r = lax.rsqrt(jnp.mean(xh * xh, axis=-1, keepdims=True) + EPS)
            gy = dy * g
            dx = r * gy - xh * (r * r * r) * jnp.mean(gy * xh, axis=-1, keepdims=True)
            o_ref[:, sl] = dx.astype(BF16)
            part = part + jnp.sum(dy * (xh * r), axis=0, keepdims=True)

        @pl.when(i == 0)
        def _():
            dg_ref[...] = part

        @pl.when(i > 0)
        def _():
            dg_ref[...] += part

    return _into(
        body, name, (n, T // tb),
        [pl.BlockSpec((tb, ch), lambda j, i: (i, j)), pl.BlockSpec((tb, ch), lambda j, i: (i, off + j)),
         pl.BlockSpec((1, HEAD_DIM), lambda j, i: (0, 0))],
        pl.BlockSpec((tb, ch), lambda j, i: (i, off + j)), jax.ShapeDtypeStruct((T, W), BF16),
        [pl.BlockSpec((None, 1, HEAD_DIM), lambda j, i: (j, 0, 0))], [jax.ShapeDtypeStruct((n, 1, HEAD_DIM), F32)],
        buf, (d, proj, gain), ("parallel", "arbitrary"))


def _sum_cast_into(buf, W, srcs, off, ch, name, tb=256):
    T, wd = srcs[0].shape
    tb = min(tb, T)
    n = wd // ch
    ns = len(srcs)

    def body(*refs):
        o_ref = refs[ns]
        tot = refs[0][...]
        for s in refs[1:ns]:
            tot = tot + s[...]
        o_ref[...] = tot.astype(BF16)

    out = _into(
        body, name, (n, T // tb),
        [pl.BlockSpec((tb, ch), lambda j, i: (i, j))] * ns,
        pl.BlockSpec((tb, ch), lambda j, i: (i, off + j)), jax.ShapeDtypeStruct((T, W), BF16),
        [], [], buf, tuple(srcs), ("parallel", "parallel"))
    return out[0]


def _tri(n, lower):
    r = lax.broadcasted_iota(jnp.int32, (n, n), 0)
    c = lax.broadcasted_iota(jnp.int32, (n, n), 1)
    return jnp.where((c <= r) if lower else (c >= r), 1.0, 0.0).astype(F32)


def _dot_exact(a, b):
    return lax.dot_general(a, b, NN, precision=lax.Precision.HIGHEST, preferred_element_type=F32)


def _log_sigmoid(z):
    return jnp.minimum(z, 0.0) - jnp.log(1.0 + jnp.exp(-jnp.abs(z)))


def _gate_fwd(f_raw, b_pad, hp, out_scale, name, blk=256):
    T = f_raw.shape[0]
    blk = min(blk, T)

    def body(f_ref, b_ref, c_ref):
        tri = _tri(blk, True)
        carry = jnp.zeros((1, HEAD_DIM), F32)
        for j in range(T // blk):
            lf = _log_sigmoid(f_ref[j * blk:(j + 1) * blk, :] + b_ref[...])
            cb = _dot_exact(tri, lf) + carry
            carry = cb[blk - 1:blk, :]
            c_ref[:, j * blk:(j + 1) * blk] = cb.T[:hp, :] * out_scale

    return pl.pallas_call(
        body, name=name,
        in_specs=[pl.BlockSpec(memory_space=pltpu.VMEM)] * 2,
        out_specs=pl.BlockSpec(memory_space=pltpu.VMEM),
        out_shape=jax.ShapeDtypeStruct((hp, T), F32),
        compiler_params=_params(),
    )(f_raw, b_pad)


def _gate_bwd(dc_rows, dc_cols, f_raw, b_pad, n_heads, hp, name, blk=256):
    T = f_raw.shape[0]
    blk = min(blk, T)

    def body(dc_ref, dcc_ref, f_ref, b_ref, dz_ref, db_ref):
        tri = _tri(blk, False)
        lane = lax.broadcasted_iota(jnp.int32, (blk, HEAD_DIM), 1)
        carry = jnp.zeros((1, HEAD_DIM), F32)
        db = jnp.zeros((1, HEAD_DIM), F32)
        for j in reversed(range(T // blk)):
            rows = dc_ref[:, j * blk:(j + 1) * blk]
            if hp < HEAD_DIM:
                rows = jnp.concatenate([rows, jnp.zeros((HEAD_DIM - hp, blk), F32)], axis=0)
            dlf = _dot_exact(tri, rows.T + dcc_ref[j * blk:(j + 1) * blk, :]) + carry
            carry = dlf[0:1, :]
            z = f_ref[j * blk:(j + 1) * blk, :] + b_ref[...]
            dz = jnp.where(lane < n_heads, dlf / (1.0 + jnp.exp(z)), 0.0)
            dz_ref[j * blk:(j + 1) * blk, :] = dz.astype(BF16)
            db = db + jnp.sum(dz, axis=0, keepdims=True)
        db_ref[...] = db

    return pl.pallas_call(
        body, name=name,
        in_specs=[pl.BlockSpec(memory_space=pltpu.VMEM)] * 4,
        out_specs=[pl.BlockSpec(memory_space=pltpu.VMEM)] * 2,
        out_shape=[jax.ShapeDtypeStruct((T, HEAD_DIM), BF16), jax.ShapeDtypeStruct((1, HEAD_DIM), F32)],
        compiler_params=_params(),
    )(dc_rows, dc_cols, f_raw, b_pad)


def _pairs(nb, key_major):
    if key_major:
        pairs = [(qi, ki) for ki in range(nb) for qi in range(ki, nb)]
    else:
        pairs = [(qi, ki) for qi in range(nb) for ki in range(qi + 1)]
    return (jnp.asarray(np.array([p[0] for p in pairs], np.int32)),
            jnp.asarray(np.array([p[1] for p in pairs], np.int32)))


LOG2E = 1.4426950408889634
LN2 = 0.6931471805599453
FOX_Q_SCALE = HEAD_DIM ** -0.5 * LOG2E


def _fox_logits(q, k, ck_row, diagonal):
    s = _dot(q, k, NT) - ck_row
    if diagonal:
        row = lax.broadcasted_iota(jnp.int32, s.shape, 0)
        col = lax.broadcasted_iota(jnp.int32, s.shape, 1)
        s = jnp.where(col <= row, s, MASKED)
    return s


def _fox_fwd(qkv, ck, H, name, tb=512):
    T = qkv.shape[0]
    tb = min(tb, T)
    nb = T // tb
    qt, kt = _pairs(nb, False)

    def body(qt_ref, kt_ref, q_ref, k_ref, v_ref, ck_ref, o_ref, lse_ref, m_sc, l_sc, acc_sc):
        p_ = pl.program_id(1)
        qi, ki = qt_ref[p_], kt_ref[p_]

        @pl.when(ki == 0)
        def _():
            m_sc[...] = jnp.full_like(m_sc, MASKED)
            l_sc[...] = jnp.zeros_like(l_sc)
            acc_sc[...] = jnp.zeros_like(acc_sc)

        def tile(diagonal):
            s = _fox_logits(q_ref[...], k_ref[...], ck_ref[...], diagonal)
            m_prev = m_sc[...]
            m_new = jnp.maximum(m_prev, jnp.max(s, axis=1, keepdims=True))
            alpha = jnp.exp2(m_prev - m_new)
            p = jnp.exp2(s - m_new[:, :1])
            l_sc[...] = alpha * l_sc[...] + jnp.sum(p, axis=1, keepdims=True)
            acc_sc[...] = alpha * acc_sc[...] + _dot(p.astype(BF16), v_ref[...], NN)
            m_sc[...] = m_new

        @pl.when(ki < qi)
        def _():
            tile(False)

        @pl.when(ki == qi)
        def _():
            tile(True)
            o_ref[...] = (acc_sc[...] / l_sc[...]).astype(BF16)
            lse_ref[...] = m_sc[...] + jnp.log(l_sc[...]) * LOG2E

    blk = lambda f: pl.BlockSpec((tb, HEAD_DIM), f)
    return pl.pallas_call(
        body, name=name,
        grid_spec=pltpu.PrefetchScalarGridSpec(
            num_scalar_prefetch=2, grid=(H, qt.shape[0]),
            in_specs=[blk(lambda h, p, qt, kt: (qt[p], h)),
                      blk(lambda h, p, qt, kt: (kt[p], H + h)),
                      blk(lambda h, p, qt, kt: (kt[p], 2 * H + h)),
                      pl.BlockSpec((None, 1, tb), lambda h, p, qt, kt: (h, 0, kt[p]))],
            out_specs=[blk(lambda h, p, qt, kt: (qt[p], h)), blk(lambda h, p, qt, kt: (qt[p], h))],
            scratch_shapes=[pltpu.VMEM((tb, HEAD_DIM), F32)] * 3),
        out_shape=[jax.ShapeDtypeStruct((T, H * HEAD_DIM), BF16), jax.ShapeDtypeStruct((T, H * HEAD_DIM), F32)],
        compiler_params=_params("parallel", "arbitrary"),
    )(qt, kt, qkv, qkv, qkv, ck)


def _row_dot(do, o, nh, width, name, lane_per_head, tb=256):
    T = do.shape[0]
    tb = min(tb, T)
    wout = HEAD_DIM if lane_per_head else nh * HEAD_DIM

    def body(do_ref, o_ref, d_ref):
        lane = lax.broadcasted_iota(jnp.int32, (tb, HEAD_DIM), 1)
        tile = jnp.zeros((tb, HEAD_DIM), F32)
        for h in range(nh):
            sl = slice(h * width, (h + 1) * width)
            d = jnp.sum(do_ref[:, sl].astype(F32) * o_ref[:, sl].astype(F32), axis=1, keepdims=True)
            if lane_per_head:
                tile = jnp.where(lane == h, d, tile)
            else:
                d_ref[:, h * HEAD_DIM:(h + 1) * HEAD_DIM] = jnp.broadcast_to(d, (tb, HEAD_DIM))
        if lane_per_head:
            d_ref[...] = tile

    row = pl.BlockSpec((tb, nh * width), lambda i: (i, 0))
    return pl.pallas_call(
        body, name=name, grid=(T // tb,),
        in_specs=[row, row], out_specs=pl.BlockSpec((tb, wout), lambda i: (i, 0)),
        out_shape=jax.ShapeDtypeStruct((T, wout), F32),
        compiler_params=_params("parallel"),
    )(do, o)


def _fox_bwd(qkv, do, ck, lse, dd, H, hp, name, tb=512):
    T = qkv.shape[0]
    tb = min(tb, T)
    nb = T // tb
    qt, kt = _pairs(nb, True)

    def body(qt_ref, kt_ref, q_ref, k_ref, v_ref, do_ref, ck_ref, lse_ref, dd_ref, dq_ref, dk_ref, dv_ref, dc_ref, dcq_ref):
        p_ = pl.program_id(1)
        qi, ki = qt_ref[p_], kt_ref[p_]

        @pl.when(p_ == 0)
        def _():
            dq_ref[...] = jnp.zeros_like(dq_ref)
            dcq_ref[...] = jnp.zeros_like(dcq_ref)

        @pl.when(qi == ki)
        def _():
            dk_ref[...] = jnp.zeros_like(dk_ref)
            dv_ref[...] = jnp.zeros_like(dv_ref)
            dc_ref[...] = jnp.zeros_like(dc_ref)

        rows = pl.ds(pl.multiple_of(qi * tb, tb), tb)

        def tile(diagonal):
            s = _fox_logits(q_ref[...], k_ref[...], ck_ref[...], diagonal)
            p = jnp.exp2(s - lse_ref[:, :1])
            dv_ref[...] += _dot(p.astype(BF16), do_ref[...], TN)
            dp = _dot(do_ref[...], v_ref[...], NT)
            ds = p * (dp - dd_ref[:, :1])
            dc_ref[...] -= jnp.sum(ds, axis=0, keepdims=True)
            dcq_ref[rows, :] += jnp.sum(ds, axis=1, keepdims=True)
            dsb = ds.astype(BF16)
            dq_ref[rows, :] += _dot(dsb, k_ref[...], NN)
            dk_ref[...] += _dot(dsb, q_ref[...], TN)

        @pl.when(ki < qi)
        def _():
            tile(False)

        @pl.when(ki == qi)
        def _():
            tile(True)

    blk = lambda f: pl.BlockSpec((tb, HEAD_DIM), f)
    at_q = lambda h, p, qt, kt: (qt[p], h)
    at_k = lambda h, p, qt, kt: (kt[p], h)
    crow = pl.BlockSpec((None, 1, tb), lambda h, p, qt, kt: (h, 0, kt[p]))
    whole = pl.BlockSpec((T, HEAD_DIM), lambda h, p, qt, kt: (0, h))
    wide = jax.ShapeDtypeStruct((T, H * HEAD_DIM), F32)
    return pl.pallas_call(
        body, name=name,
        grid_spec=pltpu.PrefetchScalarGridSpec(
            num_scalar_prefetch=2, grid=(H, qt.shape[0]),
            in_specs=[blk(at_q),
                      blk(lambda h, p, qt, kt: (kt[p], H + h)),
                      blk(lambda h, p, qt, kt: (kt[p], 2 * H + h)),
                      blk(at_q), crow, blk(at_q), blk(at_q)],
            out_specs=[whole, blk(at_k), blk(at_k), crow, whole]),
        out_shape=[wide, wide, wide, jax.ShapeDtypeStruct((hp, 1, T), F32), wide],
        compiler_params=_params("parallel", "arbitrary"),
    )(qt, kt, qkv, qkv, qkv, do, ck, lse, dd)


def _lane_per_head(wide, H, name, tb=256):
    T = wide.shape[0]
    tb = min(tb, T)

    def body(w_ref, o_ref):
        lane = lax.broadcasted_iota(jnp.int32, (tb, HEAD_DIM), 1)
        tile = jnp.zeros((tb, HEAD_DIM), F32)
        for h in range(H):
            tile = jnp.where(lane == h, w_ref[:, h * HEAD_DIM:(h + 1) * HEAD_DIM], tile)
        o_ref[...] = tile

    return pl.pallas_call(
        body, name=name, grid=(T // tb,),
        in_specs=[pl.BlockSpec((tb, H * HEAD_DIM), lambda i: (i, 0))],
        out_specs=pl.BlockSpec((tb, HEAD_DIM), lambda i: (i, 0)),
        out_shape=jax.ShapeDtypeStruct((T, HEAD_DIM), F32),
        compiler_params=_params("parallel"),
    )(wide)


def _slopes(n_groups, nh):
    n = n_groups * nh
    s = np.exp2(-ALIBI_MAX_EXP * np.arange(1, n + 1, dtype=np.float32) / np.float32(n)).astype(np.float32)
    return s.reshape(n_groups, nh)


def _window_logits(qh, kh, slope_r, prev, has_prev):
    qi = lax.broadcasted_iota(jnp.int32, (DIL_SPAN, DIL_SPAN), 0)
    kl = lax.broadcasted_iota(jnp.int32, (DIL_SPAN, DIL_SPAN), 1)
    delta = qi - kl + (DIL_SPAN if prev else 0)
    s = _dot(qh, kh, NT) * (HEAD_DIM ** -0.5) - slope_r * delta.astype(F32)
    valid = ((kl >= qi) & has_prev) if prev else (kl <= qi)
    return jnp.where(valid, s, MASKED)


def _dil_views(T, r, G, nh, dv):
    L = T // r
    C, V = nh * HEAD_DIM, nh * dv
    return L, C, V, 2 * G * C + V


def _dil_fwd(qkv, g, r, G, nh, dv, slopes, name):
    T = qkv.shape[0]
    L, C, V, W = _dil_views(T, r, G, nh, dv)
    nblk = L // DIL_SPAN
    nc, nv = W // C, W // V
    view = qkv.reshape(L, r * W)

    def body(q_ref, kp_ref, kc_ref, vp_ref, vc_ref, num_ref, m_ref, den_ref):
        has_prev = pl.program_id(1) > 0
        lane = lax.broadcasted_iota(jnp.int32, (DIL_SPAN, HEAD_DIM), 1)
        m_tile = jnp.zeros((DIL_SPAN, HEAD_DIM), F32)
        den_tile = jnp.ones((DIL_SPAN, HEAD_DIM), F32)
        for h in range(nh):
            sl = slice(h * HEAD_DIM, (h + 1) * HEAD_DIM)
            vs = slice(h * dv, (h + 1) * dv)
            sr = float(slopes[h]) * r
            sc = _window_logits(q_ref[:, sl], kc_ref[:, sl], sr, False, has_prev)
            sp = _window_logits(q_ref[:, sl], kp_ref[:, sl], sr, True, has_prev)
            m = jnp.maximum(jnp.max(sc, axis=1, keepdims=True), jnp.max(sp, axis=1, keepdims=True))
            pc, pp = jnp.exp(sc - m), jnp.exp(sp - m)
            den = jnp.sum(pc, axis=1, keepdims=True) + jnp.sum(pp, axis=1, keepdims=True)
            num_ref[:, vs] = _dot(pc.astype(BF16), vc_ref[:, vs], NN) + _dot(pp.astype(BF16), vp_ref[:, vs], NN)
            m_tile = jnp.where(lane == h, m, m_tile)
            den_tile = jnp.where(lane == h, den, den_tile)
        m_ref[...] = m_tile
        den_ref[...] = den_tile

    prev = lambda i: jnp.maximum(i - 1, 0)
    stat = pl.BlockSpec((DIL_SPAN, HEAD_DIM), lambda b, i: (i, b))
    num, m, den = pl.pallas_call(
        body, name=name, grid=(r, nblk),
        in_specs=[pl.BlockSpec((DIL_SPAN, C), lambda b, i: (i, b * nc + g)),
                  pl.BlockSpec((DIL_SPAN, C), lambda b, i: (prev(i), b * nc + G + g)),
                  pl.BlockSpec((DIL_SPAN, C), lambda b, i: (i, b * nc + G + g)),
                  pl.BlockSpec((DIL_SPAN, V), lambda b, i: (prev(i), b * nv + nv - 1)),
                  pl.BlockSpec((DIL_SPAN, V), lambda b, i: (i, b * nv + nv - 1))],
        out_specs=[pl.BlockSpec((DIL_SPAN, V), lambda b, i: (i, b)), stat, stat],
        out_shape=[jax.ShapeDtypeStruct((L, r * V), F32), jax.ShapeDtypeStruct((L, r * HEAD_DIM), F32),
                   jax.ShapeDtypeStruct((L, r * HEAD_DIM), F32)],
        compiler_params=_params("parallel", "parallel"),
    )(view, view, view, view, view)
    return num.reshape(T, V), m.reshape(T, HEAD_DIM), den.reshape(T, HEAD_DIM)


def _dil_merge(nums, ms, dens, nh, dv, name, tb=256):
    T, V = nums[0].shape
    tb = min(tb, T)
    G = len(nums)

    def body(*refs):
        num_r, m_r, den_r = refs[:G], refs[G:2 * G], refs[2 * G:3 * G]
        o_ref, lse_ref = refs[3 * G], refs[3 * G + 1]
        mm = m_r[0][...]
        for g in range(1, G):
            mm = jnp.maximum(mm, m_r[g][...])
        w = [jnp.exp(m_r[g][...] - mm) for g in range(G)]
        den = w[0] * den_r[0][...]
        for g in range(1, G):
            den = den + w[g] * den_r[g][...]
        lse_ref[...] = mm + jnp.log(den)
        for h in range(nh):
            vs = slice(h * dv, (h + 1) * dv)
            num = w[0][:, h:h + 1] * num_r[0][:, vs]
            for g in range(1, G):
                num = num + w[g][:, h:h + 1] * num_r[g][:, vs]
            o_ref[:, vs] = (num / den[:, h:h + 1]).astype(BF16)

    wide = pl.BlockSpec((tb, V), lambda i: (i, 0))
    stat = pl.BlockSpec((tb, HEAD_DIM), lambda i: (i, 0))
    return pl.pallas_call(
        body, name=name, grid=(T // tb,),
        in_specs=[wide] * G + [stat] * (2 * G),
        out_specs=[wide, stat],
        out_shape=[jax.ShapeDtypeStruct((T, V), BF16), jax.ShapeDtypeStruct((T, HEAD_DIM), F32)],
        compiler_params=_params("parallel"),
    )(*nums, *ms, *dens)


def _dil_dq(qkv, do, lse, dd, g, r, G, nh, dv, slopes, name):
    T = qkv.shape[0]
    L, C, V, W = _dil_views(T, r, G, nh, dv)
    nblk = L // DIL_SPAN
    nc, nv = W // C, W // V
    view = qkv.reshape(L, r * W)
    scale = HEAD_DIM ** -0.5

    def body(q_ref, kp_ref, kc_ref, vp_ref, vc_ref, do_ref, lse_ref, dd_ref, dq_ref):
        has_prev = pl.program_id(1) > 0
        for h in range(nh):
            sl = slice(h * HEAD_DIM, (h + 1) * HEAD_DIM)
            vs = slice(h * dv, (h + 1) * dv)
            sr = float(slopes[h]) * r
            lse_h, dd_h = lse_ref[:, h:h + 1], dd_ref[:, h:h + 1]
            acc = jnp.zeros((DIL_SPAN, HEAD_DIM), F32)
            for k_ref, v_ref, is_prev in ((kc_ref, vc_ref, False), (kp_ref, vp_ref, True)):
                s = _window_logits(q_ref[:, sl], k_ref[:, sl], sr, is_prev, has_prev)
                p = jnp.exp(s - lse_h)
                dp = _dot(do_ref[:, vs], v_ref[:, vs], NT)
                ds = (p * (dp - dd_h)).astype(BF16)
                acc = acc + _dot(ds, k_ref[:, sl], NN)
            dq_ref[:, sl] = scale * acc

    prev = lambda i: jnp.maximum(i - 1, 0)
    stat = pl.BlockSpec((DIL_SPAN, HEAD_DIM), lambda b, i: (i, b))
    dq = pl.pallas_call(
        body, name=name, grid=(r, nblk),
        in_specs=[pl.BlockSpec((DIL_SPAN, C), lambda b, i: (i, b * nc + g)),
                  pl.BlockSpec((DIL_SPAN, C), lambda b, i: (prev(i), b * nc + G + g)),
                  pl.BlockSpec((DIL_SPAN, C), lambda b, i: (i, b * nc + G + g)),
                  pl.BlockSpec((DIL_SPAN, V), lambda b, i: (prev(i), b * nv + nv - 1)),
                  pl.BlockSpec((DIL_SPAN, V), lambda b, i: (i, b * nv + nv - 1)),
                  pl.BlockSpec((DIL_SPAN, V), lambda b, i: (i, b)), stat, stat],
        out_specs=pl.BlockSpec((DIL_SPAN, C), lambda b, i: (i, b)),
        out_shape=jax.ShapeDtypeStruct((L, r * C), F32),
        compiler_params=_params("parallel", "parallel"),
    )(view, view, view, view, view, do.reshape(L, r * V), lse.reshape(L, r * HEAD_DIM), dd.reshape(L, r * HEAD_DIM))
    return dq.reshape(T, C)


def _dil_dkv(qkv, do, lse, dd, g, r, G, nh, dv, slopes, name):
    T = qkv.shape[0]
    L, C, V, W = _dil_views(T, r, G, nh, dv)
    nblk = L // DIL_SPAN
    nc, nv = W // C, W // V
    view = qkv.reshape(L, r * W)
    scale = HEAD_DIM ** -0.5

    def body(k_ref, v_ref, qc_ref, qn_ref, doc_ref, don_ref, lsec_ref, lsen_ref, ddc_ref, ddn_ref, dk_ref, dv_ref):
        has_next = pl.program_id(1) < nblk - 1
        for h in range(nh):
            sl = slice(h * HEAD_DIM, (h + 1) * HEAD_DIM)
            vs = slice(h * dv, (h + 1) * dv)
            sr = float(slopes[h]) * r
            dk = jnp.zeros((DIL_SPAN, HEAD_DIM), F32)
            dvh = jnp.zeros((DIL_SPAN, dv), F32)
            for q_ref, do_ref, lse_ref, dd_ref, is_next in ((qc_ref, doc_ref, lsec_ref, ddc_ref, False),
                                                          (qn_ref, don_ref, lsen_ref, ddn_ref, True)):
                s = _window_logits(q_ref[:, sl], k_ref[:, sl], sr, is_next, has_next)
                p = jnp.exp(s - lse_ref[:, h:h + 1])
                dvh = dvh + _dot(p.astype(BF16), do_ref[:, vs], TN)
                dp = _dot(do_ref[:, vs], v_ref[:, vs], NT)
                ds = (p * (dp - dd_ref[:, h:h + 1])).astype(BF16)
                dk = dk + _dot(ds, q_ref[:, sl], TN)
            dk_ref[:, sl] = scale * dk
            dv_ref[:, vs] = dvh

    nxt = lambda i: jnp.minimum(i + 1, nblk - 1)
    stat_c = pl.BlockSpec((DIL_SPAN, HEAD_DIM), lambda b, i: (i, b))
    stat_n = pl.BlockSpec((DIL_SPAN, HEAD_DIM), lambda b, i: (nxt(i), b))
    do_v, lse_v, dd_v = do.reshape(L, r * V), lse.reshape(L, r * HEAD_DIM), dd.reshape(L, r * HEAD_DIM)
    dk, dvv = pl.pallas_call(
        body, name=name, grid=(r, nblk),
        in_specs=[pl.BlockSpec((DIL_SPAN, C), lambda b, i: (i, b * nc + G + g)),
                  pl.BlockSpec((DIL_SPAN, V), lambda b, i: (i, b * nv + nv - 1)),
                  pl.BlockSpec((DIL_SPAN, C), lambda b, i: (i, b * nc + g)),
                  pl.BlockSpec((DIL_SPAN, C), lambda b, i: (nxt(i), b * nc + g)),
                  pl.BlockSpec((DIL_SPAN, V), lambda b, i: (i, b)),
                  pl.BlockSpec((DIL_SPAN, V), lambda b, i: (nxt(i), b)),
                  stat_c, stat_n, stat_c, stat_n],
        out_specs=[pl.BlockSpec((DIL_SPAN, C), lambda b, i: (i, b)), pl.BlockSpec((DIL_SPAN, V), lambda b, i: (i, b))],
        out_shape=[jax.ShapeDtypeStruct((L, r * C), F32), jax.ShapeDtypeStruct((L, r * V), F32)],
        compiler_params=_params("parallel", "parallel"),
    )(view, view, view, view, do_v, do_v, lse_v, lse_v, dd_v, dd_v)
    return dk.reshape(T, C), dvv.reshape(T, V)


def _relu2(r):
    a = jnp.maximum(r, 0.0)
    return (a * a,)


def _mlp_fwd(x, g, w_up, w_down, tag):
    T, D = x.shape
    F = w_down.shape[0]
    h = _rms_fwd(x, g, f"{tag}_norm")
    a2 = _mm(h, w_up, T, F, D, mode="nn", name=f"{tag}_up", b_stack=N_CHIPS, out_dtypes=(BF16,), epilogue=_relu2)
    y = _mm(a2, w_down, T, D, F, mode="nn", name=f"{tag}_down", out_dtypes=(F32,), extras=(x,),
            epilogue=lambda r, res: (res + r,))
    return y, (x, h, a2)


def _mlp_bwd(dy, dyb, saved, g, w_up, w_down, tag):
    x, h, a2 = saved
    T, D = x.shape
    F = w_down.shape[0]
    d_down = _mm(a2, dyb, F, D, T, mode="tn", name=f"{tag}_dwdown", out_dtypes=(F32,))
    du = _mm(dyb, w_down, T, F, D, mode="nt", name=f"{tag}_da", out_dtypes=(BF16,), extras=(a2,),
             epilogue=lambda r, sq: (r * (2.0 * jnp.sqrt(sq.astype(F32))),))
    d_up = _mm(h, du, D, F, T, mode="tn", name=f"{tag}_dwup", out_stack=N_CHIPS, out_dtypes=(F32,))
    dh = _mm(du, w_up, T, D, F, mode="nt", name=f"{tag}_dh", b_stack=N_CHIPS, out_dtypes=(F32,))
    dx, dxb, dg = _rms_bwd(dh, x, g, dy, f"{tag}_dnorm")
    return dx, dxb, dg, d_up, d_down


def _fox_dims(D):
    H = D // HEAD_DIM
    return H, max(8, H), (H // 2) * HEAD_DIM


def _fox_layer_fwd(x, g, w_qkv, w_f, b_pad, gains, w_out):
    T, D = x.shape
    H, hp, ch = _fox_dims(D)
    h = _rms_fwd(x, g, "fox_norm")
    proj = _mm(h, w_qkv, T, 3 * D, D, mode="nn", name="fox_proj", out_dtypes=(F32,))
    f_raw = _mm(h, w_f, T, HEAD_DIM, D, mode="nn", name="fox_gate_proj", out_dtypes=(F32,))
    qkv = _qkv_prep(proj, gains, 4, lambda j: jnp.minimum(j // 2, 1), ch, "fox_qk_norm", n_scaled=2, post_scale=FOX_Q_SCALE)
    ck = _gate_fwd(f_raw, b_pad, hp, LOG2E, "fox_gate").reshape(hp, 1, T)
    o, lse = _fox_fwd(qkv, ck, H, "fox_attn")
    y = _mm(o, w_out, T, D, D, mode="nn", name="fox_out", out_dtypes=(F32,), extras=(x,),
            epilogue=lambda r, res: (res + r,))
    return y, (x, h, proj, f_raw, qkv, ck, o, lse)


def _fox_layer_bwd(dy, dyb, saved, g, w_qkv, w_f, b_pad, gains, w_out):
    x, h, proj, f_raw, qkv, ck, o, lse = saved
    T, D = x.shape
    H, hp, ch = _fox_dims(D)
    d_out = _mm(o, dyb, D, D, T, mode="tn", name="fox_dwout", out_dtypes=(F32,))
    do = _mm(dyb, w_out, T, D, D, mode="nt", name="fox_do", out_dtypes=(BF16,))
    dd = _row_dot(do, o, H, HEAD_DIM, "fox_rowdot", False)
    dq, dk, dv, dck, dcq = _fox_bwd(qkv, do, ck, lse, dd, H, hp, "fox_attn_bwd")
    dcq = _lane_per_head(dcq, H, "fox_dc_query")
    dproj, dgq = _head_rms_bwd_into(None, 3 * D, dq, proj, gains[0], 0, ch, "fox_dq_norm", in_scale=HEAD_DIM ** -0.5)
    dproj, dgk = _head_rms_bwd_into(dproj, 3 * D, dk, proj, gains[1], 2, ch, "fox_dk_norm", in_scale=LN2)
    dproj = _sum_cast_into(dproj, 3 * D, [dv], 4, ch, "fox_dv_cast")
    dz, db = _gate_bwd(dck.reshape(hp, T), dcq, f_raw, b_pad, H, hp, "fox_gate_bwd")
    d_qkv = _mm(h, dproj, D, 3 * D, T, mode="tn", name="fox_dwqkv", out_dtypes=(F32,))
    d_f = _mm(h, dz, D, HEAD_DIM, T, mode="tn", name="fox_dwgate", out_dtypes=(F32,))
    dh = _mm(dproj, w_qkv, T, D, 3 * D, mode="nt", name="fox_dh", out_dtypes=(F32,))
    dh = _mm(dz, w_f, T, D, HEAD_DIM, mode="nt", name="fox_dh_gate", out_dtypes=(F32,), extras=(dh,),
             epilogue=lambda r, e: (e + r,))
    dx, dxb, dg = _rms_bwd(dh, x, g, dy, "fox_dnorm")
    dgains = jnp.stack([dgq.sum(axis=0), dgk.sum(axis=0)])
    return dx, dxb, dg, d_qkv, d_f, db, dgains, d_out


def _dil_dims(D):
    nh = D // (2 * HEAD_DIM)
    return nh, D // nh, len(DIL_PATTERNS)


def _dil_layer_fwd(x, g, w_in, gains, w_out):
    T, D = x.shape
    nh, dv, G = _dil_dims(D)
    C = nh * HEAD_DIM
    W = 2 * G * C + nh * dv
    slopes = _slopes(G, nh)
    h = _rms_fwd(x, g, "dil_norm")
    proj = _mm(h, w_in, T, W, D, mode="nn", name="dil_proj", b_stack=N_CHIPS, out_dtypes=(F32,))
    qkv = _qkv_prep(proj, gains, 2 * G, lambda j: jnp.minimum(j, 2 * G - 1), C, "dil_qk_norm")
    parts = [_dil_fwd(qkv, gi, r, G, nh, dv, slopes[gi], f"dil_attn_g{gi}") for gi, (_, r) in enumerate(DIL_PATTERNS)]
    o, lse = _dil_merge([p[0] for p in parts], [p[1] for p in parts], [p[2] for p in parts], nh, dv, "dil_merge")
    y = _mm(o, w_out, T, D, D, mode="nn", name="dil_out", out_dtypes=(F32,), extras=(x,),
            epilogue=lambda r, res: (res + r,))
    return y, (x, h, proj, qkv, o, lse)


def _dil_layer_bwd(dy, dyb, saved, g, w_in, gains, w_out):
    x, h, proj, qkv, o, lse = saved
    T, D = x.shape
    nh, dv, G = _dil_dims(D)
    C = nh * HEAD_DIM
    W = 2 * G * C + nh * dv
    slopes = _slopes(G, nh)
    d_out = _mm(o, dyb, D, D, T, mode="tn", name="dil_dwout", out_dtypes=(F32,))
    do = _mm(dyb, w_out, T, D, D, mode="nt", name="dil_do", out_dtypes=(BF16,))
    dd = _row_dot(do, o, nh, dv, "dil_rowdot", True)
    dproj, dgs, dvs = None, [None] * (2 * G), []
    for gi, (_, r) in enumerate(DIL_PATTERNS):
        dq = _dil_dq(qkv, do, lse, dd, gi, r, G, nh, dv, slopes[gi], f"dil_dq_g{gi}")
        dk, dvg = _dil_dkv(qkv, do, lse, dd, gi, r, G, nh, dv, slopes[gi], f"dil_dkv_g{gi}")
        dvs.append(dvg)
        dproj, dgs[gi] = _head_rms_bwd_into(dproj, W, dq, proj, gains[gi], gi, C, f"dil_dq_norm_g{gi}")
        dproj, dgs[G + gi] = _head_rms_bwd_into(dproj, W, dk, proj, gains[G + gi], G + gi, C, f"dil_dk_norm_g{gi}")
    dproj = _sum_cast_into(dproj, W, dvs, 2 * G, C, "dil_dv_cast")
    d_in = _mm(h, dproj, D, W, T, mode="tn", name="dil_dwin", out_stack=N_CHIPS, out_dtypes=(F32,))
    dh = _mm(dproj, w_in, T, D, W, mode="nt", name="dil_dh", b_stack=N_CHIPS, out_dtypes=(F32,))
    dx, dxb, dg = _rms_bwd(dh, x, g, dy, "dil_dnorm")
    dgains = jnp.concatenate(dgs, axis=0)
    return dx, dxb, dg, d_in, dgains, d_out


def _local_step(x, tgt, w):
    y0, s_fox = _fox_layer_fwd(x, w["mix_g"][0], w["fox_qkv"], w["fox_f"], w["fox_b"], w["fox_gains"], w["fox_out"])
    y1, s_mlp0 = _mlp_fwd(y0, w["mlp_g"][0], w["up"][0], w["down"][0], "mlp0")
    y2, s_dil = _dil_layer_fwd(y1, w["mix_g"][1], w["dil_in"], w["dil_gains"], w["dil_out"])
    y3, s_mlp1 = _mlp_fwd(y2, w["mlp_g"][1], w["up"][1], w["down"][1], "mlp1")
    dy, dyb, loss = _loss_head(y3, tgt, "loss_head")
    g = {}
    dy, dyb, g_mlp1, up1, down1 = _mlp_bwd(dy, dyb, s_mlp1, w["mlp_g"][1], w["up"][1], w["down"][1], "mlp1")
    dy, dyb, g_mix1, g["dil_in"], g["dil_gains"], g["dil_out"] = _dil_layer_bwd(
        dy, dyb, s_dil, w["mix_g"][1], w["dil_in"], w["dil_gains"], w["dil_out"])
    dy, dyb, g_mlp0, up0, down0 = _mlp_bwd(dy, dyb, s_mlp0, w["mlp_g"][0], w["up"][0], w["down"][0], "mlp0")
    dy, dyb, g_mix0, g["fox_qkv"], g["fox_f"], g["fox_b"], g["fox_gains"], g["fox_out"] = _fox_layer_bwd(
        dy, dyb, s_fox, w["mix_g"][0], w["fox_qkv"], w["fox_f"], w["fox_b"], w["fox_gains"], w["fox_out"])
    g["mix_g"], g["mlp_g"] = (g_mix0, g_mix1), (g_mlp0, g_mlp1)
    g["up"], g["down"] = (up0, up1), (down0, down1)
    return loss[0, 0], dy, g


ANY = pl.BlockSpec(memory_space=pl.ANY)


def _place():
    x, y, c = lax.axis_index("x"), lax.axis_index("y"), lax.axis_index("c")
    chips = [(1 - x, y), (x, 1 - y), (1 - x, 1 - y)]
    return x, y, c, chips


def _remote(src, dst, send_sem, recv_sem, to):
    return pltpu.make_async_remote_copy(src_ref=src, dst_ref=dst, send_sem=send_sem, recv_sem=recv_sem,
                                        device_id=to, device_id_type=MESH)


def _gather_weights(shards):
    n = len(shards)

    def body(*refs):
        src, dst = refs[:n], refs[n:2 * n]
        send_sems, recv_sems, local_sems = refs[2 * n:]
        x, y, c, chips = _place()
        mine = 2 * x + y
        local = [pltpu.make_async_copy(src[t], dst[t].at[mine], local_sems.at[t]) for t in range(n)]
        for cp in local:
            cp.start()

        def half(t, slot, which):
            hr = shards[t].shape[0] // 2
            return dst[t].at[slot, pl.ds(which * hr, hr), :]

        def my_half(t):
            hr = shards[t].shape[0] // 2
            return src[t].at[pl.ds(c * hr, hr), :]

        sends = []
        for t in range(n):
            for j, (px, py) in enumerate(chips):
                cp = _remote(my_half(t), half(t, mine, c), send_sems.at[t, j], recv_sems.at[t, j], (px, py, c))
                cp.start()
                sends.append(cp)
        for j, (px, py) in enumerate(chips):
            for t in range(n):
                landed = half(t, 2 * px + py, c)
                _remote(landed, landed, send_sems.at[t, j], recv_sems.at[t, j], (px, py, c)).wait_recv()
                cp = _remote(landed, landed, send_sems.at[t, 3 + j], recv_sems.at[t, 3 + j], (x, y, 1 - c))
                cp.start()
                sends.append(cp)
        for j, (px, py) in enumerate(chips):
            for t in range(n):
                other = half(t, 2 * px + py, 1 - c)
                _remote(other, other, send_sems.at[t, 3 + j], recv_sems.at[t, 3 + j], (x, y, 1 - c)).wait_recv()
        for cp in sends:
            cp.wait_send()
        for cp in local:
            cp.wait()

    return pl.pallas_call(
        body, name="gather_weights",
        in_specs=[ANY] * n, out_specs=[ANY] * n,
        out_shape=[jax.ShapeDtypeStruct((N_CHIPS,) + s.shape, s.dtype) for s in shards],
        scratch_shapes=[pltpu.SemaphoreType.DMA((n, 6)), pltpu.SemaphoreType.DMA((n, 6)), pltpu.SemaphoreType.DMA((n,))],
        compiler_params=_params(has_side_effects=True),
    )(*shards)


def _pair_exchange(grads):
    n = len(grads)

    def body(*refs):
        src, dst = refs[:n], refs[n:2 * n]
        send_sems, recv_sems = refs[2 * n:]
        x, y, c, _ = _place()
        cps = []
        for t in range(n):
            hr = grads[t].shape[1] // 2
            cp = _remote(src[t].at[:, pl.ds((1 - c) * hr, hr), :], dst[t], send_sems.at[t], recv_sems.at[t], (x, y, 1 - c))
            cp.start()
            cps.append(cp)
        for cp in cps:
            cp.wait()

    return pl.pallas_call(
        body, name="grad_pair_exchange",
        in_specs=[ANY] * n, out_specs=[ANY] * n,
        out_shape=[jax.ShapeDtypeStruct((g.shape[0], g.shape[1] // 2, g.shape[2]), g.dtype) for g in grads],
        scratch_shapes=[pltpu.SemaphoreType.DMA((n,)), pltpu.SemaphoreType.DMA((n,))],
        compiler_params=_params(has_side_effects=True),
    )(*grads)


def _pair_add(g, got, cidx, name, tb=256):
    S, R, C = g.shape
    hr = R // 2
    tb = _rows_tile(hr, tb)
    nb = hr // tb

    def body(c_ref, a_ref, b_ref, o_ref):
        o_ref[...] = (a_ref[...] + b_ref[...]).astype(BF16)

    return pl.pallas_call(
        body, name=name,
        grid_spec=pltpu.PrefetchScalarGridSpec(
            num_scalar_prefetch=1, grid=(S, nb),
            in_specs=[pl.BlockSpec((None, tb, C), lambda s, i, c: (s, c[0] * nb + i, 0)),
                      pl.BlockSpec((None, tb, C), lambda s, i, c: (s, i, 0))],
            out_specs=pl.BlockSpec((None, tb, C), lambda s, i, c: (s, i, 0))),
        out_shape=jax.ShapeDtypeStruct((S, hr, C), BF16),
        compiler_params=_params("parallel", "parallel"),
    )(cidx, g, got)


def _rows_tile(n, want):
    t = min(n, want)
    while n % t or t % 8:
        t -= 8
    return t


def _chip_scatter(sums):
    n = len(sums)

    def body(*refs):
        src, dst = refs[:n], refs[n:2 * n]
        send_sems, recv_sems, local_sems = refs[2 * n:]
        x, y, c, chips = _place()
        mine = 2 * x + y
        local = [pltpu.make_async_copy(src[t].at[mine], dst[t].at[mine], local_sems.at[t]) for t in range(n)]
        for cp in local:
            cp.start()
        sends = []
        for t in range(n):
            for j, (px, py) in enumerate(chips):
                cp = _remote(src[t].at[2 * px + py], dst[t].at[mine], send_sems.at[t, j], recv_sems.at[t, j], (px, py, c))
                cp.start()
                sends.append(cp)
        for t in range(n):
            for j, (px, py) in enumerate(chips):
                slot = dst[t].at[2 * px + py]
                _remote(slot, slot, send_sems.at[t, j], recv_sems.at[t, j], (px, py, c)).wait_recv()
        for cp in sends:
            cp.wait_send()
        for cp in local:
            cp.wait()

    return pl.pallas_call(
        body, name="grad_chip_scatter",
        in_specs=[ANY] * n, out_specs=[ANY] * n,
        out_shape=[jax.ShapeDtypeStruct(s.shape, s.dtype) for s in sums],
        scratch_shapes=[pltpu.SemaphoreType.DMA((n, 3)), pltpu.SemaphoreType.DMA((n, 3)), pltpu.SemaphoreType.DMA((n,))],
        compiler_params=_params(has_side_effects=True),
    )(*sums)


def _chip_sum(parts, cidx, name, tb=256):
    S, hr, C = parts.shape
    tb = _rows_tile(hr, tb)
    nb = hr // tb

    def body(c_ref, *refs):
        o_ref = refs[S]
        tot = refs[0][...].astype(F32)
        for s in range(1, S):
            tot = tot + refs[s][...].astype(F32)
        o_ref[...] = tot

    return pl.pallas_call(
        body, name=name,
        grid_spec=pltpu.PrefetchScalarGridSpec(
            num_scalar_prefetch=1, grid=(nb,),
            in_specs=[pl.BlockSpec((None, tb, C), functools.partial(lambda s, i, c: (s, i, 0), s)) for s in range(S)],
            out_specs=pl.BlockSpec((tb, C), lambda i, c: (c[0] * nb + i, 0))),
        out_shape=jax.ShapeDtypeStruct((2 * hr, C), F32),
        compiler_params=_params("parallel"),
    )(cidx, *([parts] * S))


def _half_exchange(halves):
    n = len(halves)

    def body(*refs):
        dst = refs[n:2 * n]
        send_sems, recv_sems = refs[2 * n:]
        x, y, c, _ = _place()
        cps = []
        for t in range(n):
            hr = halves[t].shape[0] // 2
            rows = dst[t].at[pl.ds(c * hr, hr), :]
            cp = _remote(rows, rows, send_sems.at[t], recv_sems.at[t], (x, y, 1 - c))
            cp.start()
            cps.append(cp)
        for t, cp in enumerate(cps):
            cp.wait_send()
            hr = halves[t].shape[0] // 2
            other = dst[t].at[pl.ds((1 - c) * hr, hr), :]
            _remote(other, other, send_sems.at[t], recv_sems.at[t], (x, y, 1 - c)).wait_recv()

    return pl.pallas_call(
        body, name="grad_half_exchange",
        in_specs=[ANY] * n, out_specs=[ANY] * n,
        out_shape=[jax.ShapeDtypeStruct(h.shape, h.dtype) for h in halves],
        input_output_aliases={t: t for t in range(n)},
        scratch_shapes=[pltpu.SemaphoreType.DMA((n,)), pltpu.SemaphoreType.DMA((n,))],
        compiler_params=_params(has_side_effects=True),
    )(*halves)


def _adamw_math(w, g, m, v):
    m = ADAM_B1 * m + (1.0 - ADAM_B1) * g
    v = ADAM_B2 * v + (1.0 - ADAM_B2) * (g * g)
    m_hat = m / (1.0 - ADAM_B1 ** ADAM_STEP)
    v_hat = v / (1.0 - ADAM_B2 ** ADAM_STEP)
    delta = -ADAM_LR * (m_hat / (jnp.sqrt(v_hat) + ADAM_EPS) + ADAM_WD * w)
    return delta, m, v


def _adamw(w, g, m, v, name, tb=256):
    R, C = w.shape
    tb = _rows_tile(R, tb)

    def body(w_ref, g_ref, m_ref, v_ref, d_ref, mo_ref, vo_ref):
        d, mn, vn = _adamw_math(w_ref[...], g_ref[...], m_ref[...], v_ref[...])
        d_ref[...] = d
        mo_ref[...] = mn
        vo_ref[...] = vn

    row = pl.BlockSpec((tb, C), lambda i: (i, 0))
    return pl.pallas_call(
        body, name=name, grid=(R // tb,),
        in_specs=[row] * 4, out_specs=[row] * 3,
        out_shape=[jax.ShapeDtypeStruct((R, C), F32)] * 3,
        compiler_params=_params("parallel"),
    )(w, g, m, v)


N_DEV = 8


def _small_update(g, w, m, v):
    P = g.shape[0]

    def body(g_ref, w_ref, m_ref, v_ref, go_ref, d_ref, mo_ref, vo_ref, buf, send_sems, recv_sems):
        x, y, c, _ = _place()
        me = 4 * x + 2 * y + c
        buf[me] = g_ref[...]
        cps = []
        for k in range(1, N_DEV):
            fx, fy, fc = (k >> 2) & 1, (k >> 1) & 1, k & 1
            px = (1 - x) if fx else x
            py = (1 - y) if fy else y
            pc = (1 - c) if fc else c
            cp = _remote(g_ref, buf.at[me], send_sems.at[k - 1], recv_sems.at[k - 1], (px, py, pc))
            cp.start()
            cps.append((cp, 4 * px + 2 * py + pc))
        for k, (cp, peer) in enumerate(cps):
            _remote(g_ref, buf.at[peer], send_sems.at[k], recv_sems.at[k], (x, y, c)).wait_recv()
        for cp, _ in cps:
            cp.wait_send()
        tot = buf[0]
        for d in range(1, N_DEV):
            tot = tot + buf[d]
        go_ref[...] = tot
        dl, mn, vn = _adamw_math(w_ref[...], tot, m_ref[...], v_ref[...])
        d_ref[...] = dl
        mo_ref[...] = mn
        vo_ref[...] = vn

    vm = pl.BlockSpec(memory_space=pltpu.VMEM)
    return pl.pallas_call(
        body, name="small_params_update",
        in_specs=[vm] * 4, out_specs=[vm] * 4,
        out_shape=[jax.ShapeDtypeStruct((P, HEAD_DIM), F32)] * 4,
        scratch_shapes=[pltpu.VMEM((N_DEV, P, HEAD_DIM), F32), pltpu.SemaphoreType.DMA((N_DEV - 1,)),
                        pltpu.SemaphoreType.DMA((N_DEV - 1,))],
        compiler_params=_params(has_side_effects=True),
    )(g, w, m, v)


SMALL = ("fox_b_f", "fox_q_gain", "fox_k_gain", "dil_q_gain", "dil_k_gain", "mix_norm_g", "mlp_norm_g")
LARGE = ("fox_w_in", "fox_w_out", "dil_w_in", "dil_w_out", "mlp_w_up", "mlp_w_down")
WEIGHTS = ("fox_w_in", "fox_b_f", "fox_q_gain", "fox_k_gain", "fox_w_out", "dil_w_in", "dil_q_gain", "dil_k_gain",
           "dil_w_out", "mix_norm_g", "mlp_norm_g", "mlp_w_up", "mlp_w_down")


def _pack(parts):
    rows = []
    for a in parts:
        flat = a.reshape(-1)
        n = -(-flat.shape[0] // (8 * HEAD_DIM)) * (8 * HEAD_DIM)
        rows.append(jnp.pad(flat, (0, n - flat.shape[0])).reshape(-1, HEAD_DIM))
    return jnp.concatenate(rows, axis=0)


def _unpack(packed, like):
    out, r = [], 0
    for a in like:
        size = int(np.prod(a.shape))
        n = -(-size // (8 * HEAD_DIM)) * 8
        out.append(packed[r:r + n].reshape(-1)[:size].reshape(a.shape))
        r += n
    return out


def _pad_lanes(a):
    return jnp.pad(a, [(0, 0)] * (a.ndim - 1) + [(0, HEAD_DIM - a.shape[-1])])


def _as_shards(a):
    return a.reshape(N_CHIPS, a.shape[0] // N_CHIPS, a.shape[1])


def kernel(x, fox_w_in, fox_b_f, fox_q_gain, fox_k_gain, fox_w_out, dil_w_in, dil_q_gain, dil_k_gain, dil_w_out, mix_norm_g, mlp_norm_g, mlp_w_up, mlp_w_down, loss_target, m_fox_w_in, m_fox_b_f, m_fox_q_gain, m_fox_k_gain, m_fox_w_out, m_dil_w_in, m_dil_q_gain, m_dil_k_gain, m_dil_w_out, m_mix_norm_g, m_mlp_norm_g, m_mlp_w_up, m_mlp_w_down, v_fox_w_in, v_fox_b_f, v_fox_q_gain, v_fox_k_gain, v_fox_w_out, v_dil_w_in, v_dil_q_gain, v_dil_k_gain, v_dil_w_out, v_mix_norm_g, v_mlp_norm_g, v_mlp_w_up, v_mlp_w_down):
    wts = dict(fox_w_in=fox_w_in, fox_b_f=fox_b_f, fox_q_gain=fox_q_gain, fox_k_gain=fox_k_gain, fox_w_out=fox_w_out,
               dil_w_in=dil_w_in, dil_q_gain=dil_q_gain, dil_k_gain=dil_k_gain, dil_w_out=dil_w_out,
               mix_norm_g=mix_norm_g, mlp_norm_g=mlp_norm_g, mlp_w_up=mlp_w_up, mlp_w_down=mlp_w_down)
    mom1 = dict(fox_w_in=m_fox_w_in, fox_b_f=m_fox_b_f, fox_q_gain=m_fox_q_gain, fox_k_gain=m_fox_k_gain,
                fox_w_out=m_fox_w_out, dil_w_in=m_dil_w_in, dil_q_gain=m_dil_q_gain, dil_k_gain=m_dil_k_gain,
                dil_w_out=m_dil_w_out, mix_norm_g=m_mix_norm_g, mlp_norm_g=m_mlp_norm_g, mlp_w_up=m_mlp_w_up,
                mlp_w_down=m_mlp_w_down)
    mom2 = dict(fox_w_in=v_fox_w_in, fox_b_f=v_fox_b_f, fox_q_gain=v_fox_q_gain, fox_k_gain=v_fox_k_gain,
                fox_w_out=v_fox_w_out, dil_w_in=v_dil_w_in, dil_q_gain=v_dil_q_gain, dil_k_gain=v_dil_k_gain,
                dil_w_out=v_dil_w_out, mix_norm_g=v_mix_norm_g, mlp_norm_g=v_mlp_norm_g, mlp_w_up=v_mlp_w_up,
                mlp_w_down=v_mlp_w_down)
    T, D = x.shape[1], x.shape[2]
    H = D // HEAD_DIM
    cidx = lax.axis_index("c").astype(jnp.int32).reshape(1)

    def shards_of(d):
        return [d["fox_w_in"][0], d["fox_w_out"][0], d["dil_w_in"][0], d["dil_w_out"][0],
                d["mlp_w_up"][0], d["mlp_w_up"][1], d["mlp_w_down"][0], d["mlp_w_down"][1]]

    w_sh, m_sh, v_sh = shards_of(wts), shards_of(mom1), shards_of(mom2)
    full = _gather_weights([s.astype(BF16) for s in w_sh])
    fox_in = jnp.moveaxis(full[0], 0, 1).reshape(D, -1)
    w = dict(
        fox_qkv=fox_in[:, :3 * D], fox_f=_pad_lanes(fox_in[:, 3 * D:]), fox_b=_pad_lanes(fox_b_f),
        fox_gains=jnp.stack([fox_q_gain, fox_k_gain]), fox_out=full[1].reshape(D, D),
        dil_in=full[2], dil_gains=jnp.concatenate([dil_q_gain[0], dil_k_gain[0]])[:, None, :], dil_out=full[3].reshape(D, D),
        up=[full[4], full[5]], down=[full[6].reshape(-1, D), full[7].reshape(-1, D)],
        mix_g=[mix_norm_g[0:1], mix_norm_g[1:2]], mlp_g=[mlp_norm_g[0:1], mlp_norm_g[1:2]])

    loss, grad_x, g = _local_step(x.reshape(T, D), loss_target.reshape(T, D), w)
    loss = lax.psum(loss, ("x", "y", "c"))

    g_fox_in = jnp.concatenate([g["fox_qkv"], g["fox_f"][:, :H]], axis=1)
    g_fox_in = jnp.moveaxis(g_fox_in.reshape(D, N_CHIPS, -1), 1, 0)
    stacked = [g_fox_in, _as_shards(g["fox_out"]), g["dil_in"], _as_shards(g["dil_out"]),
               g["up"][0], g["up"][1], _as_shards(g["down"][0]), _as_shards(g["down"][1])]
    got = _pair_exchange(stacked)
    sums = [_pair_add(a, b, cidx, f"grad_pair_add_{t}") for t, (a, b) in enumerate(zip(stacked, got))]
    parts = _chip_scatter(sums)
    halves = [_chip_sum(p, cidx, f"grad_chip_sum_{t}") for t, p in enumerate(parts)]
    totals = _half_exchange(halves)
    upd = [_adamw(w_sh[t], totals[t], m_sh[t], v_sh[t], f"adamw_{t}") for t in range(len(totals))]

    def large(k):
        a = [totals[t] if k == 0 else upd[t][k - 1] for t in range(len(totals))]
        return dict(fox_w_in=a[0][None], fox_w_out=a[1][None], dil_w_in=a[2][None], dil_w_out=a[3][None],
                    mlp_w_up=jnp.stack([a[4], a[5]]), mlp_w_down=jnp.stack([a[6], a[7]]))

    small_like = [wts[n] for n in SMALL]
    g_small = [g["fox_b"][:, :H], g["fox_gains"][0], g["fox_gains"][1], g["dil_gains"][:3, 0][None], g["dil_gains"][3:, 0][None],
               jnp.concatenate(g["mix_g"]), jnp.concatenate(g["mlp_g"])]
    packed = _small_update(_pack(g_small), _pack(small_like), _pack([mom1[n] for n in SMALL]), _pack([mom2[n] for n in SMALL]))
    small = [dict(zip(SMALL, _unpack(p, small_like))) for p in packed]

    outs = [loss, grad_x.reshape(x.shape)]
    for k in range(4):
        big = large(k)
        outs += [big[n] if n in big else small[k][n] for n in WEIGHTS]
    return tuple(outs)
```

```python
import functools

import numpy as np
import jax
import jax.numpy as jnp
from jax import lax
from jax.experimental import pallas as pl
from jax.experimental.pallas import tpu as pltpu

F32 = jnp.float32
BF16 = jnp.bfloat16

HEAD_DIM = 128
DIL_PATTERNS = ((128, 1), (512, 4), (2048, 16))
DIL_SPAN = 128
ALIBI_MAX_EXP = 8.0
EPS = 1e-6
MASKED = -1e30

ADAM_LR = 0.001
ADAM_B1 = 0.9
ADAM_B2 = 0.999
ADAM_EPS = 1e-08
ADAM_WD = 0.01
ADAM_STEP = 10

N_CHIPS = 4
VMEM_LIMIT_BYTES = 56 * 1024 * 1024
MESH = pl.DeviceIdType.MESH

NN = (((1,), (0,)), ((), ()))
NT = (((1,), (1,)), ((), ()))
TN = (((0,), (0,)), ((), ()))


def _params(*sem, **kw):
    return pltpu.CompilerParams(dimension_semantics=sem or None, vmem_limit_bytes=VMEM_LIMIT_BYTES, **kw)


def _dot(a, b, dims):
    return lax.dot_general(a, b, dims, preferred_element_type=F32)


def _tile(n, want):
    if n <= want:
        return n
    t = want - want % 128
    while n % t:
        t -= 128
    return t


def _mm(a, b, M, N, K, *, mode, name, out_dtypes, b_stack=0, out_stack=0, extras=(), epilogue=None,
        tm=1024, tn=1024, tk=2048):
    per_b = per_o = None
    if b_stack:
        per_b = (K if mode == "nt" else N) // b_stack
    if out_stack:
        per_o = N // out_stack
    tm = _tile(M, tm)
    tn = _tile(min(x for x in (N, per_o, per_b if mode != "nt" else None) if x), tn)
    tk = _tile(min(x for x in (K, per_b if mode == "nt" else None) if x), tk)
    assert M % tm == 0 and N % tn == 0 and K % tk == 0, (name, M, N, K, tm, tn, tk)
    gk = K // tk
    if mode == "tn":
        a_spec = pl.BlockSpec((tk, tm), lambda i, j, k: (k, i))
    else:
        a_spec = pl.BlockSpec((tm, tk), lambda i, j, k: (i, k))
    if mode == "nt":
        if b_stack:
            npk = per_b // tk
            b_spec = pl.BlockSpec((None, tn, tk), lambda i, j, k: (k // npk, j, k % npk))
        else:
            b_spec = pl.BlockSpec((tn, tk), lambda i, j, k: (j, k))
    else:
        if b_stack:
            npj = per_b // tn
            b_spec = pl.BlockSpec((None, tk, tn), lambda i, j, k: (j // npj, k, j % npj))
        else:
            b_spec = pl.BlockSpec((tk, tn), lambda i, j, k: (k, j))
    if out_stack:
        npo = per_o // tn
        o_spec = pl.BlockSpec((None, tm, tn), lambda i, j, k: (j // npo, i, j % npo))
        o_shape = (out_stack, M, per_o)
    else:
        o_spec = pl.BlockSpec((tm, tn), lambda i, j, k: (i, j))
        o_shape = (M, N)
    e_spec = pl.BlockSpec((tm, tn), lambda i, j, k: (i, j))
    dims = {"nn": NN, "nt": NT, "tn": TN}[mode]
    ne, no = len(extras), len(out_dtypes)

    def body(a_ref, b_ref, *rest):
        ex, outs = rest[:ne], rest[ne:ne + no]
        k = pl.program_id(2)

        def product():
            return _dot(a_ref[...].astype(BF16), b_ref[...].astype(BF16), dims)

        def finish(r):
            res = epilogue(r, *[e[...] for e in ex]) if epilogue is not None else (r,)
            for o, v in zip(outs, res):
                o[...] = v.astype(o.dtype)

        if gk == 1:
            finish(product())
            return
        acc = rest[ne + no]

        @pl.when(k == 0)
        def _():
            acc[...] = product()

        @pl.when((k > 0) & (k < gk - 1))
        def _():
            acc[...] += product()

        @pl.when(k == gk - 1)
        def _():
            finish(acc[...] + product())

    outs = pl.pallas_call(
        body, name=name,
        grid=(M // tm, N // tn, gk),
        in_specs=[a_spec, b_spec] + [e_spec] * ne,
        out_specs=[o_spec] * no,
        out_shape=[jax.ShapeDtypeStruct(o_shape, d) for d in out_dtypes],
        scratch_shapes=[pltpu.VMEM((tm, tn), F32)] if gk > 1 else [],
        compiler_params=_params("parallel", "parallel", "arbitrary"),
    )(a, b, *extras)
    return outs[0] if no == 1 else outs


def _rms_fwd(x, g, name, tb=512):
    T, D = x.shape
    tb = min(tb, T)

    def body(x_ref, g_ref, o_ref):
        xv = x_ref[...]
        r = lax.rsqrt(jnp.mean(xv * xv, axis=-1, keepdims=True) + EPS)
        o_ref[...] = (xv * r * g_ref[...]).astype(BF16)

    return pl.pallas_call(
        body, name=name, grid=(T // tb,),
        in_specs=[pl.BlockSpec((tb, D), lambda i: (i, 0)), pl.BlockSpec((1, D), lambda i: (0, 0))],
        out_specs=pl.BlockSpec((tb, D), lambda i: (i, 0)),
        out_shape=jax.ShapeDtypeStruct((T, D), BF16),
        compiler_params=_params("parallel"),
    )(x, g)


def _rms_bwd(dy, x, g, dres, name, tb=256):
    T, D = x.shape
    tb = min(tb, T)

    def body(dy_ref, x_ref, g_ref, dres_ref, dx_ref, dxb_ref, dg_ref):
        i = pl.program_id(0)
        xv, dyv = x_ref[...], dy_ref[...]
        r = lax.rsqrt(jnp.mean(xv * xv, axis=-1, keepdims=True) + EPS)
        gy = dyv * g_ref[...]
        dx = r * gy - xv * (r * r * r) * jnp.mean(gy * xv, axis=-1, keepdims=True)
        tot = dres_ref[...] + dx
        dx_ref[...] = tot
        dxb_ref[...] = tot.astype(BF16)
        part = jnp.sum(dyv * (xv * r), axis=0, keepdims=True)

        @pl.when(i == 0)
        def _():
            dg_ref[...] = part

        @pl.when(i > 0)
        def _():
            dg_ref[...] += part

    row = pl.BlockSpec((tb, D), lambda i: (i, 0))
    vec = pl.BlockSpec((1, D), lambda i: (0, 0))
    return pl.pallas_call(
        body, name=name, grid=(T // tb,),
        in_specs=[row, row, vec, row],
        out_specs=[row, row, vec],
        out_shape=[jax.ShapeDtypeStruct((T, D), F32), jax.ShapeDtypeStruct((T, D), BF16),
                   jax.ShapeDtypeStruct((1, D), F32)],
        compiler_params=_params("arbitrary"),
    )(dy, x, g, dres)


def _loss_head(y, tgt, name, tb=256):
    T, D = y.shape
    tb = min(tb, T)

    def body(y_ref, t_ref, dy_ref, dyb_ref, loss_ref):
        i = pl.program_id(0)
        e = y_ref[...] - t_ref[...]
        d = e * (1.0 / D)
        dy_ref[...] = d
        dyb_ref[...] = d.astype(BF16)
        part = 0.5 * jnp.sum(jnp.sum(e * e, axis=1, keepdims=True) * (1.0 / D), axis=0, keepdims=True)

        @pl.when(i == 0)
        def _():
            loss_ref[...] = part

        @pl.when(i > 0)
        def _():
            loss_ref[...] += part

    row = pl.BlockSpec((tb, D), lambda i: (i, 0))
    return pl.pallas_call(
        body, name=name, grid=(T // tb,),
        in_specs=[row, row],
        out_specs=[row, row, pl.BlockSpec((1, 1), lambda i: (0, 0))],
        out_shape=[jax.ShapeDtypeStruct((T, D), F32), jax.ShapeDtypeStruct((T, D), BF16),
                   jax.ShapeDtypeStruct((1, 1), F32)],
        compiler_params=_params("arbitrary"),
    )(y, tgt)


def _head_rms(xh, g):
    r = lax.rsqrt(jnp.mean(xh * xh, axis=-1, keepdims=True) + EPS)
    return xh * r * g


def _qkv_prep(proj, gains, n_norm, gain_row, ch, name, n_scaled=0, post_scale=1.0, tb=512):
    T, W = proj.shape
    tb = min(tb, T)
    nch = W // ch
    nh = ch // HEAD_DIM

    def body(p_ref, g_ref, o_ref):
        j = pl.program_id(0)

        @pl.when(j < n_norm)
        def _():
            g = g_ref[...]
            if n_scaled:
                g = g * jnp.where(j < n_scaled, post_scale, 1.0)
            for h in range(nh):
                sl = slice(h * HEAD_DIM, (h + 1) * HEAD_DIM)
                o_ref[:, sl] = _head_rms(p_ref[:, sl], g).astype(BF16)

        @pl.when(j >= n_norm)
        def _():
            o_ref[...] = p_ref[...].astype(BF16)

    return pl.pallas_call(
        body, name=name, grid=(nch, T // tb),
        in_specs=[pl.BlockSpec((tb, ch), lambda j, i: (i, j)),
                  pl.BlockSpec((None, 1, HEAD_DIM), lambda j, i: (gain_row(j), 0, 0))],
        out_specs=pl.BlockSpec((tb, ch), lambda j, i: (i, j)),
        out_shape=jax.ShapeDtypeStruct((T, W), BF16),
        compiler_params=_params("parallel", "parallel"),
    )(proj, gains)


def _into(body, name, grid, in_specs, out_spec, out_shape, extra_out_specs, extra_out_shapes, buf, operands, sem):
    if buf is None:
        def kernel(*refs):
            body(*refs)
        ins, alias, ops = in_specs, {}, operands
    else:
        def kernel(_, *refs):
            body(*refs)
        ins = [pl.BlockSpec(memory_space=pl.ANY)] + in_specs
        alias, ops = {0: 0}, (buf,) + tuple(operands)
    return pl.pallas_call(
        kernel, name=name, grid=grid, in_specs=ins,
        out_specs=[out_spec] + extra_out_specs,
        out_shape=[out_shape] + extra_out_shapes,
        input_output_aliases=alias,
        compiler_params=_params(*sem),
    )(*ops)


def _head_rms_bwd_into(buf, W, d, proj, gain, off, ch, name, in_scale=1.0, tb=256):
    T, wd = d.shape
    tb = min(tb, T)
    n = wd // ch
    nh = ch // HEAD_DIM

    def body(d_ref, p_ref, g_ref, o_ref, dg_ref):
        i = pl.program_id(1)
        g = g_ref[...]
        part = jnp.zeros((1, HEAD_DIM), F32)
        for h in range(nh):
            sl = slice(h * HEAD_DIM, (h + 1) * HEAD_DIM)
            xh, dy = p_ref[:, sl], d_ref[:, sl]
            if in_scale != 1.0:
                dy = dy * in_scale
            r = lax.rsqrt(jnp.mean(xh * xh, axis=-1, keepdims=True) + EPS)
            gy = dy * g
            dx = r * gy - xh * (r * r * r) * jnp.mean(gy * xh, axis=-1, keepdims=True)
            o_ref[:, sl] = dx.astype(BF16)
            part = part + jnp.sum(dy * (xh * r), axis=0, keepdims=True)

        @pl.when(i == 0)
        def _():
            dg_ref[...] = part

        @pl.when(i > 0)
        def _():
            dg_ref[...] += part

    return _into(
        body, name, (n, T // tb),
        [pl.BlockSpec((tb, ch), lambda j, i: (i, j)), pl.BlockSpec((tb, ch), lambda j, i: (i, off + j)),
         pl.BlockSpec((1, HEAD_DIM), lambda j, i: (0, 0))],
        pl.BlockSpec((tb, ch), lambda j, i: (i, off + j)), jax.ShapeDtypeStruct((T, W), BF16),
        [pl.BlockSpec((None, 1, HEAD_DIM), lambda j, i: (j, 0, 0))], [jax.ShapeDtypeStruct((n, 1, HEAD_DIM), F32)],
        buf, (d, proj, gain), ("parallel", "arbitrary"))


def _sum_cast_into(buf, W, srcs, off, ch, name, tb=256):
    T, wd = srcs[0].shape
    tb = min(tb, T)
    n = wd // ch
    ns = len(srcs)

    def body(*refs):
        o_ref = refs[ns]
        tot = refs[0][...]
        for s in refs[1:ns]:
            tot = tot + s[...]
        o_ref[...] = tot.astype(BF16)

    out = _into(
        body, name, (n, T // tb),
        [pl.BlockSpec((tb, ch), lambda j, i: (i, j))] * ns,
        pl.BlockSpec((tb, ch), lambda j, i: (i, off + j)), jax.ShapeDtypeStruct((T, W), BF16),
        [], [], buf, tuple(srcs), ("parallel", "parallel"))
    return out[0]


def _tri(n, lower):
    r = lax.broadcasted_iota(jnp.int32, (n, n), 0)
    c = lax.broadcasted_iota(jnp.int32, (n, n), 1)
    return jnp.where((c <= r) if lower else (c >= r), 1.0, 0.0).astype(F32)


def _dot_exact(a, b):
    return lax.dot_general(a, b, NN, precision=lax.Precision.HIGHEST, preferred_element_type=F32)


def _log_sigmoid(z):
    return jnp.minimum(z, 0.0) - jnp.log(1.0 + jnp.exp(-jnp.abs(z)))


def _gate_fwd(f_raw, b_pad, hp, out_scale, name, blk=256):
    T = f_raw.shape[0]
    blk = min(blk, T)

    def body(f_ref, b_ref, c_ref):
        tri = _tri(blk, True)
        carry = jnp.zeros((1, HEAD_DIM), F32)
        for j in range(T // blk):
            lf = _log_sigmoid(f_ref[j * blk:(j + 1) * blk, :] + b_ref[...])
            cb = _dot_exact(tri, lf) + carry
            carry = cb[blk - 1:blk, :]
            c_ref[:, j * blk:(j + 1) * blk] = cb.T[:hp, :] * out_scale

    return pl.pallas_call(
        body, name=name,
        in_specs=[pl.BlockSpec(memory_space=pltpu.VMEM)] * 2,
        out_specs=pl.BlockSpec(memory_space=pltpu.VMEM),
        out_shape=jax.ShapeDtypeStruct((hp, T), F32),
        compiler_params=_params(),
    )(f_raw, b_pad)


def _gate_bwd(dc_rows, dc_cols, f_raw, b_pad, n_heads, hp, name, blk=256):
    T = f_raw.shape[0]
    blk = min(blk, T)

    def body(dc_ref, dcc_ref, f_ref, b_ref, dz_ref, db_ref):
        tri = _tri(blk, False)
        lane = lax.broadcasted_iota(jnp.int32, (blk, HEAD_DIM), 1)
        carry = jnp.zeros((1, HEAD_DIM), F32)
        db = jnp.zeros((1, HEAD_DIM), F32)
        for j in reversed(range(T // blk)):
            rows = dc_ref[:, j * blk:(j + 1) * blk]
            if hp < HEAD_DIM:
                rows = jnp.concatenate([rows, jnp.zeros((HEAD_DIM - hp, blk), F32)], axis=0)
            dlf = _dot_exact(tri, rows.T + dcc_ref[j * blk:(j + 1) * blk, :]) + carry
            carry = dlf[0:1, :]
            z = f_ref[j * blk:(j + 1) * blk, :] + b_ref[...]
            dz = jnp.where(lane < n_heads, dlf / (1.0 + jnp.exp(z)), 0.0)
            dz_ref[j * blk:(j + 1) * blk, :] = dz.astype(BF16)
            db = db + jnp.sum(dz, axis=0, keepdims=True)
        db_ref[...] = db

    return pl.pallas_call(
        body, name=name,
        in_specs=[pl.BlockSpec(memory_space=pltpu.VMEM)] * 4,
        out_specs=[pl.BlockSpec(memory_space=pltpu.VMEM)] * 2,
        out_shape=[jax.ShapeDtypeStruct((T, HEAD_DIM), BF16), jax.ShapeDtypeStruct((1, HEAD_DIM), F32)],
        compiler_params=_params(),
    )(dc_rows, dc_cols, f_raw, b_pad)


def _pairs(nb, key_major):
    if key_major:
        pairs = [(qi, ki) for ki in range(nb) for qi in range(ki, nb)]
    else:
        pairs = [(qi, ki) for qi in range(nb) for ki in range(qi + 1)]
    return (jnp.asarray(np.array([p[0] for p in pairs], np.int32)),
            jnp.asarray(np.array([p[1] for p in pairs], np.int32)))


LOG2E = 1.4426950408889634
LN2 = 0.6931471805599453
FOX_Q_SCALE = HEAD_DIM ** -0.5 * LOG2E


def _fox_logits(q, k, ck_row, diagonal):
    s = _dot(q, k, NT) - ck_row
    if diagonal:
        row = lax.broadcasted_iota(jnp.int32, s.shape, 0)
        col = lax.broadcasted_iota(jnp.int32, s.shape, 1)
        s = jnp.where(col <= row, s, MASKED)
    return s


def _fox_fwd(qkv, ck, H, name, tb=1024):
    T = qkv.shape[0]
    tb = min(tb, T)
    nb = T // tb
    qt, kt = _pairs(nb, False)

    def body(qt_ref, kt_ref, q_ref, k_ref, v_ref, ck_ref, o_ref, lse_ref, m_sc, l_sc, acc_sc):
        p_ = pl.program_id(1)
        qi, ki = qt_ref[p_], kt_ref[p_]

        @pl.when(ki == 0)
        def _():
            m_sc[...] = jnp.full_like(m_sc, MASKED)
            l_sc[...] = jnp.zeros_like(l_sc)
            acc_sc[...] = jnp.zeros_like(acc_sc)

        def tile(diagonal):
            s = _fox_logits(q_ref[...], k_ref[...], ck_ref[...], diagonal)
            m_prev = m_sc[...]
            m_new = jnp.maximum(m_prev, jnp.max(s, axis=1, keepdims=True))
            alpha = jnp.exp2(m_prev - m_new)
            p = jnp.exp2(s - m_new[:, :1])
            l_sc[...] = alpha * l_sc[...] + jnp.sum(p, axis=1, keepdims=True)
            acc_sc[...] = alpha * acc_sc[...] + _dot(p.astype(BF16), v_ref[...], NN)
            m_sc[...] = m_new

        @pl.when(ki < qi)
        def _():
            tile(False)

        @pl.when(ki == qi)
        def _():
            tile(True)
            o_ref[...] = (acc_sc[...] / l_sc[...]).astype(BF16)
            lse_ref[...] = m_sc[...] + jnp.log(l_sc[...]) * LOG2E

    blk = lambda f: pl.BlockSpec((tb, HEAD_DIM), f)
    return pl.pallas_call(
        body, name=name,
        grid_spec=pltpu.PrefetchScalarGridSpec(
            num_scalar_prefetch=2, grid=(H, qt.shape[0]),
            in_specs=[blk(lambda h, p, qt, kt: (qt[p], h)),
                      blk(lambda h, p, qt, kt: (kt[p], H + h)),
                      blk(lambda h, p, qt, kt: (kt[p], 2 * H + h)),
                      pl.BlockSpec((None, 1, tb), lambda h, p, qt, kt: (h, 0, kt[p]))],
            out_specs=[blk(lambda h, p, qt, kt: (qt[p], h)), blk(lambda h, p, qt, kt: (qt[p], h))],
            scratch_shapes=[pltpu.VMEM((tb, HEAD_DIM), F32)] * 3),
        out_shape=[jax.ShapeDtypeStruct((T, H * HEAD_DIM), BF16), jax.ShapeDtypeStruct((T, H * HEAD_DIM), F32)],
        compiler_params=_params("parallel", "arbitrary"),
    )(qt, kt, qkv, qkv, qkv, ck)


def _row_dot(do, o, nh, width, name, lane_per_head, tb=256):
    T = do.shape[0]
    tb = min(tb, T)
    wout = HEAD_DIM if lane_per_head else nh * HEAD_DIM

    def body(do_ref, o_ref, d_ref):
        lane = lax.broadcasted_iota(jnp.int32, (tb, HEAD_DIM), 1)
        tile = jnp.zeros((tb, HEAD_DIM), F32)
        for h in range(nh):
            sl = slice(h * width, (h + 1) * width)
            d = jnp.sum(do_ref[:, sl].astype(F32) * o_ref[:, sl].astype(F32), axis=1, keepdims=True)
            if lane_per_head:
                tile = jnp.where(lane == h, d, tile)
            else:
                d_ref[:, h * HEAD_DIM:(h + 1) * HEAD_DIM] = jnp.broadcast_to(d, (tb, HEAD_DIM))
        if lane_per_head:
            d_ref[...] = tile

    row = pl.BlockSpec((tb, nh * width), lambda i: (i, 0))
    return pl.pallas_call(
        body, name=name, grid=(T // tb,),
        in_specs=[row, row], out_specs=pl.BlockSpec((tb, wout), lambda i: (i, 0)),
        out_shape=jax.ShapeDtypeStruct((T, wout), F32),
        compiler_params=_params("parallel"),
    )(do, o)


def _fox_bwd(qkv, do, ck, lse, dd, H, hp, name, tb=1024):
    T = qkv.shape[0]
    tb = min(tb, T)
    nb = T // tb
    qt, kt = _pairs(nb, True)

    def body(qt_ref, kt_ref, q_ref, k_ref, v_ref, do_ref, ck_ref, lse_ref, dd_ref, dq_ref, dk_ref, dv_ref, dc_ref, dcq_ref):
        p_ = pl.program_id(1)
        qi, ki = qt_ref[p_], kt_ref[p_]

        @pl.when(p_ == 0)
        def _():
            dq_ref[...] = jnp.zeros_like(dq_ref)
            dcq_ref[...] = jnp.zeros_like(dcq_ref)

        @pl.when(qi == ki)
        def _():
            dk_ref[...] = jnp.zeros_like(dk_ref)
            dv_ref[...] = jnp.zeros_like(dv_ref)
            dc_ref[...] = jnp.zeros_like(dc_ref)

        rows = pl.ds(pl.multiple_of(qi * tb, tb), tb)

        def tile(diagonal):
            s = _fox_logits(q_ref[...], k_ref[...], ck_ref[...], diagonal)
            p = jnp.exp2(s - lse_ref[:, :1])
            dv_ref[...] += _dot(p.astype(BF16), do_ref[...], TN)
            dp = _dot(do_ref[...], v_ref[...], NT)
            ds = p * (dp - dd_ref[:, :1])
            dc_ref[...] -= jnp.sum(ds, axis=0, keepdims=True)
            dcq_ref[rows, :] += jnp.sum(ds, axis=1, keepdims=True)
            dsb = ds.astype(BF16)
            dq_ref[rows, :] += _dot(dsb, k_ref[...], NN)
            dk_ref[...] += _dot(dsb, q_ref[...], TN)

        @pl.when(ki < qi)
        def _():
            tile(False)

        @pl.when(ki == qi)
        def _():
            tile(True)

    blk = lambda f: pl.BlockSpec((tb, HEAD_DIM), f)
    at_q = lambda h, p, qt, kt: (qt[p], h)
    at_k = lambda h, p, qt, kt: (kt[p], h)
    crow = pl.BlockSpec((None, 1, tb), lambda h, p, qt, kt: (h, 0, kt[p]))
    whole = pl.BlockSpec((T, HEAD_DIM), lambda h, p, qt, kt: (0, h))
    wide = jax.ShapeDtypeStruct((T, H * HEAD_DIM), F32)
    return pl.pallas_call(
        body, name=name,
        grid_spec=pltpu.PrefetchScalarGridSpec(
            num_scalar_prefetch=2, grid=(H, qt.shape[0]),
            in_specs=[blk(at_q),
                      blk(lambda h, p, qt, kt: (kt[p], H + h)),
                      blk(lambda h, p, qt, kt: (kt[p], 2 * H + h)),
                      blk(at_q), crow, blk(at_q), blk(at_q)],
            out_specs=[whole, blk(at_k), blk(at_k), crow, whole]),
        out_shape=[wide, wide, wide, jax.ShapeDtypeStruct((hp, 1, T), F32), wide],
        compiler_params=_params("parallel", "arbitrary"),
    )(qt, kt, qkv, qkv, qkv, do, ck, lse, dd)


def _lane_per_head(wide, H, name, tb=256):
    T = wide.shape[0]
    tb = min(tb, T)

    def body(w_ref, o_ref):
        lane = lax.broadcasted_iota(jnp.int32, (tb, HEAD_DIM), 1)
        tile = jnp.zeros((tb, HEAD_DIM), F32)
        for h in range(H):
            tile = jnp.where(lane == h, w_ref[:, h * HEAD_DIM:(h + 1) * HEAD_DIM], tile)
        o_ref[...] = tile

    return pl.pallas_call(
        body, name=name, grid=(T // tb,),
        in_specs=[pl.BlockSpec((tb, H * HEAD_DIM), lambda i: (i, 0))],
        out_specs=pl.BlockSpec((tb, HEAD_DIM), lambda i: (i, 0)),
        out_shape=jax.ShapeDtypeStruct((T, HEAD_DIM), F32),
        compiler_params=_params("parallel"),
    )(wide)


def _slopes(n_groups, nh):
    n = n_groups * nh
    s = np.exp2(-ALIBI_MAX_EXP * np.arange(1, n + 1, dtype=np.float32) / np.float32(n)).astype(np.float32)
    return s.reshape(n_groups, nh)


def _window_logits(qh, kh, slope_r, prev, has_prev):
    qi = lax.broadcasted_iota(jnp.int32, (DIL_SPAN, DIL_SPAN), 0)
    kl = lax.broadcasted_iota(jnp.int32, (DIL_SPAN, DIL_SPAN), 1)
    delta = qi - kl + (DIL_SPAN if prev else 0)
    s = _dot(qh, kh, NT) * (HEAD_DIM ** -0.5) - slope_r * delta.astype(F32)
    valid = ((kl >= qi) & has_prev) if prev else (kl <= qi)
    return jnp.where(valid, s, MASKED)


def _dil_views(T, r, G, nh, dv):
    L = T // r
    C, V = nh * HEAD_DIM, nh * dv
    return L, C, V, 2 * G * C + V


def _dil_fwd(qkv, g, r, G, nh, dv, slopes, name):
    T = qkv.shape[0]
    L, C, V, W = _dil_views(T, r, G, nh, dv)
    nblk = L // DIL_SPAN
    nc, nv = W // C, W // V
    view = qkv.reshape(L, r * W)

    def body(q_ref, kp_ref, kc_ref, vp_ref, vc_ref, num_ref, m_ref, den_ref):
        has_prev = pl.program_id(1) > 0
        lane = lax.broadcasted_iota(jnp.int32, (DIL_SPAN, HEAD_DIM), 1)
        m_tile = jnp.zeros((DIL_SPAN, HEAD_DIM), F32)
        den_tile = jnp.ones((DIL_SPAN, HEAD_DIM), F32)
        for h in range(nh):
            sl = slice(h * HEAD_DIM, (h + 1) * HEAD_DIM)
            vs = slice(h * dv, (h + 1) * dv)
            sr = float(slopes[h]) * r
            sc = _window_logits(q_ref[:, sl], kc_ref[:, sl], sr, False, has_prev)
            sp = _window_logits(q_ref[:, sl], kp_ref[:, sl], sr, True, has_prev)
            m = jnp.maximum(jnp.max(sc, axis=1, keepdims=True), jnp.max(sp, axis=1, keepdims=True))
            pc, pp = jnp.exp(sc - m), jnp.exp(sp - m)
            den = jnp.sum(pc, axis=1, keepdims=True) + jnp.sum(pp, axis=1, keepdims=True)
            num_ref[:, vs] = _dot(pc.astype(BF16), vc_ref[:, vs], NN) + _dot(pp.astype(BF16), vp_ref[:, vs], NN)
            m_tile = jnp.where(lane == h, m, m_tile)
            den_tile = jnp.where(lane == h, den, den_tile)
        m_ref[...] = m_tile
        den_ref[...] = den_tile

    prev = lambda i: jnp.maximum(i - 1, 0)
    stat = pl.BlockSpec((DIL_SPAN, HEAD_DIM), lambda b, i: (i, b))
    num, m, den = pl.pallas_call(
        body, name=name, grid=(r, nblk),
        in_specs=[pl.BlockSpec((DIL_SPAN, C), lambda b, i: (i, b * nc + g)),
                  pl.BlockSpec((DIL_SPAN, C), lambda b, i: (prev(i), b * nc + G + g)),
                  pl.BlockSpec((DIL_SPAN, C), lambda b, i: (i, b * nc + G + g)),
                  pl.BlockSpec((DIL_SPAN, V), lambda b, i: (prev(i), b * nv + nv - 1)),
                  pl.BlockSpec((DIL_SPAN, V), lambda b, i: (i, b * nv + nv - 1))],
        out_specs=[pl.BlockSpec((DIL_SPAN, V), lambda b, i: (i, b)), stat, stat],
        out_shape=[jax.ShapeDtypeStruct((L, r * V), F32), jax.ShapeDtypeStruct((L, r * HEAD_DIM), F32),
                   jax.ShapeDtypeStruct((L, r * HEAD_DIM), F32)],
        compiler_params=_params("parallel", "parallel"),
    )(view, view, view, view, view)
    return num.reshape(T, V), m.reshape(T, HEAD_DIM), den.reshape(T, HEAD_DIM)


def _dil_merge(nums, ms, dens, nh, dv, name, tb=256):
    T, V = nums[0].shape
    tb = min(tb, T)
    G = len(nums)

    def body(*refs):
        num_r, m_r, den_r = refs[:G], refs[G:2 * G], refs[2 * G:3 * G]
        o_ref, lse_ref = refs[3 * G], refs[3 * G + 1]
        mm = m_r[0][...]
        for g in range(1, G):
            mm = jnp.maximum(mm, m_r[g][...])
        w = [jnp.exp(m_r[g][...] - mm) for g in range(G)]
        den = w[0] * den_r[0][...]
        for g in range(1, G):
            den = den + w[g] * den_r[g][...]
        lse_ref[...] = mm + jnp.log(den)
        for h in range(nh):
            vs = slice(h * dv, (h + 1) * dv)
            num = w[0][:, h:h + 1] * num_r[0][:, vs]
            for g in range(1, G):
                num = num + w[g][:, h:h + 1] * num_r[g][:, vs]
            o_ref[:, vs] = (num / den[:, h:h + 1]).astype(BF16)

    wide = pl.BlockSpec((tb, V), lambda i: (i, 0))
    stat = pl.BlockSpec((tb, HEAD_DIM), lambda i: (i, 0))
    return pl.pallas_call(
        body, name=name, grid=(T // tb,),
        in_specs=[wide] * G + [stat] * (2 * G),
        out_specs=[wide, stat],
        out_shape=[jax.ShapeDtypeStruct((T, V), BF16), jax.ShapeDtypeStruct((T, HEAD_DIM), F32)],
        compiler_params=_params("parallel"),
    )(*nums, *ms, *dens)


def _dil_dq(qkv, do, lse, dd, g, r, G, nh, dv, slopes, name):
    T = qkv.shape[0]
    L, C, V, W = _dil_views(T, r, G, nh, dv)
    nblk = L // DIL_SPAN
    nc, nv = W // C, W // V
    view = qkv.reshape(L, r * W)
    scale = HEAD_DIM ** -0.5

    def body(q_ref, kp_ref, kc_ref, vp_ref, vc_ref, do_ref, lse_ref, dd_ref, dq_ref):
        has_prev = pl.program_id(1) > 0
        for h in range(nh):
            sl = slice(h * HEAD_DIM, (h + 1) * HEAD_DIM)
            vs = slice(h * dv, (h + 1) * dv)
            sr = float(slopes[h]) * r
            lse_h, dd_h = lse_ref[:, h:h + 1], dd_ref[:, h:h + 1]
            acc = jnp.zeros((DIL_SPAN, HEAD_DIM), F32)
            for k_ref, v_ref, is_prev in ((kc_ref, vc_ref, False), (kp_ref, vp_ref, True)):
                s = _window_logits(q_ref[:, sl], k_ref[:, sl], sr, is_prev, has_prev)
                p = jnp.exp(s - lse_h)
                dp = _dot(do_ref[:, vs], v_ref[:, vs], NT)
                ds = (p * (dp - dd_h)).astype(BF16)
                acc = acc + _dot(ds, k_ref[:, sl], NN)
            dq_ref[:, sl] = scale * acc

    prev = lambda i: jnp.maximum(i - 1, 0)
    stat = pl.BlockSpec((DIL_SPAN, HEAD_DIM), lambda b, i: (i, b))
    dq = pl.pallas_call(
        body, name=name, grid=(r, nblk),
        in_specs=[pl.BlockSpec((DIL_SPAN, C), lambda b, i: (i, b * nc + g)),
                  pl.BlockSpec((DIL_SPAN, C), lambda b, i: (prev(i), b * nc + G + g)),
                  pl.BlockSpec((DIL_SPAN, C), lambda b, i: (i, b * nc + G + g)),
                  pl.BlockSpec((DIL_SPAN, V), lambda b, i: (prev(i), b * nv + nv - 1)),
                  pl.BlockSpec((DIL_SPAN, V), lambda b, i: (i, b * nv + nv - 1)),
                  pl.BlockSpec((DIL_SPAN, V), lambda b, i: (i, b)), stat, stat],
        out_specs=pl.BlockSpec((DIL_SPAN, C), lambda b, i: (i, b)),
        out_shape=jax.ShapeDtypeStruct((L, r * C), F32),
        compiler_params=_params("parallel", "parallel"),
    )(view, view, view, view, view, do.reshape(L, r * V), lse.reshape(L, r * HEAD_DIM), dd.reshape(L, r * HEAD_DIM))
    return dq.reshape(T, C)


def _dil_dkv(qkv, do, lse, dd, g, r, G, nh, dv, slopes, name):
    T = qkv.shape[0]
    L, C, V, W = _dil_views(T, r, G, nh, dv)
    nblk = L // DIL_SPAN
    nc, nv = W // C, W // V
    view = qkv.reshape(L, r * W)
    scale = HEAD_DIM ** -0.5

    def body(k_ref, v_ref, qc_ref, qn_ref, doc_ref, don_ref, lsec_ref, lsen_ref, ddc_ref, ddn_ref, dk_ref, dv_ref):
        has_next = pl.program_id(1) < nblk - 1
        for h in range(nh):
            sl = slice(h * HEAD_DIM, (h + 1) * HEAD_DIM)
            vs = slice(h * dv, (h + 1) * dv)
            sr = float(slopes[h]) * r
            dk = jnp.zeros((DIL_SPAN, HEAD_DIM), F32)
            dvh = jnp.zeros((DIL_SPAN, dv), F32)
            for q_ref, do_ref, lse_ref, dd_ref, is_next in ((qc_ref, doc_ref, lsec_ref, ddc_ref, False),
                                                          (qn_ref, don_ref, lsen_ref, ddn_ref, True)):
                s = _window_logits(q_ref[:, sl], k_ref[:, sl], sr, is_next, has_next)
                p = jnp.exp(s - lse_ref[:, h:h + 1])
                dvh = dvh + _dot(p.astype(BF16), do_ref[:, vs], TN)
                dp = _dot(do_ref[:, vs], v_ref[:, vs], NT)
                ds = (p * (dp - dd_ref[:, h:h + 1])).astype(BF16)
                dk = dk + _dot(ds, q_ref[:, sl], TN)
            dk_ref[:, sl] = scale * dk
            dv_ref[:, vs] = dvh

    nxt = lambda i: jnp.minimum(i + 1, nblk - 1)
    stat_c = pl.BlockSpec((DIL_SPAN, HEAD_DIM), lambda b, i: (i, b))
    stat_n = pl.BlockSpec((DIL_SPAN, HEAD_DIM), lambda b, i: (nxt(i), b))
    do_v, lse_v, dd_v = do.reshape(L, r * V), lse.reshape(L, r * HEAD_DIM), dd.reshape(L, r * HEAD_DIM)
    dk, dvv = pl.pallas_call(
        body, name=name, grid=(r, nblk),
        in_specs=[pl.BlockSpec((DIL_SPAN, C), lambda b, i: (i, b * nc + G + g)),
                  pl.BlockSpec((DIL_SPAN, V), lambda b, i: (i, b * nv + nv - 1)),
                  pl.BlockSpec((DIL_SPAN, C), lambda b, i: (i, b * nc + g)),
                  pl.BlockSpec((DIL_SPAN, C), lambda b, i: (nxt(i), b * nc + g)),
                  pl.BlockSpec((DIL_SPAN, V), lambda b, i: (i, b)),
                  pl.BlockSpec((DIL_SPAN, V), lambda b, i: (nxt(i), b)),
                  stat_c, stat_n, stat_c, stat_n],
        out_specs=[pl.BlockSpec((DIL_SPAN, C), lambda b, i: (i, b)), pl.BlockSpec((DIL_SPAN, V), lambda b, i: (i, b))],
        out_shape=[jax.ShapeDtypeStruct((L, r * C), F32), jax.ShapeDtypeStruct((L, r * V), F32)],
        compiler_params=_params("parallel", "parallel"),
    )(view, view, view, view, do_v, do_v, lse_v, lse_v, dd_v, dd_v)
    return dk.reshape(T, C), dvv.reshape(T, V)


def _relu2(r):
    a = jnp.maximum(r, 0.0)
    return (a * a,)


def _mlp_fwd(x, g, w_up, w_down, tag):
    T, D = x.shape
    F = w_down.shape[0]
    h = _rms_fwd(x, g, f"{tag}_norm")
    a2 = _mm(h, w_up, T, F, D, mode="nn", name=f"{tag}_up", b_stack=N_CHIPS, out_dtypes=(BF16,), epilogue=_relu2)
    y = _mm(a2, w_down, T, D, F, mode="nn", name=f"{tag}_down", out_dtypes=(F32,), extras=(x,),
            epilogue=lambda r, res: (res + r,))
    return y, (x, h, a2)


def _mlp_bwd(dy, dyb, saved, g, w_up, w_down, tag):
    x, h, a2 = saved
    T, D = x.shape
    F = w_down.shape[0]
    d_down = _mm(a2, dyb, F, D, T, mode="tn", name=f"{tag}_dwdown", out_dtypes=(F32,))
    du = _mm(dyb, w_down, T, F, D, mode="nt", name=f"{tag}_da", out_dtypes=(BF16,), extras=(a2,),
             epilogue=lambda r, sq: (r * (2.0 * jnp.sqrt(sq.astype(F32))),))
    d_up = _mm(h, du, D, F, T, mode="tn", name=f"{tag}_dwup", out_stack=N_CHIPS, out_dtypes=(F32,))
    dh = _mm(du, w_up, T, D, F, mode="nt", name=f"{tag}_dh", b_stack=N_CHIPS, out_dtypes=(F32,))
    dx, dxb, dg = _rms_bwd(dh, x, g, dy, f"{tag}_dnorm")
    return dx, dxb, dg, d_up, d_down


def _fox_dims(D):
    H = D // HEAD_DIM
    return H, max(8, H), (H // 2) * HEAD_DIM


def _fox_layer_fwd(x, g, w_qkv, w_f, b_pad, gains, w_out):
    T, D = x.shape
    H, hp, ch = _fox_dims(D)
    h = _rms_fwd(x, g, "fox_norm")
    proj = _mm(h, w_qkv, T, 3 * D, D, mode="nn", name="fox_proj", out_dtypes=(F32,))
    f_raw = _mm(h, w_f, T, HEAD_DIM, D, mode="nn", name="fox_gate_proj", out_dtypes=(F32,))
    qkv = _qkv_prep(proj, gains, 4, lambda j: jnp.minimum(j // 2, 1), ch, "fox_qk_norm", n_scaled=2, post_scale=FOX_Q_SCALE)
    ck = _gate_fwd(f_raw, b_pad, hp, LOG2E, "fox_gate").reshape(hp, 1, T)
    o, lse = _fox_fwd(qkv, ck, H, "fox_attn")
    y = _mm(o, w_out, T, D, D, mode="nn", name="fox_out", out_dtypes=(F32,), extras=(x,),
            epilogue=lambda r, res: (res + r,))
    return y, (x, h, proj, f_raw, qkv, ck, o, lse)


def _fox_layer_bwd(dy, dyb, saved, g, w_qkv, w_f, b_pad, gains, w_out):
    x, h, proj, f_raw, qkv, ck, o, lse = saved
    T, D = x.shape
    H, hp, ch = _fox_dims(D)
    d_out = _mm(o, dyb, D, D, T, mode="tn", name="fox_dwout", out_dtypes=(F32,))
    do = _mm(dyb, w_out, T, D, D, mode="nt", name="fox_do", out_dtypes=(BF16,))
    dd = _row_dot(do, o, H, HEAD_DIM, "fox_rowdot", False)
    dq, dk, dv, dck, dcq = _fox_bwd(qkv, do, ck, lse, dd, H, hp, "fox_attn_bwd")
    dcq = _lane_per_head(dcq, H, "fox_dc_query")
    dproj, dgq = _head_rms_bwd_into(None, 3 * D, dq, proj, gains[0], 0, ch, "fox_dq_norm", in_scale=HEAD_DIM ** -0.5)
    dproj, dgk = _head_rms_bwd_into(dproj, 3 * D, dk, proj, gains[1], 2, ch, "fox_dk_norm", in_scale=LN2)
    dproj = _sum_cast_into(dproj, 3 * D, [dv], 4, ch, "fox_dv_cast")
    dz, db = _gate_bwd(dck.reshape(hp, T), dcq, f_raw, b_pad, H, hp, "fox_gate_bwd")
    d_qkv = _mm(h, dproj, D, 3 * D, T, mode="tn", name="fox_dwqkv", out_dtypes=(F32,))
    d_f = _mm(h, dz, D, HEAD_DIM, T, mode="tn", name="fox_dwgate", out_dtypes=(F32,))
    dh = _mm(dproj, w_qkv, T, D, 3 * D, mode="nt", name="fox_dh", out_dtypes=(F32,))
    dh = _mm(dz, w_f, T, D, HEAD_DIM, mode="nt", name="fox_dh_gate", out_dtypes=(F32,), extras=(dh,),
             epilogue=lambda r, e: (e + r,))
    dx, dxb, dg = _rms_bwd(dh, x, g, dy, "fox_dnorm")
    dgains = jnp.stack([dgq.sum(axis=0), dgk.sum(axis=0)])
    return dx, dxb, dg, d_qkv, d_f, db, dgains, d_out


def _dil_dims(D):
    nh = D // (2 * HEAD_DIM)
    return nh, D // nh, len(DIL_PATTERNS)


def _dil_layer_fwd(x, g, w_in, gains, w_out):
    T, D = x.shape
    nh, dv, G = _dil_dims(D)
    C = nh * HEAD_DIM
    W = 2 * G * C + nh * dv
    slopes = _slopes(G, nh)
    h = _rms_fwd(x, g, "dil_norm")
    proj = _mm(h, w_in, T, W, D, mode="nn", name="dil_proj", b_stack=N_CHIPS, out_dtypes=(F32,))
    qkv = _qkv_prep(proj, gains, 2 * G, lambda j: jnp.minimum(j, 2 * G - 1), C, "dil_qk_norm")
    parts = [_dil_fwd(qkv, gi, r, G, nh, dv, slopes[gi], f"dil_attn_g{gi}") for gi, (_, r) in enumerate(DIL_PATTERNS)]
    o, lse = _dil_merge([p[0] for p in parts], [p[1] for p in parts], [p[2] for p in parts], nh, dv, "dil_merge")
    y = _mm(o, w_out, T, D, D, mode="nn", name="dil_out", out_dtypes=(F32,), extras=(x,),
            epilogue=lambda r, res: (res + r,))
    return y, (x, h, proj, qkv, o, lse)


def _dil_layer_bwd(dy, dyb, saved, g, w_in, gains, w_out):
    x, h, proj, qkv, o, lse = saved
    T, D = x.shape
    nh, dv, G = _dil_dims(D)
    C = nh * HEAD_DIM
    W = 2 * G * C + nh * dv
    slopes = _slopes(G, nh)
    d_out = _mm(o, dyb, D, D, T, mode="tn", name="dil_dwout", out_dtypes=(F32,))
    do = _mm(dyb, w_out, T, D, D, mode="nt", name="dil_do", out_dtypes=(BF16,))
    dd = _row_dot(do, o, nh, dv, "dil_rowdot", True)
    dproj, dgs, dvs = None, [None] * (2 * G), []
    for gi, (_, r) in enumerate(DIL_PATTERNS):
        dq = _dil_dq(qkv, do, lse, dd, gi, r, G, nh, dv, slopes[gi], f"dil_dq_g{gi}")
        dk, dvg = _dil_dkv(qkv, do, lse, dd, gi, r, G, nh, dv, slopes[gi], f"dil_dkv_g{gi}")
        dvs.append(dvg)
        dproj, dgs[gi] = _head_rms_bwd_into(dproj, W, dq, proj, gains[gi], gi, C, f"dil_dq_norm_g{gi}")
        dproj, dgs[G + gi] = _head_rms_bwd_into(dproj, W, dk, proj, gains[G + gi], G + gi, C, f"dil_dk_norm_g{gi}")
    dproj = _sum_cast_into(dproj, W, dvs, 2 * G, C, "dil_dv_cast")
    d_in = _mm(h, dproj, D, W, T, mode="tn", name="dil_dwin", out_stack=N_CHIPS, out_dtypes=(F32,))
    dh = _mm(dproj, w_in, T, D, W, mode="nt", name="dil_dh", b_stack=N_CHIPS, out_dtypes=(F32,))
    dx, dxb, dg = _rms_bwd(dh, x, g, dy, "dil_dnorm")
    dgains = jnp.concatenate(dgs, axis=0)
    return dx, dxb, dg, d_in, dgains, d_out


def _local_step(x, tgt, w):
    y0, s_fox = _fox_layer_fwd(x, w["mix_g"][0], w["fox_qkv"], w["fox_f"], w["fox_b"], w["fox_gains"], w["fox_out"])
    y1, s_mlp0 = _mlp_fwd(y0, w["mlp_g"][0], w["up"][0], w["down"][0], "mlp0")
    y2, s_dil = _dil_layer_fwd(y1, w["mix_g"][1], w["dil_in"], w["dil_gains"], w["dil_out"])
    y3, s_mlp1 = _mlp_fwd(y2, w["mlp_g"][1], w["up"][1], w["down"][1], "mlp1")
    dy, dyb, loss = _loss_head(y3, tgt, "loss_head")
    g = {}
    dy, dyb, g_mlp1, up1, down1 = _mlp_bwd(dy, dyb, s_mlp1, w["mlp_g"][1], w["up"][1], w["down"][1], "mlp1")
    dy, dyb, g_mix1, g["dil_in"], g["dil_gains"], g["dil_out"] = _dil_layer_bwd(
        dy, dyb, s_dil, w["mix_g"][1], w["dil_in"], w["dil_gains"], w["dil_out"])
    dy, dyb, g_mlp0, up0, down0 = _mlp_bwd(dy, dyb, s_mlp0, w["mlp_g"][0], w["up"][0], w["down"][0], "mlp0")
    dy, dyb, g_mix0, g["fox_qkv"], g["fox_f"], g["fox_b"], g["fox_gains"], g["fox_out"] = _fox_layer_bwd(
        dy, dyb, s_fox, w["mix_g"][0], w["fox_qkv"], w["fox_f"], w["fox_b"], w["fox_gains"], w["fox_out"])
    g["mix_g"], g["mlp_g"] = (g_mix0, g_mix1), (g_mlp0, g_mlp1)
    g["up"], g["down"] = (up0, up1), (down0, down1)
    return loss[0, 0], dy, g


ANY = pl.BlockSpec(memory_space=pl.ANY)


def _place():
    x, y, c = lax.axis_index("x"), lax.axis_index("y"), lax.axis_index("c")
    chips = [(1 - x, y), (x, 1 - y), (1 - x, 1 - y)]
    return x, y, c, chips


def _remote(src, dst, send_sem, recv_sem, to):
    return pltpu.make_async_remote_copy(src_ref=src, dst_ref=dst, send_sem=send_sem, recv_sem=recv_sem,
                                        device_id=to, device_id_type=MESH)


def _gather_weights(shards):
    n = len(shards)

    def body(*refs):
        src, dst = refs[:n], refs[n:2 * n]
        send_sems, recv_sems, local_sems = refs[2 * n:]
        x, y, c, chips = _place()
        mine = 2 * x + y
        local = [pltpu.make_async_copy(src[t], dst[t].at[mine], local_sems.at[t]) for t in range(n)]
        for cp in local:
            cp.start()

        def half(t, slot, which):
            hr = shards[t].shape[0] // 2
            return dst[t].at[slot, pl.ds(which * hr, hr), :]

        def my_half(t):
            hr = shards[t].shape[0] // 2
            return src[t].at[pl.ds(c * hr, hr), :]

        sends = []
        for t in range(n):
            for j, (px, py) in enumerate(chips):
                cp = _remote(my_half(t), half(t, mine, c), send_sems.at[t, j], recv_sems.at[t, j], (px, py, c))
                cp.start()
                sends.append(cp)
        for j, (px, py) in enumerate(chips):
            for t in range(n):
                landed = half(t, 2 * px + py, c)
                _remote(landed, landed, send_sems.at[t, j], recv_sems.at[t, j], (px, py, c)).wait_recv()
                cp = _remote(landed, landed, send_sems.at[t, 3 + j], recv_sems.at[t, 3 + j], (x, y, 1 - c))
                cp.start()
                sends.append(cp)
        for j, (px, py) in enumerate(chips):
            for t in range(n):
                other = half(t, 2 * px + py, 1 - c)
                _remote(other, other, send_sems.at[t, 3 + j], recv_sems.at[t, 3 + j], (x, y, 1 - c)).wait_recv()
        for cp in sends:
            cp.wait_send()
        for cp in local:
            cp.wait()

    return pl.pallas_call(
        body, name="gather_weights",
        in_specs=[ANY] * n, out_specs=[ANY] * n,
        out_shape=[jax.ShapeDtypeStruct((N_CHIPS,) + s.shape, s.dtype) for s in shards],
        scratch_shapes=[pltpu.SemaphoreType.DMA((n, 6)), pltpu.SemaphoreType.DMA((n, 6)), pltpu.SemaphoreType.DMA((n,))],
        compiler_params=_params(has_side_effects=True),
    )(*shards)


def _pair_exchange(grads):
    n = len(grads)

    def body(*refs):
        src, dst = refs[:n], refs[n:2 * n]
        send_sems, recv_sems = refs[2 * n:]
        x, y, c, _ = _place()
        cps = []
        for t in range(n):
            hr = grads[t].shape[1] // 2
            cp = _remote(src[t].at[:, pl.ds((1 - c) * hr, hr), :], dst[t], send_sems.at[t], recv_sems.at[t], (x, y, 1 - c))
            cp.start()
            cps.append(cp)
        for cp in cps:
            cp.wait()

    return pl.pallas_call(
        body, name="grad_pair_exchange",
        in_specs=[ANY] * n, out_specs=[ANY] * n,
        out_shape=[jax.ShapeDtypeStruct((g.shape[0], g.shape[1] // 2, g.shape[2]), g.dtype) for g in grads],
        scratch_shapes=[pltpu.SemaphoreType.DMA((n,)), pltpu.SemaphoreType.DMA((n,))],
        compiler_params=_params(has_side_effects=True),
    )(*grads)


def _pair_add(g, got, cidx, name, tb=256):
    S, R, C = g.shape
    hr = R // 2
    tb = _rows_tile(hr, tb)
    nb = hr // tb

    def body(c_ref, a_ref, b_ref, o_ref):
        o_ref[...] = (a_ref[...] + b_ref[...]).astype(BF16)

    return pl.pallas_call(
        body, name=name,
        grid_spec=pltpu.PrefetchScalarGridSpec(
            num_scalar_prefetch=1, grid=(S, nb),
            in_specs=[pl.BlockSpec((None, tb, C), lambda s, i, c: (s, c[0] * nb + i, 0)),
                      pl.BlockSpec((None, tb, C), lambda s, i, c: (s, i, 0))],
            out_specs=pl.BlockSpec((None, tb, C), lambda s, i, c: (s, i, 0))),
        out_shape=jax.ShapeDtypeStruct((S, hr, C), BF16),
        compiler_params=_params("parallel", "parallel"),
    )(cidx, g, got)


def _rows_tile(n, want):
    t = min(n, want)
    while n % t or t % 8:
        t -= 8
    return t


def _chip_scatter(sums):
    n = len(sums)

    def body(*refs):
        src, dst = refs[:n], refs[n:2 * n]
        send_sems, recv_sems, local_sems = refs[2 * n:]
        x, y, c, chips = _place()
        mine = 2 * x + y
        local = [pltpu.make_async_copy(src[t].at[mine], dst[t].at[mine], local_sems.at[t]) for t in range(n)]
        for cp in local:
            cp.start()
        sends = []
        for t in range(n):
            for j, (px, py) in enumerate(chips):
                cp = _remote(src[t].at[2 * px + py], dst[t].at[mine], send_sems.at[t, j], recv_sems.at[t, j], (px, py, c))
                cp.start()
                sends.append(cp)
        for t in range(n):
            for j, (px, py) in enumerate(chips):
                slot = dst[t].at[2 * px + py]
                _remote(slot, slot, send_sems.at[t, j], recv_sems.at[t, j], (px, py, c)).wait_recv()
        for cp in sends:
            cp.wait_send()
        for cp in local:
            cp.wait()

    return pl.pallas_call(
        body, name="grad_chip_scatter",
        in_specs=[ANY] * n, out_specs=[ANY] * n,
        out_shape=[jax.ShapeDtypeStruct(s.shape, s.dtype) for s in sums],
        scratch_shapes=[pltpu.SemaphoreType.DMA((n, 3)), pltpu.SemaphoreType.DMA((n, 3)), pltpu.SemaphoreType.DMA((n,))],
        compiler_params=_params(has_side_effects=True),
    )(*sums)


def _chip_sum(parts, cidx, name, tb=256):
    S, hr, C = parts.shape
    tb = _rows_tile(hr, tb)
    nb = hr // tb

    def body(c_ref, *refs):
        o_ref = refs[S]
        tot = refs[0][...].astype(F32)
        for s in range(1, S):
            tot = tot + refs[s][...].astype(F32)
        o_ref[...] = tot

    return pl.pallas_call(
        body, name=name,
        grid_spec=pltpu.PrefetchScalarGridSpec(
            num_scalar_prefetch=1, grid=(nb,),
            in_specs=[pl.BlockSpec((None, tb, C), functools.partial(lambda s, i, c: (s, i, 0), s)) for s in range(S)],
            out_specs=pl.BlockSpec((tb, C), lambda i, c: (c[0] * nb + i, 0))),
        out_shape=jax.ShapeDtypeStruct((2 * hr, C), F32),
        compiler_params=_params("parallel"),
    )(cidx, *([parts] * S))


def _half_exchange(halves):
    n = len(halves)

    def body(*refs):
        dst = refs[n:2 * n]
        send_sems, recv_sems = refs[2 * n:]
        x, y, c, _ = _place()
        cps = []
        for t in range(n):
            hr = halves[t].shape[0] // 2
            rows = dst[t].at[pl.ds(c * hr, hr), :]
            cp = _remote(rows, rows, send_sems.at[t], recv_sems.at[t], (x, y, 1 - c))
            cp.start()
            cps.append(cp)
        for t, cp in enumerate(cps):
            cp.wait_send()
            hr = halves[t].shape[0] // 2
            other = dst[t].at[pl.ds((1 - c) * hr, hr), :]
            _remote(other, other, send_sems.at[t], recv_sems.at[t], (x, y, 1 - c)).wait_recv()

    return pl.pallas_call(
        body, name="grad_half_exchange",
        in_specs=[ANY] * n, out_specs=[ANY] * n,
        out_shape=[jax.ShapeDtypeStruct(h.shape, h.dtype) for h in halves],
        input_output_aliases={t: t for t in range(n)},
        scratch_shapes=[pltpu.SemaphoreType.DMA((n,)), pltpu.SemaphoreType.DMA((n,))],
        compiler_params=_params(has_side_effects=True),
    )(*halves)


def _adamw_math(w, g, m, v):
    m = ADAM_B1 * m + (1.0 - ADAM_B1) * g
    v = ADAM_B2 * v + (1.0 - ADAM_B2) * (g * g)
    m_hat = m / (1.0 - ADAM_B1 ** ADAM_STEP)
    v_hat = v / (1.0 - ADAM_B2 ** ADAM_STEP)
    delta = -ADAM_LR * (m_hat / (jnp.sqrt(v_hat) + ADAM_EPS) + ADAM_WD * w)
    return delta, m, v


def _adamw(w, g, m, v, name, tb=256):
    R, C = w.shape
    tb = _rows_tile(R, tb)

    def body(w_ref, g_ref, m_ref, v_ref, d_ref, mo_ref, vo_ref):
        d, mn, vn = _adamw_math(w_ref[...], g_ref[...], m_ref[...], v_ref[...])
        d_ref[...] = d
        mo_ref[...] = mn
        vo_ref[...] = vn

    row = pl.BlockSpec((tb, C), lambda i: (i, 0))
    return pl.pallas_call(
        body, name=name, grid=(R // tb,),
        in_specs=[row] * 4, out_specs=[row] * 3,
        out_shape=[jax.ShapeDtypeStruct((R, C), F32)] * 3,
        compiler_params=_params("parallel"),
    )(w, g, m, v)


N_DEV = 8


def _small_update(g, w, m, v):
    P = g.shape[0]

    def body(g_ref, w_ref, m_ref, v_ref, go_ref, d_ref, mo_ref, vo_ref, buf, send_sems, recv_sems):
        x, y, c, _ = _place()
        me = 4 * x + 2 * y + c
        buf[me] = g_ref[...]
        cps = []
        for k in range(1, N_DEV):
            fx, fy, fc = (k >> 2) & 1, (k >> 1) & 1, k & 1
            px = (1 - x) if fx else x
            py = (1 - y) if fy else y
            pc = (1 - c) if fc else c
            cp = _remote(g_ref, buf.at[me], send_sems.at[k - 1], recv_sems.at[k - 1], (px, py, pc))
            cp.start()
            cps.append((cp, 4 * px + 2 * py + pc))
        for k, (cp, peer) in enumerate(cps):
            _remote(g_ref, buf.at[peer], send_sems.at[k], recv_sems.at[k], (x, y, c)).wait_recv()
        for cp, _ in cps:
            cp.wait_send()
        tot = buf[0]
        for d in range(1, N_DEV):
            tot = tot + buf[d]
        go_ref[...] = tot
        dl, mn, vn = _adamw_math(w_ref[...], tot, m_ref[...], v_ref[...])
        d_ref[...] = dl
        mo_ref[...] = mn
        vo_ref[...] = vn

    vm = pl.BlockSpec(memory_space=pltpu.VMEM)
    return pl.pallas_call(
        body, name="small_params_update",
        in_specs=[vm] * 4, out_specs=[vm] * 4,
        out_shape=[jax.ShapeDtypeStruct((P, HEAD_DIM), F32)] * 4,
        scratch_shapes=[pltpu.VMEM((N_DEV, P, HEAD_DIM), F32), pltpu.SemaphoreType.DMA((N_DEV - 1,)),
                        pltpu.SemaphoreType.DMA((N_DEV - 1,))],
        compiler_params=_params(has_side_effects=True),
    )(g, w, m, v)


SMALL = ("fox_b_f", "fox_q_gain", "fox_k_gain", "dil_q_gain", "dil_k_gain", "mix_norm_g", "mlp_norm_g")
LARGE = ("fox_w_in", "fox_w_out", "dil_w_in", "dil_w_out", "mlp_w_up", "mlp_w_down")
WEIGHTS = ("fox_w_in", "fox_b_f", "fox_q_gain", "fox_k_gain", "fox_w_out", "dil_w_in", "dil_q_gain", "dil_k_gain",
           "dil_w_out", "mix_norm_g", "mlp_norm_g", "mlp_w_up", "mlp_w_down")


def _pack(parts):
    rows = []
    for a in parts:
        flat = a.reshape(-1)
        n = -(-flat.shape[0] // (8 * HEAD_DIM)) * (8 * HEAD_DIM)
        rows.append(jnp.pad(flat, (0, n - flat.shape[0])).reshape(-1, HEAD_DIM))
    return jnp.concatenate(rows, axis=0)


def _unpack(packed, like):
    out, r = [], 0
    for a in like:
        size = int(np.prod(a.shape))
        n = -(-size // (8 * HEAD_DIM)) * 8
        out.append(packed[r:r + n].reshape(-1)[:size].reshape(a.shape))
        r += n
    return out


def _pad_lanes(a):
    return jnp.pad(a, [(0, 0)] * (a.ndim - 1) + [(0, HEAD_DIM - a.shape[-1])])


def _as_shards(a):
    return a.reshape(N_CHIPS, a.shape[0] // N_CHIPS, a.shape[1])


def kernel(x, fox_w_in, fox_b_f, fox_q_gain, fox_k_gain, fox_w_out, dil_w_in, dil_q_gain, dil_k_gain, dil_w_out, mix_norm_g, mlp_norm_g, mlp_w_up, mlp_w_down, loss_target, m_fox_w_in, m_fox_b_f, m_fox_q_gain, m_fox_k_gain, m_fox_w_out, m_dil_w_in, m_dil_q_gain, m_dil_k_gain, m_dil_w_out, m_mix_norm_g, m_mlp_norm_g, m_mlp_w_up, m_mlp_w_down, v_fox_w_in, v_fox_b_f, v_fox_q_gain, v_fox_k_gain, v_fox_w_out, v_dil_w_in, v_dil_q_gain, v_dil_k_gain, v_dil_w_out, v_mix_norm_g, v_mlp_norm_g, v_mlp_w_up, v_mlp_w_down):
    wts = dict(fox_w_in=fox_w_in, fox_b_f=fox_b_f, fox_q_gain=fox_q_gain, fox_k_gain=fox_k_gain, fox_w_out=fox_w_out,
               dil_w_in=dil_w_in, dil_q_gain=dil_q_gain, dil_k_gain=dil_k_gain, dil_w_out=dil_w_out,
               mix_norm_g=mix_norm_g, mlp_norm_g=mlp_norm_g, mlp_w_up=mlp_w_up, mlp_w_down=mlp_w_down)
    mom1 = dict(fox_w_in=m_fox_w_in, fox_b_f=m_fox_b_f, fox_q_gain=m_fox_q_gain, fox_k_gain=m_fox_k_gain,
                fox_w_out=m_fox_w_out, dil_w_in=m_dil_w_in, dil_q_gain=m_dil_q_gain, dil_k_gain=m_dil_k_gain,
                dil_w_out=m_dil_w_out, mix_norm_g=m_mix_norm_g, mlp_norm_g=m_mlp_norm_g, mlp_w_up=m_mlp_w_up,
                mlp_w_down=m_mlp_w_down)
    mom2 = dict(fox_w_in=v_fox_w_in, fox_b_f=v_fox_b_f, fox_q_gain=v_fox_q_gain, fox_k_gain=v_fox_k_gain,
                fox_w_out=v_fox_w_out, dil_w_in=v_dil_w_in, dil_q_gain=v_dil_q_gain, dil_k_gain=v_dil_k_gain,
                dil_w_out=v_dil_w_out, mix_norm_g=v_mix_norm_g, mlp_norm_g=v_mlp_norm_g, mlp_w_up=v_mlp_w_up,
                mlp_w_down=v_mlp_w_down)
    T, D = x.shape[1], x.shape[2]
    H = D // HEAD_DIM
    cidx = lax.axis_index("c").astype(jnp.int32).reshape(1)

    def shards_of(d):
        return [d["fox_w_in"][0], d["fox_w_out"][0], d["dil_w_in"][0], d["dil_w_out"][0],
                d["mlp_w_up"][0], d["mlp_w_up"][1], d["mlp_w_down"][0], d["mlp_w_down"][1]]

    w_sh, m_sh, v_sh = shards_of(wts), shards_of(mom1), shards_of(mom2)
    full = _gather_weights([s.astype(BF16) for s in w_sh])
    fox_in = jnp.moveaxis(full[0], 0, 1).reshape(D, -1)
    w = dict(
        fox_qkv=fox_in[:, :3 * D], fox_f=_pad_lanes(fox_in[:, 3 * D:]), fox_b=_pad_lanes(fox_b_f),
        fox_gains=jnp.stack([fox_q_gain, fox_k_gain]), fox_out=full[1].reshape(D, D),
        dil_in=full[2], dil_gains=jnp.concatenate([dil_q_gain[0], dil_k_gain[0]])[:, None, :], dil_out=full[3].reshape(D, D),
        up=[full[4], full[5]], down=[full[6].reshape(-1, D), full[7].reshape(-1, D)],
        mix_g=[mix_norm_g[0:1], mix_norm_g[1:2]], mlp_g=[mlp_norm_g[0:1], mlp_norm_g[1:2]])

    loss, grad_x, g = _local_step(x.reshape(T, D), loss_target.reshape(T, D), w)
    loss = lax.psum(loss, ("x", "y", "c"))

    g_fox_in = jnp.concatenate([g["fox_qkv"], g["fox_f"][:, :H]], axis=1)
    g_fox_in = jnp.moveaxis(g_fox_in.reshape(D, N_CHIPS, -1), 1, 0)
    stacked = [g_fox_in, _as_shards(g["fox_out"]), g["dil_in"], _as_shards(g["dil_out"]),
               g["up"][0], g["up"][1], _as_shards(g["down"][0]), _as_shards(g["down"][1])]
    got = _pair_exchange(stacked)
    sums = [_pair_add(a, b, cidx, f"grad_pair_add_{t}") for t, (a, b) in enumerate(zip(stacked, got))]
    parts = _chip_scatter(sums)
    halves = [_chip_sum(p, cidx, f"grad_chip_sum_{t}") for t, p in enumerate(parts)]
    totals = _half_exchange(halves)
    upd = [_adamw(w_sh[t], totals[t], m_sh[t], v_sh[t], f"adamw_{t}") for t in range(len(totals))]

    def large(k):
        a = [totals[t] if k == 0 else upd[t][k - 1] for t in range(len(totals))]
        return dict(fox_w_in=a[0][None], fox_w_out=a[1][None], dil_w_in=a[2][None], dil_w_out=a[3][None],
                    mlp_w_up=jnp.stack([a[4], a[5]]), mlp_w_down=jnp.stack([a[6], a[7]]))

    small_like = [wts[n] for n in SMALL]
    g_small = [g["fox_b"][:, :H], g["fox_gains"][0], g["fox_gains"][1], g["dil_gains"][:3, 0][None], g["dil_gains"][3:, 0][None],
               jnp.concatenate(g["mix_g"]), jnp.concatenate(g["mlp_g"])]
    packed = _small_update(_pack(g_small), _pack(small_like), _pack([mom1[n] for n in SMALL]), _pack([mom2[n] for n in SMALL]))
    small = [dict(zip(SMALL, _unpack(p, small_like))) for p in packed]

    outs = [loss, grad_x.reshape(x.shape)]
    for k in range(4):
        big = large(k)
        outs += [big[n] if n in big else small[k][n] for n in WEIGHTS]
    return tuple(outs)
```

```python
import functools
from typing import Callable, NamedTuple

import numpy as np
import jax
import jax.numpy as jnp
from jax import lax
from jax.experimental import pallas as pl
from jax.experimental.pallas import tpu as pltpu

F32 = jnp.float32
BF16 = jnp.bfloat16

HEAD_DIM = 128
DIL_PATTERNS = ((128, 1), (512, 4), (2048, 16))
DIL_SPAN = 128
ALIBI_MAX_EXP = 8.0
EPS = 1e-6
MASKED = -1e30

ADAM_LR = 0.001
ADAM_B1 = 0.9
ADAM_B2 = 0.999
ADAM_EPS = 1e-08
ADAM_WD = 0.01
ADAM_STEP = 10

N_CHIPS = 4
VMEM_LIMIT_BYTES = 56 * 1024 * 1024
MESH = pl.DeviceIdType.MESH
ANY = pl.BlockSpec(memory_space=pl.ANY)

NN = (((1,), (0,)), ((), ()))
NT = (((1,), (1,)), ((), ()))
TN = (((0,), (0,)), ((), ()))


def _params(*sem, **kw):
    return pltpu.CompilerParams(dimension_semantics=sem or None, vmem_limit_bytes=VMEM_LIMIT_BYTES, **kw)


def _dot(a, b, dims):
    return lax.dot_general(a, b, dims, preferred_element_type=F32)


def _tile(n, want):
    if n <= want:
        return n
    t = want - want % 128
    while n % t:
        t -= 128
    return t


def _mm(a, b, M, N, K, *, mode, name, out_dtypes, b_stack=0, out_stack=0, extras=(), epilogue=None,
        tm=1024, tn=1024, tk=2048):
    per_b = per_o = None
    if b_stack:
        per_b = (K if mode == "nt" else N) // b_stack
    if out_stack:
        per_o = N // out_stack
    tm = _tile(M, tm)
    tn = _tile(min(x for x in (N, per_o, per_b if mode != "nt" else None) if x), tn)
    tk = _tile(min(x for x in (K, per_b if mode == "nt" else None) if x), tk)
    assert M % tm == 0 and N % tn == 0 and K % tk == 0, (name, M, N, K, tm, tn, tk)
    gk = K // tk
    if mode == "tn":
        a_spec = pl.BlockSpec((tk, tm), lambda i, j, k: (k, i))
    else:
        a_spec = pl.BlockSpec((tm, tk), lambda i, j, k: (i, k))
    if mode == "nt":
        if b_stack:
            npk = per_b // tk
            b_spec = pl.BlockSpec((None, tn, tk), lambda i, j, k: (k // npk, j, k % npk))
        else:
            b_spec = pl.BlockSpec((tn, tk), lambda i, j, k: (j, k))
    else:
        if b_stack:
            npj = per_b // tn
            b_spec = pl.BlockSpec((None, tk, tn), lambda i, j, k: (j // npj, k, j % npj))
        else:
            b_spec = pl.BlockSpec((tk, tn), lambda i, j, k: (k, j))
    if out_stack:
        npo = per_o // tn
        o_spec = pl.BlockSpec((None, tm, tn), lambda i, j, k: (j // npo, i, j % npo))
        o_shape = (out_stack, M, per_o)
    else:
        o_spec = pl.BlockSpec((tm, tn), lambda i, j, k: (i, j))
        o_shape = (M, N)
    e_spec = pl.BlockSpec((tm, tn), lambda i, j, k: (i, j))
    dims = {"nn": NN, "nt": NT, "tn": TN}[mode]
    ne, no = len(extras), len(out_dtypes)

    def body(a_ref, b_ref, *rest):
        ex, outs = rest[:ne], rest[ne:ne + no]
        k = pl.program_id(2)

        def product():
            return _dot(a_ref[...].astype(BF16), b_ref[...].astype(BF16), dims)

        def finish(r):
            res = epilogue(r, *[e[...] for e in ex]) if epilogue is not None else (r,)
            for o, v in zip(outs, res):
                o[...] = v.astype(o.dtype)

        if gk == 1:
            finish(product())
            return
        acc = rest[ne + no]

        @pl.when(k == 0)
        def _():
            acc[...] = product()

        @pl.when((k > 0) & (k < gk - 1))
        def _():
            acc[...] += product()

        @pl.when(k == gk - 1)
        def _():
            finish(acc[...] + product())

    outs = pl.pallas_call(
        body, name=name,
        grid=(M // tm, N // tn, gk),
        in_specs=[a_spec, b_spec] + [e_spec] * ne,
        out_specs=[o_spec] * no,
        out_shape=[jax.ShapeDtypeStruct(o_shape, d) for d in out_dtypes],
        scratch_shapes=[pltpu.VMEM((tm, tn), F32)] if gk > 1 else [],
        compiler_params=_params("parallel", "parallel", "arbitrary"),
    )(a, b, *extras)
    return outs[0] if no == 1 else outs


def _rms_fwd(x, g, name, tb=512):
    T, D = x.shape
    tb = min(tb, T)

    def body(x_ref, g_ref, o_ref):
        xv = x_ref[...]
        r = lax.rsqrt(jnp.mean(xv * xv, axis=-1, keepdims=True) + EPS)
        o_ref[...] = (xv * r * g_ref[...]).astype(BF16)

    return pl.pallas_call(
        body, name=name, grid=(T // tb,),
        in_specs=[pl.BlockSpec((tb, D), lambda i: (i, 0)), pl.BlockSpec((1, D), lambda i: (0, 0))],
        out_specs=pl.BlockSpec((tb, D), lambda i: (i, 0)),
        out_shape=jax.ShapeDtypeStruct((T, D), BF16),
        compiler_params=_params("parallel"),
    )(x, g)


def _rms_bwd(dy, x, g, dres, name, tb=256):
    T, D = x.shape
    tb = min(tb, T)

    def body(dy_ref, x_ref, g_ref, dres_ref, dx_ref, dxb_ref, dg_ref):
        i = pl.program_id(0)
        xv, dyv = x_ref[...], dy_ref[...]
        r = lax.rsqrt(jnp.mean(xv * xv, axis=-1, keepdims=True) + EPS)
        gy = dyv * g_ref[...]
        dx = r * gy - xv * (r * r * r) * jnp.mean(gy * xv, axis=-1, keepdims=True)
        tot = dres_ref[...] + dx
        dx_ref[...] = tot
        dxb_ref[...] = tot.astype(BF16)
        part = jnp.sum(dyv * (xv * r), axis=0, keepdims=True)

        @pl.when(i == 0)
        def _():
            dg_ref[...] = part

        @pl.when(i > 0)
        def _():
            dg_ref[...] += part

    row = pl.BlockSpec((tb, D), lambda i: (i, 0))
    vec = pl.BlockSpec((1, D), lambda i: (0, 0))
    return pl.pallas_call(
        body, name=name, grid=(T // tb,),
        in_specs=[row, row, vec, row],
        out_specs=[row, row, vec],
        out_shape=[jax.ShapeDtypeStruct((T, D), F32), jax.ShapeDtypeStruct((T, D), BF16),
                   jax.ShapeDtypeStruct((1, D), F32)],
        compiler_params=_params("arbitrary"),
    )(dy, x, g, dres)


def _loss_head(y, tgt, name, tb=256):
    T, D = y.shape
    tb = min(tb, T)

    def body(y_ref, t_ref, dy_ref, dyb_ref, loss_ref):
        i = pl.program_id(0)
        e = y_ref[...] - t_ref[...]
        d = e * (1.0 / D)
        dy_ref[...] = d
        dyb_ref[...] = d.astype(BF16)
        part = 0.5 * jnp.sum(jnp.sum(e * e, axis=1, keepdims=True) * (1.0 / D), axis=0, keepdims=True)

        @pl.when(i == 0)
        def _():
            loss_ref[...] = part

        @pl.when(i > 0)
        def _():
            loss_ref[...] += part

    row = pl.BlockSpec((tb, D), lambda i: (i, 0))
    return pl.pallas_call(
        body, name=name, grid=(T // tb,),
        in_specs=[row, row],
        out_specs=[row, row, pl.BlockSpec((1, 1), lambda i: (0, 0))],
        out_shape=[jax.ShapeDtypeStruct((T, D), F32), jax.ShapeDtypeStruct((T, D), BF16),
                   jax.ShapeDtypeStruct((1, 1), F32)],
        compiler_params=_params("arbitrary"),
    )(y, tgt)


def _head_rms(xh, g):
    r = lax.rsqrt(jnp.mean(xh * xh, axis=-1, keepdims=True) + EPS)
    return xh * r * g


def _qkv_prep(proj, gains, n_norm, gain_row, ch, name, n_scaled=0, post_scale=1.0, tb=512):
    T, W = proj.shape
    tb = min(tb, T)
    nch = W // ch
    nh = ch // HEAD_DIM

    def body(p_ref, g_ref, o_ref):
        j = pl.program_id(0)

        @pl.when(j < n_norm)
        def _():
            g = g_ref[...]
            if n_scaled:
                g = g * jnp.where(j < n_scaled, post_scale, 1.0)
            for h in range(nh):
                sl = slice(h * HEAD_DIM, (h + 1) * HEAD_DIM)
                o_ref[:, sl] = _head_rms(p_ref[:, sl], g).astype(BF16)

        @pl.when(j >= n_norm)
        def _():
            o_ref[...] = p_ref[...].astype(BF16)

    return pl.pallas_call(
        body, name=name, grid=(nch, T // tb),
        in_specs=[pl.BlockSpec((tb, ch), lambda j, i: (i, j)),
                  pl.BlockSpec((None, 1, HEAD_DIM), lambda j, i: (gain_row(j), 0, 0))],
        out_specs=pl.BlockSpec((tb, ch), lambda j, i: (i, j)),
        out_shape=jax.ShapeDtypeStruct((T, W), BF16),
        compiler_params=_params("parallel", "parallel"),
    )(proj, gains)


def _into(body, name, grid, in_specs, out_spec, out_shape, extra_out_specs, extra_out_shapes, buf, operands, sem):
    if buf is None:
        def kernel(*refs):
            body(*refs)
        ins, alias, ops = in_specs, {}, operands
    else:
        def kernel(_, *refs):
            body(*refs)
        ins = [pl.BlockSpec(memory_space=pl.ANY)] + in_specs
        alias, ops = {0: 0}, (buf,) + tuple(operands)
    return pl.pallas_call(
        kernel, name=name, grid=grid, in_specs=ins,
        out_specs=[out_spec] + extra_out_specs,
        out_shape=[out_shape] + extra_out_shapes,
        input_output_aliases=alias,
        compiler_params=_params(*sem),
    )(*ops)


def _head_rms_bwd_into(buf, W, d, proj, gain, off, ch, name, in_scale=1.0, tb=256):
    T, wd = d.shape
    tb = min(tb, T)
    n = wd // ch
    nh = ch // HEAD_DIM

    def body(d_ref, p_ref, g_ref, o_ref, dg_ref):
        i = pl.program_id(1)
        g = g_ref[...]
        part = jnp.zeros((1, HEAD_DIM), F32)
        for h in range(nh):
            sl = slice(h * HEAD_DIM, (h + 1) * HEAD_DIM)
            xh, dy = p_ref[:, sl], d_ref[:, sl]
            if in_scale != 1.0:
                dy = dy * in_scale
            r = lax.rsqrt(jnp.mean(xh * xh, axis=-1, keepdims=True) + EPS)
            gy = dy * g
            dx = r * gy - xh * (r * r * r) * jnp.mean(gy * xh, axis=-1, keepdims=True)
            o_ref[:, sl] = dx.astype(BF16)
            part = part + jnp.sum(dy * (xh * r), axis=0, keepdims=True)

        @pl.when(i == 0)
        def _():
            dg_ref[...] = part

        @pl.when(i > 0)
        def _():
            dg_ref[...] += part

    return _into(
        body, name, (n, T // tb),
        [pl.BlockSpec((tb, ch), lambda j, i: (i, j)), pl.BlockSpec((tb, ch), lambda j, i: (i, off + j)),
         pl.BlockSpec((1, HEAD_DIM), lambda j, i: (0, 0))],
        pl.BlockSpec((tb, ch), lambda j, i: (i, off + j)), jax.ShapeDtypeStruct((T, W), BF16),
        [pl.BlockSpec((None, 1, HEAD_DIM), lambda j, i: (j, 0, 0))], [jax.ShapeDtypeStruct((n, 1, HEAD_DIM), F32)],
        buf, (d, proj, gain), ("parallel", "arbitrary"))


def _sum_cast_into(buf, W, srcs, off, ch, name, tb=256):
    T, wd = srcs[0].shape
    tb = min(tb, T)
    n = wd // ch
    ns = len(srcs)

    def body(*refs):
        o_ref = refs[ns]
        tot = refs[0][...]
        for s in refs[1:ns]:
            tot = tot + s[...]
        o_ref[...] = tot.astype(BF16)

    out = _into(
        body, name, (n, T // tb),
        [pl.BlockSpec((tb, ch), lambda j, i: (i, j))] * ns,
        pl.BlockSpec((tb, ch), lambda j, i: (i, off + j)), jax.ShapeDtypeStruct((T, W), BF16),
        [], [], buf, tuple(srcs), ("parallel", "parallel"))
    return out[0]


def _tri(n, lower):
    r = lax.broadcasted_iota(jnp.int32, (n, n), 0)
    c = lax.broadcasted_iota(jnp.int32, (n, n), 1)
    return jnp.where((c <= r) if lower else (c >= r), 1.0, 0.0).astype(F32)


def _dot_exact(a, b):
    return lax.dot_general(a, b, NN, precision=lax.Precision.HIGHEST, preferred_element_type=F32)


def _log_sigmoid(z):
    return jnp.minimum(z, 0.0) - jnp.log(1.0 + jnp.exp(-jnp.abs(z)))


def _gate_fwd(f_raw, b_pad, hp, out_scale, name, blk=256):
    T = f_raw.shape[0]
    blk = min(blk, T)

    def body(f_ref, b_ref, c_ref):
        tri = _tri(blk, True)
        carry = jnp.zeros((1, HEAD_DIM), F32)
        for j in range(T // blk):
            lf = _log_sigmoid(f_ref[j * blk:(j + 1) * blk, :] + b_ref[...])
            cb = _dot_exact(tri, lf) + carry
            carry = cb[blk - 1:blk, :]
            c_ref[:, j * blk:(j + 1) * blk] = cb.T[:hp, :] * out_scale

    return pl.pallas_call(
        body, name=name,
        in_specs=[pl.BlockSpec(memory_space=pltpu.VMEM)] * 2,
        out_specs=pl.BlockSpec(memory_space=pltpu.VMEM),
        out_shape=jax.ShapeDtypeStruct((hp, T), F32),
        compiler_params=_params(),
    )(f_raw, b_pad)


def _gate_bwd(dc_rows, dc_cols, f_raw, b_pad, n_heads, hp, name, blk=256):
    T = f_raw.shape[0]
    blk = min(blk, T)

    def body(dc_ref, dcc_ref, f_ref, b_ref, dz_ref, db_ref):
        tri = _tri(blk, False)
        lane = lax.broadcasted_iota(jnp.int32, (blk, HEAD_DIM), 1)
        carry = jnp.zeros((1, HEAD_DIM), F32)
        db = jnp.zeros((1, HEAD_DIM), F32)
        for j in reversed(range(T // blk)):
            rows = dc_ref[:, j * blk:(j + 1) * blk]
            if hp < HEAD_DIM:
                rows = jnp.concatenate([rows, jnp.zeros((HEAD_DIM - hp, blk), F32)], axis=0)
            dlf = _dot_exact(tri, rows.T + dcc_ref[j * blk:(j + 1) * blk, :]) + carry
            carry = dlf[0:1, :]
            z = f_ref[j * blk:(j + 1) * blk, :] + b_ref[...]
            dz = jnp.where(lane < n_heads, dlf / (1.0 + jnp.exp(z)), 0.0)
            dz_ref[j * blk:(j + 1) * blk, :] = dz.astype(BF16)
            db = db + jnp.sum(dz, axis=0, keepdims=True)
        db_ref[...] = db

    return pl.pallas_call(
        body, name=name,
        in_specs=[pl.BlockSpec(memory_space=pltpu.VMEM)] * 4,
        out_specs=[pl.BlockSpec(memory_space=pltpu.VMEM)] * 2,
        out_shape=[jax.ShapeDtypeStruct((T, HEAD_DIM), BF16), jax.ShapeDtypeStruct((1, HEAD_DIM), F32)],
        compiler_params=_params(),
    )(dc_rows, dc_cols, f_raw, b_pad)


def _pairs(nb, key_major):
    if key_major:
        pairs = [(qi, ki) for ki in range(nb) for qi in range(ki, nb)]
    else:
        pairs = [(qi, ki) for qi in range(nb) for ki in range(qi + 1)]
    return (jnp.asarray(np.array([p[0] for p in pairs], np.int32)),
            jnp.asarray(np.array([p[1] for p in pairs], np.int32)))


LOG2E = 1.4426950408889634
LN2 = 0.6931471805599453
FOX_Q_SCALE = HEAD_DIM ** -0.5 * LOG2E


def _fox_logits(q, k, ck_row, diagonal):
    s = _dot(q, k, NT) - ck_row
    if diagonal:
        row = lax.broadcasted_iota(jnp.int32, s.shape, 0)
        col = lax.broadcasted_iota(jnp.int32, s.shape, 1)
        s = jnp.where(col <= row, s, MASKED)
    return s


def _fox_fwd(qkv, ck, H, name, tb=1024, rider=None):
    T = qkv.shape[0]
    tb = min(tb, T)
    nb = T // tb
    qt, kt = _pairs(nb, False)
    n_pairs = int(qt.shape[0])
    r_in, r_out, r_scr = (len(rider.operands), len(rider.out_shapes), len(rider.scratch)) if rider else (0, 0, 0)

    def body(qt_ref, kt_ref, q_ref, k_ref, v_ref, ck_ref, *rest):
        r_src, (o_ref, lse_ref), r_dst = rest[:r_in], rest[r_in:r_in + 2], rest[r_in + 2:r_in + 2 + r_out]
        m_sc, l_sc, acc_sc = rest[r_in + 2 + r_out:r_in + 5 + r_out]
        r_sems = rest[r_in + 5 + r_out:]
        p_ = pl.program_id(1)
        qi, ki = qt_ref[p_], kt_ref[p_]
        if rider:
            @pl.when((pl.program_id(0) == 0) & (p_ == 0))
            def _():
                rider.start(r_src, r_dst, r_sems)

        @pl.when(ki == 0)
        def _():
            m_sc[...] = jnp.full_like(m_sc, MASKED)
            l_sc[...] = jnp.zeros_like(l_sc)
            acc_sc[...] = jnp.zeros_like(acc_sc)

        def tile(diagonal):
            s = _fox_logits(q_ref[...], k_ref[...], ck_ref[...], diagonal)
            m_prev = m_sc[...]
            m_new = jnp.maximum(m_prev, jnp.max(s, axis=1, keepdims=True))
            alpha = jnp.exp2(m_prev - m_new)
            p = jnp.exp2(s - m_new[:, :1])
            l_sc[...] = alpha * l_sc[...] + jnp.sum(p, axis=1, keepdims=True)
            acc_sc[...] = alpha * acc_sc[...] + _dot(p.astype(BF16), v_ref[...], NN)
            m_sc[...] = m_new

        @pl.when(ki < qi)
        def _():
            tile(False)

        @pl.when(ki == qi)
        def _():
            tile(True)
            o_ref[...] = (acc_sc[...] / l_sc[...]).astype(BF16)
            lse_ref[...] = m_sc[...] + jnp.log(l_sc[...]) * LOG2E

        if rider:
            @pl.when((pl.program_id(0) == H - 1) & (p_ == n_pairs - 1))
            def _():
                rider.finish(r_src, r_dst, r_sems)

    blk = lambda f: pl.BlockSpec((tb, HEAD_DIM), f)
    outs = pl.pallas_call(
        body, name=name,
        grid_spec=pltpu.PrefetchScalarGridSpec(
            num_scalar_prefetch=2, grid=(H, n_pairs),
            in_specs=[blk(lambda h, p, qt, kt: (qt[p], h)),
                      blk(lambda h, p, qt, kt: (kt[p], H + h)),
                      blk(lambda h, p, qt, kt: (kt[p], 2 * H + h)),
                      pl.BlockSpec((None, 1, tb), lambda h, p, qt, kt: (h, 0, kt[p]))] + [ANY] * r_in,
            out_specs=[blk(lambda h, p, qt, kt: (qt[p], h)), blk(lambda h, p, qt, kt: (qt[p], h))] + [ANY] * r_out,
            scratch_shapes=[pltpu.VMEM((tb, HEAD_DIM), F32)] * 3 + (list(rider.scratch) if rider else [])),
        out_shape=[jax.ShapeDtypeStruct((T, H * HEAD_DIM), BF16), jax.ShapeDtypeStruct((T, H * HEAD_DIM), F32)]
        + (list(rider.out_shapes) if rider else []),
        compiler_params=_params("arbitrary", "arbitrary", has_side_effects=bool(rider)),
    )(qt, kt, qkv, qkv, qkv, ck, *(rider.operands if rider else ()))
    return outs[0], outs[1], list(outs[2:])


def _row_dot(do, o, nh, width, name, lane_per_head, tb=256):
    T = do.shape[0]
    tb = min(tb, T)
    wout = HEAD_DIM if lane_per_head else nh * HEAD_DIM

    def body(do_ref, o_ref, d_ref):
        lane = lax.broadcasted_iota(jnp.int32, (tb, HEAD_DIM), 1)
        tile = jnp.zeros((tb, HEAD_DIM), F32)
        for h in range(nh):
            sl = slice(h * width, (h + 1) * width)
            d = jnp.sum(do_ref[:, sl].astype(F32) * o_ref[:, sl].astype(F32), axis=1, keepdims=True)
            if lane_per_head:
                tile = jnp.where(lane == h, d, tile)
            else:
                d_ref[:, h * HEAD_DIM:(h + 1) * HEAD_DIM] = jnp.broadcast_to(d, (tb, HEAD_DIM))
        if lane_per_head:
            d_ref[...] = tile

    row = pl.BlockSpec((tb, nh * width), lambda i: (i, 0))
    return pl.pallas_call(
        body, name=name, grid=(T // tb,),
        in_specs=[row, row], out_specs=pl.BlockSpec((tb, wout), lambda i: (i, 0)),
        out_shape=jax.ShapeDtypeStruct((T, wout), F32),
        compiler_params=_params("parallel"),
    )(do, o)


def _fox_bwd(qkv, do, ck, lse, dd, H, hp, name, tb=1024):
    T = qkv.shape[0]
    tb = min(tb, T)
    nb = T // tb
    qt, kt = _pairs(nb, True)

    def body(qt_ref, kt_ref, q_ref, k_ref, v_ref, do_ref, ck_ref, lse_ref, dd_ref, dq_ref, dk_ref, dv_ref, dc_ref, dcq_ref):
        p_ = pl.program_id(1)
        qi, ki = qt_ref[p_], kt_ref[p_]

        @pl.when(p_ == 0)
        def _():
            dq_ref[...] = jnp.zeros_like(dq_ref)
            dcq_ref[...] = jnp.zeros_like(dcq_ref)

        @pl.when(qi == ki)
        def _():
            dk_ref[...] = jnp.zeros_like(dk_ref)
            dv_ref[...] = jnp.zeros_like(dv_ref)
            dc_ref[...] = jnp.zeros_like(dc_ref)

        rows = pl.ds(pl.multiple_of(qi * tb, tb), tb)

        def tile(diagonal):
            s = _fox_logits(q_ref[...], k_ref[...], ck_ref[...], diagonal)
            p = jnp.exp2(s - lse_ref[:, :1])
            dv_ref[...] += _dot(p.astype(BF16), do_ref[...], TN)
            dp = _dot(do_ref[...], v_ref[...], NT)
            ds = p * (dp - dd_ref[:, :1])
            dc_ref[...] -= jnp.sum(ds, axis=0, keepdims=True)
            dcq_ref[rows, :] += jnp.sum(ds, axis=1, keepdims=True)
            dsb = ds.astype(BF16)
            dq_ref[rows, :] += _dot(dsb, k_ref[...], NN)
            dk_ref[...] += _dot(dsb, q_ref[...], TN)

        @pl.when(ki < qi)
        def _():
            tile(False)

        @pl.when(ki == qi)
        def _():
            tile(True)

    blk = lambda f: pl.BlockSpec((tb, HEAD_DIM), f)
    at_q = lambda h, p, qt, kt: (qt[p], h)
    at_k = lambda h, p, qt, kt: (kt[p], h)
    crow = pl.BlockSpec((None, 1, tb), lambda h, p, qt, kt: (h, 0, kt[p]))
    whole = pl.BlockSpec((T, HEAD_DIM), lambda h, p, qt, kt: (0, h))
    wide = jax.ShapeDtypeStruct((T, H * HEAD_DIM), F32)
    return pl.pallas_call(
        body, name=name,
        grid_spec=pltpu.PrefetchScalarGridSpec(
            num_scalar_prefetch=2, grid=(H, qt.shape[0]),
            in_specs=[blk(at_q),
                      blk(lambda h, p, qt, kt: (kt[p], H + h)),
                      blk(lambda h, p, qt, kt: (kt[p], 2 * H + h)),
                      blk(at_q), crow, blk(at_q), blk(at_q)],
            out_specs=[whole, blk(at_k), blk(at_k), crow, whole]),
        out_shape=[wide, wide, wide, jax.ShapeDtypeStruct((hp, 1, T), F32), wide],
        compiler_params=_params("parallel", "arbitrary"),
    )(qt, kt, qkv, qkv, qkv, do, ck, lse, dd)


def _lane_per_head(wide, H, name, tb=256):
    T = wide.shape[0]
    tb = min(tb, T)

    def body(w_ref, o_ref):
        lane = lax.broadcasted_iota(jnp.int32, (tb, HEAD_DIM), 1)
        tile = jnp.zeros((tb, HEAD_DIM), F32)
        for h in range(H):
            tile = jnp.where(lane == h, w_ref[:, h * HEAD_DIM:(h + 1) * HEAD_DIM], tile)
        o_ref[...] = tile

    return pl.pallas_call(
        body, name=name, grid=(T // tb,),
        in_specs=[pl.BlockSpec((tb, H * HEAD_DIM), lambda i: (i, 0))],
        out_specs=pl.BlockSpec((tb, HEAD_DIM), lambda i: (i, 0)),
        out_shape=jax.ShapeDtypeStruct((T, HEAD_DIM), F32),
        compiler_params=_params("parallel"),
    )(wide)


def _slopes(n_groups, nh):
    n = n_groups * nh
    s = np.exp2(-ALIBI_MAX_EXP * np.arange(1, n + 1, dtype=np.float32) / np.float32(n)).astype(np.float32)
    return s.reshape(n_groups, nh)


def _window_logits(qh, kh, slope_r, prev, has_prev):
    qi = lax.broadcasted_iota(jnp.int32, (DIL_SPAN, DIL_SPAN), 0)
    kl = lax.broadcasted_iota(jnp.int32, (DIL_SPAN, DIL_SPAN), 1)
    delta = qi - kl + (DIL_SPAN if prev else 0)
    s = _dot(qh, kh, NT) * (HEAD_DIM ** -0.5) - slope_r * delta.astype(F32)
    valid = ((kl >= qi) & has_prev) if prev else (kl <= qi)
    return jnp.where(valid, s, MASKED)


def _dil_views(T, r, G, nh, dv):
    L = T // r
    C, V = nh * HEAD_DIM, nh * dv
    return L, C, V, 2 * G * C + V


def _dil_fwd(qkv, g, r, G, nh, dv, slopes, name):
    T = qkv.shape[0]
    L, C, V, W = _dil_views(T, r, G, nh, dv)
    nblk = L // DIL_SPAN
    nc, nv = W // C, W // V
    view = qkv.reshape(L, r * W)

    def body(q_ref, kp_ref, kc_ref, vp_ref, vc_ref, num_ref, m_ref, den_ref):
        has_prev = pl.program_id(1) > 0
        lane = lax.broadcasted_iota(jnp.int32, (DIL_SPAN, HEAD_DIM), 1)
        m_tile = jnp.zeros((DIL_SPAN, HEAD_DIM), F32)
        den_tile = jnp.ones((DIL_SPAN, HEAD_DIM), F32)
        for h in range(nh):
            sl = slice(h * HEAD_DIM, (h + 1) * HEAD_DIM)
            vs = slice(h * dv, (h + 1) * dv)
            sr = float(slopes[h]) * r
            sc = _window_logits(q_ref[:, sl], kc_ref[:, sl], sr, False, has_prev)
            sp = _window_logits(q_ref[:, sl], kp_ref[:, sl], sr, True, has_prev)
            m = jnp.maximum(jnp.max(sc, axis=1, keepdims=True), jnp.max(sp, axis=1, keepdims=True))
            pc, pp = jnp.exp(sc - m), jnp.exp(sp - m)
            den = jnp.sum(pc, axis=1, keepdims=True) + jnp.sum(pp, axis=1, keepdims=True)
            num_ref[:, vs] = _dot(pc.astype(BF16), vc_ref[:, vs], NN) + _dot(pp.astype(BF16), vp_ref[:, vs], NN)
            m_tile = jnp.where(lane == h, m, m_tile)
            den_tile = jnp.where(lane == h, den, den_tile)
        m_ref[...] = m_tile
        den_ref[...] = den_tile

    prev = lambda i: jnp.maximum(i - 1, 0)
    stat = pl.BlockSpec((DIL_SPAN, HEAD_DIM), lambda b, i: (i, b))
    num, m, den = pl.pallas_call(
        body, name=name, grid=(r, nblk),
        in_specs=[pl.BlockSpec((DIL_SPAN, C), lambda b, i: (i, b * nc + g)),
                  pl.BlockSpec((DIL_SPAN, C), lambda b, i: (prev(i), b * nc + G + g)),
                  pl.BlockSpec((DIL_SPAN, C), lambda b, i: (i, b * nc + G + g)),
                  pl.BlockSpec((DIL_SPAN, V), lambda b, i: (prev(i), b * nv + nv - 1)),
                  pl.BlockSpec((DIL_SPAN, V), lambda b, i: (i, b * nv + nv - 1))],
        out_specs=[pl.BlockSpec((DIL_SPAN, V), lambda b, i: (i, b)), stat, stat],
        out_shape=[jax.ShapeDtypeStruct((L, r * V), F32), jax.ShapeDtypeStruct((L, r * HEAD_DIM), F32),
                   jax.ShapeDtypeStruct((L, r * HEAD_DIM), F32)],
        compiler_params=_params("parallel", "parallel"),
    )(view, view, view, view, view)
    return num.reshape(T, V), m.reshape(T, HEAD_DIM), den.reshape(T, HEAD_DIM)


def _dil_merge(nums, ms, dens, nh, dv, name, tb=256):
    T, V = nums[0].shape
    tb = min(tb, T)
    G = len(nums)

    def body(*refs):
        num_r, m_r, den_r = refs[:G], refs[G:2 * G], refs[2 * G:3 * G]
        o_ref, lse_ref = refs[3 * G], refs[3 * G + 1]
        mm = m_r[0][...]
        for g in range(1, G):
            mm = jnp.maximum(mm, m_r[g][...])
        w = [jnp.exp(m_r[g][...] - mm) for g in range(G)]
        den = w[0] * den_r[0][...]
        for g in range(1, G):
            den = den + w[g] * den_r[g][...]
        lse_ref[...] = mm + jnp.log(den)
        for h in range(nh):
            vs = slice(h * dv, (h + 1) * dv)
            num = w[0][:, h:h + 1] * num_r[0][:, vs]
            for g in range(1, G):
                num = num + w[g][:, h:h + 1] * num_r[g][:, vs]
            o_ref[:, vs] = (num / den[:, h:h + 1]).astype(BF16)

    wide = pl.BlockSpec((tb, V), lambda i: (i, 0))
    stat = pl.BlockSpec((tb, HEAD_DIM), lambda i: (i, 0))
    return pl.pallas_call(
        body, name=name, grid=(T // tb,),
        in_specs=[wide] * G + [stat] * (2 * G),
        out_specs=[wide, stat],
        out_shape=[jax.ShapeDtypeStruct((T, V), BF16), jax.ShapeDtypeStruct((T, HEAD_DIM), F32)],
        compiler_params=_params("parallel"),
    )(*nums, *ms, *dens)


def _dil_dq(qkv, do, lse, dd, g, r, G, nh, dv, slopes, name):
    T = qkv.shape[0]
    L, C, V, W = _dil_views(T, r, G, nh, dv)
    nblk = L // DIL_SPAN
    nc, nv = W // C, W // V
    view = qkv.reshape(L, r * W)
    scale = HEAD_DIM ** -0.5

    def body(q_ref, kp_ref, kc_ref, vp_ref, vc_ref, do_ref, lse_ref, dd_ref, dq_ref):
        has_prev = pl.program_id(1) > 0
        for h in range(nh):
            sl = slice(h * HEAD_DIM, (h + 1) * HEAD_DIM)
            vs = slice(h * dv, (h + 1) * dv)
            sr = float(slopes[h]) * r
            lse_h, dd_h = lse_ref[:, h:h + 1], dd_ref[:, h:h + 1]
            acc = jnp.zeros((DIL_SPAN, HEAD_DIM), F32)
            for k_ref, v_ref, is_prev in ((kc_ref, vc_ref, False), (kp_ref, vp_ref, True)):
                s = _window_logits(q_ref[:, sl], k_ref[:, sl], sr, is_prev, has_prev)
                p = jnp.exp(s - lse_h)
                dp = _dot(do_ref[:, vs], v_ref[:, vs], NT)
                ds = (p * (dp - dd_h)).astype(BF16)
                acc = acc + _dot(ds, k_ref[:, sl], NN)
            dq_ref[:, sl] = scale * acc

    prev = lambda i: jnp.maximum(i - 1, 0)
    stat = pl.BlockSpec((DIL_SPAN, HEAD_DIM), lambda b, i: (i, b))
    dq = pl.pallas_call(
        body, name=name, grid=(r, nblk),
        in_specs=[pl.BlockSpec((DIL_SPAN, C), lambda b, i: (i, b * nc + g)),
                  pl.BlockSpec((DIL_SPAN, C), lambda b, i: (prev(i), b * nc + G + g)),
                  pl.BlockSpec((DIL_SPAN, C), lambda b, i: (i, b * nc + G + g)),
                  pl.BlockSpec((DIL_SPAN, V), lambda b, i: (prev(i), b * nv + nv - 1)),
                  pl.BlockSpec((DIL_SPAN, V), lambda b, i: (i, b * nv + nv - 1)),
                  pl.BlockSpec((DIL_SPAN, V), lambda b, i: (i, b)), stat, stat],
        out_specs=pl.BlockSpec((DIL_SPAN, C), lambda b, i: (i, b)),
        out_shape=jax.ShapeDtypeStruct((L, r * C), F32),
        compiler_params=_params("parallel", "parallel"),
    )(view, view, view, view, view, do.reshape(L, r * V), lse.reshape(L, r * HEAD_DIM), dd.reshape(L, r * HEAD_DIM))
    return dq.reshape(T, C)


def _dil_dkv(qkv, do, lse, dd, g, r, G, nh, dv, slopes, name):
    T = qkv.shape[0]
    L, C, V, W = _dil_views(T, r, G, nh, dv)
    nblk = L // DIL_SPAN
    nc, nv = W // C, W // V
    view = qkv.reshape(L, r * W)
    scale = HEAD_DIM ** -0.5

    def body(k_ref, v_ref, qc_ref, qn_ref, doc_ref, don_ref, lsec_ref, lsen_ref, ddc_ref, ddn_ref, dk_ref, dv_ref):
        has_next = pl.program_id(1) < nblk - 1
        for h in range(nh):
            sl = slice(h * HEAD_DIM, (h + 1) * HEAD_DIM)
            vs = slice(h * dv, (h + 1) * dv)
            sr = float(slopes[h]) * r
            dk = jnp.zeros((DIL_SPAN, HEAD_DIM), F32)
            dvh = jnp.zeros((DIL_SPAN, dv), F32)
            for q_ref, do_ref, lse_ref, dd_ref, is_next in ((qc_ref, doc_ref, lsec_ref, ddc_ref, False),
                                                          (qn_ref, don_ref, lsen_ref, ddn_ref, True)):
                s = _window_logits(q_ref[:, sl], k_ref[:, sl], sr, is_next, has_next)
                p = jnp.exp(s - lse_ref[:, h:h + 1])
                dvh = dvh + _dot(p.astype(BF16), do_ref[:, vs], TN)
                dp = _dot(do_ref[:, vs], v_ref[:, vs], NT)
                ds = (p * (dp - dd_ref[:, h:h + 1])).astype(BF16)
                dk = dk + _dot(ds, q_ref[:, sl], TN)
            dk_ref[:, sl] = scale * dk
            dv_ref[:, vs] = dvh

    nxt = lambda i: jnp.minimum(i + 1, nblk - 1)
    stat_c = pl.BlockSpec((DIL_SPAN, HEAD_DIM), lambda b, i: (i, b))
    stat_n = pl.BlockSpec((DIL_SPAN, HEAD_DIM), lambda b, i: (nxt(i), b))
    do_v, lse_v, dd_v = do.reshape(L, r * V), lse.reshape(L, r * HEAD_DIM), dd.reshape(L, r * HEAD_DIM)
    dk, dvv = pl.pallas_call(
        body, name=name, grid=(r, nblk),
        in_specs=[pl.BlockSpec((DIL_SPAN, C), lambda b, i: (i, b * nc + G + g)),
                  pl.BlockSpec((DIL_SPAN, V), lambda b, i: (i, b * nv + nv - 1)),
                  pl.BlockSpec((DIL_SPAN, C), lambda b, i: (i, b * nc + g)),
                  pl.BlockSpec((DIL_SPAN, C), lambda b, i: (nxt(i), b * nc + g)),
                  pl.BlockSpec((DIL_SPAN, V), lambda b, i: (i, b)),
                  pl.BlockSpec((DIL_SPAN, V), lambda b, i: (nxt(i), b)),
                  stat_c, stat_n, stat_c, stat_n],
        out_specs=[pl.BlockSpec((DIL_SPAN, C), lambda b, i: (i, b)), pl.BlockSpec((DIL_SPAN, V), lambda b, i: (i, b))],
        out_shape=[jax.ShapeDtypeStruct((L, r * C), F32), jax.ShapeDtypeStruct((L, r * V), F32)],
        compiler_params=_params("parallel", "parallel"),
    )(view, view, view, view, do_v, do_v, lse_v, lse_v, dd_v, dd_v)
    return dk.reshape(T, C), dvv.reshape(T, V)


def _relu2(r):
    a = jnp.maximum(r, 0.0)
    return (a * a,)


def _mlp_fwd(x, g, w_up, w_down, tag):
    T, D = x.shape
    F = w_down.shape[0]
    h = _rms_fwd(x, g, f"{tag}_norm")
    a2 = _mm(h, w_up, T, F, D, mode="nn", name=f"{tag}_up", b_stack=N_CHIPS, out_dtypes=(BF16,), epilogue=_relu2)
    y = _mm(a2, w_down, T, D, F, mode="nn", name=f"{tag}_down", out_dtypes=(F32,), extras=(x,),
            epilogue=lambda r, res: (res + r,))
    return y, (x, h, a2)


def _mlp_bwd(dy, dyb, saved, g, w_up, w_down, tag):
    x, h, a2 = saved
    T, D = x.shape
    F = w_down.shape[0]
    d_down = _mm(a2, dyb, F, D, T, mode="tn", name=f"{tag}_dwdown", out_dtypes=(F32,))
    du = _mm(dyb, w_down, T, F, D, mode="nt", name=f"{tag}_da", out_dtypes=(BF16,), extras=(a2,),
             epilogue=lambda r, sq: (r * (2.0 * jnp.sqrt(sq.astype(F32))),))
    d_up = _mm(h, du, D, F, T, mode="tn", name=f"{tag}_dwup", out_stack=N_CHIPS, out_dtypes=(F32,))
    dh = _mm(du, w_up, T, D, F, mode="nt", name=f"{tag}_dh", b_stack=N_CHIPS, out_dtypes=(F32,))
    dx, dxb, dg = _rms_bwd(dh, x, g, dy, f"{tag}_dnorm")
    return dx, dxb, dg, d_up, d_down


def _fox_dims(D):
    H = D // HEAD_DIM
    return H, max(8, H), (H // 2) * HEAD_DIM


def _fox_layer_fwd(x, g, w_qkv, w_f, b_pad, gains, w_out, rider=None):
    T, D = x.shape
    H, hp, ch = _fox_dims(D)
    h = _rms_fwd(x, g, "fox_norm")
    proj = _mm(h, w_qkv, T, 3 * D, D, mode="nn", name="fox_proj", out_dtypes=(F32,))
    f_raw = _mm(h, w_f, T, HEAD_DIM, D, mode="nn", name="fox_gate_proj", out_dtypes=(F32,))
    qkv = _qkv_prep(proj, gains, 4, lambda j: jnp.minimum(j // 2, 1), ch, "fox_qk_norm", n_scaled=2, post_scale=FOX_Q_SCALE)
    ck = _gate_fwd(f_raw, b_pad, hp, LOG2E, "fox_gate").reshape(hp, 1, T)
    o, lse, carried = _fox_fwd(qkv, ck, H, "fox_attn", rider=rider)
    y = _mm(o, w_out, T, D, D, mode="nn", name="fox_out", out_dtypes=(F32,), extras=(x,),
            epilogue=lambda r, res: (res + r,))
    return y, (x, h, proj, f_raw, qkv, ck, o, lse), carried


def _fox_layer_bwd(dy, dyb, saved, g, w_qkv, w_f, b_pad, gains, w_out):
    x, h, proj, f_raw, qkv, ck, o, lse = saved
    T, D = x.shape
    H, hp, ch = _fox_dims(D)
    d_out = _mm(o, dyb, D, D, T, mode="tn", name="fox_dwout", out_dtypes=(F32,))
    do = _mm(dyb, w_out, T, D, D, mode="nt", name="fox_do", out_dtypes=(BF16,))
    dd = _row_dot(do, o, H, HEAD_DIM, "fox_rowdot", False)
    dq, dk, dv, dck, dcq = _fox_bwd(qkv, do, ck, lse, dd, H, hp, "fox_attn_bwd")
    dcq = _lane_per_head(dcq, H, "fox_dc_query")
    dproj, dgq = _head_rms_bwd_into(None, 3 * D, dq, proj, gains[0], 0, ch, "fox_dq_norm", in_scale=HEAD_DIM ** -0.5)
    dproj, dgk = _head_rms_bwd_into(dproj, 3 * D, dk, proj, gains[1], 2, ch, "fox_dk_norm", in_scale=LN2)
    dproj = _sum_cast_into(dproj, 3 * D, [dv], 4, ch, "fox_dv_cast")
    dz, db = _gate_bwd(dck.reshape(hp, T), dcq, f_raw, b_pad, H, hp, "fox_gate_bwd")
    d_qkv = _mm(h, dproj, D, 3 * D, T, mode="tn", name="fox_dwqkv", out_dtypes=(F32,))
    d_f = _mm(h, dz, D, HEAD_DIM, T, mode="tn", name="fox_dwgate", out_dtypes=(F32,))
    dh = _mm(dproj, w_qkv, T, D, 3 * D, mode="nt", name="fox_dh", out_dtypes=(F32,))
    dh = _mm(dz, w_f, T, D, HEAD_DIM, mode="nt", name="fox_dh_gate", out_dtypes=(F32,), extras=(dh,),
             epilogue=lambda r, e: (e + r,))
    dx, dxb, dg = _rms_bwd(dh, x, g, dy, "fox_dnorm")
    dgains = jnp.stack([dgq.sum(axis=0), dgk.sum(axis=0)])
    return dx, dxb, dg, d_qkv, d_f, db, dgains, d_out


def _dil_dims(D):
    nh = D // (2 * HEAD_DIM)
    return nh, D // nh, len(DIL_PATTERNS)


def _dil_layer_fwd(x, g, w_in, gains, w_out):
    T, D = x.shape
    nh, dv, G = _dil_dims(D)
    C = nh * HEAD_DIM
    W = 2 * G * C + nh * dv
    slopes = _slopes(G, nh)
    h = _rms_fwd(x, g, "dil_norm")
    proj = _mm(h, w_in, T, W, D, mode="nn", name="dil_proj", b_stack=N_CHIPS, out_dtypes=(F32,))
    qkv = _qkv_prep(proj, gains, 2 * G, lambda j: jnp.minimum(j, 2 * G - 1), C, "dil_qk_norm")
    parts = [_dil_fwd(qkv, gi, r, G, nh, dv, slopes[gi], f"dil_attn_g{gi}") for gi, (_, r) in enumerate(DIL_PATTERNS)]
    o, lse = _dil_merge([p[0] for p in parts], [p[1] for p in parts], [p[2] for p in parts], nh, dv, "dil_merge")
    y = _mm(o, w_out, T, D, D, mode="nn", name="dil_out", out_dtypes=(F32,), extras=(x,),
            epilogue=lambda r, res: (res + r,))
    return y, (x, h, proj, qkv, o, lse)


def _dil_layer_bwd(dy, dyb, saved, g, w_in, gains, w_out):
    x, h, proj, qkv, o, lse = saved
    T, D = x.shape
    nh, dv, G = _dil_dims(D)
    C = nh * HEAD_DIM
    W = 2 * G * C + nh * dv
    slopes = _slopes(G, nh)
    d_out = _mm(o, dyb, D, D, T, mode="tn", name="dil_dwout", out_dtypes=(F32,))
    do = _mm(dyb, w_out, T, D, D, mode="nt", name="dil_do", out_dtypes=(BF16,))
    dd = _row_dot(do, o, nh, dv, "dil_rowdot", True)
    dproj, dgs, dvs = None, [None] * (2 * G), []
    for gi, (_, r) in enumerate(DIL_PATTERNS):
        dq = _dil_dq(qkv, do, lse, dd, gi, r, G, nh, dv, slopes[gi], f"dil_dq_g{gi}")
        dk, dvg = _dil_dkv(qkv, do, lse, dd, gi, r, G, nh, dv, slopes[gi], f"dil_dkv_g{gi}")
        dvs.append(dvg)
        dproj, dgs[gi] = _head_rms_bwd_into(dproj, W, dq, proj, gains[gi], gi, C, f"dil_dq_norm_g{gi}")
        dproj, dgs[G + gi] = _head_rms_bwd_into(dproj, W, dk, proj, gains[G + gi], G + gi, C, f"dil_dk_norm_g{gi}")
    dproj = _sum_cast_into(dproj, W, dvs, 2 * G, C, "dil_dv_cast")
    d_in = _mm(h, dproj, D, W, T, mode="tn", name="dil_dwin", out_stack=N_CHIPS, out_dtypes=(F32,))
    dh = _mm(dproj, w_in, T, D, W, mode="nt", name="dil_dh", b_stack=N_CHIPS, out_dtypes=(F32,))
    dx, dxb, dg = _rms_bwd(dh, x, g, dy, "dil_dnorm")
    dgains = jnp.concatenate(dgs, axis=0)
    return dx, dxb, dg, d_in, dgains, d_out


def _local_step(x, tgt, w, late_shards=None, late_weights=None):
    rider = _gather_ici_rider(late_shards) if late_shards is not None else None
    y0, s_fox, landed = _fox_layer_fwd(x, w["mix_g"][0], w["fox_qkv"], w["fox_f"], w["fox_b"], w["fox_gains"], w["fox_out"],
                                       rider=rider)
    if rider:
        w = {**w, **late_weights(_forward_halves(landed))}
    y1, s_mlp0 = _mlp_fwd(y0, w["mlp_g"][0], w["up"][0], w["down"][0], "mlp0")
    y2, s_dil = _dil_layer_fwd(y1, w["mix_g"][1], w["dil_in"], w["dil_gains"], w["dil_out"])
    y3, s_mlp1 = _mlp_fwd(y2, w["mlp_g"][1], w["up"][1], w["down"][1], "mlp1")
    dy, dyb, loss = _loss_head(y3, tgt, "loss_head")
    g = {}
    dy, dyb, g_mlp1, up1, down1 = _mlp_bwd(dy, dyb, s_mlp1, w["mlp_g"][1], w["up"][1], w["down"][1], "mlp1")
    dy, dyb, g_mix1, g["dil_in"], g["dil_gains"], g["dil_out"] = _dil_layer_bwd(
        dy, dyb, s_dil, w["mix_g"][1], w["dil_in"], w["dil_gains"], w["dil_out"])
    dy, dyb, g_mlp0, up0, down0 = _mlp_bwd(dy, dyb, s_mlp0, w["mlp_g"][0], w["up"][0], w["down"][0], "mlp0")
    dy, dyb, g_mix0, g["fox_qkv"], g["fox_f"], g["fox_b"], g["fox_gains"], g["fox_out"] = _fox_layer_bwd(
        dy, dyb, s_fox, w["mix_g"][0], w["fox_qkv"], w["fox_f"], w["fox_b"], w["fox_gains"], w["fox_out"])
    g["mix_g"], g["mlp_g"] = (g_mix0, g_mix1), (g_mlp0, g_mlp1)
    g["up"], g["down"] = (up0, up1), (down0, down1)
    return loss[0, 0], dy, g


def _place():
    x, y, c = lax.axis_index("x"), lax.axis_index("y"), lax.axis_index("c")
    chips = [(1 - x, y), (x, 1 - y), (1 - x, 1 - y)]
    return x, y, c, chips


def _remote(src, dst, send_sem, recv_sem, to):
    return pltpu.make_async_remote_copy(src_ref=src, dst_ref=dst, send_sem=send_sem, recv_sem=recv_sem,
                                        device_id=to, device_id_type=MESH)


def _gather_weights(shards):
    n = len(shards)

    def body(*refs):
        src, dst = refs[:n], refs[n:2 * n]
        send_sems, recv_sems, local_sems = refs[2 * n:]
        x, y, c, chips = _place()
        mine = 2 * x + y
        local = [pltpu.make_async_copy(src[t], dst[t].at[mine], local_sems.at[t]) for t in range(n)]
        for cp in local:
            cp.start()

        def half(t, slot, which):
            hr = shards[t].shape[0] // 2
            return dst[t].at[slot, pl.ds(which * hr, hr), :]

        def my_half(t):
            hr = shards[t].shape[0] // 2
            return src[t].at[pl.ds(c * hr, hr), :]

        sends = []
        for t in range(n):
            for j, (px, py) in enumerate(chips):
                cp = _remote(my_half(t), half(t, mine, c), send_sems.at[t, j], recv_sems.at[t, j], (px, py, c))
                cp.start()
                sends.append(cp)
        for j, (px, py) in enumerate(chips):
            for t in range(n):
                landed = half(t, 2 * px + py, c)
                _remote(landed, landed, send_sems.at[t, j], recv_sems.at[t, j], (px, py, c)).wait_recv()
                cp = _remote(landed, landed, send_sems.at[t, 3 + j], recv_sems.at[t, 3 + j], (x, y, 1 - c))
                cp.start()
                sends.append(cp)
        for j, (px, py) in enumerate(chips):
            for t in range(n):
                other = half(t, 2 * px + py, 1 - c)
                _remote(other, other, send_sems.at[t, 3 + j], recv_sems.at[t, 3 + j], (x, y, 1 - c)).wait_recv()
        for cp in sends:
            cp.wait_send()
        for cp in local:
            cp.wait()

    return pl.pallas_call(
        body, name="gather_weights",
        in_specs=[ANY] * n, out_specs=[ANY] * n,
        out_shape=[jax.ShapeDtypeStruct((N_CHIPS,) + s.shape, s.dtype) for s in shards],
        scratch_shapes=[pltpu.SemaphoreType.DMA((n, 6)), pltpu.SemaphoreType.DMA((n, 6)), pltpu.SemaphoreType.DMA((n,))],
        compiler_params=_params(has_side_effects=True),
    )(*shards)


class _Rider(NamedTuple):
    operands: tuple
    out_shapes: tuple
    scratch: tuple
    start: Callable
    finish: Callable


def _gather_ici_rider(shards):
    n = len(shards)

    def copies(src, dst, sems):
        send_sems, recv_sems, local_sems = sems
        x, y, c, chips = _place()
        mine = 2 * x + y
        local, sends, recvs = [], [], []
        for t in range(n):
            hr = shards[t].shape[0] // 2
            local.append(pltpu.make_async_copy(src[t], dst[t].at[mine], local_sems.at[t]))
            for j, (px, py) in enumerate(chips):
                sends.append(_remote(src[t].at[pl.ds(c * hr, hr), :], dst[t].at[mine, pl.ds(c * hr, hr), :],
                                     send_sems.at[t, j], recv_sems.at[t, j], (px, py, c)))
                landed = dst[t].at[2 * px + py, pl.ds(c * hr, hr), :]
                recvs.append(_remote(landed, landed, send_sems.at[t, j], recv_sems.at[t, j], (px, py, c)))
        return local, sends, recvs

    def start(src, dst, sems):
        local, sends, _ = copies(src, dst, sems)
        for cp in local + sends:
            cp.start()

    def finish(src, dst, sems):
        local, sends, recvs = copies(src, dst, sems)
        for cp in recvs:
            cp.wait_recv()
        for cp in sends:
            cp.wait_send()
        for cp in local:
            cp.wait()

    return _Rider(tuple(shards), tuple(jax.ShapeDtypeStruct((N_CHIPS,) + s.shape, s.dtype) for s in shards),
                  (pltpu.SemaphoreType.DMA((n, 3)), pltpu.SemaphoreType.DMA((n, 3)), pltpu.SemaphoreType.DMA((n,))),
                  start, finish)


def _forward_halves(landed):
    n = len(landed)

    def body(*refs):
        dst = refs[n:2 * n]
        send_sems, recv_sems = refs[2 * n:]
        x, y, c, chips = _place()
        sends = []
        for t in range(n):
            hr = landed[t].shape[1] // 2
            for j, (px, py) in enumerate(chips):
                got = dst[t].at[2 * px + py, pl.ds(c * hr, hr), :]
                cp = _remote(got, got, send_sems.at[t, j], recv_sems.at[t, j], (x, y, 1 - c))
                cp.start()
                sends.append(cp)
        for t in range(n):
            hr = landed[t].shape[1] // 2
            for j, (px, py) in enumerate(chips):
                other = dst[t].at[2 * px + py, pl.ds((1 - c) * hr, hr), :]
                _remote(other, other, send_sems.at[t, j], recv_sems.at[t, j], (x, y, 1 - c)).wait_recv()
        for cp in sends:
            cp.wait_send()

    return pl.pallas_call(
        body, name="gather_forward_halves",
        in_specs=[ANY] * n, out_specs=[ANY] * n,
        out_shape=[jax.ShapeDtypeStruct(a.shape, a.dtype) for a in landed],
        input_output_aliases={t: t for t in range(n)},
        scratch_shapes=[pltpu.SemaphoreType.DMA((n, 3)), pltpu.SemaphoreType.DMA((n, 3))],
        compiler_params=_params(has_side_effects=True),
    )(*landed)


def _pair_exchange(grads):
    n = len(grads)

    def body(*refs):
        src, dst = refs[:n], refs[n:2 * n]
        send_sems, recv_sems = refs[2 * n:]
        x, y, c, _ = _place()
        cps = []
        for t in range(n):
            hr = grads[t].shape[1] // 2
            cp = _remote(src[t].at[:, pl.ds((1 - c) * hr, hr), :], dst[t], send_sems.at[t], recv_sems.at[t], (x, y, 1 - c))
            cp.start()
            cps.append(cp)
        for cp in cps:
            cp.wait()

    return pl.pallas_call(
        body, name="grad_pair_exchange",
        in_specs=[ANY] * n, out_specs=[ANY] * n,
        out_shape=[jax.ShapeDtypeStruct((g.shape[0], g.shape[1] // 2, g.shape[2]), g.dtype) for g in grads],
        scratch_shapes=[pltpu.SemaphoreType.DMA((n,)), pltpu.SemaphoreType.DMA((n,))],
        compiler_params=_params(has_side_effects=True),
    )(*grads)


def _pair_add(g, got, cidx, name, tb=256):
    S, R, C = g.shape
    hr = R // 2
    tb = _rows_tile(hr, tb)
    nb = hr // tb

    def body(c_ref, a_ref, b_ref, o_ref):
        o_ref[...] = (a_ref[...] + b_ref[...]).astype(BF16)

    return pl.pallas_call(
        body, name=name,
        grid_spec=pltpu.PrefetchScalarGridSpec(
            num_scalar_prefetch=1, grid=(S, nb),
            in_specs=[pl.BlockSpec((None, tb, C), lambda s, i, c: (s, c[0] * nb + i, 0)),
                      pl.BlockSpec((None, tb, C), lambda s, i, c: (s, i, 0))],
            out_specs=pl.BlockSpec((None, tb, C), lambda s, i, c: (s, i, 0))),
        out_shape=jax.ShapeDtypeStruct((S, hr, C), BF16),
        compiler_params=_params("parallel", "parallel"),
    )(cidx, g, got)


def _rows_tile(n, want):
    t = min(n, want)
    while n % t or t % 8:
        t -= 8
    return t


def _chip_scatter(sums):
    n = len(sums)

    def body(*refs):
        src, dst = refs[:n], refs[n:2 * n]
        send_sems, recv_sems, local_sems = refs[2 * n:]
        x, y, c, chips = _place()
        mine = 2 * x + y
        local = [pltpu.make_async_copy(src[t].at[mine], dst[t].at[mine], local_sems.at[t]) for t in range(n)]
        for cp in local:
            cp.start()
        sends = []
        for t in range(n):
            for j, (px, py) in enumerate(chips):
                cp = _remote(src[t].at[2 * px + py], dst[t].at[mine], send_sems.at[t, j], recv_sems.at[t, j], (px, py, c))
                cp.start()
                sends.append(cp)
        for t in range(n):
            for j, (px, py) in enumerate(chips):
                slot = dst[t].at[2 * px + py]
                _remote(slot, slot, send_sems.at[t, j], recv_sems.at[t, j], (px, py, c)).wait_recv()
        for cp in sends:
            cp.wait_send()
        for cp in local:
            cp.wait()

    return pl.pallas_call(
        body, name="grad_chip_scatter",
        in_specs=[ANY] * n, out_specs=[ANY] * n,
        out_shape=[jax.ShapeDtypeStruct(s.shape, s.dtype) for s in sums],
        scratch_shapes=[pltpu.SemaphoreType.DMA((n, 3)), pltpu.SemaphoreType.DMA((n, 3)), pltpu.SemaphoreType.DMA((n,))],
        compiler_params=_params(has_side_effects=True),
    )(*sums)


def _chip_sum(parts, cidx, name, tb=256):
    S, hr, C = parts.shape
    tb = _rows_tile(hr, tb)
    nb = hr // tb

    def body(c_ref, *refs):
        o_ref = refs[S]
        tot = refs[0][...].astype(F32)
        for s in range(1, S):
            tot = tot + refs[s][...].astype(F32)
        o_ref[...] = tot

    return pl.pallas_call(
        body, name=name,
        grid_spec=pltpu.PrefetchScalarGridSpec(
            num_scalar_prefetch=1, grid=(nb,),
            in_specs=[pl.BlockSpec((None, tb, C), functools.partial(lambda s, i, c: (s, i, 0), s)) for s in range(S)],
            out_specs=pl.BlockSpec((tb, C), lambda i, c: (c[0] * nb + i, 0))),
        out_shape=jax.ShapeDtypeStruct((2 * hr, C), F32),
        compiler_params=_params("parallel"),
    )(cidx, *([parts] * S))


def _half_exchange(halves):
    n = len(halves)

    def body(*refs):
        dst = refs[n:2 * n]
        send_sems, recv_sems = refs[2 * n:]
        x, y, c, _ = _place()
        cps = []
        for t in range(n):
            hr = halves[t].shape[0] // 2
            rows = dst[t].at[pl.ds(c * hr, hr), :]
            cp = _remote(rows, rows, send_sems.at[t], recv_sems.at[t], (x, y, 1 - c))
            cp.start()
            cps.append(cp)
        for t, cp in enumerate(cps):
            cp.wait_send()
            hr = halves[t].shape[0] // 2
            other = dst[t].at[pl.ds((1 - c) * hr, hr), :]
            _remote(other, other, send_sems.at[t], recv_sems.at[t], (x, y, 1 - c)).wait_recv()

    return pl.pallas_call(
        body, name="grad_half_exchange",
        in_specs=[ANY] * n, out_specs=[ANY] * n,
        out_shape=[jax.ShapeDtypeStruct(h.shape, h.dtype) for h in halves],
        input_output_aliases={t: t for t in range(n)},
        scratch_shapes=[pltpu.SemaphoreType.DMA((n,)), pltpu.SemaphoreType.DMA((n,))],
        compiler_params=_params(has_side_effects=True),
    )(*halves)


def _adamw_math(w, g, m, v):
    m = ADAM_B1 * m + (1.0 - ADAM_B1) * g
    v = ADAM_B2 * v + (1.0 - ADAM_B2) * (g * g)
    m_hat = m / (1.0 - ADAM_B1 ** ADAM_STEP)
    v_hat = v / (1.0 - ADAM_B2 ** ADAM_STEP)
    delta = -ADAM_LR * (m_hat / (jnp.sqrt(v_hat) + ADAM_EPS) + ADAM_WD * w)
    return delta, m, v


def _adamw(w, g, m, v, name, tb=256):
    R, C = w.shape
    tb = _rows_tile(R, tb)

    def body(w_ref, g_ref, m_ref, v_ref, d_ref, mo_ref, vo_ref):
        d, mn, vn = _adamw_math(w_ref[...], g_ref[...], m_ref[...], v_ref[...])
        d_ref[...] = d
        mo_ref[...] = mn
        vo_ref[...] = vn

    row = pl.BlockSpec((tb, C), lambda i: (i, 0))
    return pl.pallas_call(
        body, name=name, grid=(R // tb,),
        in_specs=[row] * 4, out_specs=[row] * 3,
        out_shape=[jax.ShapeDtypeStruct((R, C), F32)] * 3,
        compiler_params=_params("parallel"),
    )(w, g, m, v)


N_DEV = 8


def _small_update(g, w, m, v):
    P = g.shape[0]

    def body(g_ref, w_ref, m_ref, v_ref, go_ref, d_ref, mo_ref, vo_ref, buf, send_sems, recv_sems):
        x, y, c, _ = _place()
        me = 4 * x + 2 * y + c
        buf[me] = g_ref[...]
        cps = []
        for k in range(1, N_DEV):
            fx, fy, fc = (k >> 2) & 1, (k >> 1) & 1, k & 1
            px = (1 - x) if fx else x
            py = (1 - y) if fy else y
            pc = (1 - c) if fc else c
            cp = _remote(g_ref, buf.at[me], send_sems.at[k - 1], recv_sems.at[k - 1], (px, py, pc))
            cp.start()
            cps.append((cp, 4 * px + 2 * py + pc))
        for k, (cp, peer) in enumerate(cps):
            _remote(g_ref, buf.at[peer], send_sems.at[k], recv_sems.at[k], (x, y, c)).wait_recv()
        for cp, _ in cps:
            cp.wait_send()
        tot = buf[0]
        for d in range(1, N_DEV):
            tot = tot + buf[d]
        go_ref[...] = tot
        dl, mn, vn = _adamw_math(w_ref[...], tot, m_ref[...], v_ref[...])
        d_ref[...] = dl
        mo_ref[...] = mn
        vo_ref[...] = vn

    vm = pl.BlockSpec(memory_space=pltpu.VMEM)
    return pl.pallas_call(
        body, name="small_params_update",
        in_specs=[vm] * 4, out_specs=[vm] * 4,
        out_shape=[jax.ShapeDtypeStruct((P, HEAD_DIM), F32)] * 4,
        scratch_shapes=[pltpu.VMEM((N_DEV, P, HEAD_DIM), F32), pltpu.SemaphoreType.DMA((N_DEV - 1,)),
                        pltpu.SemaphoreType.DMA((N_DEV - 1,))],
        compiler_params=_params(has_side_effects=True),
    )(g, w, m, v)


SMALL = ("fox_b_f", "fox_q_gain", "fox_k_gain", "dil_q_gain", "dil_k_gain", "mix_norm_g", "mlp_norm_g")
LARGE = ("fox_w_in", "fox_w_out", "dil_w_in", "dil_w_out", "mlp_w_up", "mlp_w_down")
WEIGHTS = ("fox_w_in", "fox_b_f", "fox_q_gain", "fox_k_gain", "fox_w_out", "dil_w_in", "dil_q_gain", "dil_k_gain",
           "dil_w_out", "mix_norm_g", "mlp_norm_g", "mlp_w_up", "mlp_w_down")


def _pack(parts):
    rows = []
    for a in parts:
        flat = a.reshape(-1)
        n = -(-flat.shape[0] // (8 * HEAD_DIM)) * (8 * HEAD_DIM)
        rows.append(jnp.pad(flat, (0, n - flat.shape[0])).reshape(-1, HEAD_DIM))
    return jnp.concatenate(rows, axis=0)


def _unpack(packed, like):
    out, r = [], 0
    for a in like:
        size = int(np.prod(a.shape))
        n = -(-size // (8 * HEAD_DIM)) * 8
        out.append(packed[r:r + n].reshape(-1)[:size].reshape(a.shape))
        r += n
    return out


def _pad_lanes(a):
    return jnp.pad(a, [(0, 0)] * (a.ndim - 1) + [(0, HEAD_DIM - a.shape[-1])])


def _as_shards(a):
    return a.reshape(N_CHIPS, a.shape[0] // N_CHIPS, a.shape[1])


def kernel(x, fox_w_in, fox_b_f, fox_q_gain, fox_k_gain, fox_w_out, dil_w_in, dil_q_gain, dil_k_gain, dil_w_out, mix_norm_g, mlp_norm_g, mlp_w_up, mlp_w_down, loss_target, m_fox_w_in, m_fox_b_f, m_fox_q_gain, m_fox_k_gain, m_fox_w_out, m_dil_w_in, m_dil_q_gain, m_dil_k_gain, m_dil_w_out, m_mix_norm_g, m_mlp_norm_g, m_mlp_w_up, m_mlp_w_down, v_fox_w_in, v_fox_b_f, v_fox_q_gain, v_fox_k_gain, v_fox_w_out, v_dil_w_in, v_dil_q_gain, v_dil_k_gain, v_dil_w_out, v_mix_norm_g, v_mlp_norm_g, v_mlp_w_up, v_mlp_w_down):
    wts = dict(fox_w_in=fox_w_in, fox_b_f=fox_b_f, fox_q_gain=fox_q_gain, fox_k_gain=fox_k_gain, fox_w_out=fox_w_out,
               dil_w_in=dil_w_in, dil_q_gain=dil_q_gain, dil_k_gain=dil_k_gain, dil_w_out=dil_w_out,
               mix_norm_g=mix_norm_g, mlp_norm_g=mlp_norm_g, mlp_w_up=mlp_w_up, mlp_w_down=mlp_w_down)
    mom1 = dict(fox_w_in=m_fox_w_in, fox_b_f=m_fox_b_f, fox_q_gain=m_fox_q_gain, fox_k_gain=m_fox_k_gain,
                fox_w_out=m_fox_w_out, dil_w_in=m_dil_w_in, dil_q_gain=m_dil_q_gain, dil_k_gain=m_dil_k_gain,
                dil_w_out=m_dil_w_out, mix_norm_g=m_mix_norm_g, mlp_norm_g=m_mlp_norm_g, mlp_w_up=m_mlp_w_up,
                mlp_w_down=m_mlp_w_down)
    mom2 = dict(fox_w_in=v_fox_w_in, fox_b_f=v_fox_b_f, fox_q_gain=v_fox_q_gain, fox_k_gain=v_fox_k_gain,
                fox_w_out=v_fox_w_out, dil_w_in=v_dil_w_in, dil_q_gain=v_dil_q_gain, dil_k_gain=v_dil_k_gain,
                dil_w_out=v_dil_w_out, mix_norm_g=v_mix_norm_g, mlp_norm_g=v_mlp_norm_g, mlp_w_up=v_mlp_w_up,
                mlp_w_down=v_mlp_w_down)
    T, D = x.shape[1], x.shape[2]
    H = D // HEAD_DIM
    cidx = lax.axis_index("c").astype(jnp.int32).reshape(1)

    def shards_of(d):
        return [d["fox_w_in"][0], d["fox_w_out"][0], d["dil_w_in"][0], d["dil_w_out"][0],
                d["mlp_w_up"][0], d["mlp_w_up"][1], d["mlp_w_down"][0], d["mlp_w_down"][1]]

    w_sh, m_sh, v_sh = shards_of(wts), shards_of(mom1), shards_of(mom2)
    w_bf = [s.astype(BF16) for s in w_sh]
    first = _gather_weights(w_bf[:2])
    fox_in = jnp.moveaxis(first[0], 0, 1).reshape(D, -1)
    w = dict(
        fox_qkv=fox_in[:, :3 * D], fox_f=_pad_lanes(fox_in[:, 3 * D:]), fox_b=_pad_lanes(fox_b_f),
        fox_gains=jnp.stack([fox_q_gain, fox_k_gain]), fox_out=first[1].reshape(D, D),
        dil_gains=jnp.concatenate([dil_q_gain[0], dil_k_gain[0]])[:, None, :],
        mix_g=[mix_norm_g[0:1], mix_norm_g[1:2]], mlp_g=[mlp_norm_g[0:1], mlp_norm_g[1:2]])

    def late_weights(full):
        return dict(dil_in=full[0], dil_out=full[1].reshape(D, D), up=[full[2], full[3]],
                    down=[full[4].reshape(-1, D), full[5].reshape(-1, D)])

    loss, grad_x, g = _local_step(x.reshape(T, D), loss_target.reshape(T, D), w, w_bf[2:], late_weights)
    loss = lax.psum(loss, ("x", "y", "c"))

    g_fox_in = jnp.concatenate([g["fox_qkv"], g["fox_f"][:, :H]], axis=1)
    g_fox_in = jnp.moveaxis(g_fox_in.reshape(D, N_CHIPS, -1), 1, 0)
    stacked = [g_fox_in, _as_shards(g["fox_out"]), g["dil_in"], _as_shards(g["dil_out"]),
               g["up"][0], g["up"][1], _as_shards(g["down"][0]), _as_shards(g["down"][1])]
    got = _pair_exchange(stacked)
    sums = [_pair_add(a, b, cidx, f"grad_pair_add_{t}") for t, (a, b) in enumerate(zip(stacked, got))]
    parts = _chip_scatter(sums)
    halves = [_chip_sum(p, cidx, f"grad_chip_sum_{t}") for t, p in enumerate(parts)]
    totals = _half_exchange(halves)
    upd = [_adamw(w_sh[t], totals[t], m_sh[t], v_sh[t], f"adamw_{t}") for t in range(len(totals))]

    def large(k):
        a = [totals[t] if k == 0 else upd[t][k - 1] for t in range(len(totals))]
        return dict(fox_w_in=a[0][None], fox_w_out=a[1][None], dil_w_in=a[2][None], dil_w_out=a[3][None],
                    mlp_w_up=jnp.stack([a[4], a[5]]), mlp_w_down=jnp.stack([a[6], a[7]]))

    small_like = [wts[n] for n in SMALL]
    g_small = [g["fox_b"][:, :H], g["fox_gains"][0], g["fox_gains"][1], g["dil_gains"][:3, 0][None], g["dil_gains"][3:, 0][None],
               jnp.concatenate(g["mix_g"]), jnp.concatenate(g["mlp_g"])]
    packed = _small_update(_pack(g_small), _pack(small_like), _pack([mom1[n] for n in SMALL]), _pack([mom2[n] for n in SMALL]))
    small = [dict(zip(SMALL, _unpack(p, small_like))) for p in packed]

    outs = [loss, grad_x.reshape(x.shape)]
    for k in range(4):
        big = large(k)
        outs += [big[n] if n in big else small[k][n] for n in WEIGHTS]
    return tuple(outs)
```

```python
import functools
from typing import Callable, NamedTuple

import numpy as np
import jax
import jax.numpy as jnp
from jax import lax
from jax.experimental import pallas as pl
from jax.experimental.pallas import tpu as pltpu

F32 = jnp.float32
BF16 = jnp.bfloat16

HEAD_DIM = 128
DIL_PATTERNS = ((128, 1), (512, 4), (2048, 16))
DIL_SPAN = 128
ALIBI_MAX_EXP = 8.0
EPS = 1e-6
MASKED = -1e30

ADAM_LR = 0.001
ADAM_B1 = 0.9
ADAM_B2 = 0.999
ADAM_EPS = 1e-08
ADAM_WD = 0.01
ADAM_STEP = 10

N_CHIPS = 4
VMEM_LIMIT_BYTES = 56 * 1024 * 1024
MESH = pl.DeviceIdType.MESH
ANY = pl.BlockSpec(memory_space=pl.ANY)

NN = (((1,), (0,)), ((), ()))
NT = (((1,), (1,)), ((), ()))
TN = (((0,), (0,)), ((), ()))


def _params(*sem, **kw):
    return pltpu.CompilerParams(dimension_semantics=sem or None, vmem_limit_bytes=VMEM_LIMIT_BYTES, **kw)


def _dot(a, b, dims):
    return lax.dot_general(a, b, dims, preferred_element_type=F32)


def _tile(n, want):
    if n <= want:
        return n
    t = want - want % 128
    while n % t:
        t -= 128
    return t


def _mm(a, b, M, N, K, *, mode, name, out_dtypes, b_stack=0, out_stack=0, extras=(), epilogue=None,
        tm=1024, tn=1024, tk=2048):
    per_b = per_o = None
    if b_stack:
        per_b = (K if mode == "nt" else N) // b_stack
    if out_stack:
        per_o = N // out_stack
    tm = _tile(M, tm)
    tn = _tile(min(x for x in (N, per_o, per_b if mode != "nt" else None) if x), tn)
    tk = _tile(min(x for x in (K, per_b if mode == "nt" else None) if x), tk)
    assert M % tm == 0 and N % tn == 0 and K % tk == 0, (name, M, N, K, tm, tn, tk)
    gk = K // tk
    if mode == "tn":
        a_spec = pl.BlockSpec((tk, tm), lambda i, j, k: (k, i))
    else:
        a_spec = pl.BlockSpec((tm, tk), lambda i, j, k: (i, k))
    if mode == "nt":
        if b_stack:
            npk = per_b // tk
            b_spec = pl.BlockSpec((None, tn, tk), lambda i, j, k: (k // npk, j, k % npk))
        else:
            b_spec = pl.BlockSpec((tn, tk), lambda i, j, k: (j, k))
    else:
        if b_stack:
            npj = per_b // tn
            b_spec = pl.BlockSpec((None, tk, tn), lambda i, j, k: (j // npj, k, j % npj))
        else:
            b_spec = pl.BlockSpec((tk, tn), lambda i, j, k: (k, j))
    if out_stack:
        npo = per_o // tn
        o_spec = pl.BlockSpec((None, tm, tn), lambda i, j, k: (j // npo, i, j % npo))
        o_shape = (out_stack, M, per_o)
    else:
        o_spec = pl.BlockSpec((tm, tn), lambda i, j, k: (i, j))
        o_shape = (M, N)
    e_spec = pl.BlockSpec((tm, tn), lambda i, j, k: (i, j))
    dims = {"nn": NN, "nt": NT, "tn": TN}[mode]
    ne, no = len(extras), len(out_dtypes)

    def body(a_ref, b_ref, *rest):
        ex, outs = rest[:ne], rest[ne:ne + no]
        k = pl.program_id(2)

        def product():
            return _dot(a_ref[...].astype(BF16), b_ref[...].astype(BF16), dims)

        def finish(r):
            res = epilogue(r, *[e[...] for e in ex]) if epilogue is not None else (r,)
            for o, v in zip(outs, res):
                o[...] = v.astype(o.dtype)

        if gk == 1:
            finish(product())
            return
        acc = rest[ne + no]

        @pl.when(k == 0)
        def _():
            acc[...] = product()

        @pl.when((k > 0) & (k < gk - 1))
        def _():
            acc[...] += product()

        @pl.when(k == gk - 1)
        def _():
            finish(acc[...] + product())

    outs = pl.pallas_call(
        body, name=name,
        grid=(M // tm, N // tn, gk),
        in_specs=[a_spec, b_spec] + [e_spec] * ne,
        out_specs=[o_spec] * no,
        out_shape=[jax.ShapeDtypeStruct(o_shape, d) for d in out_dtypes],
        scratch_shapes=[pltpu.VMEM((tm, tn), F32)] if gk > 1 else [],
        compiler_params=_params("parallel", "parallel", "arbitrary"),
    )(a, b, *extras)
    return outs[0] if no == 1 else outs


def _rms_fwd(x, g, name, tb=512):
    T, D = x.shape
    tb = min(tb, T)

    def body(x_ref, g_ref, o_ref):
        xv = x_ref[...]
        r = lax.rsqrt(jnp.mean(xv * xv, axis=-1, keepdims=True) + EPS)
        o_ref[...] = (xv * r * g_ref[...]).astype(BF16)

    return pl.pallas_call(
        body, name=name, grid=(T // tb,),
        in_specs=[pl.BlockSpec((tb, D), lambda i: (i, 0)), pl.BlockSpec((1, D), lambda i: (0, 0))],
        out_specs=pl.BlockSpec((tb, D), lambda i: (i, 0)),
        out_shape=jax.ShapeDtypeStruct((T, D), BF16),
        compiler_params=_params("parallel"),
    )(x, g)


def _rms_bwd(dy, x, g, dres, name, tb=256):
    T, D = x.shape
    tb = min(tb, T)

    def body(dy_ref, x_ref, g_ref, dres_ref, dx_ref, dxb_ref, dg_ref):
        i = pl.program_id(0)
        xv, dyv = x_ref[...], dy_ref[...]
        r = lax.rsqrt(jnp.mean(xv * xv, axis=-1, keepdims=True) + EPS)
        gy = dyv * g_ref[...]
        dx = r * gy - xv * (r * r * r) * jnp.mean(gy * xv, axis=-1, keepdims=True)
        tot = dres_ref[...] + dx
        dx_ref[...] = tot
        dxb_ref[...] = tot.astype(BF16)
        part = jnp.sum(dyv * (xv * r), axis=0, keepdims=True)

        @pl.when(i == 0)
        def _():
            dg_ref[...] = part

        @pl.when(i > 0)
        def _():
            dg_ref[...] += part

    row = pl.BlockSpec((tb, D), lambda i: (i, 0))
    vec = pl.BlockSpec((1, D), lambda i: (0, 0))
    return pl.pallas_call(
        body, name=name, grid=(T // tb,),
        in_specs=[row, row, vec, row],
        out_specs=[row, row, vec],
        out_shape=[jax.ShapeDtypeStruct((T, D), F32), jax.ShapeDtypeStruct((T, D), BF16),
                   jax.ShapeDtypeStruct((1, D), F32)],
        compiler_params=_params("arbitrary"),
    )(dy, x, g, dres)


def _loss_head(y, tgt, name, tb=256):
    T, D = y.shape
    tb = min(tb, T)

    def body(y_ref, t_ref, dy_ref, dyb_ref, loss_ref):
        i = pl.program_id(0)
        e = y_ref[...] - t_ref[...]
        d = e * (1.0 / D)
        dy_ref[...] = d
        dyb_ref[...] = d.astype(BF16)
        part = 0.5 * jnp.sum(jnp.sum(e * e, axis=1, keepdims=True) * (1.0 / D), axis=0, keepdims=True)

        @pl.when(i == 0)
        def _():
            loss_ref[...] = part

        @pl.when(i > 0)
        def _():
            loss_ref[...] += part

    row = pl.BlockSpec((tb, D), lambda i: (i, 0))
    return pl.pallas_call(
        body, name=name, grid=(T // tb,),
        in_specs=[row, row],
        out_specs=[row, row, pl.BlockSpec((1, 1), lambda i: (0, 0))],
        out_shape=[jax.ShapeDtypeStruct((T, D), F32), jax.ShapeDtypeStruct((T, D), BF16),
                   jax.ShapeDtypeStruct((1, 1), F32)],
        compiler_params=_params("arbitrary"),
    )(y, tgt)


def _head_rms(xh, g):
    r = lax.rsqrt(jnp.mean(xh * xh, axis=-1, keepdims=True) + EPS)
    return xh * r * g


def _qkv_prep(proj, gains, n_norm, gain_row, ch, name, n_scaled=0, post_scale=1.0, tb=512):
    T, W = proj.shape
    tb = min(tb, T)
    nch = W // ch
    nh = ch // HEAD_DIM

    def body(p_ref, g_ref, o_ref):
        j = pl.program_id(0)

        @pl.when(j < n_norm)
        def _():
            g = g_ref[...]
            if n_scaled:
                g = g * jnp.where(j < n_scaled, post_scale, 1.0)
            for h in range(nh):
                sl = slice(h * HEAD_DIM, (h + 1) * HEAD_DIM)
                o_ref[:, sl] = _head_rms(p_ref[:, sl], g).astype(BF16)

        @pl.when(j >= n_norm)
        def _():
            o_ref[...] = p_ref[...].astype(BF16)

    return pl.pallas_call(
        body, name=name, grid=(nch, T // tb),
        in_specs=[pl.BlockSpec((tb, ch), lambda j, i: (i, j)),
                  pl.BlockSpec((None, 1, HEAD_DIM), lambda j, i: (gain_row(j), 0, 0))],
        out_specs=pl.BlockSpec((tb, ch), lambda j, i: (i, j)),
        out_shape=jax.ShapeDtypeStruct((T, W), BF16),
        compiler_params=_params("parallel", "parallel"),
    )(proj, gains)


def _into(body, name, grid, in_specs, out_spec, out_shape, extra_out_specs, extra_out_shapes, buf, operands, sem):
    if buf is None:
        def kernel(*refs):
            body(*refs)
        ins, alias, ops = in_specs, {}, operands
    else:
        def kernel(_, *refs):
            body(*refs)
        ins = [pl.BlockSpec(memory_space=pl.ANY)] + in_specs
        alias, ops = {0: 0}, (buf,) + tuple(operands)
    return pl.pallas_call(
        kernel, name=name, grid=grid, in_specs=ins,
        out_specs=[out_spec] + extra_out_specs,
        out_shape=[out_shape] + extra_out_shapes,
        input_output_aliases=alias,
        compiler_params=_params(*sem),
    )(*ops)


def _head_rms_bwd_into(buf, W, d, proj, gain, off, ch, name, in_scale=1.0, tb=256):
    T, wd = d.shape
    tb = min(tb, T)
    n = wd // ch
    nh = ch // HEAD_DIM

    def body(d_ref, p_ref, g_ref, o_ref, dg_ref):
        i = pl.program_id(1)
        g = g_ref[...]
        part = jnp.zeros((1, HEAD_DIM), F32)
        for h in range(nh):
            sl = slice(h * HEAD_DIM, (h + 1) * HEAD_DIM)
            xh, dy = p_ref[:, sl], d_ref[:, sl]
            if in_scale != 1.0:
                dy = dy * in_scale
            r = lax.rsqrt(jnp.mean(xh * xh, axis=-1, keepdims=True) + EPS)
            gy = dy * g
            dx = r * gy - xh * (r * r * r) * jnp.mean(gy * xh, axis=-1, keepdims=True)
            o_ref[:, sl] = dx.astype(BF16)
            part = part + jnp.sum(dy * (xh * r), axis=0, keepdims=True)

        @pl.when(i == 0)
        def _():
            dg_ref[...] = part

        @pl.when(i > 0)
        def _():
            dg_ref[...] += part

    return _into(
        body, name, (n, T // tb),
        [pl.BlockSpec((tb, ch), lambda j, i: (i, j)), pl.BlockSpec((tb, ch), lambda j, i: (i, off + j)),
         pl.BlockSpec((1, HEAD_DIM), lambda j, i: (0, 0))],
        pl.BlockSpec((tb, ch), lambda j, i: (i, off + j)), jax.ShapeDtypeStruct((T, W), BF16),
        [pl.BlockSpec((None, 1, HEAD_DIM), lambda j, i: (j, 0, 0))], [jax.ShapeDtypeStruct((n, 1, HEAD_DIM), F32)],
        buf, (d, proj, gain), ("parallel", "arbitrary"))


def _sum_cast_into(buf, W, srcs, off, ch, name, tb=256):
    T, wd = srcs[0].shape
    tb = min(tb, T)
    n = wd // ch
    ns = len(srcs)

    def body(*refs):
        o_ref = refs[ns]
        tot = refs[0][...]
        for s in refs[1:ns]:
            tot = tot + s[...]
        o_ref[...] = tot.astype(BF16)

    out = _into(
        body, name, (n, T // tb),
        [pl.BlockSpec((tb, ch), lambda j, i: (i, j))] * ns,
        pl.BlockSpec((tb, ch), lambda j, i: (i, off + j)), jax.ShapeDtypeStruct((T, W), BF16),
        [], [], buf, tuple(srcs), ("parallel", "parallel"))
    return out[0]


def _tri(n, lower):
    r = lax.broadcasted_iota(jnp.int32, (n, n), 0)
    c = lax.broadcasted_iota(jnp.int32, (n, n), 1)
    return jnp.where((c <= r) if lower else (c >= r), 1.0, 0.0).astype(F32)


def _dot_exact(a, b):
    return lax.dot_general(a, b, NN, precision=lax.Precision.HIGHEST, preferred_element_type=F32)


def _log_sigmoid(z):
    return jnp.minimum(z, 0.0) - jnp.log(1.0 + jnp.exp(-jnp.abs(z)))


def _gate_fwd(f_raw, b_pad, hp, out_scale, name, blk=256):
    T = f_raw.shape[0]
    blk = min(blk, T)

    def body(f_ref, b_ref, c_ref):
        tri = _tri(blk, True)
        carry = jnp.zeros((1, HEAD_DIM), F32)
        for j in range(T // blk):
            lf = _log_sigmoid(f_ref[j * blk:(j + 1) * blk, :] + b_ref[...])
            cb = _dot_exact(tri, lf) + carry
            carry = cb[blk - 1:blk, :]
            c_ref[:, j * blk:(j + 1) * blk] = cb.T[:hp, :] * out_scale

    return pl.pallas_call(
        body, name=name,
        in_specs=[pl.BlockSpec(memory_space=pltpu.VMEM)] * 2,
        out_specs=pl.BlockSpec(memory_space=pltpu.VMEM),
        out_shape=jax.ShapeDtypeStruct((hp, T), F32),
        compiler_params=_params(),
    )(f_raw, b_pad)


def _gate_bwd(dc_rows, dc_cols, f_raw, b_pad, n_heads, hp, name, blk=256):
    T = f_raw.shape[0]
    blk = min(blk, T)

    def body(dc_ref, dcc_ref, f_ref, b_ref, dz_ref, db_ref):
        tri = _tri(blk, False)
        lane = lax.broadcasted_iota(jnp.int32, (blk, HEAD_DIM), 1)
        carry = jnp.zeros((1, HEAD_DIM), F32)
        db = jnp.zeros((1, HEAD_DIM), F32)
        for j in reversed(range(T // blk)):
            rows = dc_ref[:, j * blk:(j + 1) * blk]
            if hp < HEAD_DIM:
                rows = jnp.concatenate([rows, jnp.zeros((HEAD_DIM - hp, blk), F32)], axis=0)
            dlf = _dot_exact(tri, rows.T + dcc_ref[j * blk:(j + 1) * blk, :]) + carry
            carry = dlf[0:1, :]
            z = f_ref[j * blk:(j + 1) * blk, :] + b_ref[...]
            dz = jnp.where(lane < n_heads, dlf / (1.0 + jnp.exp(z)), 0.0)
            dz_ref[j * blk:(j + 1) * blk, :] = dz.astype(BF16)
            db = db + jnp.sum(dz, axis=0, keepdims=True)
        db_ref[...] = db

    return pl.pallas_call(
        body, name=name,
        in_specs=[pl.BlockSpec(memory_space=pltpu.VMEM)] * 4,
        out_specs=[pl.BlockSpec(memory_space=pltpu.VMEM)] * 2,
        out_shape=[jax.ShapeDtypeStruct((T, HEAD_DIM), BF16), jax.ShapeDtypeStruct((1, HEAD_DIM), F32)],
        compiler_params=_params(),
    )(dc_rows, dc_cols, f_raw, b_pad)


def _pairs(nb, key_major):
    if key_major:
        pairs = [(qi, ki) for ki in range(nb) for qi in range(ki, nb)]
    else:
        pairs = [(qi, ki) for qi in range(nb) for ki in range(qi + 1)]
    return (jnp.asarray(np.array([p[0] for p in pairs], np.int32)),
            jnp.asarray(np.array([p[1] for p in pairs], np.int32)))


LOG2E = 1.4426950408889634
LN2 = 0.6931471805599453
FOX_Q_SCALE = HEAD_DIM ** -0.5 * LOG2E


def _fox_logits(q, k, ck_row, diagonal):
    s = _dot(q, k, NT) - ck_row
    if diagonal:
        row = lax.broadcasted_iota(jnp.int32, s.shape, 0)
        col = lax.broadcasted_iota(jnp.int32, s.shape, 1)
        s = jnp.where(col <= row, s, MASKED)
    return s


def _fox_fwd(qkv, ck, H, name, tb=1024, rider=None):
    T = qkv.shape[0]
    tb = min(tb, T)
    nb = T // tb
    qt, kt = _pairs(nb, False)
    n_pairs = int(qt.shape[0])
    r_in, r_out, r_scr = (len(rider.operands), len(rider.out_shapes), len(rider.scratch)) if rider else (0, 0, 0)

    def body(qt_ref, kt_ref, q_ref, k_ref, v_ref, ck_ref, *rest):
        r_src, (o_ref, lse_ref), r_dst = rest[:r_in], rest[r_in:r_in + 2], rest[r_in + 2:r_in + 2 + r_out]
        m_sc, l_sc, acc_sc = rest[r_in + 2 + r_out:r_in + 5 + r_out]
        r_sems = rest[r_in + 5 + r_out:]
        p_ = pl.program_id(1)
        qi, ki = qt_ref[p_], kt_ref[p_]
        if rider:
            @pl.when((pl.program_id(0) == 0) & (p_ == 0))
            def _():
                rider.start(r_src, r_dst, r_sems)

        @pl.when(ki == 0)
        def _():
            m_sc[...] = jnp.full_like(m_sc, MASKED)
            l_sc[...] = jnp.zeros_like(l_sc)
            acc_sc[...] = jnp.zeros_like(acc_sc)

        def tile(diagonal):
            s = _fox_logits(q_ref[...], k_ref[...], ck_ref[...], diagonal)
            m_prev = m_sc[...]
            m_new = jnp.maximum(m_prev, jnp.max(s, axis=1, keepdims=True))
            alpha = jnp.exp2(m_prev - m_new)
            p = jnp.exp2(s - m_new[:, :1])
            l_sc[...] = alpha * l_sc[...] + jnp.sum(p, axis=1, keepdims=True)
            acc_sc[...] = alpha * acc_sc[...] + _dot(p.astype(BF16), v_ref[...], NN)
            m_sc[...] = m_new

        @pl.when(ki < qi)
        def _():
            tile(False)

        @pl.when(ki == qi)
        def _():
            tile(True)
            o_ref[...] = (acc_sc[...] / l_sc[...]).astype(BF16)
            lse_ref[...] = m_sc[...] + jnp.log(l_sc[...]) * LOG2E

        if rider:
            @pl.when((pl.program_id(0) == H - 1) & (p_ == n_pairs - 1))
            def _():
                rider.finish(r_src, r_dst, r_sems)

    blk = lambda f: pl.BlockSpec((tb, HEAD_DIM), f)
    outs = pl.pallas_call(
        body, name=name,
        grid_spec=pltpu.PrefetchScalarGridSpec(
            num_scalar_prefetch=2, grid=(H, n_pairs),
            in_specs=[blk(lambda h, p, qt, kt: (qt[p], h)),
                      blk(lambda h, p, qt, kt: (kt[p], H + h)),
                      blk(lambda h, p, qt, kt: (kt[p], 2 * H + h)),
                      pl.BlockSpec((None, 1, tb), lambda h, p, qt, kt: (h, 0, kt[p]))] + [ANY] * r_in,
            out_specs=[blk(lambda h, p, qt, kt: (qt[p], h)), blk(lambda h, p, qt, kt: (qt[p], h))] + [ANY] * r_out,
            scratch_shapes=[pltpu.VMEM((tb, HEAD_DIM), F32)] * 3 + (list(rider.scratch) if rider else [])),
        out_shape=[jax.ShapeDtypeStruct((T, H * HEAD_DIM), BF16), jax.ShapeDtypeStruct((T, H * HEAD_DIM), F32)]
        + (list(rider.out_shapes) if rider else []),
        compiler_params=_params("arbitrary", "arbitrary", has_side_effects=bool(rider)),
    )(qt, kt, qkv, qkv, qkv, ck, *(rider.operands if rider else ()))
    return outs[0], outs[1], list(outs[2:])


def _row_dot(do, o, nh, width, name, lane_per_head, tb=256):
    T = do.shape[0]
    tb = min(tb, T)
    wout = HEAD_DIM if lane_per_head else nh * HEAD_DIM

    def body(do_ref, o_ref, d_ref):
        lane = lax.broadcasted_iota(jnp.int32, (tb, HEAD_DIM), 1)
        tile = jnp.zeros((tb, HEAD_DIM), F32)
        for h in range(nh):
            sl = slice(h * width, (h + 1) * width)
            d = jnp.sum(do_ref[:, sl].astype(F32) * o_ref[:, sl].astype(F32), axis=1, keepdims=True)
            if lane_per_head:
                tile = jnp.where(lane == h, d, tile)
            else:
                d_ref[:, h * HEAD_DIM:(h + 1) * HEAD_DIM] = jnp.broadcast_to(d, (tb, HEAD_DIM))
        if lane_per_head:
            d_ref[...] = tile

    row = pl.BlockSpec((tb, nh * width), lambda i: (i, 0))
    return pl.pallas_call(
        body, name=name, grid=(T // tb,),
        in_specs=[row, row], out_specs=pl.BlockSpec((tb, wout), lambda i: (i, 0)),
        out_shape=jax.ShapeDtypeStruct((T, wout), F32),
        compiler_params=_params("parallel"),
    )(do, o)


def _fox_bwd(qkv, do, ck, lse, dd, H, hp, name, tb=1024, rider=None):
    T = qkv.shape[0]
    tb = min(tb, T)
    nb = T // tb
    qt, kt = _pairs(nb, True)
    n_pairs = int(qt.shape[0])
    r_in, r_out = (len(rider.operands), len(rider.out_shapes)) if rider else (0, 0)

    def body(qt_ref, kt_ref, q_ref, k_ref, v_ref, do_ref, ck_ref, lse_ref, dd_ref, *rest):
        r_src, r_dst, r_sems = rest[:r_in], rest[r_in + 5:r_in + 5 + r_out], rest[r_in + 5 + r_out:]
        dq_ref, dk_ref, dv_ref, dc_ref, dcq_ref = rest[r_in:r_in + 5]
        p_ = pl.program_id(1)
        qi, ki = qt_ref[p_], kt_ref[p_]
        if rider:
            @pl.when((pl.program_id(0) == 0) & (p_ == 0))
            def _():
                rider.start(r_src, r_dst, r_sems)

        @pl.when(p_ == 0)
        def _():
            dq_ref[...] = jnp.zeros_like(dq_ref)
            dcq_ref[...] = jnp.zeros_like(dcq_ref)

        @pl.when(qi == ki)
        def _():
            dk_ref[...] = jnp.zeros_like(dk_ref)
            dv_ref[...] = jnp.zeros_like(dv_ref)
            dc_ref[...] = jnp.zeros_like(dc_ref)

        rows = pl.ds(pl.multiple_of(qi * tb, tb), tb)

        def tile(diagonal):
            s = _fox_logits(q_ref[...], k_ref[...], ck_ref[...], diagonal)
            p = jnp.exp2(s - lse_ref[:, :1])
            dv_ref[...] += _dot(p.astype(BF16), do_ref[...], TN)
            dp = _dot(do_ref[...], v_ref[...], NT)
            ds = p * (dp - dd_ref[:, :1])
            dc_ref[...] -= jnp.sum(ds, axis=0, keepdims=True)
            dcq_ref[rows, :] += jnp.sum(ds, axis=1, keepdims=True)
            dsb = ds.astype(BF16)
            dq_ref[rows, :] += _dot(dsb, k_ref[...], NN)
            dk_ref[...] += _dot(dsb, q_ref[...], TN)

        @pl.when(ki < qi)
        def _():
            tile(False)

        @pl.when(ki == qi)
        def _():
            tile(True)

        if rider:
            @pl.when((pl.program_id(0) == H - 1) & (p_ == n_pairs - 1))
            def _():
                rider.finish(r_src, r_dst, r_sems)

    blk = lambda f: pl.BlockSpec((tb, HEAD_DIM), f)
    at_q = lambda h, p, qt, kt: (qt[p], h)
    at_k = lambda h, p, qt, kt: (kt[p], h)
    crow = pl.BlockSpec((None, 1, tb), lambda h, p, qt, kt: (h, 0, kt[p]))
    whole = pl.BlockSpec((T, HEAD_DIM), lambda h, p, qt, kt: (0, h))
    wide = jax.ShapeDtypeStruct((T, H * HEAD_DIM), F32)
    outs = pl.pallas_call(
        body, name=name,
        grid_spec=pltpu.PrefetchScalarGridSpec(
            num_scalar_prefetch=2, grid=(H, n_pairs),
            in_specs=[blk(at_q),
                      blk(lambda h, p, qt, kt: (kt[p], H + h)),
                      blk(lambda h, p, qt, kt: (kt[p], 2 * H + h)),
                      blk(at_q), crow, blk(at_q), blk(at_q)] + [ANY] * r_in,
            out_specs=[whole, blk(at_k), blk(at_k), crow, whole] + [ANY] * r_out,
            scratch_shapes=list(rider.scratch) if rider else []),
        out_shape=[wide, wide, wide, jax.ShapeDtypeStruct((hp, 1, T), F32), wide] + (list(rider.out_shapes) if rider else []),
        compiler_params=_params("arbitrary", "arbitrary", has_side_effects=bool(rider)),
    )(qt, kt, qkv, qkv, qkv, do, ck, lse, dd, *(rider.operands if rider else ()))
    return outs[:5], list(outs[5:])


def _lane_per_head(wide, H, name, tb=256):
    T = wide.shape[0]
    tb = min(tb, T)

    def body(w_ref, o_ref):
        lane = lax.broadcasted_iota(jnp.int32, (tb, HEAD_DIM), 1)
        tile = jnp.zeros((tb, HEAD_DIM), F32)
        for h in range(H):
            tile = jnp.where(lane == h, w_ref[:, h * HEAD_DIM:(h + 1) * HEAD_DIM], tile)
        o_ref[...] = tile

    return pl.pallas_call(
        body, name=name, grid=(T // tb,),
        in_specs=[pl.BlockSpec((tb, H * HEAD_DIM), lambda i: (i, 0))],
        out_specs=pl.BlockSpec((tb, HEAD_DIM), lambda i: (i, 0)),
        out_shape=jax.ShapeDtypeStruct((T, HEAD_DIM), F32),
        compiler_params=_params("parallel"),
    )(wide)


def _slopes(n_groups, nh):
    n = n_groups * nh
    s = np.exp2(-ALIBI_MAX_EXP * np.arange(1, n + 1, dtype=np.float32) / np.float32(n)).astype(np.float32)
    return s.reshape(n_groups, nh)


def _window_logits(qh, kh, slope_r, prev, has_prev):
    qi = lax.broadcasted_iota(jnp.int32, (DIL_SPAN, DIL_SPAN), 0)
    kl = lax.broadcasted_iota(jnp.int32, (DIL_SPAN, DIL_SPAN), 1)
    delta = qi - kl + (DIL_SPAN if prev else 0)
    s = _dot(qh, kh, NT) * (HEAD_DIM ** -0.5) - slope_r * delta.astype(F32)
    valid = ((kl >= qi) & has_prev) if prev else (kl <= qi)
    return jnp.where(valid, s, MASKED)


def _dil_views(T, r, G, nh, dv):
    L = T // r
    C, V = nh * HEAD_DIM, nh * dv
    return L, C, V, 2 * G * C + V


def _dil_fwd(qkv, g, r, G, nh, dv, slopes, name):
    T = qkv.shape[0]
    L, C, V, W = _dil_views(T, r, G, nh, dv)
    nblk = L // DIL_SPAN
    nc, nv = W // C, W // V
    view = qkv.reshape(L, r * W)

    def body(q_ref, kp_ref, kc_ref, vp_ref, vc_ref, num_ref, m_ref, den_ref):
        has_prev = pl.program_id(1) > 0
        lane = lax.broadcasted_iota(jnp.int32, (DIL_SPAN, HEAD_DIM), 1)
        m_tile = jnp.zeros((DIL_SPAN, HEAD_DIM), F32)
        den_tile = jnp.ones((DIL_SPAN, HEAD_DIM), F32)
        for h in range(nh):
            sl = slice(h * HEAD_DIM, (h + 1) * HEAD_DIM)
            vs = slice(h * dv, (h + 1) * dv)
            sr = float(slopes[h]) * r
            sc = _window_logits(q_ref[:, sl], kc_ref[:, sl], sr, False, has_prev)
            sp = _window_logits(q_ref[:, sl], kp_ref[:, sl], sr, True, has_prev)
            m = jnp.maximum(jnp.max(sc, axis=1, keepdims=True), jnp.max(sp, axis=1, keepdims=True))
            pc, pp = jnp.exp(sc - m), jnp.exp(sp - m)
            den = jnp.sum(pc, axis=1, keepdims=True) + jnp.sum(pp, axis=1, keepdims=True)
            num_ref[:, vs] = _dot(pc.astype(BF16), vc_ref[:, vs], NN) + _dot(pp.astype(BF16), vp_ref[:, vs], NN)
            m_tile = jnp.where(lane == h, m, m_tile)
            den_tile = jnp.where(lane == h, den, den_tile)
        m_ref[...] = m_tile
        den_ref[...] = den_tile

    prev = lambda i: jnp.maximum(i - 1, 0)
    stat = pl.BlockSpec((DIL_SPAN, HEAD_DIM), lambda b, i: (i, b))
    num, m, den = pl.pallas_call(
        body, name=name, grid=(r, nblk),
        in_specs=[pl.BlockSpec((DIL_SPAN, C), lambda b, i: (i, b * nc + g)),
                  pl.BlockSpec((DIL_SPAN, C), lambda b, i: (prev(i), b * nc + G + g)),
                  pl.BlockSpec((DIL_SPAN, C), lambda b, i: (i, b * nc + G + g)),
                  pl.BlockSpec((DIL_SPAN, V), lambda b, i: (prev(i), b * nv + nv - 1)),
                  pl.BlockSpec((DIL_SPAN, V), lambda b, i: (i, b * nv + nv - 1))],
        out_specs=[pl.BlockSpec((DIL_SPAN, V), lambda b, i: (i, b)), stat, stat],
        out_shape=[jax.ShapeDtypeStruct((L, r * V), F32), jax.ShapeDtypeStruct((L, r * HEAD_DIM), F32),
                   jax.ShapeDtypeStruct((L, r * HEAD_DIM), F32)],
        compiler_params=_params("parallel", "parallel"),
    )(view, view, view, view, view)
    return num.reshape(T, V), m.reshape(T, HEAD_DIM), den.reshape(T, HEAD_DIM)


def _dil_merge(nums, ms, dens, nh, dv, name, tb=256):
    T, V = nums[0].shape
    tb = min(tb, T)
    G = len(nums)

    def body(*refs):
        num_r, m_r, den_r = refs[:G], refs[G:2 * G], refs[2 * G:3 * G]
        o_ref, lse_ref = refs[3 * G], refs[3 * G + 1]
        mm = m_r[0][...]
        for g in range(1, G):
            mm = jnp.maximum(mm, m_r[g][...])
        w = [jnp.exp(m_r[g][...] - mm) for g in range(G)]
        den = w[0] * den_r[0][...]
        for g in range(1, G):
            den = den + w[g] * den_r[g][...]
        lse_ref[...] = mm + jnp.log(den)
        for h in range(nh):
            vs = slice(h * dv, (h + 1) * dv)
            num = w[0][:, h:h + 1] * num_r[0][:, vs]
            for g in range(1, G):
                num = num + w[g][:, h:h + 1] * num_r[g][:, vs]
            o_ref[:, vs] = (num / den[:, h:h + 1]).astype(BF16)

    wide = pl.BlockSpec((tb, V), lambda i: (i, 0))
    stat = pl.BlockSpec((tb, HEAD_DIM), lambda i: (i, 0))
    return pl.pallas_call(
        body, name=name, grid=(T // tb,),
        in_specs=[wide] * G + [stat] * (2 * G),
        out_specs=[wide, stat],
        out_shape=[jax.ShapeDtypeStruct((T, V), BF16), jax.ShapeDtypeStruct((T, HEAD_DIM), F32)],
        compiler_params=_params("parallel"),
    )(*nums, *ms, *dens)


def _dil_dq(qkv, do, lse, dd, g, r, G, nh, dv, slopes, name):
    T = qkv.shape[0]
    L, C, V, W = _dil_views(T, r, G, nh, dv)
    nblk = L // DIL_SPAN
    nc, nv = W // C, W // V
    view = qkv.reshape(L, r * W)
    scale = HEAD_DIM ** -0.5

    def body(q_ref, kp_ref, kc_ref, vp_ref, vc_ref, do_ref, lse_ref, dd_ref, dq_ref):
        has_prev = pl.program_id(1) > 0
        for h in range(nh):
            sl = slice(h * HEAD_DIM, (h + 1) * HEAD_DIM)
            vs = slice(h * dv, (h + 1) * dv)
            sr = float(slopes[h]) * r
            lse_h, dd_h = lse_ref[:, h:h + 1], dd_ref[:, h:h + 1]
            acc = jnp.zeros((DIL_SPAN, HEAD_DIM), F32)
            for k_ref, v_ref, is_prev in ((kc_ref, vc_ref, False), (kp_ref, vp_ref, True)):
                s = _window_logits(q_ref[:, sl], k_ref[:, sl], sr, is_prev, has_prev)
                p = jnp.exp(s - lse_h)
                dp = _dot(do_ref[:, vs], v_ref[:, vs], NT)
                ds = (p * (dp - dd_h)).astype(BF16)
                acc = acc + _dot(ds, k_ref[:, sl], NN)
            dq_ref[:, sl] = scale * acc

    prev = lambda i: jnp.maximum(i - 1, 0)
    stat = pl.BlockSpec((DIL_SPAN, HEAD_DIM), lambda b, i: (i, b))
    dq = pl.pallas_call(
        body, name=name, grid=(r, nblk),
        in_specs=[pl.BlockSpec((DIL_SPAN, C), lambda b, i: (i, b * nc + g)),
                  pl.BlockSpec((DIL_SPAN, C), lambda b, i: (prev(i), b * nc + G + g)),
                  pl.BlockSpec((DIL_SPAN, C), lambda b, i: (i, b * nc + G + g)),
                  pl.BlockSpec((DIL_SPAN, V), lambda b, i: (prev(i), b * nv + nv - 1)),
                  pl.BlockSpec((DIL_SPAN, V), lambda b, i: (i, b * nv + nv - 1)),
                  pl.BlockSpec((DIL_SPAN, V), lambda b, i: (i, b)), stat, stat],
        out_specs=pl.BlockSpec((DIL_SPAN, C), lambda b, i: (i, b)),
        out_shape=jax.ShapeDtypeStruct((L, r * C), F32),
        compiler_params=_params("parallel", "parallel"),
    )(view, view, view, view, view, do.reshape(L, r * V), lse.reshape(L, r * HEAD_DIM), dd.reshape(L, r * HEAD_DIM))
    return dq.reshape(T, C)


def _dil_dkv(qkv, do, lse, dd, g, r, G, nh, dv, slopes, name):
    T = qkv.shape[0]
    L, C, V, W = _dil_views(T, r, G, nh, dv)
    nblk = L // DIL_SPAN
    nc, nv = W // C, W // V
    view = qkv.reshape(L, r * W)
    scale = HEAD_DIM ** -0.5

    def body(k_ref, v_ref, qc_ref, qn_ref, doc_ref, don_ref, lsec_ref, lsen_ref, ddc_ref, ddn_ref, dk_ref, dv_ref):
        has_next = pl.program_id(1) < nblk - 1
        for h in range(nh):
            sl = slice(h * HEAD_DIM, (h + 1) * HEAD_DIM)
            vs = slice(h * dv, (h + 1) * dv)
            sr = float(slopes[h]) * r
            dk = jnp.zeros((DIL_SPAN, HEAD_DIM), F32)
            dvh = jnp.zeros((DIL_SPAN, dv), F32)
            for q_ref, do_ref, lse_ref, dd_ref, is_next in ((qc_ref, doc_ref, lsec_ref, ddc_ref, False),
                                                          (qn_ref, don_ref, lsen_ref, ddn_ref, True)):
                s = _window_logits(q_ref[:, sl], k_ref[:, sl], sr, is_next, has_next)
                p = jnp.exp(s - lse_ref[:, h:h + 1])
                dvh = dvh + _dot(p.astype(BF16), do_ref[:, vs], TN)
                dp = _dot(do_ref[:, vs], v_ref[:, vs], NT)
                ds = (p * (dp - dd_ref[:, h:h + 1])).astype(BF16)
                dk = dk + _dot(ds, q_ref[:, sl], TN)
            dk_ref[:, sl] = scale * dk
            dv_ref[:, vs] = dvh

    nxt = lambda i: jnp.minimum(i + 1, nblk - 1)
    stat_c = pl.BlockSpec((DIL_SPAN, HEAD_DIM), lambda b, i: (i, b))
    stat_n = pl.BlockSpec((DIL_SPAN, HEAD_DIM), lambda b, i: (nxt(i), b))
    do_v, lse_v, dd_v = do.reshape(L, r * V), lse.reshape(L, r * HEAD_DIM), dd.reshape(L, r * HEAD_DIM)
    dk, dvv = pl.pallas_call(
        body, name=name, grid=(r, nblk),
        in_specs=[pl.BlockSpec((DIL_SPAN, C), lambda b, i: (i, b * nc + G + g)),
                  pl.BlockSpec((DIL_SPAN, V), lambda b, i: (i, b * nv + nv - 1)),
                  pl.BlockSpec((DIL_SPAN, C), lambda b, i: (i, b * nc + g)),
                  pl.BlockSpec((DIL_SPAN, C), lambda b, i: (nxt(i), b * nc + g)),
                  pl.BlockSpec((DIL_SPAN, V), lambda b, i: (i, b)),
                  pl.BlockSpec((DIL_SPAN, V), lambda b, i: (nxt(i), b)),
                  stat_c, stat_n, stat_c, stat_n],
        out_specs=[pl.BlockSpec((DIL_SPAN, C), lambda b, i: (i, b)), pl.BlockSpec((DIL_SPAN, V), lambda b, i: (i, b))],
        out_shape=[jax.ShapeDtypeStruct((L, r * C), F32), jax.ShapeDtypeStruct((L, r * V), F32)],
        compiler_params=_params("parallel", "parallel"),
    )(view, view, view, view, do_v, do_v, lse_v, lse_v, dd_v, dd_v)
    return dk.reshape(T, C), dvv.reshape(T, V)


def _relu2(r):
    a = jnp.maximum(r, 0.0)
    return (a * a,)


def _mlp_fwd(x, g, w_up, w_down, tag):
    T, D = x.shape
    F = w_down.shape[0]
    h = _rms_fwd(x, g, f"{tag}_norm")
    a2 = _mm(h, w_up, T, F, D, mode="nn", name=f"{tag}_up", b_stack=N_CHIPS, out_dtypes=(BF16,), epilogue=_relu2)
    y = _mm(a2, w_down, T, D, F, mode="nn", name=f"{tag}_down", out_dtypes=(F32,), extras=(x,),
            epilogue=lambda r, res: (res + r,))
    return y, (x, h, a2)


def _mlp_bwd(dy, dyb, saved, g, w_up, w_down, tag):
    x, h, a2 = saved
    T, D = x.shape
    F = w_down.shape[0]
    d_down = _mm(a2, dyb, F, D, T, mode="tn", name=f"{tag}_dwdown", out_dtypes=(F32,))
    du = _mm(dyb, w_down, T, F, D, mode="nt", name=f"{tag}_da", out_dtypes=(BF16,), extras=(a2,),
             epilogue=lambda r, sq: (r * (2.0 * jnp.sqrt(sq.astype(F32))),))
    d_up = _mm(h, du, D, F, T, mode="tn", name=f"{tag}_dwup", out_stack=N_CHIPS, out_dtypes=(F32,))
    dh = _mm(du, w_up, T, D, F, mode="nt", name=f"{tag}_dh", b_stack=N_CHIPS, out_dtypes=(F32,))
    dx, dxb, dg = _rms_bwd(dh, x, g, dy, f"{tag}_dnorm")
    return dx, dxb, dg, d_up, d_down


def _fox_dims(D):
    H = D // HEAD_DIM
    return H, max(8, H), (H // 2) * HEAD_DIM


def _fox_layer_fwd(x, g, w_qkv, w_f, b_pad, gains, w_out, rider=None):
    T, D = x.shape
    H, hp, ch = _fox_dims(D)
    h = _rms_fwd(x, g, "fox_norm")
    proj = _mm(h, w_qkv, T, 3 * D, D, mode="nn", name="fox_proj", out_dtypes=(F32,))
    f_raw = _mm(h, w_f, T, HEAD_DIM, D, mode="nn", name="fox_gate_proj", out_dtypes=(F32,))
    qkv = _qkv_prep(proj, gains, 4, lambda j: jnp.minimum(j // 2, 1), ch, "fox_qk_norm", n_scaled=2, post_scale=FOX_Q_SCALE)
    ck = _gate_fwd(f_raw, b_pad, hp, LOG2E, "fox_gate").reshape(hp, 1, T)
    o, lse, carried = _fox_fwd(qkv, ck, H, "fox_attn", rider=rider)
    y = _mm(o, w_out, T, D, D, mode="nn", name="fox_out", out_dtypes=(F32,), extras=(x,),
            epilogue=lambda r, res: (res + r,))
    return y, (x, h, proj, f_raw, qkv, ck, o, lse), carried


def _fox_layer_bwd(dy, dyb, saved, g, w_qkv, w_f, b_pad, gains, w_out, rider=None):
    x, h, proj, f_raw, qkv, ck, o, lse = saved
    T, D = x.shape
    H, hp, ch = _fox_dims(D)
    d_out = _mm(o, dyb, D, D, T, mode="tn", name="fox_dwout", out_dtypes=(F32,))
    do = _mm(dyb, w_out, T, D, D, mode="nt", name="fox_do", out_dtypes=(BF16,))
    dd = _row_dot(do, o, H, HEAD_DIM, "fox_rowdot", False)
    (dq, dk, dv, dck, dcq), carried = _fox_bwd(qkv, do, ck, lse, dd, H, hp, "fox_attn_bwd", rider=rider)
    dcq = _lane_per_head(dcq, H, "fox_dc_query")
    dproj, dgq = _head_rms_bwd_into(None, 3 * D, dq, proj, gains[0], 0, ch, "fox_dq_norm", in_scale=HEAD_DIM ** -0.5)
    dproj, dgk = _head_rms_bwd_into(dproj, 3 * D, dk, proj, gains[1], 2, ch, "fox_dk_norm", in_scale=LN2)
    dproj = _sum_cast_into(dproj, 3 * D, [dv], 4, ch, "fox_dv_cast")
    dz, db = _gate_bwd(dck.reshape(hp, T), dcq, f_raw, b_pad, H, hp, "fox_gate_bwd")
    d_qkv = _mm(h, dproj, D, 3 * D, T, mode="tn", name="fox_dwqkv", out_dtypes=(F32,))
    d_f = _mm(h, dz, D, HEAD_DIM, T, mode="tn", name="fox_dwgate", out_dtypes=(F32,))
    dh = _mm(dproj, w_qkv, T, D, 3 * D, mode="nt", name="fox_dh", out_dtypes=(F32,))
    dh = _mm(dz, w_f, T, D, HEAD_DIM, mode="nt", name="fox_dh_gate", out_dtypes=(F32,), extras=(dh,),
             epilogue=lambda r, e: (e + r,))
    dx, dxb, dg = _rms_bwd(dh, x, g, dy, "fox_dnorm")
    dgains = jnp.stack([dgq.sum(axis=0), dgk.sum(axis=0)])
    return dx, dxb, dg, d_qkv, d_f, db, dgains, d_out, carried


def _dil_dims(D):
    nh = D // (2 * HEAD_DIM)
    return nh, D // nh, len(DIL_PATTERNS)


def _dil_layer_fwd(x, g, w_in, gains, w_out):
    T, D = x.shape
    nh, dv, G = _dil_dims(D)
    C = nh * HEAD_DIM
    W = 2 * G * C + nh * dv
    slopes = _slopes(G, nh)
    h = _rms_fwd(x, g, "dil_norm")
    proj = _mm(h, w_in, T, W, D, mode="nn", name="dil_proj", b_stack=N_CHIPS, out_dtypes=(F32,))
    qkv = _qkv_prep(proj, gains, 2 * G, lambda j: jnp.minimum(j, 2 * G - 1), C, "dil_qk_norm")
    parts = [_dil_fwd(qkv, gi, r, G, nh, dv, slopes[gi], f"dil_attn_g{gi}") for gi, (_, r) in enumerate(DIL_PATTERNS)]
    o, lse = _dil_merge([p[0] for p in parts], [p[1] for p in parts], [p[2] for p in parts], nh, dv, "dil_merge")
    y = _mm(o, w_out, T, D, D, mode="nn", name="dil_out", out_dtypes=(F32,), extras=(x,),
            epilogue=lambda r, res: (res + r,))
    return y, (x, h, proj, qkv, o, lse)


def _dil_layer_bwd(dy, dyb, saved, g, w_in, gains, w_out):
    x, h, proj, qkv, o, lse = saved
    T, D = x.shape
    nh, dv, G = _dil_dims(D)
    C = nh * HEAD_DIM
    W = 2 * G * C + nh * dv
    slopes = _slopes(G, nh)
    d_out = _mm(o, dyb, D, D, T, mode="tn", name="dil_dwout", out_dtypes=(F32,))
    do = _mm(dyb, w_out, T, D, D, mode="nt", name="dil_do", out_dtypes=(BF16,))
    dd = _row_dot(do, o, nh, dv, "dil_rowdot", True)
    dproj, dgs, dvs = None, [None] * (2 * G), []
    for gi, (_, r) in enumerate(DIL_PATTERNS):
        dq = _dil_dq(qkv, do, lse, dd, gi, r, G, nh, dv, slopes[gi], f"dil_dq_g{gi}")
        dk, dvg = _dil_dkv(qkv, do, lse, dd, gi, r, G, nh, dv, slopes[gi], f"dil_dkv_g{gi}")
        dvs.append(dvg)
        dproj, dgs[gi] = _head_rms_bwd_into(dproj, W, dq, proj, gains[gi], gi, C, f"dil_dq_norm_g{gi}")
        dproj, dgs[G + gi] = _head_rms_bwd_into(dproj, W, dk, proj, gains[G + gi], G + gi, C, f"dil_dk_norm_g{gi}")
    dproj = _sum_cast_into(dproj, W, dvs, 2 * G, C, "dil_dv_cast")
    d_in = _mm(h, dproj, D, W, T, mode="tn", name="dil_dwin", out_stack=N_CHIPS, out_dtypes=(F32,))
    dh = _mm(dproj, w_in, T, D, W, mode="nt", name="dil_dh", b_stack=N_CHIPS, out_dtypes=(F32,))
    dx, dxb, dg = _rms_bwd(dh, x, g, dy, "dil_dnorm")
    dgains = jnp.concatenate(dgs, axis=0)
    return dx, dxb, dg, d_in, dgains, d_out


def _local_step(x, tgt, w, late_shards=None, late_weights=None, early_reduce=None):
    rider = _gather_ici_rider(late_shards) if late_shards is not None else None
    y0, s_fox, landed = _fox_layer_fwd(x, w["mix_g"][0], w["fox_qkv"], w["fox_f"], w["fox_b"], w["fox_gains"], w["fox_out"],
                                       rider=rider)
    if rider:
        w = {**w, **late_weights(_forward_halves(landed))}
    y1, s_mlp0 = _mlp_fwd(y0, w["mlp_g"][0], w["up"][0], w["down"][0], "mlp0")
    y2, s_dil = _dil_layer_fwd(y1, w["mix_g"][1], w["dil_in"], w["dil_gains"], w["dil_out"])
    y3, s_mlp1 = _mlp_fwd(y2, w["mlp_g"][1], w["up"][1], w["down"][1], "mlp1")
    dy, dyb, loss = _loss_head(y3, tgt, "loss_head")
    g = {}
    dy, dyb, g_mlp1, up1, down1 = _mlp_bwd(dy, dyb, s_mlp1, w["mlp_g"][1], w["up"][1], w["down"][1], "mlp1")
    dy, dyb, g_mix1, g["dil_in"], g["dil_gains"], g["dil_out"] = _dil_layer_bwd(
        dy, dyb, s_dil, w["mix_g"][1], w["dil_in"], w["dil_gains"], w["dil_out"])
    dy, dyb, g_mlp0, up0, down0 = _mlp_bwd(dy, dyb, s_mlp0, w["mlp_g"][0], w["up"][0], w["down"][0], "mlp0")
    g["up"], g["down"] = (up0, up1), (down0, down1)
    rider = early_reduce(g) if early_reduce is not None else None
    dy, dyb, g_mix0, g["fox_qkv"], g["fox_f"], g["fox_b"], g["fox_gains"], g["fox_out"], early = _fox_layer_bwd(
        dy, dyb, s_fox, w["mix_g"][0], w["fox_qkv"], w["fox_f"], w["fox_b"], w["fox_gains"], w["fox_out"], rider=rider)
    g["mix_g"], g["mlp_g"] = (g_mix0, g_mix1), (g_mlp0, g_mlp1)
    return loss[0, 0], dy, g, early


def _place():
    x, y, c = lax.axis_index("x"), lax.axis_index("y"), lax.axis_index("c")
    chips = [(1 - x, y), (x, 1 - y), (1 - x, 1 - y)]
    return x, y, c, chips


def _remote(src, dst, send_sem, recv_sem, to):
    return pltpu.make_async_remote_copy(src_ref=src, dst_ref=dst, send_sem=send_sem, recv_sem=recv_sem,
                                        device_id=to, device_id_type=MESH)


def _gather_weights(shards):
    n = len(shards)

    def body(*refs):
        src, dst = refs[:n], refs[n:2 * n]
        send_sems, recv_sems, local_sems = refs[2 * n:]
        x, y, c, chips = _place()
        mine = 2 * x + y
        local = [pltpu.make_async_copy(src[t], dst[t].at[mine], local_sems.at[t]) for t in range(n)]
        for cp in local:
            cp.start()

        def half(t, slot, which):
            hr = shards[t].shape[0] // 2
            return dst[t].at[slot, pl.ds(which * hr, hr), :]

        def my_half(t):
            hr = shards[t].shape[0] // 2
            return src[t].at[pl.ds(c * hr, hr), :]

        sends = []
        for t in range(n):
            for j, (px, py) in enumerate(chips):
                cp = _remote(my_half(t), half(t, mine, c), send_sems.at[t, j], recv_sems.at[t, j], (px, py, c))
                cp.start()
                sends.append(cp)
        for j, (px, py) in enumerate(chips):
            for t in range(n):
                landed = half(t, 2 * px + py, c)
                _remote(landed, landed, send_sems.at[t, j], recv_sems.at[t, j], (px, py, c)).wait_recv()
                cp = _remote(landed, landed, send_sems.at[t, 3 + j], recv_sems.at[t, 3 + j], (x, y, 1 - c))
                cp.start()
                sends.append(cp)
        for j, (px, py) in enumerate(chips):
            for t in range(n):
                other = half(t, 2 * px + py, 1 - c)
                _remote(other, other, send_sems.at[t, 3 + j], recv_sems.at[t, 3 + j], (x, y, 1 - c)).wait_recv()
        for cp in sends:
            cp.wait_send()
        for cp in local:
            cp.wait()

    return pl.pallas_call(
        body, name="gather_weights",
        in_specs=[ANY] * n, out_specs=[ANY] * n,
        out_shape=[jax.ShapeDtypeStruct((N_CHIPS,) + s.shape, s.dtype) for s in shards],
        scratch_shapes=[pltpu.SemaphoreType.DMA((n, 6)), pltpu.SemaphoreType.DMA((n, 6)), pltpu.SemaphoreType.DMA((n,))],
        compiler_params=_params(has_side_effects=True),
    )(*shards)


class _Rider(NamedTuple):
    operands: tuple
    out_shapes: tuple
    scratch: tuple
    start: Callable
    finish: Callable


def _gather_ici_rider(shards):
    n = len(shards)

    def copies(src, dst, sems):
        send_sems, recv_sems, local_sems = sems
        x, y, c, chips = _place()
        mine = 2 * x + y
        local, sends, recvs = [], [], []
        for t in range(n):
            hr = shards[t].shape[0] // 2
            local.append(pltpu.make_async_copy(src[t], dst[t].at[mine], local_sems.at[t]))
            for j, (px, py) in enumerate(chips):
                sends.append(_remote(src[t].at[pl.ds(c * hr, hr), :], dst[t].at[mine, pl.ds(c * hr, hr), :],
                                     send_sems.at[t, j], recv_sems.at[t, j], (px, py, c)))
                landed = dst[t].at[2 * px + py, pl.ds(c * hr, hr), :]
                recvs.append(_remote(landed, landed, send_sems.at[t, j], recv_sems.at[t, j], (px, py, c)))
        return local, sends, recvs

    def start(src, dst, sems):
        local, sends, _ = copies(src, dst, sems)
        for cp in local + sends:
            cp.start()

    def finish(src, dst, sems):
        local, sends, recvs = copies(src, dst, sems)
        for cp in recvs:
            cp.wait_recv()
        for cp in sends:
            cp.wait_send()
        for cp in local:
            cp.wait()

    return _Rider(tuple(shards), tuple(jax.ShapeDtypeStruct((N_CHIPS,) + s.shape, s.dtype) for s in shards),
                  (pltpu.SemaphoreType.DMA((n, 3)), pltpu.SemaphoreType.DMA((n, 3)), pltpu.SemaphoreType.DMA((n,))),
                  start, finish)


def _forward_halves(landed):
    n = len(landed)

    def body(*refs):
        dst = refs[n:2 * n]
        send_sems, recv_sems = refs[2 * n:]
        x, y, c, chips = _place()
        sends = []
        for t in range(n):
            hr = landed[t].shape[1] // 2
            for j, (px, py) in enumerate(chips):
                got = dst[t].at[2 * px + py, pl.ds(c * hr, hr), :]
                cp = _remote(got, got, send_sems.at[t, j], recv_sems.at[t, j], (x, y, 1 - c))
                cp.start()
                sends.append(cp)
        for t in range(n):
            hr = landed[t].shape[1] // 2
            for j, (px, py) in enumerate(chips):
                other = dst[t].at[2 * px + py, pl.ds((1 - c) * hr, hr), :]
                _remote(other, other, send_sems.at[t, j], recv_sems.at[t, j], (x, y, 1 - c)).wait_recv()
        for cp in sends:
            cp.wait_send()

    return pl.pallas_call(
        body, name="gather_forward_halves",
        in_specs=[ANY] * n, out_specs=[ANY] * n,
        out_shape=[jax.ShapeDtypeStruct(a.shape, a.dtype) for a in landed],
        input_output_aliases={t: t for t in range(n)},
        scratch_shapes=[pltpu.SemaphoreType.DMA((n, 3)), pltpu.SemaphoreType.DMA((n, 3))],
        compiler_params=_params(has_side_effects=True),
    )(*landed)


def _pair_exchange(grads, name):
    n = len(grads)

    def body(*refs):
        src, dst = refs[:n], refs[n:2 * n]
        send_sems, recv_sems = refs[2 * n:]
        x, y, c, _ = _place()
        cps = []
        for t in range(n):
            hr = grads[t].shape[1] // 2
            cp = _remote(src[t].at[:, pl.ds((1 - c) * hr, hr), :], dst[t], send_sems.at[t], recv_sems.at[t], (x, y, 1 - c))
            cp.start()
            cps.append(cp)
        for cp in cps:
            cp.wait()

    return pl.pallas_call(
        body, name=name,
        in_specs=[ANY] * n, out_specs=[ANY] * n,
        out_shape=[jax.ShapeDtypeStruct((g.shape[0], g.shape[1] // 2, g.shape[2]), g.dtype) for g in grads],
        scratch_shapes=[pltpu.SemaphoreType.DMA((n,)), pltpu.SemaphoreType.DMA((n,))],
        compiler_params=_params(has_side_effects=True),
    )(*grads)


def _pair_add(g, got, cidx, name, tb=256):
    S, R, C = g.shape
    hr = R // 2
    tb = _rows_tile(hr, tb)
    nb = hr // tb

    def body(c_ref, a_ref, b_ref, o_ref):
        o_ref[...] = (a_ref[...] + b_ref[...]).astype(BF16)

    return pl.pallas_call(
        body, name=name,
        grid_spec=pltpu.PrefetchScalarGridSpec(
            num_scalar_prefetch=1, grid=(S, nb),
            in_specs=[pl.BlockSpec((None, tb, C), lambda s, i, c: (s, c[0] * nb + i, 0)),
                      pl.BlockSpec((None, tb, C), lambda s, i, c: (s, i, 0))],
            out_specs=pl.BlockSpec((None, tb, C), lambda s, i, c: (s, i, 0))),
        out_shape=jax.ShapeDtypeStruct((S, hr, C), BF16),
        compiler_params=_params("parallel", "parallel"),
    )(cidx, g, got)


def _rows_tile(n, want):
    t = min(n, want)
    while n % t or t % 8:
        t -= 8
    return t


def _chip_scatter_rider(sums):
    n = len(sums)

    def copies(src, dst, sems):
        send_sems, recv_sems, local_sems = sems
        x, y, c, chips = _place()
        mine = 2 * x + y
        local, sends, recvs = [], [], []
        for t in range(n):
            local.append(pltpu.make_async_copy(src[t].at[mine], dst[t].at[mine], local_sems.at[t]))
            for j, (px, py) in enumerate(chips):
                sends.append(_remote(src[t].at[2 * px + py], dst[t].at[mine], send_sems.at[t, j], recv_sems.at[t, j], (px, py, c)))
                slot = dst[t].at[2 * px + py]
                recvs.append(_remote(slot, slot, send_sems.at[t, j], recv_sems.at[t, j], (px, py, c)))
        return local, sends, recvs

    def start(src, dst, sems):
        local, sends, _ = copies(src, dst, sems)
        for cp in local + sends:
            cp.start()

    def finish(src, dst, sems):
        local, sends, recvs = copies(src, dst, sems)
        for cp in recvs:
            cp.wait_recv()
        for cp in sends:
            cp.wait_send()
        for cp in local:
            cp.wait()

    return _Rider(tuple(sums), tuple(jax.ShapeDtypeStruct(s.shape, s.dtype) for s in sums),
                  (pltpu.SemaphoreType.DMA((n, 3)), pltpu.SemaphoreType.DMA((n, 3)), pltpu.SemaphoreType.DMA((n,))),
                  start, finish)


def _run_rider(rider, name):
    r_in, r_out = len(rider.operands), len(rider.out_shapes)

    def body(*refs):
        src, dst, sems = refs[:r_in], refs[r_in:r_in + r_out], refs[r_in + r_out:]
        rider.start(src, dst, sems)
        rider.finish(src, dst, sems)

    return pl.pallas_call(
        body, name=name,
        in_specs=[ANY] * r_in, out_specs=[ANY] * r_out,
        out_shape=list(rider.out_shapes), scratch_shapes=list(rider.scratch),
        compiler_params=_params(has_side_effects=True),
    )(*rider.operands)


def _chip_sum(parts, cidx, name, tb=256):
    S, hr, C = parts.shape
    tb = _rows_tile(hr, tb)
    nb = hr // tb

    def body(c_ref, *refs):
        o_ref = refs[S]
        tot = refs[0][...].astype(F32)
        for s in range(1, S):
            tot = tot + refs[s][...].astype(F32)
        o_ref[...] = tot

    return pl.pallas_call(
        body, name=name,
        grid_spec=pltpu.PrefetchScalarGridSpec(
            num_scalar_prefetch=1, grid=(nb,),
            in_specs=[pl.BlockSpec((None, tb, C), functools.partial(lambda s, i, c: (s, i, 0), s)) for s in range(S)],
            out_specs=pl.BlockSpec((tb, C), lambda i, c: (c[0] * nb + i, 0))),
        out_shape=jax.ShapeDtypeStruct((2 * hr, C), F32),
        compiler_params=_params("parallel"),
    )(cidx, *([parts] * S))


def _half_exchange(halves):
    n = len(halves)

    def body(*refs):
        dst = refs[n:2 * n]
        send_sems, recv_sems = refs[2 * n:]
        x, y, c, _ = _place()
        cps = []
        for t in range(n):
            hr = halves[t].shape[0] // 2
            rows = dst[t].at[pl.ds(c * hr, hr), :]
            cp = _remote(rows, rows, send_sems.at[t], recv_sems.at[t], (x, y, 1 - c))
            cp.start()
            cps.append(cp)
        for t, cp in enumerate(cps):
            cp.wait_send()
            hr = halves[t].shape[0] // 2
            other = dst[t].at[pl.ds((1 - c) * hr, hr), :]
            _remote(other, other, send_sems.at[t], recv_sems.at[t], (x, y, 1 - c)).wait_recv()

    return pl.pallas_call(
        body, name="grad_half_exchange",
        in_specs=[ANY] * n, out_specs=[ANY] * n,
        out_shape=[jax.ShapeDtypeStruct(h.shape, h.dtype) for h in halves],
        input_output_aliases={t: t for t in range(n)},
        scratch_shapes=[pltpu.SemaphoreType.DMA((n,)), pltpu.SemaphoreType.DMA((n,))],
        compiler_params=_params(has_side_effects=True),
    )(*halves)


def _adamw_math(w, g, m, v):
    m = ADAM_B1 * m + (1.0 - ADAM_B1) * g
    v = ADAM_B2 * v + (1.0 - ADAM_B2) * (g * g)
    m_hat = m / (1.0 - ADAM_B1 ** ADAM_STEP)
    v_hat = v / (1.0 - ADAM_B2 ** ADAM_STEP)
    delta = -ADAM_LR * (m_hat / (jnp.sqrt(v_hat) + ADAM_EPS) + ADAM_WD * w)
    return delta, m, v


def _adamw(w, g, m, v, name, tb=256):
    R, C = w.shape
    tb = _rows_tile(R, tb)

    def body(w_ref, g_ref, m_ref, v_ref, d_ref, mo_ref, vo_ref):
        d, mn, vn = _adamw_math(w_ref[...], g_ref[...], m_ref[...], v_ref[...])
        d_ref[...] = d
        mo_ref[...] = mn
        vo_ref[...] = vn

    row = pl.BlockSpec((tb, C), lambda i: (i, 0))
    return pl.pallas_call(
        body, name=name, grid=(R // tb,),
        in_specs=[row] * 4, out_specs=[row] * 3,
        out_shape=[jax.ShapeDtypeStruct((R, C), F32)] * 3,
        compiler_params=_params("parallel"),
    )(w, g, m, v)


N_DEV = 8


def _small_update(g, w, m, v):
    P = g.shape[0]

    def body(g_ref, w_ref, m_ref, v_ref, go_ref, d_ref, mo_ref, vo_ref, buf, send_sems, recv_sems):
        x, y, c, _ = _place()
        me = 4 * x + 2 * y + c
        buf[me] = g_ref[...]
        cps = []
        for k in range(1, N_DEV):
            fx, fy, fc = (k >> 2) & 1, (k >> 1) & 1, k & 1
            px = (1 - x) if fx else x
            py = (1 - y) if fy else y
            pc = (1 - c) if fc else c
            cp = _remote(g_ref, buf.at[me], send_sems.at[k - 1], recv_sems.at[k - 1], (px, py, pc))
            cp.start()
            cps.append((cp, 4 * px + 2 * py + pc))
        for k, (cp, peer) in enumerate(cps):
            _remote(g_ref, buf.at[peer], send_sems.at[k], recv_sems.at[k], (x, y, c)).wait_recv()
        for cp, _ in cps:
            cp.wait_send()
        tot = buf[0]
        for d in range(1, N_DEV):
            tot = tot + buf[d]
        go_ref[...] = tot
        dl, mn, vn = _adamw_math(w_ref[...], tot, m_ref[...], v_ref[...])
        d_ref[...] = dl
        mo_ref[...] = mn
        vo_ref[...] = vn

    vm = pl.BlockSpec(memory_space=pltpu.VMEM)
    return pl.pallas_call(
        body, name="small_params_update",
        in_specs=[vm] * 4, out_specs=[vm] * 4,
        out_shape=[jax.ShapeDtypeStruct((P, HEAD_DIM), F32)] * 4,
        scratch_shapes=[pltpu.VMEM((N_DEV, P, HEAD_DIM), F32), pltpu.SemaphoreType.DMA((N_DEV - 1,)),
                        pltpu.SemaphoreType.DMA((N_DEV - 1,))],
        compiler_params=_params(has_side_effects=True),
    )(g, w, m, v)


SMALL = ("fox_b_f", "fox_q_gain", "fox_k_gain", "dil_q_gain", "dil_k_gain", "mix_norm_g", "mlp_norm_g")
LARGE = ("fox_w_in", "fox_w_out", "dil_w_in", "dil_w_out", "mlp_w_up", "mlp_w_down")
WEIGHTS = ("fox_w_in", "fox_b_f", "fox_q_gain", "fox_k_gain", "fox_w_out", "dil_w_in", "dil_q_gain", "dil_k_gain",
           "dil_w_out", "mix_norm_g", "mlp_norm_g", "mlp_w_up", "mlp_w_down")


def _pack(parts):
    rows = []
    for a in parts:
        flat = a.reshape(-1)
        n = -(-flat.shape[0] // (8 * HEAD_DIM)) * (8 * HEAD_DIM)
        rows.append(jnp.pad(flat, (0, n - flat.shape[0])).reshape(-1, HEAD_DIM))
    return jnp.concatenate(rows, axis=0)


def _unpack(packed, like):
    out, r = [], 0
    for a in like:
        size = int(np.prod(a.shape))
        n = -(-size // (8 * HEAD_DIM)) * 8
        out.append(packed[r:r + n].reshape(-1)[:size].reshape(a.shape))
        r += n
    return out


def _pad_lanes(a):
    return jnp.pad(a, [(0, 0)] * (a.ndim - 1) + [(0, HEAD_DIM - a.shape[-1])])


def _as_shards(a):
    return a.reshape(N_CHIPS, a.shape[0] // N_CHIPS, a.shape[1])


def kernel(x, fox_w_in, fox_b_f, fox_q_gain, fox_k_gain, fox_w_out, dil_w_in, dil_q_gain, dil_k_gain, dil_w_out, mix_norm_g, mlp_norm_g, mlp_w_up, mlp_w_down, loss_target, m_fox_w_in, m_fox_b_f, m_fox_q_gain, m_fox_k_gain, m_fox_w_out, m_dil_w_in, m_dil_q_gain, m_dil_k_gain, m_dil_w_out, m_mix_norm_g, m_mlp_norm_g, m_mlp_w_up, m_mlp_w_down, v_fox_w_in, v_fox_b_f, v_fox_q_gain, v_fox_k_gain, v_fox_w_out, v_dil_w_in, v_dil_q_gain, v_dil_k_gain, v_dil_w_out, v_mix_norm_g, v_mlp_norm_g, v_mlp_w_up, v_mlp_w_down):
    wts = dict(fox_w_in=fox_w_in, fox_b_f=fox_b_f, fox_q_gain=fox_q_gain, fox_k_gain=fox_k_gain, fox_w_out=fox_w_out,
               dil_w_in=dil_w_in, dil_q_gain=dil_q_gain, dil_k_gain=dil_k_gain, dil_w_out=dil_w_out,
               mix_norm_g=mix_norm_g, mlp_norm_g=mlp_norm_g, mlp_w_up=mlp_w_up, mlp_w_down=mlp_w_down)
    mom1 = dict(fox_w_in=m_fox_w_in, fox_b_f=m_fox_b_f, fox_q_gain=m_fox_q_gain, fox_k_gain=m_fox_k_gain,
                fox_w_out=m_fox_w_out, dil_w_in=m_dil_w_in, dil_q_gain=m_dil_q_gain, dil_k_gain=m_dil_k_gain,
                dil_w_out=m_dil_w_out, mix_norm_g=m_mix_norm_g, mlp_norm_g=m_mlp_norm_g, mlp_w_up=m_mlp_w_up,
                mlp_w_down=m_mlp_w_down)
    mom2 = dict(fox_w_in=v_fox_w_in, fox_b_f=v_fox_b_f, fox_q_gain=v_fox_q_gain, fox_k_gain=v_fox_k_gain,
                fox_w_out=v_fox_w_out, dil_w_in=v_dil_w_in, dil_q_gain=v_dil_q_gain, dil_k_gain=v_dil_k_gain,
                dil_w_out=v_dil_w_out, mix_norm_g=v_mix_norm_g, mlp_norm_g=v_mlp_norm_g, mlp_w_up=v_mlp_w_up,
                mlp_w_down=v_mlp_w_down)
    T, D = x.shape[1], x.shape[2]
    H = D // HEAD_DIM
    cidx = lax.axis_index("c").astype(jnp.int32).reshape(1)

    def shards_of(d):
        return [d["fox_w_in"][0], d["fox_w_out"][0], d["dil_w_in"][0], d["dil_w_out"][0],
                d["mlp_w_up"][0], d["mlp_w_up"][1], d["mlp_w_down"][0], d["mlp_w_down"][1]]

    w_sh, m_sh, v_sh = shards_of(wts), shards_of(mom1), shards_of(mom2)
    w_bf = [s.astype(BF16) for s in w_sh]
    first = _gather_weights(w_bf[:2])
    fox_in = jnp.moveaxis(first[0], 0, 1).reshape(D, -1)
    w = dict(
        fox_qkv=fox_in[:, :3 * D], fox_f=_pad_lanes(fox_in[:, 3 * D:]), fox_b=_pad_lanes(fox_b_f),
        fox_gains=jnp.stack([fox_q_gain, fox_k_gain]), fox_out=first[1].reshape(D, D),
        dil_gains=jnp.concatenate([dil_q_gain[0], dil_k_gain[0]])[:, None, :],
        mix_g=[mix_norm_g[0:1], mix_norm_g[1:2]], mlp_g=[mlp_norm_g[0:1], mlp_norm_g[1:2]])

    def late_weights(full):
        return dict(dil_in=full[0], dil_out=full[1].reshape(D, D), up=[full[2], full[3]],
                    down=[full[4].reshape(-1, D), full[5].reshape(-1, D)])

    def chip_sums(stacked, first, tag):
        got = _pair_exchange(stacked, f"grad_pair_exchange_{tag}")
        return [_pair_add(a, b, cidx, f"grad_pair_add_{first + t}") for t, (a, b) in enumerate(zip(stacked, got))]

    def early_reduce(g):
        stacked = [g["dil_in"], _as_shards(g["dil_out"]), g["up"][0], g["up"][1],
                   _as_shards(g["down"][0]), _as_shards(g["down"][1])]
        return _chip_scatter_rider(chip_sums(stacked, 2, "early"))

    loss, grad_x, g, early_parts = _local_step(x.reshape(T, D), loss_target.reshape(T, D), w, w_bf[2:], late_weights,
                                               early_reduce)
    loss = lax.psum(loss, ("x", "y", "c"))

    g_fox_in = jnp.concatenate([g["fox_qkv"], g["fox_f"][:, :H]], axis=1)
    g_fox_in = jnp.moveaxis(g_fox_in.reshape(D, N_CHIPS, -1), 1, 0)
    late_sums = chip_sums([g_fox_in, _as_shards(g["fox_out"])], 0, "late")
    parts = list(_run_rider(_chip_scatter_rider(late_sums), "grad_chip_scatter_late")) + early_parts
    halves = [_chip_sum(p, cidx, f"grad_chip_sum_{t}") for t, p in enumerate(parts)]
    totals = _half_exchange(halves)
    upd = [_adamw(w_sh[t], totals[t], m_sh[t], v_sh[t], f"adamw_{t}") for t in range(len(totals))]

    def large(k):
        a = [totals[t] if k == 0 else upd[t][k - 1] for t in range(len(totals))]
        return dict(fox_w_in=a[0][None], fox_w_out=a[1][None], dil_w_in=a[2][None], dil_w_out=a[3][None],
                    mlp_w_up=jnp.stack([a[4], a[5]]), mlp_w_down=jnp.stack([a[6], a[7]]))

    small_like = [wts[n] for n in SMALL]
    g_small = [g["fox_b"][:, :H], g["fox_gains"][0], g["fox_gains"][1], g["dil_gains"][:3, 0][None], g["dil_gains"][3:, 0][None],
               jnp.concatenate(g["mix_g"]), jnp.concatenate(g["mlp_g"])]
    packed = _small_update(_pack(g_small), _pack(small_like), _pack([mom1[n] for n in SMALL]), _pack([mom2[n] for n in SMALL]))
    small = [dict(zip(SMALL, _unpack(p, small_like))) for p in packed]

    outs = [loss, grad_x.reshape(x.shape)]
    for k in range(4):
        big = large(k)
        outs += [big[n] if n in big else small[k][n] for n in WEIGHTS]
    return tuple(outs)
```

```python
import functools
from typing import Callable, NamedTuple

import numpy as np
import jax
import jax.numpy as jnp
from jax import lax
from jax.experimental import pallas as pl
from jax.experimental.pallas import tpu as pltpu

F32 = jnp.float32
BF16 = jnp.bfloat16

HEAD_DIM = 128
DIL_PATTERNS = ((128, 1), (512, 4), (2048, 16))
DIL_SPAN = 128
ALIBI_MAX_EXP = 8.0
EPS = 1e-6
MASKED = -1e30

ADAM_LR = 0.001
ADAM_B1 = 0.9
ADAM_B2 = 0.999
ADAM_EPS = 1e-08
ADAM_WD = 0.01
ADAM_STEP = 10

N_CHIPS = 4
VMEM_LIMIT_BYTES = 56 * 1024 * 1024
MESH = pl.DeviceIdType.MESH
ANY = pl.BlockSpec(memory_space=pl.ANY)

NN = (((1,), (0,)), ((), ()))
NT = (((1,), (1,)), ((), ()))
TN = (((0,), (0,)), ((), ()))


def _params(*sem, **kw):
    return pltpu.CompilerParams(dimension_semantics=sem or None, vmem_limit_bytes=VMEM_LIMIT_BYTES, **kw)


def _dot(a, b, dims):
    return lax.dot_general(a, b, dims, preferred_element_type=F32)


def _tile(n, want):
    if n <= want:
        return n
    t = want - want % 128
    while n % t:
        t -= 128
    return t


def _mm(a, b, M, N, K, *, mode, name, out_dtypes, b_stack=0, out_stack=0, extras=(), epilogue=None,
        tm=1024, tn=1024, tk=2048):
    per_b = per_o = None
    if b_stack:
        per_b = (K if mode == "nt" else N) // b_stack
    if out_stack:
        per_o = N // out_stack
    tm = _tile(M, tm)
    tn = _tile(min(x for x in (N, per_o, per_b if mode != "nt" else None) if x), tn)
    tk = _tile(min(x for x in (K, per_b if mode == "nt" else None) if x), tk)
    assert M % tm == 0 and N % tn == 0 and K % tk == 0, (name, M, N, K, tm, tn, tk)
    gk = K // tk
    if mode == "tn":
        a_spec = pl.BlockSpec((tk, tm), lambda i, j, k: (k, i))
    else:
        a_spec = pl.BlockSpec((tm, tk), lambda i, j, k: (i, k))
    if mode == "nt":
        if b_stack:
            npk = per_b // tk
            b_spec = pl.BlockSpec((None, tn, tk), lambda i, j, k: (k // npk, j, k % npk))
        else:
            b_spec = pl.BlockSpec((tn, tk), lambda i, j, k: (j, k))
    else:
        if b_stack:
            npj = per_b // tn
            b_spec = pl.BlockSpec((None, tk, tn), lambda i, j, k: (j // npj, k, j % npj))
        else:
            b_spec = pl.BlockSpec((tk, tn), lambda i, j, k: (k, j))
    if out_stack:
        npo = per_o // tn
        o_spec = pl.BlockSpec((None, tm, tn), lambda i, j, k: (j // npo, i, j % npo))
        o_shape = (out_stack, M, per_o)
    else:
        o_spec = pl.BlockSpec((tm, tn), lambda i, j, k: (i, j))
        o_shape = (M, N)
    e_spec = pl.BlockSpec((tm, tn), lambda i, j, k: (i, j))
    dims = {"nn": NN, "nt": NT, "tn": TN}[mode]
    ne, no = len(extras), len(out_dtypes)

    def body(a_ref, b_ref, *rest):
        ex, outs = rest[:ne], rest[ne:ne + no]
        k = pl.program_id(2)

        def product():
            return _dot(a_ref[...].astype(BF16), b_ref[...].astype(BF16), dims)

        def finish(r):
            res = epilogue(r, *[e[...] for e in ex]) if epilogue is not None else (r,)
            for o, v in zip(outs, res):
                o[...] = v.astype(o.dtype)

        if gk == 1:
            finish(product())
            return
        acc = rest[ne + no]

        @pl.when(k == 0)
        def _():
            acc[...] = product()

        @pl.when((k > 0) & (k < gk - 1))
        def _():
            acc[...] += product()

        @pl.when(k == gk - 1)
        def _():
            finish(acc[...] + product())

    outs = pl.pallas_call(
        body, name=name,
        grid=(M // tm, N // tn, gk),
        in_specs=[a_spec, b_spec] + [e_spec] * ne,
        out_specs=[o_spec] * no,
        out_shape=[jax.ShapeDtypeStruct(o_shape, d) for d in out_dtypes],
        scratch_shapes=[pltpu.VMEM((tm, tn), F32)] if gk > 1 else [],
        compiler_params=_params("parallel", "parallel", "arbitrary"),
    )(a, b, *extras)
    return outs[0] if no == 1 else outs


def _rms_fwd(x, g, name, tb=512):
    T, D = x.shape
    tb = min(tb, T)

    def body(x_ref, g_ref, o_ref):
        xv = x_ref[...]
        r = lax.rsqrt(jnp.mean(xv * xv, axis=-1, keepdims=True) + EPS)
        o_ref[...] = (xv * r * g_ref[...]).astype(BF16)

    return pl.pallas_call(
        body, name=name, grid=(T // tb,),
        in_specs=[pl.BlockSpec((tb, D), lambda i: (i, 0)), pl.BlockSpec((1, D), lambda i: (0, 0))],
        out_specs=pl.BlockSpec((tb, D), lambda i: (i, 0)),
        out_shape=jax.ShapeDtypeStruct((T, D), BF16),
        compiler_params=_params("parallel"),
    )(x, g)


def _rms_bwd(dy, x, g, dres, name, tb=256):
    T, D = x.shape
    tb = min(tb, T)

    def body(dy_ref, x_ref, g_ref, dres_ref, dx_ref, dxb_ref, dg_ref):
        i = pl.program_id(0)
        xv, dyv = x_ref[...], dy_ref[...]
        r = lax.rsqrt(jnp.mean(xv * xv, axis=-1, keepdims=True) + EPS)
        gy = dyv * g_ref[...]
        dx = r * gy - xv * (r * r * r) * jnp.mean(gy * xv, axis=-1, keepdims=True)
        tot = dres_ref[...] + dx
        dx_ref[...] = tot
        dxb_ref[...] = tot.astype(BF16)
        part = jnp.sum(dyv * (xv * r), axis=0, keepdims=True)

        @pl.when(i == 0)
        def _():
            dg_ref[...] = part

        @pl.when(i > 0)
        def _():
            dg_ref[...] += part

    row = pl.BlockSpec((tb, D), lambda i: (i, 0))
    vec = pl.BlockSpec((1, D), lambda i: (0, 0))
    return pl.pallas_call(
        body, name=name, grid=(T // tb,),
        in_specs=[row, row, vec, row],
        out_specs=[row, row, vec],
        out_shape=[jax.ShapeDtypeStruct((T, D), F32), jax.ShapeDtypeStruct((T, D), BF16),
                   jax.ShapeDtypeStruct((1, D), F32)],
        compiler_params=_params("arbitrary"),
    )(dy, x, g, dres)


def _loss_head(y, tgt, name, tb=256):
    T, D = y.shape
    tb = min(tb, T)

    def body(y_ref, t_ref, dy_ref, dyb_ref, loss_ref):
        i = pl.program_id(0)
        e = y_ref[...] - t_ref[...]
        d = e * (1.0 / D)
        dy_ref[...] = d
        dyb_ref[...] = d.astype(BF16)
        part = 0.5 * jnp.sum(jnp.sum(e * e, axis=1, keepdims=True) * (1.0 / D), axis=0, keepdims=True)

        @pl.when(i == 0)
        def _():
            loss_ref[...] = part

        @pl.when(i > 0)
        def _():
            loss_ref[...] += part

    row = pl.BlockSpec((tb, D), lambda i: (i, 0))
    return pl.pallas_call(
        body, name=name, grid=(T // tb,),
        in_specs=[row, row],
        out_specs=[row, row, pl.BlockSpec((1, 1), lambda i: (0, 0))],
        out_shape=[jax.ShapeDtypeStruct((T, D), F32), jax.ShapeDtypeStruct((T, D), BF16),
                   jax.ShapeDtypeStruct((1, 1), F32)],
        compiler_params=_params("arbitrary"),
    )(y, tgt)


def _head_rms(xh, g):
    r = lax.rsqrt(jnp.mean(xh * xh, axis=-1, keepdims=True) + EPS)
    return xh * r * g


def _qkv_prep(proj, gains, n_norm, gain_row, ch, name, n_scaled=0, post_scale=1.0, tb=512):
    T, W = proj.shape
    tb = min(tb, T)
    nch = W // ch
    nh = ch // HEAD_DIM

    def body(p_ref, g_ref, o_ref):
        j = pl.program_id(0)

        @pl.when(j < n_norm)
        def _():
            g = g_ref[...]
            if n_scaled:
                g = g * jnp.where(j < n_scaled, post_scale, 1.0)
            for h in range(nh):
                sl = slice(h * HEAD_DIM, (h + 1) * HEAD_DIM)
                o_ref[:, sl] = _head_rms(p_ref[:, sl], g).astype(BF16)

        @pl.when(j >= n_norm)
        def _():
            o_ref[...] = p_ref[...].astype(BF16)

    return pl.pallas_call(
        body, name=name, grid=(nch, T // tb),
        in_specs=[pl.BlockSpec((tb, ch), lambda j, i: (i, j)),
                  pl.BlockSpec((None, 1, HEAD_DIM), lambda j, i: (gain_row(j), 0, 0))],
        out_specs=pl.BlockSpec((tb, ch), lambda j, i: (i, j)),
        out_shape=jax.ShapeDtypeStruct((T, W), BF16),
        compiler_params=_params("parallel", "parallel"),
    )(proj, gains)


def _into(body, name, grid, in_specs, out_spec, out_shape, extra_out_specs, extra_out_shapes, buf, operands, sem):
    if buf is None:
        def kernel(*refs):
            body(*refs)
        ins, alias, ops = in_specs, {}, operands
    else:
        def kernel(_, *refs):
            body(*refs)
        ins = [pl.BlockSpec(memory_space=pl.ANY)] + in_specs
        alias, ops = {0: 0}, (buf,) + tuple(operands)
    return pl.pallas_call(
        kernel, name=name, grid=grid, in_specs=ins,
        out_specs=[out_spec] + extra_out_specs,
        out_shape=[out_shape] + extra_out_shapes,
        input_output_aliases=alias,
        compiler_params=_params(*sem),
    )(*ops)


def _head_rms_bwd_into(buf, W, d, proj, gain, off, ch, name, in_scale=1.0, tb=256):
    T, wd = d.shape
    tb = min(tb, T)
    n = wd // ch
    nh = ch // HEAD_DIM

    def body(d_ref, p_ref, g_ref, o_ref, dg_ref):
        i = pl.program_id(1)
        g = g_ref[...]
        part = jnp.zeros((1, HEAD_DIM), F32)
        for h in range(nh):
            sl = slice(h * HEAD_DIM, (h + 1) * HEAD_DIM)
            xh, dy = p_ref[:, sl], d_ref[:, sl]
            if in_scale != 1.0:
                dy = dy * in_scale
            r = lax.rsqrt(jnp.mean(xh * xh, axis=-1, keepdims=True) + EPS)
            gy = dy * g
            dx = r * gy - xh * (r * r * r) * jnp.mean(gy * xh, axis=-1, keepdims=True)
            o_ref[:, sl] = dx.astype(BF16)
            part = part + jnp.sum(dy * (xh * r), axis=0, keepdims=True)

        @pl.when(i == 0)
        def _():
            dg_ref[...] = part

        @pl.when(i > 0)
        def _():
            dg_ref[...] += part

    return _into(
        body, name, (n, T // tb),
        [pl.BlockSpec((tb, ch), lambda j, i: (i, j)), pl.BlockSpec((tb, ch), lambda j, i: (i, off + j)),
         pl.BlockSpec((1, HEAD_DIM), lambda j, i: (0, 0))],
        pl.BlockSpec((tb, ch), lambda j, i: (i, off + j)), jax.ShapeDtypeStruct((T, W), BF16),
        [pl.BlockSpec((None, 1, HEAD_DIM), lambda j, i: (j, 0, 0))], [jax.ShapeDtypeStruct((n, 1, HEAD_DIM), F32)],
        buf, (d, proj, gain), ("parallel", "arbitrary"))


def _sum_cast_into(buf, W, srcs, off, ch, name, tb=256):
    T, wd = srcs[0].shape
    tb = min(tb, T)
    n = wd // ch
    ns = len(srcs)

    def body(*refs):
        o_ref = refs[ns]
        tot = refs[0][...]
        for s in refs[1:ns]:
            tot = tot + s[...]
        o_ref[...] = tot.astype(BF16)

    out = _into(
        body, name, (n, T // tb),
        [pl.BlockSpec((tb, ch), lambda j, i: (i, j))] * ns,
        pl.BlockSpec((tb, ch), lambda j, i: (i, off + j)), jax.ShapeDtypeStruct((T, W), BF16),
        [], [], buf, tuple(srcs), ("parallel", "parallel"))
    return out[0]


def _tri(n, lower):
    r = lax.broadcasted_iota(jnp.int32, (n, n), 0)
    c = lax.broadcasted_iota(jnp.int32, (n, n), 1)
    return jnp.where((c <= r) if lower else (c >= r), 1.0, 0.0).astype(F32)


def _dot_exact(a, b):
    return lax.dot_general(a, b, NN, precision=lax.Precision.HIGHEST, preferred_element_type=F32)


def _log_sigmoid(z):
    return jnp.minimum(z, 0.0) - jnp.log(1.0 + jnp.exp(-jnp.abs(z)))


def _gate_fwd(f_raw, b_pad, hp, out_scale, name, blk=256):
    T = f_raw.shape[0]
    blk = min(blk, T)

    def body(f_ref, b_ref, c_ref):
        tri = _tri(blk, True)
        carry = jnp.zeros((1, HEAD_DIM), F32)
        for j in range(T // blk):
            lf = _log_sigmoid(f_ref[j * blk:(j + 1) * blk, :] + b_ref[...])
            cb = _dot_exact(tri, lf) + carry
            carry = cb[blk - 1:blk, :]
            c_ref[:, j * blk:(j + 1) * blk] = cb.T[:hp, :] * out_scale

    return pl.pallas_call(
        body, name=name,
        in_specs=[pl.BlockSpec(memory_space=pltpu.VMEM)] * 2,
        out_specs=pl.BlockSpec(memory_space=pltpu.VMEM),
        out_shape=jax.ShapeDtypeStruct((hp, T), F32),
        compiler_params=_params(),
    )(f_raw, b_pad)


def _gate_bwd(dc_rows, dc_cols, f_raw, b_pad, n_heads, hp, name, blk=256):
    T = f_raw.shape[0]
    blk = min(blk, T)

    def body(dc_ref, dcc_ref, f_ref, b_ref, dz_ref, db_ref):
        tri = _tri(blk, False)
        lane = lax.broadcasted_iota(jnp.int32, (blk, HEAD_DIM), 1)
        carry = jnp.zeros((1, HEAD_DIM), F32)
        db = jnp.zeros((1, HEAD_DIM), F32)
        for j in reversed(range(T // blk)):
            rows = dc_ref[:, j * blk:(j + 1) * blk]
            if hp < HEAD_DIM:
                rows = jnp.concatenate([rows, jnp.zeros((HEAD_DIM - hp, blk), F32)], axis=0)
            dlf = _dot_exact(tri, rows.T + dcc_ref[j * blk:(j + 1) * blk, :]) + carry
            carry = dlf[0:1, :]
            z = f_ref[j * blk:(j + 1) * blk, :] + b_ref[...]
            dz = jnp.where(lane < n_heads, dlf / (1.0 + jnp.exp(z)), 0.0)
            dz_ref[j * blk:(j + 1) * blk, :] = dz.astype(BF16)
            db = db + jnp.sum(dz, axis=0, keepdims=True)
        db_ref[...] = db

    return pl.pallas_call(
        body, name=name,
        in_specs=[pl.BlockSpec(memory_space=pltpu.VMEM)] * 4,
        out_specs=[pl.BlockSpec(memory_space=pltpu.VMEM)] * 2,
        out_shape=[jax.ShapeDtypeStruct((T, HEAD_DIM), BF16), jax.ShapeDtypeStruct((1, HEAD_DIM), F32)],
        compiler_params=_params(),
    )(dc_rows, dc_cols, f_raw, b_pad)


def _pairs(nb, key_major):
    if key_major:
        pairs = [(qi, ki) for ki in range(nb) for qi in range(ki, nb)]
    else:
        pairs = [(qi, ki) for qi in range(nb) for ki in range(qi + 1)]
    return (jnp.asarray(np.array([p[0] for p in pairs], np.int32)),
            jnp.asarray(np.array([p[1] for p in pairs], np.int32)))


LOG2E = 1.4426950408889634
LN2 = 0.6931471805599453
FOX_Q_SCALE = HEAD_DIM ** -0.5 * LOG2E


def _fox_logits(q, k, ck_row, diagonal):
    s = _dot(q, k, NT) - ck_row
    if diagonal:
        row = lax.broadcasted_iota(jnp.int32, s.shape, 0)
        col = lax.broadcasted_iota(jnp.int32, s.shape, 1)
        s = jnp.where(col <= row, s, MASKED)
    return s


def _fox_fwd(qkv, ck, H, name, tb=1024, rider=None):
    T = qkv.shape[0]
    tb = min(tb, T)
    nb = T // tb
    qt, kt = _pairs(nb, False)
    n_pairs = int(qt.shape[0])
    r_in, r_out, r_scr = (len(rider.operands), len(rider.out_shapes), len(rider.scratch)) if rider else (0, 0, 0)

    def body(qt_ref, kt_ref, q_ref, k_ref, v_ref, ck_ref, *rest):
        r_src, (o_ref, lse_ref), r_dst = rest[:r_in], rest[r_in:r_in + 2], rest[r_in + 2:r_in + 2 + r_out]
        m_sc, l_sc, acc_sc = rest[r_in + 2 + r_out:r_in + 5 + r_out]
        r_sems = rest[r_in + 5 + r_out:]
        p_ = pl.program_id(1)
        qi, ki = qt_ref[p_], kt_ref[p_]
        if rider:
            @pl.when((pl.program_id(0) == 0) & (p_ == 0))
            def _():
                rider.start(r_src, r_dst, r_sems)

        @pl.when(ki == 0)
        def _():
            m_sc[...] = jnp.full_like(m_sc, MASKED)
            l_sc[...] = jnp.zeros_like(l_sc)
            acc_sc[...] = jnp.zeros_like(acc_sc)

        def tile(diagonal):
            s = _fox_logits(q_ref[...], k_ref[...], ck_ref[...], diagonal)
            m_prev = m_sc[...]
            m_new = jnp.maximum(m_prev, jnp.max(s, axis=1, keepdims=True))
            alpha = jnp.exp2(m_prev - m_new)
            p = jnp.exp2(s - m_new[:, :1])
            l_sc[...] = alpha * l_sc[...] + jnp.sum(p, axis=1, keepdims=True)
            acc_sc[...] = alpha * acc_sc[...] + _dot(p.astype(BF16), v_ref[...], NN)
            m_sc[...] = m_new

        @pl.when(ki < qi)
        def _():
            tile(False)

        @pl.when(ki == qi)
        def _():
            tile(True)
            o_ref[...] = (acc_sc[...] / l_sc[...]).astype(BF16)
            lse_ref[...] = m_sc[...] + jnp.log(l_sc[...]) * LOG2E

        if rider:
            @pl.when((pl.program_id(0) == H - 1) & (p_ == n_pairs - 1))
            def _():
                rider.finish(r_src, r_dst, r_sems)

    blk = lambda f: pl.BlockSpec((tb, HEAD_DIM), f)
    outs = pl.pallas_call(
        body, name=name,
        grid_spec=pltpu.PrefetchScalarGridSpec(
            num_scalar_prefetch=2, grid=(H, n_pairs),
            in_specs=[blk(lambda h, p, qt, kt: (qt[p], h)),
                      blk(lambda h, p, qt, kt: (kt[p], H + h)),
                      blk(lambda h, p, qt, kt: (kt[p], 2 * H + h)),
                      pl.BlockSpec((None, 1, tb), lambda h, p, qt, kt: (h, 0, kt[p]))] + [ANY] * r_in,
            out_specs=[blk(lambda h, p, qt, kt: (qt[p], h)), blk(lambda h, p, qt, kt: (qt[p], h))] + [ANY] * r_out,
            scratch_shapes=[pltpu.VMEM((tb, HEAD_DIM), F32)] * 3 + (list(rider.scratch) if rider else [])),
        out_shape=[jax.ShapeDtypeStruct((T, H * HEAD_DIM), BF16), jax.ShapeDtypeStruct((T, H * HEAD_DIM), F32)]
        + (list(rider.out_shapes) if rider else []),
        compiler_params=_params("arbitrary", "arbitrary", has_side_effects=bool(rider)),
    )(qt, kt, qkv, qkv, qkv, ck, *(rider.operands if rider else ()))
    return outs[0], outs[1], list(outs[2:])


def _row_dot(do, o, nh, width, name, lane_per_head, tb=256):
    T = do.shape[0]
    tb = min(tb, T)
    wout = HEAD_DIM if lane_per_head else nh * HEAD_DIM

    def body(do_ref, o_ref, d_ref):
        lane = lax.broadcasted_iota(jnp.int32, (tb, HEAD_DIM), 1)
        tile = jnp.zeros((tb, HEAD_DIM), F32)
        for h in range(nh):
            sl = slice(h * width, (h + 1) * width)
            d = jnp.sum(do_ref[:, sl].astype(F32) * o_ref[:, sl].astype(F32), axis=1, keepdims=True)
            if lane_per_head:
                tile = jnp.where(lane == h, d, tile)
            else:
                d_ref[:, h * HEAD_DIM:(h + 1) * HEAD_DIM] = jnp.broadcast_to(d, (tb, HEAD_DIM))
        if lane_per_head:
            d_ref[...] = tile

    row = pl.BlockSpec((tb, nh * width), lambda i: (i, 0))
    return pl.pallas_call(
        body, name=name, grid=(T // tb,),
        in_specs=[row, row], out_specs=pl.BlockSpec((tb, wout), lambda i: (i, 0)),
        out_shape=jax.ShapeDtypeStruct((T, wout), F32),
        compiler_params=_params("parallel"),
    )(do, o)


def _fox_bwd(qkv, do, ck, lse, dd, H, hp, name, tb=1024, rider=None):
    T = qkv.shape[0]
    tb = min(tb, T)
    nb = T // tb
    qt, kt = _pairs(nb, True)
    n_pairs = int(qt.shape[0])
    r_in, r_out = (len(rider.operands), len(rider.out_shapes)) if rider else (0, 0)

    def body(qt_ref, kt_ref, q_ref, k_ref, v_ref, do_ref, ck_ref, lse_ref, dd_ref, *rest):
        r_src, r_dst, r_sems = rest[:r_in], rest[r_in + 5:r_in + 5 + r_out], rest[r_in + 5 + r_out:]
        dq_ref, dk_ref, dv_ref, dc_ref, dcq_ref = rest[r_in:r_in + 5]
        p_ = pl.program_id(1)
        qi, ki = qt_ref[p_], kt_ref[p_]
        if rider:
            @pl.when((pl.program_id(0) == 0) & (p_ == 0))
            def _():
                rider.start(r_src, r_dst, r_sems)

        @pl.when(p_ == 0)
        def _():
            dq_ref[...] = jnp.zeros_like(dq_ref)
            dcq_ref[...] = jnp.zeros_like(dcq_ref)

        @pl.when(qi == ki)
        def _():
            dk_ref[...] = jnp.zeros_like(dk_ref)
            dv_ref[...] = jnp.zeros_like(dv_ref)
            dc_ref[...] = jnp.zeros_like(dc_ref)

        rows = pl.ds(pl.multiple_of(qi * tb, tb), tb)

        def tile(diagonal):
            s = _fox_logits(q_ref[...], k_ref[...], ck_ref[...], diagonal)
            p = jnp.exp2(s - lse_ref[:, :1])
            dv_ref[...] += _dot(p.astype(BF16), do_ref[...], TN)
            dp = _dot(do_ref[...], v_ref[...], NT)
            ds = p * (dp - dd_ref[:, :1])
            dc_ref[...] -= jnp.sum(ds, axis=0, keepdims=True)
            dcq_ref[rows, :] += jnp.sum(ds, axis=1, keepdims=True)
            dsb = ds.astype(BF16)
            dq_ref[rows, :] += _dot(dsb, k_ref[...], NN)
            dk_ref[...] += _dot(dsb, q_ref[...], TN)

        @pl.when(ki < qi)
        def _():
            tile(False)

        @pl.when(ki == qi)
        def _():
            tile(True)

        if rider:
            @pl.when((pl.program_id(0) == H - 1) & (p_ == n_pairs - 1))
            def _():
                rider.finish(r_src, r_dst, r_sems)

    blk = lambda f: pl.BlockSpec((tb, HEAD_DIM), f)
    at_q = lambda h, p, qt, kt: (qt[p], h)
    at_k = lambda h, p, qt, kt: (kt[p], h)
    crow = pl.BlockSpec((None, 1, tb), lambda h, p, qt, kt: (h, 0, kt[p]))
    whole = pl.BlockSpec((T, HEAD_DIM), lambda h, p, qt, kt: (0, h))
    wide = jax.ShapeDtypeStruct((T, H * HEAD_DIM), F32)
    outs = pl.pallas_call(
        body, name=name,
        grid_spec=pltpu.PrefetchScalarGridSpec(
            num_scalar_prefetch=2, grid=(H, n_pairs),
            in_specs=[blk(at_q),
                      blk(lambda h, p, qt, kt: (kt[p], H + h)),
                      blk(lambda h, p, qt, kt: (kt[p], 2 * H + h)),
                      blk(at_q), crow, blk(at_q), blk(at_q)] + [ANY] * r_in,
            out_specs=[whole, blk(at_k), blk(at_k), crow, whole] + [ANY] * r_out,
            scratch_shapes=list(rider.scratch) if rider else []),
        out_shape=[wide, wide, wide, jax.ShapeDtypeStruct((hp, 1, T), F32), wide] + (list(rider.out_shapes) if rider else []),
        compiler_params=_params("arbitrary", "arbitrary", has_side_effects=bool(rider)),
    )(qt, kt, qkv, qkv, qkv, do, ck, lse, dd, *(rider.operands if rider else ()))
    return outs[:5], list(outs[5:])


def _lane_per_head(wide, H, name, tb=256):
    T = wide.shape[0]
    tb = min(tb, T)

    def body(w_ref, o_ref):
        lane = lax.broadcasted_iota(jnp.int32, (tb, HEAD_DIM), 1)
        tile = jnp.zeros((tb, HEAD_DIM), F32)
        for h in range(H):
            tile = jnp.where(lane == h, w_ref[:, h * HEAD_DIM:(h + 1) * HEAD_DIM], tile)
        o_ref[...] = tile

    return pl.pallas_call(
        body, name=name, grid=(T // tb,),
        in_specs=[pl.BlockSpec((tb, H * HEAD_DIM), lambda i: (i, 0))],
        out_specs=pl.BlockSpec((tb, HEAD_DIM), lambda i: (i, 0)),
        out_shape=jax.ShapeDtypeStruct((T, HEAD_DIM), F32),
        compiler_params=_params("parallel"),
    )(wide)


def _slopes(n_groups, nh):
    n = n_groups * nh
    s = np.exp2(-ALIBI_MAX_EXP * np.arange(1, n + 1, dtype=np.float32) / np.float32(n)).astype(np.float32)
    return s.reshape(n_groups, nh)


def _window_logits(qh, kh, slope_r, prev, has_prev):
    qi = lax.broadcasted_iota(jnp.int32, (DIL_SPAN, DIL_SPAN), 0)
    kl = lax.broadcasted_iota(jnp.int32, (DIL_SPAN, DIL_SPAN), 1)
    delta = qi - kl + (DIL_SPAN if prev else 0)
    s = _dot(qh, kh, NT) * (HEAD_DIM ** -0.5) - slope_r * delta.astype(F32)
    valid = ((kl >= qi) & has_prev) if prev else (kl <= qi)
    return jnp.where(valid, s, MASKED)


def _dil_views(T, r, G, nh, dv):
    L = T // r
    C, V = nh * HEAD_DIM, nh * dv
    return L, C, V, 2 * G * C + V


def _dil_fwd(qkv, g, r, G, nh, dv, slopes, name):
    T = qkv.shape[0]
    L, C, V, W = _dil_views(T, r, G, nh, dv)
    nblk = L // DIL_SPAN
    nc, nv = W // C, W // V
    view = qkv.reshape(L, r * W)

    def body(q_ref, kp_ref, kc_ref, vp_ref, vc_ref, num_ref, m_ref, den_ref):
        has_prev = pl.program_id(1) > 0
        lane = lax.broadcasted_iota(jnp.int32, (DIL_SPAN, HEAD_DIM), 1)
        m_tile = jnp.zeros((DIL_SPAN, HEAD_DIM), F32)
        den_tile = jnp.ones((DIL_SPAN, HEAD_DIM), F32)
        for h in range(nh):
            sl = slice(h * HEAD_DIM, (h + 1) * HEAD_DIM)
            vs = slice(h * dv, (h + 1) * dv)
            sr = float(slopes[h]) * r
            sc = _window_logits(q_ref[:, sl], kc_ref[:, sl], sr, False, has_prev)
            sp = _window_logits(q_ref[:, sl], kp_ref[:, sl], sr, True, has_prev)
            m = jnp.maximum(jnp.max(sc, axis=1, keepdims=True), jnp.max(sp, axis=1, keepdims=True))
            pc, pp = jnp.exp(sc - m), jnp.exp(sp - m)
            den = jnp.sum(pc, axis=1, keepdims=True) + jnp.sum(pp, axis=1, keepdims=True)
            num_ref[:, vs] = _dot(pc.astype(BF16), vc_ref[:, vs], NN) + _dot(pp.astype(BF16), vp_ref[:, vs], NN)
            m_tile = jnp.where(lane == h, m, m_tile)
            den_tile = jnp.where(lane == h, den, den_tile)
        m_ref[...] = m_tile
        den_ref[...] = den_tile

    prev = lambda i: jnp.maximum(i - 1, 0)
    stat = pl.BlockSpec((DIL_SPAN, HEAD_DIM), lambda b, i: (i, b))
    num, m, den = pl.pallas_call(
        body, name=name, grid=(r, nblk),
        in_specs=[pl.BlockSpec((DIL_SPAN, C), lambda b, i: (i, b * nc + g)),
                  pl.BlockSpec((DIL_SPAN, C), lambda b, i: (prev(i), b * nc + G + g)),
                  pl.BlockSpec((DIL_SPAN, C), lambda b, i: (i, b * nc + G + g)),
                  pl.BlockSpec((DIL_SPAN, V), lambda b, i: (prev(i), b * nv + nv - 1)),
                  pl.BlockSpec((DIL_SPAN, V), lambda b, i: (i, b * nv + nv - 1))],
        out_specs=[pl.BlockSpec((DIL_SPAN, V), lambda b, i: (i, b)), stat, stat],
        out_shape=[jax.ShapeDtypeStruct((L, r * V), F32), jax.ShapeDtypeStruct((L, r * HEAD_DIM), F32),
                   jax.ShapeDtypeStruct((L, r * HEAD_DIM), F32)],
        compiler_params=_params("parallel", "parallel"),
    )(view, view, view, view, view)
    return num.reshape(T, V), m.reshape(T, HEAD_DIM), den.reshape(T, HEAD_DIM)


def _dil_merge(nums, ms, dens, nh, dv, name, tb=256):
    T, V = nums[0].shape
    tb = min(tb, T)
    G = len(nums)

    def body(*refs):
        num_r, m_r, den_r = refs[:G], refs[G:2 * G], refs[2 * G:3 * G]
        o_ref, lse_ref = refs[3 * G], refs[3 * G + 1]
        mm = m_r[0][...]
        for g in range(1, G):
            mm = jnp.maximum(mm, m_r[g][...])
        w = [jnp.exp(m_r[g][...] - mm) for g in range(G)]
        den = w[0] * den_r[0][...]
        for g in range(1, G):
            den = den + w[g] * den_r[g][...]
        lse_ref[...] = mm + jnp.log(den)
        for h in range(nh):
            vs = slice(h * dv, (h + 1) * dv)
            num = w[0][:, h:h + 1] * num_r[0][:, vs]
            for g in range(1, G):
                num = num + w[g][:, h:h + 1] * num_r[g][:, vs]
            o_ref[:, vs] = (num / den[:, h:h + 1]).astype(BF16)

    wide = pl.BlockSpec((tb, V), lambda i: (i, 0))
    stat = pl.BlockSpec((tb, HEAD_DIM), lambda i: (i, 0))
    return pl.pallas_call(
        body, name=name, grid=(T // tb,),
        in_specs=[wide] * G + [stat] * (2 * G),
        out_specs=[wide, stat],
        out_shape=[jax.ShapeDtypeStruct((T, V), BF16), jax.ShapeDtypeStruct((T, HEAD_DIM), F32)],
        compiler_params=_params("parallel"),
    )(*nums, *ms, *dens)


def _dil_dq(qkv, do, lse, dd, g, r, G, nh, dv, slopes, name):
    T = qkv.shape[0]
    L, C, V, W = _dil_views(T, r, G, nh, dv)
    nblk = L // DIL_SPAN
    nc, nv = W // C, W // V
    view = qkv.reshape(L, r * W)
    scale = HEAD_DIM ** -0.5

    def body(q_ref, kp_ref, kc_ref, vp_ref, vc_ref, do_ref, lse_ref, dd_ref, dq_ref):
        has_prev = pl.program_id(1) > 0
        for h in range(nh):
            sl = slice(h * HEAD_DIM, (h + 1) * HEAD_DIM)
            vs = slice(h * dv, (h + 1) * dv)
            sr = float(slopes[h]) * r
            lse_h, dd_h = lse_ref[:, h:h + 1], dd_ref[:, h:h + 1]
            acc = jnp.zeros((DIL_SPAN, HEAD_DIM), F32)
            for k_ref, v_ref, is_prev in ((kc_ref, vc_ref, False), (kp_ref, vp_ref, True)):
                s = _window_logits(q_ref[:, sl], k_ref[:, sl], sr, is_prev, has_prev)
                p = jnp.exp(s - lse_h)
                dp = _dot(do_ref[:, vs], v_ref[:, vs], NT)
                ds = (p * (dp - dd_h)).astype(BF16)
                acc = acc + _dot(ds, k_ref[:, sl], NN)
            dq_ref[:, sl] = scale * acc

    prev = lambda i: jnp.maximum(i - 1, 0)
    stat = pl.BlockSpec((DIL_SPAN, HEAD_DIM), lambda b, i: (i, b))
    dq = pl.pallas_call(
        body, name=name, grid=(r, nblk),
        in_specs=[pl.BlockSpec((DIL_SPAN, C), lambda b, i: (i, b * nc + g)),
                  pl.BlockSpec((DIL_SPAN, C), lambda b, i: (prev(i), b * nc + G + g)),
                  pl.BlockSpec((DIL_SPAN, C), lambda b, i: (i, b * nc + G + g)),
                  pl.BlockSpec((DIL_SPAN, V), lambda b, i: (prev(i), b * nv + nv - 1)),
                  pl.BlockSpec((DIL_SPAN, V), lambda b, i: (i, b * nv + nv - 1)),
                  pl.BlockSpec((DIL_SPAN, V), lambda b, i: (i, b)), stat, stat],
        out_specs=pl.BlockSpec((DIL_SPAN, C), lambda b, i: (i, b)),
        out_shape=jax.ShapeDtypeStruct((L, r * C), F32),
        compiler_params=_params("parallel", "parallel"),
    )(view, view, view, view, view, do.reshape(L, r * V), lse.reshape(L, r * HEAD_DIM), dd.reshape(L, r * HEAD_DIM))
    return dq.reshape(T, C)


def _dil_dkv(qkv, do, lse, dd, g, r, G, nh, dv, slopes, name):
    T = qkv.shape[0]
    L, C, V, W = _dil_views(T, r, G, nh, dv)
    nblk = L // DIL_SPAN
    nc, nv = W // C, W // V
    view = qkv.reshape(L, r * W)
    scale = HEAD_DIM ** -0.5

    def body(k_ref, v_ref, qc_ref, qn_ref, doc_ref, don_ref, lsec_ref, lsen_ref, ddc_ref, ddn_ref, dk_ref, dv_ref):
        has_next = pl.program_id(1) < nblk - 1
        for h in range(nh):
            sl = slice(h * HEAD_DIM, (h + 1) * HEAD_DIM)
            vs = slice(h * dv, (h + 1) * dv)
            sr = float(slopes[h]) * r
            dk = jnp.zeros((DIL_SPAN, HEAD_DIM), F32)
            dvh = jnp.zeros((DIL_SPAN, dv), F32)
            for q_ref, do_ref, lse_ref, dd_ref, is_next in ((qc_ref, doc_ref, lsec_ref, ddc_ref, False),
                                                          (qn_ref, don_ref, lsen_ref, ddn_ref, True)):
                s = _window_logits(q_ref[:, sl], k_ref[:, sl], sr, is_next, has_next)
                p = jnp.exp(s - lse_ref[:, h:h + 1])
                dvh = dvh + _dot(p.astype(BF16), do_ref[:, vs], TN)
                dp = _dot(do_ref[:, vs], v_ref[:, vs], NT)
                ds = (p * (dp - dd_ref[:, h:h + 1])).astype(BF16)
                dk = dk + _dot(ds, q_ref[:, sl], TN)
            dk_ref[:, sl] = scale * dk
            dv_ref[:, vs] = dvh

    nxt = lambda i: jnp.minimum(i + 1, nblk - 1)
    stat_c = pl.BlockSpec((DIL_SPAN, HEAD_DIM), lambda b, i: (i, b))
    stat_n = pl.BlockSpec((DIL_SPAN, HEAD_DIM), lambda b, i: (nxt(i), b))
    do_v, lse_v, dd_v = do.reshape(L, r * V), lse.reshape(L, r * HEAD_DIM), dd.reshape(L, r * HEAD_DIM)
    dk, dvv = pl.pallas_call(
        body, name=name, grid=(r, nblk),
        in_specs=[pl.BlockSpec((DIL_SPAN, C), lambda b, i: (i, b * nc + G + g)),
                  pl.BlockSpec((DIL_SPAN, V), lambda b, i: (i, b * nv + nv - 1)),
                  pl.BlockSpec((DIL_SPAN, C), lambda b, i: (i, b * nc + g)),
                  pl.BlockSpec((DIL_SPAN, C), lambda b, i: (nxt(i), b * nc + g)),
                  pl.BlockSpec((DIL_SPAN, V), lambda b, i: (i, b)),
                  pl.BlockSpec((DIL_SPAN, V), lambda b, i: (nxt(i), b)),
                  stat_c, stat_n, stat_c, stat_n],
        out_specs=[pl.BlockSpec((DIL_SPAN, C), lambda b, i: (i, b)), pl.BlockSpec((DIL_SPAN, V), lambda b, i: (i, b))],
        out_shape=[jax.ShapeDtypeStruct((L, r * C), F32), jax.ShapeDtypeStruct((L, r * V), F32)],
        compiler_params=_params("parallel", "parallel"),
    )(view, view, view, view, do_v, do_v, lse_v, lse_v, dd_v, dd_v)
    return dk.reshape(T, C), dvv.reshape(T, V)


def _relu2(r):
    a = jnp.maximum(r, 0.0)
    return (a * a,)


def _mlp_fwd(x, g, w_up, w_down, tag):
    T, D = x.shape
    F = w_down.shape[0]
    h = _rms_fwd(x, g, f"{tag}_norm")
    a2 = _mm(h, w_up, T, F, D, mode="nn", name=f"{tag}_up", b_stack=N_CHIPS, out_dtypes=(BF16,), epilogue=_relu2)
    y = _mm(a2, w_down, T, D, F, mode="nn", name=f"{tag}_down", out_dtypes=(F32,), extras=(x,),
            epilogue=lambda r, res: (res + r,))
    return y, (x, h, a2)


def _mlp_bwd(dy, dyb, saved, g, w_up, w_down, tag):
    x, h, a2 = saved
    T, D = x.shape
    F = w_down.shape[0]
    d_down = _mm(a2, dyb, F, D, T, mode="tn", name=f"{tag}_dwdown", out_dtypes=(F32,))
    du = _mm(dyb, w_down, T, F, D, mode="nt", name=f"{tag}_da", out_dtypes=(BF16,), extras=(a2,),
             epilogue=lambda r, sq: (r * (2.0 * jnp.sqrt(sq.astype(F32))),))
    d_up = _mm(h, du, D, F, T, mode="tn", name=f"{tag}_dwup", out_stack=N_CHIPS, out_dtypes=(F32,))
    dh = _mm(du, w_up, T, D, F, mode="nt", name=f"{tag}_dh", b_stack=N_CHIPS, out_dtypes=(F32,))
    dx, dxb, dg = _rms_bwd(dh, x, g, dy, f"{tag}_dnorm")
    return dx, dxb, dg, d_up, d_down


def _fox_dims(D):
    H = D // HEAD_DIM
    return H, max(8, H), (H // 2) * HEAD_DIM


def _fox_layer_fwd(x, g, w_qkv, w_f, b_pad, gains, w_out, rider=None):
    T, D = x.shape
    H, hp, ch = _fox_dims(D)
    h = _rms_fwd(x, g, "fox_norm")
    proj = _mm(h, w_qkv, T, 3 * D, D, mode="nn", name="fox_proj", out_dtypes=(F32,))
    f_raw = _mm(h, w_f, T, HEAD_DIM, D, mode="nn", name="fox_gate_proj", out_dtypes=(F32,))
    qkv = _qkv_prep(proj, gains, 4, lambda j: jnp.minimum(j // 2, 1), ch, "fox_qk_norm", n_scaled=2, post_scale=FOX_Q_SCALE)
    ck = _gate_fwd(f_raw, b_pad, hp, LOG2E, "fox_gate").reshape(hp, 1, T)
    o, lse, carried = _fox_fwd(qkv, ck, H, "fox_attn", rider=rider)
    y = _mm(o, w_out, T, D, D, mode="nn", name="fox_out", out_dtypes=(F32,), extras=(x,),
            epilogue=lambda r, res: (res + r,))
    return y, (x, h, proj, f_raw, qkv, ck, o, lse), carried


def _fox_layer_bwd(dy, dyb, saved, g, w_qkv, w_f, b_pad, gains, w_out, rider=None):
    x, h, proj, f_raw, qkv, ck, o, lse = saved
    T, D = x.shape
    H, hp, ch = _fox_dims(D)
    d_out = _mm(o, dyb, D, D, T, mode="tn", name="fox_dwout", out_dtypes=(F32,))
    do = _mm(dyb, w_out, T, D, D, mode="nt", name="fox_do", out_dtypes=(BF16,))
    dd = _row_dot(do, o, H, HEAD_DIM, "fox_rowdot", False)
    (dq, dk, dv, dck, dcq), carried = _fox_bwd(qkv, do, ck, lse, dd, H, hp, "fox_attn_bwd", rider=rider)
    dcq = _lane_per_head(dcq, H, "fox_dc_query")
    dproj, dgq = _head_rms_bwd_into(None, 3 * D, dq, proj, gains[0], 0, ch, "fox_dq_norm", in_scale=HEAD_DIM ** -0.5)
    dproj, dgk = _head_rms_bwd_into(dproj, 3 * D, dk, proj, gains[1], 2, ch, "fox_dk_norm", in_scale=LN2)
    dproj = _sum_cast_into(dproj, 3 * D, [dv], 4, ch, "fox_dv_cast")
    dz, db = _gate_bwd(dck.reshape(hp, T), dcq, f_raw, b_pad, H, hp, "fox_gate_bwd")
    d_qkv = _mm(h, dproj, D, 3 * D, T, mode="tn", name="fox_dwqkv", out_dtypes=(F32,))
    d_f = _mm(h, dz, D, HEAD_DIM, T, mode="tn", name="fox_dwgate", out_dtypes=(F32,))
    dh = _mm(dproj, w_qkv, T, D, 3 * D, mode="nt", name="fox_dh", out_dtypes=(F32,))
    dh = _mm(dz, w_f, T, D, HEAD_DIM, mode="nt", name="fox_dh_gate", out_dtypes=(F32,), extras=(dh,),
             epilogue=lambda r, e: (e + r,))
    dx, dxb, dg = _rms_bwd(dh, x, g, dy, "fox_dnorm")
    dgains = jnp.stack([dgq.sum(axis=0), dgk.sum(axis=0)])
    return dx, dxb, dg, d_qkv, d_f, db, dgains, d_out, carried


def _dil_dims(D):
    nh = D // (2 * HEAD_DIM)
    return nh, D // nh, len(DIL_PATTERNS)


def _dil_layer_fwd(x, g, w_in, gains, w_out):
    T, D = x.shape
    nh, dv, G = _dil_dims(D)
    C = nh * HEAD_DIM
    W = 2 * G * C + nh * dv
    slopes = _slopes(G, nh)
    h = _rms_fwd(x, g, "dil_norm")
    proj = _mm(h, w_in, T, W, D, mode="nn", name="dil_proj", b_stack=N_CHIPS, out_dtypes=(F32,))
    qkv = _qkv_prep(proj, gains, 2 * G, lambda j: jnp.minimum(j, 2 * G - 1), C, "dil_qk_norm")
    parts = [_dil_fwd(qkv, gi, r, G, nh, dv, slopes[gi], f"dil_attn_g{gi}") for gi, (_, r) in enumerate(DIL_PATTERNS)]
    o, lse = _dil_merge([p[0] for p in parts], [p[1] for p in parts], [p[2] for p in parts], nh, dv, "dil_merge")
    y = _mm(o, w_out, T, D, D, mode="nn", name="dil_out", out_dtypes=(F32,), extras=(x,),
            epilogue=lambda r, res: (res + r,))
    return y, (x, h, proj, qkv, o, lse)


def _dil_layer_bwd(dy, dyb, saved, g, w_in, gains, w_out):
    x, h, proj, qkv, o, lse = saved
    T, D = x.shape
    nh, dv, G = _dil_dims(D)
    C = nh * HEAD_DIM
    W = 2 * G * C + nh * dv
    slopes = _slopes(G, nh)
    d_out = _mm(o, dyb, D, D, T, mode="tn", name="dil_dwout", out_dtypes=(F32,))
    do = _mm(dyb, w_out, T, D, D, mode="nt", name="dil_do", out_dtypes=(BF16,))
    dd = _row_dot(do, o, nh, dv, "dil_rowdot", True)
    dproj, dgs, dvs = None, [None] * (2 * G), []
    for gi, (_, r) in enumerate(DIL_PATTERNS):
        dq = _dil_dq(qkv, do, lse, dd, gi, r, G, nh, dv, slopes[gi], f"dil_dq_g{gi}")
        dk, dvg = _dil_dkv(qkv, do, lse, dd, gi, r, G, nh, dv, slopes[gi], f"dil_dkv_g{gi}")
        dvs.append(dvg)
        dproj, dgs[gi] = _head_rms_bwd_into(dproj, W, dq, proj, gains[gi], gi, C, f"dil_dq_norm_g{gi}")
        dproj, dgs[G + gi] = _head_rms_bwd_into(dproj, W, dk, proj, gains[G + gi], G + gi, C, f"dil_dk_norm_g{gi}")
    dproj = _sum_cast_into(dproj, W, dvs, 2 * G, C, "dil_dv_cast")
    d_in = _mm(h, dproj, D, W, T, mode="tn", name="dil_dwin", out_stack=N_CHIPS, out_dtypes=(F32,))
    dh = _mm(dproj, w_in, T, D, W, mode="nt", name="dil_dh", b_stack=N_CHIPS, out_dtypes=(F32,))
    dx, dxb, dg = _rms_bwd(dh, x, g, dy, "dil_dnorm")
    dgains = jnp.concatenate(dgs, axis=0)
    return dx, dxb, dg, d_in, dgains, d_out


def _local_step(x, tgt, w, late_shards=None, late_weights=None, early_reduce=None):
    rider = _gather_ici_rider(late_shards) if late_shards is not None else None
    y0, s_fox, landed = _fox_layer_fwd(x, w["mix_g"][0], w["fox_qkv"], w["fox_f"], w["fox_b"], w["fox_gains"], w["fox_out"],
                                       rider=rider)
    if rider:
        w = {**w, **late_weights(_forward_halves(landed))}
    y1, s_mlp0 = _mlp_fwd(y0, w["mlp_g"][0], w["up"][0], w["down"][0], "mlp0")
    y2, s_dil = _dil_layer_fwd(y1, w["mix_g"][1], w["dil_in"], w["dil_gains"], w["dil_out"])
    y3, s_mlp1 = _mlp_fwd(y2, w["mlp_g"][1], w["up"][1], w["down"][1], "mlp1")
    dy, dyb, loss = _loss_head(y3, tgt, "loss_head")
    g = {}
    dy, dyb, g_mlp1, up1, down1 = _mlp_bwd(dy, dyb, s_mlp1, w["mlp_g"][1], w["up"][1], w["down"][1], "mlp1")
    dy, dyb, g_mix1, g["dil_in"], g["dil_gains"], g["dil_out"] = _dil_layer_bwd(
        dy, dyb, s_dil, w["mix_g"][1], w["dil_in"], w["dil_gains"], w["dil_out"])
    dy, dyb, g_mlp0, up0, down0 = _mlp_bwd(dy, dyb, s_mlp0, w["mlp_g"][0], w["up"][0], w["down"][0], "mlp0")
    g["up"], g["down"] = (up0, up1), (down0, down1)
    rider = early_reduce(g) if early_reduce is not None else None
    dy, dyb, g_mix0, g["fox_qkv"], g["fox_f"], g["fox_b"], g["fox_gains"], g["fox_out"], early = _fox_layer_bwd(
        dy, dyb, s_fox, w["mix_g"][0], w["fox_qkv"], w["fox_f"], w["fox_b"], w["fox_gains"], w["fox_out"], rider=rider)
    g["mix_g"], g["mlp_g"] = (g_mix0, g_mix1), (g_mlp0, g_mlp1)
    return loss[0, 0], dy, g, early


def _place():
    x, y, c = lax.axis_index("x"), lax.axis_index("y"), lax.axis_index("c")
    chips = [(1 - x, y), (x, 1 - y), (1 - x, 1 - y)]
    return x, y, c, chips


def _remote(src, dst, send_sem, recv_sem, to):
    return pltpu.make_async_remote_copy(src_ref=src, dst_ref=dst, send_sem=send_sem, recv_sem=recv_sem,
                                        device_id=to, device_id_type=MESH)


def _gather_weights(shards):
    n = len(shards)

    def body(*refs):
        src, dst = refs[:n], refs[n:2 * n]
        send_sems, recv_sems, local_sems = refs[2 * n:]
        x, y, c, chips = _place()
        mine = 2 * x + y
        local = [pltpu.make_async_copy(src[t], dst[t].at[mine], local_sems.at[t]) for t in range(n)]
        for cp in local:
            cp.start()

        def half(t, slot, which):
            hr = shards[t].shape[0] // 2
            return dst[t].at[slot, pl.ds(which * hr, hr), :]

        def my_half(t):
            hr = shards[t].shape[0] // 2
            return src[t].at[pl.ds(c * hr, hr), :]

        sends = []
        for t in range(n):
            for j, (px, py) in enumerate(chips):
                cp = _remote(my_half(t), half(t, mine, c), send_sems.at[t, j], recv_sems.at[t, j], (px, py, c))
                cp.start()
                sends.append(cp)
        for j, (px, py) in enumerate(chips):
            for t in range(n):
                landed = half(t, 2 * px + py, c)
                _remote(landed, landed, send_sems.at[t, j], recv_sems.at[t, j], (px, py, c)).wait_recv()
                cp = _remote(landed, landed, send_sems.at[t, 3 + j], recv_sems.at[t, 3 + j], (x, y, 1 - c))
                cp.start()
                sends.append(cp)
        for j, (px, py) in enumerate(chips):
            for t in range(n):
                other = half(t, 2 * px + py, 1 - c)
                _remote(other, other, send_sems.at[t, 3 + j], recv_sems.at[t, 3 + j], (x, y, 1 - c)).wait_recv()
        for cp in sends:
            cp.wait_send()
        for cp in local:
            cp.wait()

    return pl.pallas_call(
        body, name="gather_weights",
        in_specs=[ANY] * n, out_specs=[ANY] * n,
        out_shape=[jax.ShapeDtypeStruct((N_CHIPS,) + s.shape, s.dtype) for s in shards],
        scratch_shapes=[pltpu.SemaphoreType.DMA((n, 6)), pltpu.SemaphoreType.DMA((n, 6)), pltpu.SemaphoreType.DMA((n,))],
        compiler_params=_params(has_side_effects=True),
    )(*shards)


class _Rider(NamedTuple):
    operands: tuple
    out_shapes: tuple
    scratch: tuple
    start: Callable
    finish: Callable


def _gather_ici_rider(shards):
    n = len(shards)

    def copies(src, dst, sems):
        send_sems, recv_sems, local_sems = sems
        x, y, c, chips = _place()
        mine = 2 * x + y
        local, sends, recvs = [], [], []
        for t in range(n):
            hr = shards[t].shape[0] // 2
            local.append(pltpu.make_async_copy(src[t], dst[t].at[mine], local_sems.at[t]))
            for j, (px, py) in enumerate(chips):
                sends.append(_remote(src[t].at[pl.ds(c * hr, hr), :], dst[t].at[mine, pl.ds(c * hr, hr), :],
                                     send_sems.at[t, j], recv_sems.at[t, j], (px, py, c)))
                landed = dst[t].at[2 * px + py, pl.ds(c * hr, hr), :]
                recvs.append(_remote(landed, landed, send_sems.at[t, j], recv_sems.at[t, j], (px, py, c)))
        return local, sends, recvs

    def start(src, dst, sems):
        local, sends, _ = copies(src, dst, sems)
        for cp in local + sends:
            cp.start()

    def finish(src, dst, sems):
        local, sends, recvs = copies(src, dst, sems)
        for cp in recvs:
            cp.wait_recv()
        for cp in sends:
            cp.wait_send()
        for cp in local:
            cp.wait()

    return _Rider(tuple(shards), tuple(jax.ShapeDtypeStruct((N_CHIPS,) + s.shape, s.dtype) for s in shards),
                  (pltpu.SemaphoreType.DMA((n, 3)), pltpu.SemaphoreType.DMA((n, 3)), pltpu.SemaphoreType.DMA((n,))),
                  start, finish)


def _forward_halves(landed):
    n = len(landed)

    def body(*refs):
        dst = refs[n:2 * n]
        send_sems, recv_sems = refs[2 * n:]
        x, y, c, chips = _place()
        sends = []
        for t in range(n):
            hr = landed[t].shape[1] // 2
            for j, (px, py) in enumerate(chips):
                got = dst[t].at[2 * px + py, pl.ds(c * hr, hr), :]
                cp = _remote(got, got, send_sems.at[t, j], recv_sems.at[t, j], (x, y, 1 - c))
                cp.start()
                sends.append(cp)
        for t in range(n):
            hr = landed[t].shape[1] // 2
            for j, (px, py) in enumerate(chips):
                other = dst[t].at[2 * px + py, pl.ds((1 - c) * hr, hr), :]
                _remote(other, other, send_sems.at[t, j], recv_sems.at[t, j], (x, y, 1 - c)).wait_recv()
        for cp in sends:
            cp.wait_send()

    return pl.pallas_call(
        body, name="gather_forward_halves",
        in_specs=[ANY] * n, out_specs=[ANY] * n,
        out_shape=[jax.ShapeDtypeStruct(a.shape, a.dtype) for a in landed],
        input_output_aliases={t: t for t in range(n)},
        scratch_shapes=[pltpu.SemaphoreType.DMA((n, 3)), pltpu.SemaphoreType.DMA((n, 3))],
        compiler_params=_params(has_side_effects=True),
    )(*landed)


def _pair_exchange(grads, name):
    n = len(grads)

    def body(*refs):
        src, dst = refs[:n], refs[n:2 * n]
        send_sems, recv_sems = refs[2 * n:]
        x, y, c, _ = _place()
        cps = []
        for t in range(n):
            hr = grads[t].shape[1] // 2
            cp = _remote(src[t].at[:, pl.ds((1 - c) * hr, hr), :], dst[t], send_sems.at[t], recv_sems.at[t], (x, y, 1 - c))
            cp.start()
            cps.append(cp)
        for cp in cps:
            cp.wait()

    return pl.pallas_call(
        body, name=name,
        in_specs=[ANY] * n, out_specs=[ANY] * n,
        out_shape=[jax.ShapeDtypeStruct((g.shape[0], g.shape[1] // 2, g.shape[2]), g.dtype) for g in grads],
        scratch_shapes=[pltpu.SemaphoreType.DMA((n,)), pltpu.SemaphoreType.DMA((n,))],
        compiler_params=_params(has_side_effects=True),
    )(*grads)


def _pair_add(g, got, cidx, name, tb=256):
    S, R, C = g.shape
    hr = R // 2
    tb = _rows_tile(hr, tb)
    nb = hr // tb

    def body(c_ref, a_ref, b_ref, o_ref):
        o_ref[...] = (a_ref[...] + b_ref[...]).astype(BF16)

    return pl.pallas_call(
        body, name=name,
        grid_spec=pltpu.PrefetchScalarGridSpec(
            num_scalar_prefetch=1, grid=(S, nb),
            in_specs=[pl.BlockSpec((None, tb, C), lambda s, i, c: (s, c[0] * nb + i, 0)),
                      pl.BlockSpec((None, tb, C), lambda s, i, c: (s, i, 0))],
            out_specs=pl.BlockSpec((None, tb, C), lambda s, i, c: (s, i, 0))),
        out_shape=jax.ShapeDtypeStruct((S, hr, C), BF16),
        compiler_params=_params("parallel", "parallel"),
    )(cidx, g, got)


def _rows_tile(n, want):
    t = min(n, want)
    while n % t or t % 8:
        t -= 8
    return t


def _chip_scatter_rider(sums):
    n = len(sums)

    def copies(src, dst, sems):
        send_sems, recv_sems, local_sems = sems
        x, y, c, chips = _place()
        mine = 2 * x + y
        local, sends, recvs = [], [], []
        for t in range(n):
            local.append(pltpu.make_async_copy(src[t].at[mine], dst[t].at[mine], local_sems.at[t]))
            for j, (px, py) in enumerate(chips):
                sends.append(_remote(src[t].at[2 * px + py], dst[t].at[mine], send_sems.at[t, j], recv_sems.at[t, j], (px, py, c)))
                slot = dst[t].at[2 * px + py]
                recvs.append(_remote(slot, slot, send_sems.at[t, j], recv_sems.at[t, j], (px, py, c)))
        return local, sends, recvs

    def start(src, dst, sems):
        local, sends, _ = copies(src, dst, sems)
        for cp in local + sends:
            cp.start()

    def finish(src, dst, sems):
        local, sends, recvs = copies(src, dst, sems)
        for cp in recvs:
            cp.wait_recv()
        for cp in sends:
            cp.wait_send()
        for cp in local:
            cp.wait()

    return _Rider(tuple(sums), tuple(jax.ShapeDtypeStruct(s.shape, s.dtype) for s in sums),
                  (pltpu.SemaphoreType.DMA((n, 3)), pltpu.SemaphoreType.DMA((n, 3)), pltpu.SemaphoreType.DMA((n,))),
                  start, finish)


def _run_rider(rider, name):
    r_in, r_out = len(rider.operands), len(rider.out_shapes)

    def body(*refs):
        src, dst, sems = refs[:r_in], refs[r_in:r_in + r_out], refs[r_in + r_out:]
        rider.start(src, dst, sems)
        rider.finish(src, dst, sems)

    return pl.pallas_call(
        body, name=name,
        in_specs=[ANY] * r_in, out_specs=[ANY] * r_out,
        out_shape=list(rider.out_shapes), scratch_shapes=list(rider.scratch),
        compiler_params=_params(has_side_effects=True),
    )(*rider.operands)


def _chip_sum(parts, cidx, name, tb=256):
    S, hr, C = parts.shape
    tb = _rows_tile(hr, tb)
    nb = hr // tb

    def body(c_ref, *refs):
        o_ref = refs[S]
        tot = refs[0][...].astype(F32)
        for s in range(1, S):
            tot = tot + refs[s][...].astype(F32)
        o_ref[...] = tot

    return pl.pallas_call(
        body, name=name,
        grid_spec=pltpu.PrefetchScalarGridSpec(
            num_scalar_prefetch=1, grid=(nb,),
            in_specs=[pl.BlockSpec((None, tb, C), functools.partial(lambda s, i, c: (s, i, 0), s)) for s in range(S)],
            out_specs=pl.BlockSpec((tb, C), lambda i, c: (c[0] * nb + i, 0))),
        out_shape=jax.ShapeDtypeStruct((2 * hr, C), F32),
        compiler_params=_params("parallel"),
    )(cidx, *([parts] * S))


def _half_exchange(halves):
    n = len(halves)

    def body(*refs):
        dst = refs[n:2 * n]
        send_sems, recv_sems = refs[2 * n:]
        x, y, c, _ = _place()
        cps = []
        for t in range(n):
            hr = halves[t].shape[0] // 2
            rows = dst[t].at[pl.ds(c * hr, hr), :]
            cp = _remote(rows, rows, send_sems.at[t], recv_sems.at[t], (x, y, 1 - c))
            cp.start()
            cps.append(cp)
        for t, cp in enumerate(cps):
            cp.wait_send()
            hr = halves[t].shape[0] // 2
            other = dst[t].at[pl.ds((1 - c) * hr, hr), :]
            _remote(other, other, send_sems.at[t], recv_sems.at[t], (x, y, 1 - c)).wait_recv()

    return pl.pallas_call(
        body, name="grad_half_exchange",
        in_specs=[ANY] * n, out_specs=[ANY] * n,
        out_shape=[jax.ShapeDtypeStruct(h.shape, h.dtype) for h in halves],
        input_output_aliases={t: t for t in range(n)},
        scratch_shapes=[pltpu.SemaphoreType.DMA((n,)), pltpu.SemaphoreType.DMA((n,))],
        compiler_params=_params(has_side_effects=True),
    )(*halves)


def _adamw_math(w, g, m, v):
    m = ADAM_B1 * m + (1.0 - ADAM_B1) * g
    v = ADAM_B2 * v + (1.0 - ADAM_B2) * (g * g)
    m_hat = m / (1.0 - ADAM_B1 ** ADAM_STEP)
    v_hat = v / (1.0 - ADAM_B2 ** ADAM_STEP)
    delta = -ADAM_LR * (m_hat / (jnp.sqrt(v_hat) + ADAM_EPS) + ADAM_WD * w)
    return delta, m, v


def _adamw(w, gs, m, v, name, tb=256):
    L, R, C = w.shape
    tb = _rows_tile(R, tb)
    nb = R // tb

    def body(w_ref, m_ref, v_ref, *rest):
        g_refs, (go_ref, d_ref, mo_ref, vo_ref) = rest[:L], rest[L:]
        layer = pl.program_id(0)
        for k in range(L):
            @pl.when(layer == k)
            def _(k=k):
                g = g_refs[k][...]
                d, mn, vn = _adamw_math(w_ref[...], g, m_ref[...], v_ref[...])
                go_ref[...] = g
                d_ref[...] = d
                mo_ref[...] = mn
                vo_ref[...] = vn

    def g_spec(k):
        return pl.BlockSpec((tb, C), lambda l, i: (jnp.where(l == k, i, jnp.where(l < k, 0, nb - 1)), 0))

    stacked = pl.BlockSpec((None, tb, C), lambda l, i: (l, i, 0))
    return pl.pallas_call(
        body, name=name, grid=(L, nb),
        in_specs=[stacked] * 3 + [g_spec(k) for k in range(L)], out_specs=[stacked] * 4,
        out_shape=[jax.ShapeDtypeStruct((L, R, C), F32)] * 4,
        compiler_params=_params("arbitrary", "arbitrary"),
    )(w, m, v, *gs)


N_DEV = 8


def _small_update(g, w, m, v):
    P = g.shape[0]

    def body(g_ref, w_ref, m_ref, v_ref, go_ref, d_ref, mo_ref, vo_ref, buf, send_sems, recv_sems):
        x, y, c, _ = _place()
        me = 4 * x + 2 * y + c
        buf[me] = g_ref[...]
        cps = []
        for k in range(1, N_DEV):
            fx, fy, fc = (k >> 2) & 1, (k >> 1) & 1, k & 1
            px = (1 - x) if fx else x
            py = (1 - y) if fy else y
            pc = (1 - c) if fc else c
            cp = _remote(g_ref, buf.at[me], send_sems.at[k - 1], recv_sems.at[k - 1], (px, py, pc))
            cp.start()
            cps.append((cp, 4 * px + 2 * py + pc))
        for k, (cp, peer) in enumerate(cps):
            _remote(g_ref, buf.at[peer], send_sems.at[k], recv_sems.at[k], (x, y, c)).wait_recv()
        for cp, _ in cps:
            cp.wait_send()
        tot = buf[0]
        for d in range(1, N_DEV):
            tot = tot + buf[d]
        go_ref[...] = tot
        dl, mn, vn = _adamw_math(w_ref[...], tot, m_ref[...], v_ref[...])
        d_ref[...] = dl
        mo_ref[...] = mn
        vo_ref[...] = vn

    vm = pl.BlockSpec(memory_space=pltpu.VMEM)
    return pl.pallas_call(
        body, name="small_params_update",
        in_specs=[vm] * 4, out_specs=[vm] * 4,
        out_shape=[jax.ShapeDtypeStruct((P, HEAD_DIM), F32)] * 4,
        scratch_shapes=[pltpu.VMEM((N_DEV, P, HEAD_DIM), F32), pltpu.SemaphoreType.DMA((N_DEV - 1,)),
                        pltpu.SemaphoreType.DMA((N_DEV - 1,))],
        compiler_params=_params(has_side_effects=True),
    )(g, w, m, v)


SMALL = ("fox_b_f", "fox_q_gain", "fox_k_gain", "dil_q_gain", "dil_k_gain", "mix_norm_g", "mlp_norm_g")
LARGE = ("fox_w_in", "fox_w_out", "dil_w_in", "dil_w_out", "mlp_w_up", "mlp_w_down")
WEIGHTS = ("fox_w_in", "fox_b_f", "fox_q_gain", "fox_k_gain", "fox_w_out", "dil_w_in", "dil_q_gain", "dil_k_gain",
           "dil_w_out", "mix_norm_g", "mlp_norm_g", "mlp_w_up", "mlp_w_down")


def _pack(parts):
    rows = []
    for a in parts:
        flat = a.reshape(-1)
        n = -(-flat.shape[0] // (8 * HEAD_DIM)) * (8 * HEAD_DIM)
        rows.append(jnp.pad(flat, (0, n - flat.shape[0])).reshape(-1, HEAD_DIM))
    return jnp.concatenate(rows, axis=0)


def _unpack(packed, like):
    out, r = [], 0
    for a in like:
        size = int(np.prod(a.shape))
        n = -(-size // (8 * HEAD_DIM)) * 8
        out.append(packed[r:r + n].reshape(-1)[:size].reshape(a.shape))
        r += n
    return out


def _pad_lanes(a):
    return jnp.pad(a, [(0, 0)] * (a.ndim - 1) + [(0, HEAD_DIM - a.shape[-1])])


def _as_shards(a):
    return a.reshape(N_CHIPS, a.shape[0] // N_CHIPS, a.shape[1])


def kernel(x, fox_w_in, fox_b_f, fox_q_gain, fox_k_gain, fox_w_out, dil_w_in, dil_q_gain, dil_k_gain, dil_w_out, mix_norm_g, mlp_norm_g, mlp_w_up, mlp_w_down, loss_target, m_fox_w_in, m_fox_b_f, m_fox_q_gain, m_fox_k_gain, m_fox_w_out, m_dil_w_in, m_dil_q_gain, m_dil_k_gain, m_dil_w_out, m_mix_norm_g, m_mlp_norm_g, m_mlp_w_up, m_mlp_w_down, v_fox_w_in, v_fox_b_f, v_fox_q_gain, v_fox_k_gain, v_fox_w_out, v_dil_w_in, v_dil_q_gain, v_dil_k_gain, v_dil_w_out, v_mix_norm_g, v_mlp_norm_g, v_mlp_w_up, v_mlp_w_down):
    wts = dict(fox_w_in=fox_w_in, fox_b_f=fox_b_f, fox_q_gain=fox_q_gain, fox_k_gain=fox_k_gain, fox_w_out=fox_w_out,
               dil_w_in=dil_w_in, dil_q_gain=dil_q_gain, dil_k_gain=dil_k_gain, dil_w_out=dil_w_out,
               mix_norm_g=mix_norm_g, mlp_norm_g=mlp_norm_g, mlp_w_up=mlp_w_up, mlp_w_down=mlp_w_down)
    mom1 = dict(fox_w_in=m_fox_w_in, fox_b_f=m_fox_b_f, fox_q_gain=m_fox_q_gain, fox_k_gain=m_fox_k_gain,
                fox_w_out=m_fox_w_out, dil_w_in=m_dil_w_in, dil_q_gain=m_dil_q_gain, dil_k_gain=m_dil_k_gain,
                dil_w_out=m_dil_w_out, mix_norm_g=m_mix_norm_g, mlp_norm_g=m_mlp_norm_g, mlp_w_up=m_mlp_w_up,
                mlp_w_down=m_mlp_w_down)
    mom2 = dict(fox_w_in=v_fox_w_in, fox_b_f=v_fox_b_f, fox_q_gain=v_fox_q_gain, fox_k_gain=v_fox_k_gain,
                fox_w_out=v_fox_w_out, dil_w_in=v_dil_w_in, dil_q_gain=v_dil_q_gain, dil_k_gain=v_dil_k_gain,
                dil_w_out=v_dil_w_out, mix_norm_g=v_mix_norm_g, mlp_norm_g=v_mlp_norm_g, mlp_w_up=v_mlp_w_up,
                mlp_w_down=v_mlp_w_down)
    T, D = x.shape[1], x.shape[2]
    H = D // HEAD_DIM
    cidx = lax.axis_index("c").astype(jnp.int32).reshape(1)

    def shards_of(d):
        return [d["fox_w_in"][0], d["fox_w_out"][0], d["dil_w_in"][0], d["dil_w_out"][0],
                d["mlp_w_up"][0], d["mlp_w_up"][1], d["mlp_w_down"][0], d["mlp_w_down"][1]]

    w_bf = [s.astype(BF16) for s in shards_of(wts)]
    first = _gather_weights(w_bf[:2])
    fox_in = jnp.moveaxis(first[0], 0, 1).reshape(D, -1)
    w = dict(
        fox_qkv=fox_in[:, :3 * D], fox_f=_pad_lanes(fox_in[:, 3 * D:]), fox_b=_pad_lanes(fox_b_f),
        fox_gains=jnp.stack([fox_q_gain, fox_k_gain]), fox_out=first[1].reshape(D, D),
        dil_gains=jnp.concatenate([dil_q_gain[0], dil_k_gain[0]])[:, None, :],
        mix_g=[mix_norm_g[0:1], mix_norm_g[1:2]], mlp_g=[mlp_norm_g[0:1], mlp_norm_g[1:2]])

    def late_weights(full):
        return dict(dil_in=full[0], dil_out=full[1].reshape(D, D), up=[full[2], full[3]],
                    down=[full[4].reshape(-1, D), full[5].reshape(-1, D)])

    def chip_sums(stacked, first, tag):
        got = _pair_exchange(stacked, f"grad_pair_exchange_{tag}")
        return [_pair_add(a, b, cidx, f"grad_pair_add_{first + t}") for t, (a, b) in enumerate(zip(stacked, got))]

    def early_reduce(g):
        stacked = [g["dil_in"], _as_shards(g["dil_out"]), g["up"][0], g["up"][1],
                   _as_shards(g["down"][0]), _as_shards(g["down"][1])]
        return _chip_scatter_rider(chip_sums(stacked, 2, "early"))

    loss, grad_x, g, early_parts = _local_step(x.reshape(T, D), loss_target.reshape(T, D), w, w_bf[2:], late_weights,
                                               early_reduce)
    loss = lax.psum(loss, ("x", "y", "c"))

    g_fox_in = jnp.concatenate([g["fox_qkv"], g["fox_f"][:, :H]], axis=1)
    g_fox_in = jnp.moveaxis(g_fox_in.reshape(D, N_CHIPS, -1), 1, 0)
    late_sums = chip_sums([g_fox_in, _as_shards(g["fox_out"])], 0, "late")
    parts = list(_run_rider(_chip_scatter_rider(late_sums), "grad_chip_scatter_late")) + early_parts
    halves = [_chip_sum(p, cidx, f"grad_chip_sum_{t}") for t, p in enumerate(parts)]
    totals = _half_exchange(halves)
    layers = dict(fox_w_in=[0], fox_w_out=[1], dil_w_in=[2], dil_w_out=[3], mlp_w_up=[4, 5], mlp_w_down=[6, 7])
    upd = {n: _adamw(wts[n], [totals[t] for t in ts], mom1[n], mom2[n], f"adamw_{n}") for n, ts in layers.items()}

    def large(k):
        return {n: upd[n][k] for n in LARGE}

    small_like = [wts[n] for n in SMALL]
    g_small = [g["fox_b"][:, :H], g["fox_gains"][0], g["fox_gains"][1], g["dil_gains"][:3, 0][None], g["dil_gains"][3:, 0][None],
               jnp.concatenate(g["mix_g"]), jnp.concatenate(g["mlp_g"])]
    packed = _small_update(_pack(g_small), _pack(small_like), _pack([mom1[n] for n in SMALL]), _pack([mom2[n] for n in SMALL]))
    small = [dict(zip(SMALL, _unpack(p, small_like))) for p in packed]

    outs = [loss, grad_x.reshape(x.shape)]
    for k in range(4):
        big = large(k)
        outs += [big[n] if n in big else small[k][n] for n in WEIGHTS]
    return tuple(outs)
```

```python
import functools
from typing import Callable, NamedTuple

import numpy as np
import jax
import jax.numpy as jnp
from jax import lax
from jax.experimental import pallas as pl
from jax.experimental.pallas import tpu as pltpu

F32 = jnp.float32
BF16 = jnp.bfloat16

HEAD_DIM = 128
DIL_PATTERNS = ((128, 1), (512, 4), (2048, 16))
DIL_SPAN = 128
ALIBI_MAX_EXP = 8.0
EPS = 1e-6
MASKED = -1e30

ADAM_LR = 0.001
ADAM_B1 = 0.9
ADAM_B2 = 0.999
ADAM_EPS = 1e-08
ADAM_WD = 0.01
ADAM_STEP = 10

N_CHIPS = 4
VMEM_LIMIT_BYTES = 56 * 1024 * 1024
MESH = pl.DeviceIdType.MESH
ANY = pl.BlockSpec(memory_space=pl.ANY)

NN = (((1,), (0,)), ((), ()))
NT = (((1,), (1,)), ((), ()))
TN = (((0,), (0,)), ((), ()))


def _params(*sem, **kw):
    return pltpu.CompilerParams(dimension_semantics=sem or None, vmem_limit_bytes=VMEM_LIMIT_BYTES, **kw)


def _dot(a, b, dims):
    return lax.dot_general(a, b, dims, preferred_element_type=F32)


def _tile(n, want):
    if n <= want:
        return n
    t = want - want % 128
    while n % t:
        t -= 128
    return t


def _mm(a, b, M, N, K, *, mode, name, out_dtypes, b_stack=0, out_stack=0, extras=(), epilogue=None,
        tm=1024, tn=1024, tk=2048):
    per_b = per_o = None
    if b_stack:
        per_b = (K if mode == "nt" else N) // b_stack
    if out_stack:
        per_o = N // out_stack
    tm = _tile(M, tm)
    tn = _tile(min(x for x in (N, per_o, per_b if mode != "nt" else None) if x), tn)
    tk = _tile(min(x for x in (K, per_b if mode == "nt" else None) if x), tk)
    assert M % tm == 0 and N % tn == 0 and K % tk == 0, (name, M, N, K, tm, tn, tk)
    gk = K // tk
    if mode == "tn":
        a_spec = pl.BlockSpec((tk, tm), lambda i, j, k: (k, i))
    else:
        a_spec = pl.BlockSpec((tm, tk), lambda i, j, k: (i, k))
    if mode == "nt":
        if b_stack:
            npk = per_b // tk
            b_spec = pl.BlockSpec((None, tn, tk), lambda i, j, k: (k // npk, j, k % npk))
        else:
            b_spec = pl.BlockSpec((tn, tk), lambda i, j, k: (j, k))
    else:
        if b_stack:
            npj = per_b // tn
            b_spec = pl.BlockSpec((None, tk, tn), lambda i, j, k: (j // npj, k, j % npj))
        else:
            b_spec = pl.BlockSpec((tk, tn), lambda i, j, k: (k, j))
    if out_stack:
        npo = per_o // tn
        o_spec = pl.BlockSpec((None, tm, tn), lambda i, j, k: (j // npo, i, j % npo))
        o_shape = (out_stack, M, per_o)
    else:
        o_spec = pl.BlockSpec((tm, tn), lambda i, j, k: (i, j))
        o_shape = (M, N)
    e_spec = pl.BlockSpec((tm, tn), lambda i, j, k: (i, j))
    dims = {"nn": NN, "nt": NT, "tn": TN}[mode]
    ne, no = len(extras), len(out_dtypes)

    def body(a_ref, b_ref, *rest):
        ex, outs = rest[:ne], rest[ne:ne + no]
        k = pl.program_id(2)

        def product():
            return _dot(a_ref[...].astype(BF16), b_ref[...].astype(BF16), dims)

        def finish(r):
            res = epilogue(r, *[e[...] for e in ex]) if epilogue is not None else (r,)
            for o, v in zip(outs, res):
                o[...] = v.astype(o.dtype)

        if gk == 1:
            finish(product())
            return
        acc = rest[ne + no]

        @pl.when(k == 0)
        def _():
            acc[...] = product()

        @pl.when((k > 0) & (k < gk - 1))
        def _():
            acc[...] += product()

        @pl.when(k == gk - 1)
        def _():
            finish(acc[...] + product())

    outs = pl.pallas_call(
        body, name=name,
        grid=(M // tm, N // tn, gk),
        in_specs=[a_spec, b_spec] + [e_spec] * ne,
        out_specs=[o_spec] * no,
        out_shape=[jax.ShapeDtypeStruct(o_shape, d) for d in out_dtypes],
        scratch_shapes=[pltpu.VMEM((tm, tn), F32)] if gk > 1 else [],
        compiler_params=_params("parallel", "parallel", "arbitrary"),
    )(a, b, *extras)
    return outs[0] if no == 1 else outs


def _rms_fwd(x, g, name, tb=512):
    T, D = x.shape
    tb = min(tb, T)

    def body(x_ref, g_ref, o_ref):
        xv = x_ref[...]
        r = lax.rsqrt(jnp.mean(xv * xv, axis=-1, keepdims=True) + EPS)
        o_ref[...] = (xv * r * g_ref[...]).astype(BF16)

    return pl.pallas_call(
        body, name=name, grid=(T // tb,),
        in_specs=[pl.BlockSpec((tb, D), lambda i: (i, 0)), pl.BlockSpec((1, D), lambda i: (0, 0))],
        out_specs=pl.BlockSpec((tb, D), lambda i: (i, 0)),
        out_shape=jax.ShapeDtypeStruct((T, D), BF16),
        compiler_params=_params("parallel"),
    )(x, g)


def _rms_bwd(dy, x, g, dres, name, tb=256):
    T, D = x.shape
    tb = min(tb, T)

    def body(dy_ref, x_ref, g_ref, dres_ref, dx_ref, dxb_ref, dg_ref):
        i = pl.program_id(0)
        xv, dyv = x_ref[...], dy_ref[...]
        r = lax.rsqrt(jnp.mean(xv * xv, axis=-1, keepdims=True) + EPS)
        gy = dyv * g_ref[...]
        dx = r * gy - xv * (r * r * r) * jnp.mean(gy * xv, axis=-1, keepdims=True)
        tot = dres_ref[...] + dx
        dx_ref[...] = tot
        dxb_ref[...] = tot.astype(BF16)
        part = jnp.sum(dyv * (xv * r), axis=0, keepdims=True)

        @pl.when(i == 0)
        def _():
            dg_ref[...] = part

        @pl.when(i > 0)
        def _():
            dg_ref[...] += part

    row = pl.BlockSpec((tb, D), lambda i: (i, 0))
    vec = pl.BlockSpec((1, D), lambda i: (0, 0))
    return pl.pallas_call(
        body, name=name, grid=(T // tb,),
        in_specs=[row, row, vec, row],
        out_specs=[row, row, vec],
        out_shape=[jax.ShapeDtypeStruct((T, D), F32), jax.ShapeDtypeStruct((T, D), BF16),
                   jax.ShapeDtypeStruct((1, D), F32)],
        compiler_params=_params("arbitrary"),
    )(dy, x, g, dres)


def _loss_head(y, tgt, name, tb=256):
    T, D = y.shape
    tb = min(tb, T)

    def body(y_ref, t_ref, dy_ref, dyb_ref, loss_ref):
        i = pl.program_id(0)
        e = y_ref[...] - t_ref[...]
        d = e * (1.0 / D)
        dy_ref[...] = d
        dyb_ref[...] = d.astype(BF16)
        part = 0.5 * jnp.sum(jnp.sum(e * e, axis=1, keepdims=True) * (1.0 / D), axis=0, keepdims=True)

        @pl.when(i == 0)
        def _():
            loss_ref[...] = part

        @pl.when(i > 0)
        def _():
            loss_ref[...] += part

    row = pl.BlockSpec((tb, D), lambda i: (i, 0))
    return pl.pallas_call(
        body, name=name, grid=(T // tb,),
        in_specs=[row, row],
        out_specs=[row, row, pl.BlockSpec((1, 1), lambda i: (0, 0))],
        out_shape=[jax.ShapeDtypeStruct((T, D), F32), jax.ShapeDtypeStruct((T, D), BF16),
                   jax.ShapeDtypeStruct((1, 1), F32)],
        compiler_params=_params("arbitrary"),
    )(y, tgt)


def _head_rms(xh, g):
    r = lax.rsqrt(jnp.mean(xh * xh, axis=-1, keepdims=True) + EPS)
    return xh * r * g


def _qkv_prep(proj, gains, n_norm, gain_row, ch, name, n_scaled=0, post_scale=1.0, tb=512):
    T, W = proj.shape
    tb = min(tb, T)
    nch = W // ch
    nh = ch // HEAD_DIM

    def body(p_ref, g_ref, o_ref):
        j = pl.program_id(0)

        @pl.when(j < n_norm)
        def _():
            g = g_ref[...]
            if n_scaled:
                g = g * jnp.where(j < n_scaled, post_scale, 1.0)
            for h in range(nh):
                sl = slice(h * HEAD_DIM, (h + 1) * HEAD_DIM)
                o_ref[:, sl] = _head_rms(p_ref[:, sl], g).astype(BF16)

        @pl.when(j >= n_norm)
        def _():
            o_ref[...] = p_ref[...].astype(BF16)

    return pl.pallas_call(
        body, name=name, grid=(nch, T // tb),
        in_specs=[pl.BlockSpec((tb, ch), lambda j, i: (i, j)),
                  pl.BlockSpec((None, 1, HEAD_DIM), lambda j, i: (gain_row(j), 0, 0))],
        out_specs=pl.BlockSpec((tb, ch), lambda j, i: (i, j)),
        out_shape=jax.ShapeDtypeStruct((T, W), BF16),
        compiler_params=_params("parallel", "parallel"),
    )(proj, gains)


def _into(body, name, grid, in_specs, out_spec, out_shape, extra_out_specs, extra_out_shapes, buf, operands, sem):
    if buf is None:
        def kernel(*refs):
            body(*refs)
        ins, alias, ops = in_specs, {}, operands
    else:
        def kernel(_, *refs):
            body(*refs)
        ins = [pl.BlockSpec(memory_space=pl.ANY)] + in_specs
        alias, ops = {0: 0}, (buf,) + tuple(operands)
    return pl.pallas_call(
        kernel, name=name, grid=grid, in_specs=ins,
        out_specs=[out_spec] + extra_out_specs,
        out_shape=[out_shape] + extra_out_shapes,
        input_output_aliases=alias,
        compiler_params=_params(*sem),
    )(*ops)


def _head_rms_bwd_into(buf, W, d, proj, gain, off, ch, name, in_scale=1.0, tb=256):
    T, wd = d.shape
    tb = min(tb, T)
    n = wd // ch
    nh = ch // HEAD_DIM

    def body(d_ref, p_ref, g_ref, o_ref, dg_ref):
        i = pl.program_id(1)
        g = g_ref[...]
        part = jnp.zeros((1, HEAD_DIM), F32)
        for h in range(nh):
            sl = slice(h * HEAD_DIM, (h + 1) * HEAD_DIM)
            xh, dy = p_ref[:, sl], d_ref[:, sl]
            if in_scale != 1.0:
                dy = dy * in_scale
            r = lax.rsqrt(jnp.mean(xh * xh, axis=-1, keepdims=True) + EPS)
            gy = dy * g
            dx = r * gy - xh * (r * r * r) * jnp.mean(gy * xh, axis=-1, keepdims=True)
            o_ref[:, sl] = dx.astype(BF16)
            part = part + jnp.sum(dy * (xh * r), axis=0, keepdims=True)

        @pl.when(i == 0)
        def _():
            dg_ref[...] = part

        @pl.when(i > 0)
        def _():
            dg_ref[...] += part

    return _into(
        body, name, (n, T // tb),
        [pl.BlockSpec((tb, ch), lambda j, i: (i, j)), pl.BlockSpec((tb, ch), lambda j, i: (i, off + j)),
         pl.BlockSpec((1, HEAD_DIM), lambda j, i: (0, 0))],
        pl.BlockSpec((tb, ch), lambda j, i: (i, off + j)), jax.ShapeDtypeStruct((T, W), BF16),
        [pl.BlockSpec((None, 1, HEAD_DIM), lambda j, i: (j, 0, 0))], [jax.ShapeDtypeStruct((n, 1, HEAD_DIM), F32)],
        buf, (d, proj, gain), ("parallel", "arbitrary"))


def _sum_cast_into(buf, W, srcs, off, ch, name, tb=256):
    T, wd = srcs[0].shape
    tb = min(tb, T)
    n = wd // ch
    ns = len(srcs)

    def body(*refs):
        o_ref = refs[ns]
        tot = refs[0][...]
        for s in refs[1:ns]:
            tot = tot + s[...]
        o_ref[...] = tot.astype(BF16)

    out = _into(
        body, name, (n, T // tb),
        [pl.BlockSpec((tb, ch), lambda j, i: (i, j))] * ns,
        pl.BlockSpec((tb, ch), lambda j, i: (i, off + j)), jax.ShapeDtypeStruct((T, W), BF16),
        [], [], buf, tuple(srcs), ("parallel", "parallel"))
    return out[0]


def _tri(n, lower):
    r = lax.broadcasted_iota(jnp.int32, (n, n), 0)
    c = lax.broadcasted_iota(jnp.int32, (n, n), 1)
    return jnp.where((c <= r) if lower else (c >= r), 1.0, 0.0).astype(F32)


def _dot_exact(a, b):
    return lax.dot_general(a, b, NN, precision=lax.Precision.HIGHEST, preferred_element_type=F32)


def _log_sigmoid(z):
    return jnp.minimum(z, 0.0) - jnp.log(1.0 + jnp.exp(-jnp.abs(z)))


def _gate_fwd(f_raw, b_pad, hp, out_scale, name, blk=256):
    T = f_raw.shape[0]
    blk = min(blk, T)

    def body(f_ref, b_ref, c_ref):
        tri = _tri(blk, True)
        carry = jnp.zeros((1, HEAD_DIM), F32)
        for j in range(T // blk):
            lf = _log_sigmoid(f_ref[j * blk:(j + 1) * blk, :] + b_ref[...])
            cb = _dot_exact(tri, lf) + carry
            carry = cb[blk - 1:blk, :]
            c_ref[:, j * blk:(j + 1) * blk] = cb.T[:hp, :] * out_scale

    return pl.pallas_call(
        body, name=name,
        in_specs=[pl.BlockSpec(memory_space=pltpu.VMEM)] * 2,
        out_specs=pl.BlockSpec(memory_space=pltpu.VMEM),
        out_shape=jax.ShapeDtypeStruct((hp, T), F32),
        compiler_params=_params(),
    )(f_raw, b_pad)


def _gate_bwd(dc_rows, dc_cols, f_raw, b_pad, n_heads, hp, name, blk=256):
    T = f_raw.shape[0]
    blk = min(blk, T)

    def body(dc_ref, dcc_ref, f_ref, b_ref, dz_ref, db_ref):
        tri = _tri(blk, False)
        lane = lax.broadcasted_iota(jnp.int32, (blk, HEAD_DIM), 1)
        carry = jnp.zeros((1, HEAD_DIM), F32)
        db = jnp.zeros((1, HEAD_DIM), F32)
        for j in reversed(range(T // blk)):
            rows = dc_ref[:, j * blk:(j + 1) * blk]
            if hp < HEAD_DIM:
                rows = jnp.concatenate([rows, jnp.zeros((HEAD_DIM - hp, blk), F32)], axis=0)
            dlf = _dot_exact(tri, rows.T + dcc_ref[j * blk:(j + 1) * blk, :]) + carry
            carry = dlf[0:1, :]
            z = f_ref[j * blk:(j + 1) * blk, :] + b_ref[...]
            dz = jnp.where(lane < n_heads, dlf / (1.0 + jnp.exp(z)), 0.0)
            dz_ref[j * blk:(j + 1) * blk, :] = dz.astype(BF16)
            db = db + jnp.sum(dz, axis=0, keepdims=True)
        db_ref[...] = db

    return pl.pallas_call(
        body, name=name,
        in_specs=[pl.BlockSpec(memory_space=pltpu.VMEM)] * 4,
        out_specs=[pl.BlockSpec(memory_space=pltpu.VMEM)] * 2,
        out_shape=[jax.ShapeDtypeStruct((T, HEAD_DIM), BF16), jax.ShapeDtypeStruct((1, HEAD_DIM), F32)],
        compiler_params=_params(),
    )(dc_rows, dc_cols, f_raw, b_pad)


def _pairs(nb, key_major):
    if key_major:
        pairs = [(qi, ki) for ki in range(nb) for qi in range(ki, nb)]
    else:
        pairs = [(qi, ki) for qi in range(nb) for ki in range(qi + 1)]
    return (jnp.asarray(np.array([p[0] for p in pairs], np.int32)),
            jnp.asarray(np.array([p[1] for p in pairs], np.int32)))


LOG2E = 1.4426950408889634
LN2 = 0.6931471805599453
FOX_Q_SCALE = HEAD_DIM ** -0.5 * LOG2E


def _fox_logits(q, k, ck_row, diagonal):
    s = _dot(q, k, NT) - ck_row
    if diagonal:
        row = lax.broadcasted_iota(jnp.int32, s.shape, 0)
        col = lax.broadcasted_iota(jnp.int32, s.shape, 1)
        s = jnp.where(col <= row, s, MASKED)
    return s


def _fox_fwd(qkv, ck, H, name, tb=1024, hs=4, rider=None):
    T = qkv.shape[0]
    tb = min(tb, T)
    nb = T // tb
    qt, kt = _pairs(nb, False)
    n_pairs = int(qt.shape[0])
    hb = H // hs
    r_in, r_out = (len(rider.operands), len(rider.out_shapes)) if rider else (0, 0)

    def body(qt_ref, kt_ref, q_ref, k_ref, v_ref, ck_ref, *rest):
        r_src, (o_ref, lse_ref), r_dst = rest[:r_in], rest[r_in:r_in + 2], rest[r_in + 2:r_in + 2 + r_out]
        m_sc, l_sc, acc_sc = rest[r_in + 2 + r_out:r_in + 5 + r_out]
        r_sems = rest[r_in + 5 + r_out:]
        p_ = pl.program_id(1)
        qi, ki = qt_ref[p_], kt_ref[p_]
        if rider:
            @pl.when((pl.program_id(0) == 0) & (p_ == 0))
            def _():
                rider.start(r_src, r_dst, r_sems)

        @pl.when(ki == 0)
        def _():
            m_sc[...] = jnp.full_like(m_sc, MASKED)
            l_sc[...] = jnp.zeros_like(l_sc)
            acc_sc[...] = jnp.zeros_like(acc_sc)

        heads = [(hh, slice(hh * HEAD_DIM, (hh + 1) * HEAD_DIM)) for hh in range(hs)]

        def tile(diagonal):
            for hh, sl in heads:
                s = _fox_logits(q_ref[:, sl], k_ref[:, sl], ck_ref[hh], diagonal)
                m_prev = m_sc[hh]
                m_new = jnp.maximum(m_prev, jnp.max(s, axis=1, keepdims=True))
                alpha = jnp.exp2(m_prev - m_new)
                p = jnp.exp2(s - m_new[:, :1])
                l_sc[hh] = alpha * l_sc[hh] + jnp.sum(p, axis=1, keepdims=True)
                acc_sc[hh] = alpha * acc_sc[hh] + _dot(p.astype(BF16), v_ref[:, sl], NN)
                m_sc[hh] = m_new

        @pl.when(ki < qi)
        def _():
            tile(False)

        @pl.when(ki == qi)
        def _():
            tile(True)
            for hh, sl in heads:
                o_ref[:, sl] = (acc_sc[hh] / l_sc[hh]).astype(BF16)
                lse_ref[:, sl] = m_sc[hh] + jnp.log(l_sc[hh]) * LOG2E

        if rider:
            @pl.when((pl.program_id(0) == hb - 1) & (p_ == n_pairs - 1))
            def _():
                rider.finish(r_src, r_dst, r_sems)

    blk = lambda f: pl.BlockSpec((tb, hs * HEAD_DIM), f)
    outs = pl.pallas_call(
        body, name=name,
        grid_spec=pltpu.PrefetchScalarGridSpec(
            num_scalar_prefetch=2, grid=(hb, n_pairs),
            in_specs=[blk(lambda h, p, qt, kt: (qt[p], h)),
                      blk(lambda h, p, qt, kt: (kt[p], hb + h)),
                      blk(lambda h, p, qt, kt: (kt[p], 2 * hb + h)),
                      pl.BlockSpec((hs, 1, tb), lambda h, p, qt, kt: (h, 0, kt[p]))] + [ANY] * r_in,
            out_specs=[blk(lambda h, p, qt, kt: (qt[p], h)), blk(lambda h, p, qt, kt: (qt[p], h))] + [ANY] * r_out,
            scratch_shapes=[pltpu.VMEM((hs, tb, HEAD_DIM), F32)] * 3 + (list(rider.scratch) if rider else [])),
        out_shape=[jax.ShapeDtypeStruct((T, H * HEAD_DIM), BF16), jax.ShapeDtypeStruct((T, H * HEAD_DIM), F32)]
        + (list(rider.out_shapes) if rider else []),
        compiler_params=_params("arbitrary", "arbitrary", has_side_effects=bool(rider)),
    )(qt, kt, qkv, qkv, qkv, ck, *(rider.operands if rider else ()))
    return outs[0], outs[1], list(outs[2:])


def _row_dot(do, o, nh, width, name, lane_per_head, tb=256):
    T = do.shape[0]
    tb = min(tb, T)
    wout = HEAD_DIM if lane_per_head else nh * HEAD_DIM

    def body(do_ref, o_ref, d_ref):
        lane = lax.broadcasted_iota(jnp.int32, (tb, HEAD_DIM), 1)
        tile = jnp.zeros((tb, HEAD_DIM), F32)
        for h in range(nh):
            sl = slice(h * width, (h + 1) * width)
            d = jnp.sum(do_ref[:, sl].astype(F32) * o_ref[:, sl].astype(F32), axis=1, keepdims=True)
            if lane_per_head:
                tile = jnp.where(lane == h, d, tile)
            else:
                d_ref[:, h * HEAD_DIM:(h + 1) * HEAD_DIM] = jnp.broadcast_to(d, (tb, HEAD_DIM))
        if lane_per_head:
            d_ref[...] = tile

    row = pl.BlockSpec((tb, nh * width), lambda i: (i, 0))
    return pl.pallas_call(
        body, name=name, grid=(T // tb,),
        in_specs=[row, row], out_specs=pl.BlockSpec((tb, wout), lambda i: (i, 0)),
        out_shape=jax.ShapeDtypeStruct((T, wout), F32),
        compiler_params=_params("parallel"),
    )(do, o)


def _fox_bwd(qkv, do, ck, lse, dd, H, hp, name, tb=1024, hs=2, rider=None):
    T = qkv.shape[0]
    tb = min(tb, T)
    nb = T // tb
    qt, kt = _pairs(nb, True)
    n_pairs = int(qt.shape[0])
    r_in, r_out = (len(rider.operands), len(rider.out_shapes)) if rider else (0, 0)
    hb = H // hs

    def body(qt_ref, kt_ref, q_ref, k_ref, v_ref, do_ref, ck_ref, lse_ref, dd_ref, *rest):
        r_src, r_dst, r_sems = rest[:r_in], rest[r_in + 5:r_in + 5 + r_out], rest[r_in + 5 + r_out:]
        dq_ref, dk_ref, dv_ref, dc_ref, dcq_ref = rest[r_in:r_in + 5]
        p_ = pl.program_id(1)
        qi, ki = qt_ref[p_], kt_ref[p_]
        if rider:
            @pl.when((pl.program_id(0) == 0) & (p_ == 0))
            def _():
                rider.start(r_src, r_dst, r_sems)

        @pl.when(p_ == 0)
        def _():
            dq_ref[...] = jnp.zeros_like(dq_ref)
            dcq_ref[...] = jnp.zeros_like(dcq_ref)

        @pl.when(qi == ki)
        def _():
            dk_ref[...] = jnp.zeros_like(dk_ref)
            dv_ref[...] = jnp.zeros_like(dv_ref)
            dc_ref[...] = jnp.zeros_like(dc_ref)

        rows = pl.ds(pl.multiple_of(qi * tb, tb), tb)

        def tile(diagonal):
            for hh in range(hs):
                sl = slice(hh * HEAD_DIM, (hh + 1) * HEAD_DIM)
                s = _fox_logits(q_ref[:, sl], k_ref[:, sl], ck_ref[hh], diagonal)
                p = jnp.exp2(s - lse_ref[:, hh * HEAD_DIM:hh * HEAD_DIM + 1])
                dv_ref[:, sl] += _dot(p.astype(BF16), do_ref[:, sl], TN)
                dp = _dot(do_ref[:, sl], v_ref[:, sl], NT)
                ds = p * (dp - dd_ref[:, hh * HEAD_DIM:hh * HEAD_DIM + 1])
                dc_ref[hh] -= jnp.sum(ds, axis=0, keepdims=True)
                dcq_ref[rows, sl] += jnp.sum(ds, axis=1, keepdims=True)
                dsb = ds.astype(BF16)
                dq_ref[rows, sl] += _dot(dsb, k_ref[:, sl], NN)
                dk_ref[:, sl] += _dot(dsb, q_ref[:, sl], TN)

        @pl.when(ki < qi)
        def _():
            tile(False)

        @pl.when(ki == qi)
        def _():
            tile(True)

        if rider:
            @pl.when((pl.program_id(0) == hb - 1) & (p_ == n_pairs - 1))
            def _():
                rider.finish(r_src, r_dst, r_sems)

    blk = lambda f: pl.BlockSpec((tb, hs * HEAD_DIM), f)
    at_q = lambda h, p, qt, kt: (qt[p], h)
    at_k = lambda h, p, qt, kt: (kt[p], h)
    crow = pl.BlockSpec((hs, 1, tb), lambda h, p, qt, kt: (h, 0, kt[p]))
    whole = pl.BlockSpec((T, hs * HEAD_DIM), lambda h, p, qt, kt: (0, h), pipeline_mode=pl.Buffered(1))
    wide = jax.ShapeDtypeStruct((T, H * HEAD_DIM), F32)
    outs = pl.pallas_call(
        body, name=name,
        grid_spec=pltpu.PrefetchScalarGridSpec(
            num_scalar_prefetch=2, grid=(hb, n_pairs),
            in_specs=[blk(at_q),
                      blk(lambda h, p, qt, kt: (kt[p], hb + h)),
                      blk(lambda h, p, qt, kt: (kt[p], 2 * hb + h)),
                      blk(at_q), crow, blk(at_q), blk(at_q)] + [ANY] * r_in,
            out_specs=[whole, blk(at_k), blk(at_k), crow, whole] + [ANY] * r_out,
            scratch_shapes=list(rider.scratch) if rider else []),
        out_shape=[wide, wide, wide, jax.ShapeDtypeStruct((hp, 1, T), F32), wide] + (list(rider.out_shapes) if rider else []),
        compiler_params=_params("arbitrary", "arbitrary", has_side_effects=bool(rider)),
    )(qt, kt, qkv, qkv, qkv, do, ck, lse, dd, *(rider.operands if rider else ()))
    return outs[:5], list(outs[5:])


def _lane_per_head(wide, H, name, tb=256):
    T = wide.shape[0]
    tb = min(tb, T)

    def body(w_ref, o_ref):
        lane = lax.broadcasted_iota(jnp.int32, (tb, HEAD_DIM), 1)
        tile = jnp.zeros((tb, HEAD_DIM), F32)
        for h in range(H):
            tile = jnp.where(lane == h, w_ref[:, h * HEAD_DIM:(h + 1) * HEAD_DIM], tile)
        o_ref[...] = tile

    return pl.pallas_call(
        body, name=name, grid=(T // tb,),
        in_specs=[pl.BlockSpec((tb, H * HEAD_DIM), lambda i: (i, 0))],
        out_specs=pl.BlockSpec((tb, HEAD_DIM), lambda i: (i, 0)),
        out_shape=jax.ShapeDtypeStruct((T, HEAD_DIM), F32),
        compiler_params=_params("parallel"),
    )(wide)


def _slopes(n_groups, nh):
    n = n_groups * nh
    s = np.exp2(-ALIBI_MAX_EXP * np.arange(1, n + 1, dtype=np.float32) / np.float32(n)).astype(np.float32)
    return s.reshape(n_groups, nh)


def _window_logits(qh, kh, slope_r, prev, has_prev):
    qi = lax.broadcasted_iota(jnp.int32, (DIL_SPAN, DIL_SPAN), 0)
    kl = lax.broadcasted_iota(jnp.int32, (DIL_SPAN, DIL_SPAN), 1)
    delta = qi - kl + (DIL_SPAN if prev else 0)
    s = _dot(qh, kh, NT) * (HEAD_DIM ** -0.5) - slope_r * delta.astype(F32)
    valid = ((kl >= qi) & has_prev) if prev else (kl <= qi)
    return jnp.where(valid, s, MASKED)


def _dil_views(T, r, G, nh, dv):
    L = T // r
    C, V = nh * HEAD_DIM, nh * dv
    return L, C, V, 2 * G * C + V


def _dil_fwd(qkv, g, r, G, nh, dv, slopes, name):
    T = qkv.shape[0]
    L, C, V, W = _dil_views(T, r, G, nh, dv)
    nblk = L // DIL_SPAN
    nc, nv = W // C, W // V
    view = qkv.reshape(L, r * W)

    def body(q_ref, kp_ref, kc_ref, vp_ref, vc_ref, num_ref, m_ref, den_ref):
        has_prev = pl.program_id(1) > 0
        lane = lax.broadcasted_iota(jnp.int32, (DIL_SPAN, HEAD_DIM), 1)
        m_tile = jnp.zeros((DIL_SPAN, HEAD_DIM), F32)
        den_tile = jnp.ones((DIL_SPAN, HEAD_DIM), F32)
        for h in range(nh):
            sl = slice(h * HEAD_DIM, (h + 1) * HEAD_DIM)
            vs = slice(h * dv, (h + 1) * dv)
            sr = float(slopes[h]) * r
            sc = _window_logits(q_ref[:, sl], kc_ref[:, sl], sr, False, has_prev)
            sp = _window_logits(q_ref[:, sl], kp_ref[:, sl], sr, True, has_prev)
            m = jnp.maximum(jnp.max(sc, axis=1, keepdims=True), jnp.max(sp, axis=1, keepdims=True))
            pc, pp = jnp.exp(sc - m), jnp.exp(sp - m)
            den = jnp.sum(pc, axis=1, keepdims=True) + jnp.sum(pp, axis=1, keepdims=True)
            num_ref[:, vs] = _dot(pc.astype(BF16), vc_ref[:, vs], NN) + _dot(pp.astype(BF16), vp_ref[:, vs], NN)
            m_tile = jnp.where(lane == h, m, m_tile)
            den_tile = jnp.where(lane == h, den, den_tile)
        m_ref[...] = m_tile
        den_ref[...] = den_tile

    prev = lambda i: jnp.maximum(i - 1, 0)
    stat = pl.BlockSpec((DIL_SPAN, HEAD_DIM), lambda b, i: (i, b))
    num, m, den = pl.pallas_call(
        body, name=name, grid=(r, nblk),
        in_specs=[pl.BlockSpec((DIL_SPAN, C), lambda b, i: (i, b * nc + g)),
                  pl.BlockSpec((DIL_SPAN, C), lambda b, i: (prev(i), b * nc + G + g)),
                  pl.BlockSpec((DIL_SPAN, C), lambda b, i: (i, b * nc + G + g)),
                  pl.BlockSpec((DIL_SPAN, V), lambda b, i: (prev(i), b * nv + nv - 1)),
                  pl.BlockSpec((DIL_SPAN, V), lambda b, i: (i, b * nv + nv - 1))],
        out_specs=[pl.BlockSpec((DIL_SPAN, V), lambda b, i: (i, b)), stat, stat],
        out_shape=[jax.ShapeDtypeStruct((L, r * V), F32), jax.ShapeDtypeStruct((L, r * HEAD_DIM), F32),
                   jax.ShapeDtypeStruct((L, r * HEAD_DIM), F32)],
        compiler_params=_params("parallel", "parallel"),
    )(view, view, view, view, view)
    return num.reshape(T, V), m.reshape(T, HEAD_DIM), den.reshape(T, HEAD_DIM)


def _dil_merge(nums, ms, dens, nh, dv, name, tb=256):
    T, V = nums[0].shape
    tb = min(tb, T)
    G = len(nums)

    def body(*refs):
        num_r, m_r, den_r = refs[:G], refs[G:2 * G], refs[2 * G:3 * G]
        o_ref, lse_ref = refs[3 * G], refs[3 * G + 1]
        mm = m_r[0][...]
        for g in range(1, G):
            mm = jnp.maximum(mm, m_r[g][...])
        w = [jnp.exp(m_r[g][...] - mm) for g in range(G)]
        den = w[0] * den_r[0][...]
        for g in range(1, G):
            den = den + w[g] * den_r[g][...]
        lse_ref[...] = mm + jnp.log(den)
        for h in range(nh):
            vs = slice(h * dv, (h + 1) * dv)
            num = w[0][:, h:h + 1] * num_r[0][:, vs]
            for g in range(1, G):
                num = num + w[g][:, h:h + 1] * num_r[g][:, vs]
            o_ref[:, vs] = (num / den[:, h:h + 1]).astype(BF16)

    wide = pl.BlockSpec((tb, V), lambda i: (i, 0))
    stat = pl.BlockSpec((tb, HEAD_DIM), lambda i: (i, 0))
    return pl.pallas_call(
        body, name=name, grid=(T // tb,),
        in_specs=[wide] * G + [stat] * (2 * G),
        out_specs=[wide, stat],
        out_shape=[jax.ShapeDtypeStruct((T, V), BF16), jax.ShapeDtypeStruct((T, HEAD_DIM), F32)],
        compiler_params=_params("parallel"),
    )(*nums, *ms, *dens)


def _dil_dq(qkv, do, lse, dd, g, r, G, nh, dv, slopes, name):
    T = qkv.shape[0]
    L, C, V, W = _dil_views(T, r, G, nh, dv)
    nblk = L // DIL_SPAN
    nc, nv = W // C, W // V
    view = qkv.reshape(L, r * W)
    scale = HEAD_DIM ** -0.5

    def body(q_ref, kp_ref, kc_ref, vp_ref, vc_ref, do_ref, lse_ref, dd_ref, dq_ref):
        has_prev = pl.program_id(1) > 0
        for h in range(nh):
            sl = slice(h * HEAD_DIM, (h + 1) * HEAD_DIM)
            vs = slice(h * dv, (h + 1) * dv)
            sr = float(slopes[h]) * r
            lse_h, dd_h = lse_ref[:, h:h + 1], dd_ref[:, h:h + 1]
            acc = jnp.zeros((DIL_SPAN, HEAD_DIM), F32)
            for k_ref, v_ref, is_prev in ((kc_ref, vc_ref, False), (kp_ref, vp_ref, True)):
                s = _window_logits(q_ref[:, sl], k_ref[:, sl], sr, is_prev, has_prev)
                p = jnp.exp(s - lse_h)
                dp = _dot(do_ref[:, vs], v_ref[:, vs], NT)
                ds = (p * (dp - dd_h)).astype(BF16)
                acc = acc + _dot(ds, k_ref[:, sl], NN)
            dq_ref[:, sl] = scale * acc

    prev = lambda i: jnp.maximum(i - 1, 0)
    stat = pl.BlockSpec((DIL_SPAN, HEAD_DIM), lambda b, i: (i, b))
    dq = pl.pallas_call(
        body, name=name, grid=(r, nblk),
        in_specs=[pl.BlockSpec((DIL_SPAN, C), lambda b, i: (i, b * nc + g)),
                  pl.BlockSpec((DIL_SPAN, C), lambda b, i: (prev(i), b * nc + G + g)),
                  pl.BlockSpec((DIL_SPAN, C), lambda b, i: (i, b * nc + G + g)),
                  pl.BlockSpec((DIL_SPAN, V), lambda b, i: (prev(i), b * nv + nv - 1)),
                  pl.BlockSpec((DIL_SPAN, V), lambda b, i: (i, b * nv + nv - 1)),
                  pl.BlockSpec((DIL_SPAN, V), lambda b, i: (i, b)), stat, stat],
        out_specs=pl.BlockSpec((DIL_SPAN, C), lambda b, i: (i, b)),
        out_shape=jax.ShapeDtypeStruct((L, r * C), F32),
        compiler_params=_params("parallel", "parallel"),
    )(view, view, view, view, view, do.reshape(L, r * V), lse.reshape(L, r * HEAD_DIM), dd.reshape(L, r * HEAD_DIM))
    return dq.reshape(T, C)


def _dil_dkv(qkv, do, lse, dd, g, r, G, nh, dv, slopes, name):
    T = qkv.shape[0]
    L, C, V, W = _dil_views(T, r, G, nh, dv)
    nblk = L // DIL_SPAN
    nc, nv = W // C, W // V
    view = qkv.reshape(L, r * W)
    scale = HEAD_DIM ** -0.5

    def body(k_ref, v_ref, qc_ref, qn_ref, doc_ref, don_ref, lsec_ref, lsen_ref, ddc_ref, ddn_ref, dk_ref, dv_ref):
        has_next = pl.program_id(1) < nblk - 1
        for h in range(nh):
            sl = slice(h * HEAD_DIM, (h + 1) * HEAD_DIM)
            vs = slice(h * dv, (h + 1) * dv)
            sr = float(slopes[h]) * r
            dk = jnp.zeros((DIL_SPAN, HEAD_DIM), F32)
            dvh = jnp.zeros((DIL_SPAN, dv), F32)
            for q_ref, do_ref, lse_ref, dd_ref, is_next in ((qc_ref, doc_ref, lsec_ref, ddc_ref, False),
                                                          (qn_ref, don_ref, lsen_ref, ddn_ref, True)):
                s = _window_logits(q_ref[:, sl], k_ref[:, sl], sr, is_next, has_next)
                p = jnp.exp(s - lse_ref[:, h:h + 1])
                dvh = dvh + _dot(p.astype(BF16), do_ref[:, vs], TN)
                dp = _dot(do_ref[:, vs], v_ref[:, vs], NT)
                ds = (p * (dp - dd_ref[:, h:h + 1])).astype(BF16)
                dk = dk + _dot(ds, q_ref[:, sl], TN)
            dk_ref[:, sl] = scale * dk
            dv_ref[:, vs] = dvh

    nxt = lambda i: jnp.minimum(i + 1, nblk - 1)
    stat_c = pl.BlockSpec((DIL_SPAN, HEAD_DIM), lambda b, i: (i, b))
    stat_n = pl.BlockSpec((DIL_SPAN, HEAD_DIM), lambda b, i: (nxt(i), b))
    do_v, lse_v, dd_v = do.reshape(L, r * V), lse.reshape(L, r * HEAD_DIM), dd.reshape(L, r * HEAD_DIM)
    dk, dvv = pl.pallas_call(
        body, name=name, grid=(r, nblk),
        in_specs=[pl.BlockSpec((DIL_SPAN, C), lambda b, i: (i, b * nc + G + g)),
                  pl.BlockSpec((DIL_SPAN, V), lambda b, i: (i, b * nv + nv - 1)),
                  pl.BlockSpec((DIL_SPAN, C), lambda b, i: (i, b * nc + g)),
                  pl.BlockSpec((DIL_SPAN, C), lambda b, i: (nxt(i), b * nc + g)),
                  pl.BlockSpec((DIL_SPAN, V), lambda b, i: (i, b)),
                  pl.BlockSpec((DIL_SPAN, V), lambda b, i: (nxt(i), b)),
                  stat_c, stat_n, stat_c, stat_n],
        out_specs=[pl.BlockSpec((DIL_SPAN, C), lambda b, i: (i, b)), pl.BlockSpec((DIL_SPAN, V), lambda b, i: (i, b))],
        out_shape=[jax.ShapeDtypeStruct((L, r * C), F32), jax.ShapeDtypeStruct((L, r * V), F32)],
        compiler_params=_params("parallel", "parallel"),
    )(view, view, view, view, do_v, do_v, lse_v, lse_v, dd_v, dd_v)
    return dk.reshape(T, C), dvv.reshape(T, V)


def _relu2(r):
    a = jnp.maximum(r, 0.0)
    return (a * a,)


def _mlp_fwd(x, g, w_up, w_down, tag):
    T, D = x.shape
    F = w_down.shape[0]
    h = _rms_fwd(x, g, f"{tag}_norm")
    a2 = _mm(h, w_up, T, F, D, mode="nn", name=f"{tag}_up", b_stack=N_CHIPS, out_dtypes=(BF16,), epilogue=_relu2)
    y = _mm(a2, w_down, T, D, F, mode="nn", name=f"{tag}_down", out_dtypes=(F32,), extras=(x,),
            epilogue=lambda r, res: (res + r,))
    return y, (x, h, a2)


def _mlp_bwd(dy, dyb, saved, g, w_up, w_down, tag):
    x, h, a2 = saved
    T, D = x.shape
    F = w_down.shape[0]
    d_down = _mm(a2, dyb, F, D, T, mode="tn", name=f"{tag}_dwdown", out_dtypes=(F32,))
    du = _mm(dyb, w_down, T, F, D, mode="nt", name=f"{tag}_da", out_dtypes=(BF16,), extras=(a2,),
             epilogue=lambda r, sq: (r * (2.0 * jnp.sqrt(sq.astype(F32))),))
    d_up = _mm(h, du, D, F, T, mode="tn", name=f"{tag}_dwup", out_stack=N_CHIPS, out_dtypes=(F32,))
    dh = _mm(du, w_up, T, D, F, mode="nt", name=f"{tag}_dh", b_stack=N_CHIPS, out_dtypes=(F32,))
    dx, dxb, dg = _rms_bwd(dh, x, g, dy, f"{tag}_dnorm")
    return dx, dxb, dg, d_up, d_down


def _fox_dims(D):
    H = D // HEAD_DIM
    return H, max(8, H), (H // 2) * HEAD_DIM


def _fox_layer_fwd(x, g, w_qkv, w_f, b_pad, gains, w_out, rider=None):
    T, D = x.shape
    H, hp, ch = _fox_dims(D)
    h = _rms_fwd(x, g, "fox_norm")
    proj = _mm(h, w_qkv, T, 3 * D, D, mode="nn", name="fox_proj", out_dtypes=(F32,))
    f_raw = _mm(h, w_f, T, HEAD_DIM, D, mode="nn", name="fox_gate_proj", out_dtypes=(F32,))
    qkv = _qkv_prep(proj, gains, 4, lambda j: jnp.minimum(j // 2, 1), ch, "fox_qk_norm", n_scaled=2, post_scale=FOX_Q_SCALE)
    ck = _gate_fwd(f_raw, b_pad, hp, LOG2E, "fox_gate").reshape(hp, 1, T)
    o, lse, carried = _fox_fwd(qkv, ck, H, "fox_attn", rider=rider)
    y = _mm(o, w_out, T, D, D, mode="nn", name="fox_out", out_dtypes=(F32,), extras=(x,),
            epilogue=lambda r, res: (res + r,))
    return y, (x, h, proj, f_raw, qkv, ck, o, lse), carried


def _fox_layer_bwd(dy, dyb, saved, g, w_qkv, w_f, b_pad, gains, w_out, rider=None):
    x, h, proj, f_raw, qkv, ck, o, lse = saved
    T, D = x.shape
    H, hp, ch = _fox_dims(D)
    d_out = _mm(o, dyb, D, D, T, mode="tn", name="fox_dwout", out_dtypes=(F32,))
    do = _mm(dyb, w_out, T, D, D, mode="nt", name="fox_do", out_dtypes=(BF16,))
    dd = _row_dot(do, o, H, HEAD_DIM, "fox_rowdot", False)
    (dq, dk, dv, dck, dcq), carried = _fox_bwd(qkv, do, ck, lse, dd, H, hp, "fox_attn_bwd", rider=rider)
    dcq = _lane_per_head(dcq, H, "fox_dc_query")
    dproj, dgq = _head_rms_bwd_into(None, 3 * D, dq, proj, gains[0], 0, ch, "fox_dq_norm", in_scale=HEAD_DIM ** -0.5)
    dproj, dgk = _head_rms_bwd_into(dproj, 3 * D, dk, proj, gains[1], 2, ch, "fox_dk_norm", in_scale=LN2)
    dproj = _sum_cast_into(dproj, 3 * D, [dv], 4, ch, "fox_dv_cast")
    dz, db = _gate_bwd(dck.reshape(hp, T), dcq, f_raw, b_pad, H, hp, "fox_gate_bwd")
    d_qkv = _mm(h, dproj, D, 3 * D, T, mode="tn", name="fox_dwqkv", out_dtypes=(F32,))
    d_f = _mm(h, dz, D, HEAD_DIM, T, mode="tn", name="fox_dwgate", out_dtypes=(F32,))
    dh = _mm(dproj, w_qkv, T, D, 3 * D, mode="nt", name="fox_dh", out_dtypes=(F32,))
    dh = _mm(dz, w_f, T, D, HEAD_DIM, mode="nt", name="fox_dh_gate", out_dtypes=(F32,), extras=(dh,),
             epilogue=lambda r, e: (e + r,))
    dx, dxb, dg = _rms_bwd(dh, x, g, dy, "fox_dnorm")
    dgains = jnp.stack([dgq.sum(axis=0), dgk.sum(axis=0)])
    return dx, dxb, dg, d_qkv, d_f, db, dgains, d_out, carried


def _dil_dims(D):
    nh = D // (2 * HEAD_DIM)
    return nh, D // nh, len(DIL_PATTERNS)


def _dil_layer_fwd(x, g, w_in, gains, w_out):
    T, D = x.shape
    nh, dv, G = _dil_dims(D)
    C = nh * HEAD_DIM
    W = 2 * G * C + nh * dv
    slopes = _slopes(G, nh)
    h = _rms_fwd(x, g, "dil_norm")
    proj = _mm(h, w_in, T, W, D, mode="nn", name="dil_proj", b_stack=N_CHIPS, out_dtypes=(F32,))
    qkv = _qkv_prep(proj, gains, 2 * G, lambda j: jnp.minimum(j, 2 * G - 1), C, "dil_qk_norm")
    parts = [_dil_fwd(qkv, gi, r, G, nh, dv, slopes[gi], f"dil_attn_g{gi}") for gi, (_, r) in enumerate(DIL_PATTERNS)]
    o, lse = _dil_merge([p[0] for p in parts], [p[1] for p in parts], [p[2] for p in parts], nh, dv, "dil_merge")
    y = _mm(o, w_out, T, D, D, mode="nn", name="dil_out", out_dtypes=(F32,), extras=(x,),
            epilogue=lambda r, res: (res + r,))
    return y, (x, h, proj, qkv, o, lse)


def _dil_layer_bwd(dy, dyb, saved, g, w_in, gains, w_out):
    x, h, proj, qkv, o, lse = saved
    T, D = x.shape
    nh, dv, G = _dil_dims(D)
    C = nh * HEAD_DIM
    W = 2 * G * C + nh * dv
    slopes = _slopes(G, nh)
    d_out = _mm(o, dyb, D, D, T, mode="tn", name="dil_dwout", out_dtypes=(F32,))
    do = _mm(dyb, w_out, T, D, D, mode="nt", name="dil_do", out_dtypes=(BF16,))
    dd = _row_dot(do, o, nh, dv, "dil_rowdot", True)
    dproj, dgs, dvs = None, [None] * (2 * G), []
    for gi, (_, r) in enumerate(DIL_PATTERNS):
        dq = _dil_dq(qkv, do, lse, dd, gi, r, G, nh, dv, slopes[gi], f"dil_dq_g{gi}")
        dk, dvg = _dil_dkv(qkv, do, lse, dd, gi, r, G, nh, dv, slopes[gi], f"dil_dkv_g{gi}")
        dvs.append(dvg)
        dproj, dgs[gi] = _head_rms_bwd_into(dproj, W, dq, proj, gains[gi], gi, C, f"dil_dq_norm_g{gi}")
        dproj, dgs[G + gi] = _head_rms_bwd_into(dproj, W, dk, proj, gains[G + gi], G + gi, C, f"dil_dk_norm_g{gi}")
    dproj = _sum_cast_into(dproj, W, dvs, 2 * G, C, "dil_dv_cast")
    d_in = _mm(h, dproj, D, W, T, mode="tn", name="dil_dwin", out_stack=N_CHIPS, out_dtypes=(F32,))
    dh = _mm(dproj, w_in, T, D, W, mode="nt", name="dil_dh", b_stack=N_CHIPS, out_dtypes=(F32,))
    dx, dxb, dg = _rms_bwd(dh, x, g, dy, "dil_dnorm")
    dgains = jnp.concatenate(dgs, axis=0)
    return dx, dxb, dg, d_in, dgains, d_out


def _local_step(x, tgt, w, late_shards=None, late_weights=None, early_reduce=None):
    rider = _gather_ici_rider(late_shards) if late_shards is not None else None
    y0, s_fox, landed = _fox_layer_fwd(x, w["mix_g"][0], w["fox_qkv"], w["fox_f"], w["fox_b"], w["fox_gains"], w["fox_out"],
                                       rider=rider)
    if rider:
        w = {**w, **late_weights(_forward_halves(landed))}
    y1, s_mlp0 = _mlp_fwd(y0, w["mlp_g"][0], w["up"][0], w["down"][0], "mlp0")
    y2, s_dil = _dil_layer_fwd(y1, w["mix_g"][1], w["dil_in"], w["dil_gains"], w["dil_out"])
    y3, s_mlp1 = _mlp_fwd(y2, w["mlp_g"][1], w["up"][1], w["down"][1], "mlp1")
    dy, dyb, loss = _loss_head(y3, tgt, "loss_head")
    g = {}
    dy, dyb, g_mlp1, up1, down1 = _mlp_bwd(dy, dyb, s_mlp1, w["mlp_g"][1], w["up"][1], w["down"][1], "mlp1")
    dy, dyb, g_mix1, g["dil_in"], g["dil_gains"], g["dil_out"] = _dil_layer_bwd(
        dy, dyb, s_dil, w["mix_g"][1], w["dil_in"], w["dil_gains"], w["dil_out"])
    dy, dyb, g_mlp0, up0, down0 = _mlp_bwd(dy, dyb, s_mlp0, w["mlp_g"][0], w["up"][0], w["down"][0], "mlp0")
    g["up"], g["down"] = (up0, up1), (down0, down1)
    rider = early_reduce(g) if early_reduce is not None else None
    dy, dyb, g_mix0, g["fox_qkv"], g["fox_f"], g["fox_b"], g["fox_gains"], g["fox_out"], early = _fox_layer_bwd(
        dy, dyb, s_fox, w["mix_g"][0], w["fox_qkv"], w["fox_f"], w["fox_b"], w["fox_gains"], w["fox_out"], rider=rider)
    g["mix_g"], g["mlp_g"] = (g_mix0, g_mix1), (g_mlp0, g_mlp1)
    return loss[0, 0], dy, g, early


def _place():
    x, y, c = lax.axis_index("x"), lax.axis_index("y"), lax.axis_index("c")
    chips = [(1 - x, y), (x, 1 - y), (1 - x, 1 - y)]
    return x, y, c, chips


def _remote(src, dst, send_sem, recv_sem, to):
    return pltpu.make_async_remote_copy(src_ref=src, dst_ref=dst, send_sem=send_sem, recv_sem=recv_sem,
                                        device_id=to, device_id_type=MESH)


def _gather_weights(shards):
    n = len(shards)

    def body(*refs):
        src, dst = refs[:n], refs[n:2 * n]
        send_sems, recv_sems, local_sems = refs[2 * n:]
        x, y, c, chips = _place()
        mine = 2 * x + y
        local = [pltpu.make_async_copy(src[t], dst[t].at[mine], local_sems.at[t]) for t in range(n)]
        for cp in local:
            cp.start()

        def half(t, slot, which):
            hr = shards[t].shape[0] // 2
            return dst[t].at[slot, pl.ds(which * hr, hr), :]

        def my_half(t):
            hr = shards[t].shape[0] // 2
            return src[t].at[pl.ds(c * hr, hr), :]

        sends = []
        for t in range(n):
            for j, (px, py) in enumerate(chips):
                cp = _remote(my_half(t), half(t, mine, c), send_sems.at[t, j], recv_sems.at[t, j], (px, py, c))
                cp.start()
                sends.append(cp)
        for j, (px, py) in enumerate(chips):
            for t in range(n):
                landed = half(t, 2 * px + py, c)
                _remote(landed, landed, send_sems.at[t, j], recv_sems.at[t, j], (px, py, c)).wait_recv()
                cp = _remote(landed, landed, send_sems.at[t, 3 + j], recv_sems.at[t, 3 + j], (x, y, 1 - c))
                cp.start()
                sends.append(cp)
        for j, (px, py) in enumerate(chips):
            for t in range(n):
                other = half(t, 2 * px + py, 1 - c)
                _remote(other, other, send_sems.at[t, 3 + j], recv_sems.at[t, 3 + j], (x, y, 1 - c)).wait_recv()
        for cp in sends:
            cp.wait_send()
        for cp in local:
            cp.wait()

    return pl.pallas_call(
        body, name="gather_weights",
        in_specs=[ANY] * n, out_specs=[ANY] * n,
        out_shape=[jax.ShapeDtypeStruct((N_CHIPS,) + s.shape, s.dtype) for s in shards],
        scratch_shapes=[pltpu.SemaphoreType.DMA((n, 6)), pltpu.SemaphoreType.DMA((n, 6)), pltpu.SemaphoreType.DMA((n,))],
        compiler_params=_params(has_side_effects=True),
    )(*shards)


class _Rider(NamedTuple):
    operands: tuple
    out_shapes: tuple
    scratch: tuple
    start: Callable
    finish: Callable


def _gather_ici_rider(shards):
    n = len(shards)

    def copies(src, dst, sems):
        send_sems, recv_sems, local_sems = sems
        x, y, c, chips = _place()
        mine = 2 * x + y
        local, sends, recvs = [], [], []
        for t in range(n):
            hr = shards[t].shape[0] // 2
            local.append(pltpu.make_async_copy(src[t], dst[t].at[mine], local_sems.at[t]))
            for j, (px, py) in enumerate(chips):
                sends.append(_remote(src[t].at[pl.ds(c * hr, hr), :], dst[t].at[mine, pl.ds(c * hr, hr), :],
                                     send_sems.at[t, j], recv_sems.at[t, j], (px, py, c)))
                landed = dst[t].at[2 * px + py, pl.ds(c * hr, hr), :]
                recvs.append(_remote(landed, landed, send_sems.at[t, j], recv_sems.at[t, j], (px, py, c)))
        return local, sends, recvs

    def start(src, dst, sems):
        local, sends, _ = copies(src, dst, sems)
        for cp in local + sends:
            cp.start()

    def finish(src, dst, sems):
        local, sends, recvs = copies(src, dst, sems)
        for cp in recvs:
            cp.wait_recv()
        for cp in sends:
            cp.wait_send()
        for cp in local:
            cp.wait()

    return _Rider(tuple(shards), tuple(jax.ShapeDtypeStruct((N_CHIPS,) + s.shape, s.dtype) for s in shards),
                  (pltpu.SemaphoreType.DMA((n, 3)), pltpu.SemaphoreType.DMA((n, 3)), pltpu.SemaphoreType.DMA((n,))),
                  start, finish)


def _forward_halves(landed):
    n = len(landed)

    def body(*refs):
        dst = refs[n:2 * n]
        send_sems, recv_sems = refs[2 * n:]
        x, y, c, chips = _place()
        sends = []
        for t in range(n):
            hr = landed[t].shape[1] // 2
            for j, (px, py) in enumerate(chips):
                got = dst[t].at[2 * px + py, pl.ds(c * hr, hr), :]
                cp = _remote(got, got, send_sems.at[t, j], recv_sems.at[t, j], (x, y, 1 - c))
                cp.start()
                sends.append(cp)
        for t in range(n):
            hr = landed[t].shape[1] // 2
            for j, (px, py) in enumerate(chips):
                other = dst[t].at[2 * px + py, pl.ds((1 - c) * hr, hr), :]
                _remote(other, other, send_sems.at[t, j], recv_sems.at[t, j], (x, y, 1 - c)).wait_recv()
        for cp in sends:
            cp.wait_send()

    return pl.pallas_call(
        body, name="gather_forward_halves",
        in_specs=[ANY] * n, out_specs=[ANY] * n,
        out_shape=[jax.ShapeDtypeStruct(a.shape, a.dtype) for a in landed],
        input_output_aliases={t: t for t in range(n)},
        scratch_shapes=[pltpu.SemaphoreType.DMA((n, 3)), pltpu.SemaphoreType.DMA((n, 3))],
        compiler_params=_params(has_side_effects=True),
    )(*landed)


def _pair_exchange(grads, name):
    n = len(grads)

    def body(*refs):
        src, dst = refs[:n], refs[n:2 * n]
        send_sems, recv_sems = refs[2 * n:]
        x, y, c, _ = _place()
        cps = []
        for t in range(n):
            hr = grads[t].shape[1] // 2
            cp = _remote(src[t].at[:, pl.ds((1 - c) * hr, hr), :], dst[t], send_sems.at[t], recv_sems.at[t], (x, y, 1 - c))
            cp.start()
            cps.append(cp)
        for cp in cps:
            cp.wait()

    return pl.pallas_call(
        body, name=name,
        in_specs=[ANY] * n, out_specs=[ANY] * n,
        out_shape=[jax.ShapeDtypeStruct((g.shape[0], g.shape[1] // 2, g.shape[2]), g.dtype) for g in grads],
        scratch_shapes=[pltpu.SemaphoreType.DMA((n,)), pltpu.SemaphoreType.DMA((n,))],
        compiler_params=_params(has_side_effects=True),
    )(*grads)


def _pair_add(g, got, cidx, name, tb=256):
    S, R, C = g.shape
    hr = R // 2
    tb = _rows_tile(hr, tb)
    nb = hr // tb

    def body(c_ref, a_ref, b_ref, o_ref):
        o_ref[...] = (a_ref[...] + b_ref[...]).astype(BF16)

    return pl.pallas_call(
        body, name=name,
        grid_spec=pltpu.PrefetchScalarGridSpec(
            num_scalar_prefetch=1, grid=(S, nb),
            in_specs=[pl.BlockSpec((None, tb, C), lambda s, i, c: (s, c[0] * nb + i, 0)),
                      pl.BlockSpec((None, tb, C), lambda s, i, c: (s, i, 0))],
            out_specs=pl.BlockSpec((None, tb, C), lambda s, i, c: (s, i, 0))),
        out_shape=jax.ShapeDtypeStruct((S, hr, C), BF16),
        compiler_params=_params("parallel", "parallel"),
    )(cidx, g, got)


def _rows_tile(n, want):
    t = min(n, want)
    while n % t or t % 8:
        t -= 8
    return t


def _chip_scatter_rider(sums):
    n = len(sums)

    def copies(src, dst, sems):
        send_sems, recv_sems, local_sems = sems
        x, y, c, chips = _place()
        mine = 2 * x + y
        local, sends, recvs = [], [], []
        for t in range(n):
            local.append(pltpu.make_async_copy(src[t].at[mine], dst[t].at[mine], local_sems.at[t]))
            for j, (px, py) in enumerate(chips):
                sends.append(_remote(src[t].at[2 * px + py], dst[t].at[mine], send_sems.at[t, j], recv_sems.at[t, j], (px, py, c)))
                slot = dst[t].at[2 * px + py]
                recvs.append(_remote(slot, slot, send_sems.at[t, j], recv_sems.at[t, j], (px, py, c)))
        return local, sends, recvs

    def start(src, dst, sems):
        local, sends, _ = copies(src, dst, sems)
        for cp in local + sends:
            cp.start()

    def finish(src, dst, sems):
        local, sends, recvs = copies(src, dst, sems)
        for cp in recvs:
            cp.wait_recv()
        for cp in sends:
            cp.wait_send()
        for cp in local:
            cp.wait()

    return _Rider(tuple(sums), tuple(jax.ShapeDtypeStruct(s.shape, s.dtype) for s in sums),
                  (pltpu.SemaphoreType.DMA((n, 3)), pltpu.SemaphoreType.DMA((n, 3)), pltpu.SemaphoreType.DMA((n,))),
                  start, finish)


def _run_rider(rider, name):
    r_in, r_out = len(rider.operands), len(rider.out_shapes)

    def body(*refs):
        src, dst, sems = refs[:r_in], refs[r_in:r_in + r_out], refs[r_in + r_out:]
        rider.start(src, dst, sems)
        rider.finish(src, dst, sems)

    return pl.pallas_call(
        body, name=name,
        in_specs=[ANY] * r_in, out_specs=[ANY] * r_out,
        out_shape=list(rider.out_shapes), scratch_shapes=list(rider.scratch),
        compiler_params=_params(has_side_effects=True),
    )(*rider.operands)


def _chip_sum(parts, cidx, name, tb=256):
    S, hr, C = parts.shape
    tb = _rows_tile(hr, tb)
    nb = hr // tb

    def body(c_ref, *refs):
        o_ref = refs[S]
        tot = refs[0][...].astype(F32)
        for s in range(1, S):
            tot = tot + refs[s][...].astype(F32)
        o_ref[...] = tot

    return pl.pallas_call(
        body, name=name,
        grid_spec=pltpu.PrefetchScalarGridSpec(
            num_scalar_prefetch=1, grid=(nb,),
            in_specs=[pl.BlockSpec((None, tb, C), functools.partial(lambda s, i, c: (s, i, 0), s)) for s in range(S)],
            out_specs=pl.BlockSpec((tb, C), lambda i, c: (c[0] * nb + i, 0))),
        out_shape=jax.ShapeDtypeStruct((2 * hr, C), F32),
        compiler_params=_params("parallel"),
    )(cidx, *([parts] * S))


def _half_exchange(halves):
    n = len(halves)

    def body(*refs):
        dst = refs[n:2 * n]
        send_sems, recv_sems = refs[2 * n:]
        x, y, c, _ = _place()
        cps = []
        for t in range(n):
            hr = halves[t].shape[0] // 2
            rows = dst[t].at[pl.ds(c * hr, hr), :]
            cp = _remote(rows, rows, send_sems.at[t], recv_sems.at[t], (x, y, 1 - c))
            cp.start()
            cps.append(cp)
        for t, cp in enumerate(cps):
            cp.wait_send()
            hr = halves[t].shape[0] // 2
            other = dst[t].at[pl.ds((1 - c) * hr, hr), :]
            _remote(other, other, send_sems.at[t], recv_sems.at[t], (x, y, 1 - c)).wait_recv()

    return pl.pallas_call(
        body, name="grad_half_exchange",
        in_specs=[ANY] * n, out_specs=[ANY] * n,
        out_shape=[jax.ShapeDtypeStruct(h.shape, h.dtype) for h in halves],
        input_output_aliases={t: t for t in range(n)},
        scratch_shapes=[pltpu.SemaphoreType.DMA((n,)), pltpu.SemaphoreType.DMA((n,))],
        compiler_params=_params(has_side_effects=True),
    )(*halves)


def _adamw_math(w, g, m, v):
    m = ADAM_B1 * m + (1.0 - ADAM_B1) * g
    v = ADAM_B2 * v + (1.0 - ADAM_B2) * (g * g)
    m_hat = m / (1.0 - ADAM_B1 ** ADAM_STEP)
    v_hat = v / (1.0 - ADAM_B2 ** ADAM_STEP)
    delta = -ADAM_LR * (m_hat / (jnp.sqrt(v_hat) + ADAM_EPS) + ADAM_WD * w)
    return delta, m, v


def _adamw(w, gs, m, v, name, tb=256):
    L, R, C = w.shape
    tb = _rows_tile(R, tb)
    nb = R // tb

    def body(w_ref, m_ref, v_ref, *rest):
        g_refs, (go_ref, d_ref, mo_ref, vo_ref) = rest[:L], rest[L:]
        layer = pl.program_id(0)
        for k in range(L):
            @pl.when(layer == k)
            def _(k=k):
                g = g_refs[k][...]
                d, mn, vn = _adamw_math(w_ref[...], g, m_ref[...], v_ref[...])
                go_ref[...] = g
                d_ref[...] = d
                mo_ref[...] = mn
                vo_ref[...] = vn

    def g_spec(k):
        return pl.BlockSpec((tb, C), lambda l, i: (jnp.where(l == k, i, jnp.where(l < k, 0, nb - 1)), 0))

    stacked = pl.BlockSpec((None, tb, C), lambda l, i: (l, i, 0))
    return pl.pallas_call(
        body, name=name, grid=(L, nb),
        in_specs=[stacked] * 3 + [g_spec(k) for k in range(L)], out_specs=[stacked] * 4,
        out_shape=[jax.ShapeDtypeStruct((L, R, C), F32)] * 4,
        compiler_params=_params("arbitrary", "arbitrary"),
    )(w, m, v, *gs)


N_DEV = 8


def _small_update(g, w, m, v):
    P = g.shape[0]

    def body(g_ref, w_ref, m_ref, v_ref, go_ref, d_ref, mo_ref, vo_ref, buf, send_sems, recv_sems):
        x, y, c, _ = _place()
        me = 4 * x + 2 * y + c
        buf[me] = g_ref[...]
        cps = []
        for k in range(1, N_DEV):
            fx, fy, fc = (k >> 2) & 1, (k >> 1) & 1, k & 1
            px = (1 - x) if fx else x
            py = (1 - y) if fy else y
            pc = (1 - c) if fc else c
            cp = _remote(g_ref, buf.at[me], send_sems.at[k - 1], recv_sems.at[k - 1], (px, py, pc))
            cp.start()
            cps.append((cp, 4 * px + 2 * py + pc))
        for k, (cp, peer) in enumerate(cps):
            _remote(g_ref, buf.at[peer], send_sems.at[k], recv_sems.at[k], (x, y, c)).wait_recv()
        for cp, _ in cps:
            cp.wait_send()
        tot = buf[0]
        for d in range(1, N_DEV):
            tot = tot + buf[d]
        go_ref[...] = tot
        dl, mn, vn = _adamw_math(w_ref[...], tot, m_ref[...], v_ref[...])
        d_ref[...] = dl
        mo_ref[...] = mn
        vo_ref[...] = vn

    vm = pl.BlockSpec(memory_space=pltpu.VMEM)
    return pl.pallas_call(
        body, name="small_params_update",
        in_specs=[vm] * 4, out_specs=[vm] * 4,
        out_shape=[jax.ShapeDtypeStruct((P, HEAD_DIM), F32)] * 4,
        scratch_shapes=[pltpu.VMEM((N_DEV, P, HEAD_DIM), F32), pltpu.SemaphoreType.DMA((N_DEV - 1,)),
                        pltpu.SemaphoreType.DMA((N_DEV - 1,))],
        compiler_params=_params(has_side_effects=True),
    )(g, w, m, v)


SMALL = ("fox_b_f", "fox_q_gain", "fox_k_gain", "dil_q_gain", "dil_k_gain", "mix_norm_g", "mlp_norm_g")
LARGE = ("fox_w_in", "fox_w_out", "dil_w_in", "dil_w_out", "mlp_w_up", "mlp_w_down")
WEIGHTS = ("fox_w_in", "fox_b_f", "fox_q_gain", "fox_k_gain", "fox_w_out", "dil_w_in", "dil_q_gain", "dil_k_gain",
           "dil_w_out", "mix_norm_g", "mlp_norm_g", "mlp_w_up", "mlp_w_down")


def _pack(parts):
    rows = []
    for a in parts:
        flat = a.reshape(-1)
        n = -(-flat.shape[0] // (8 * HEAD_DIM)) * (8 * HEAD_DIM)
        rows.append(jnp.pad(flat, (0, n - flat.shape[0])).reshape(-1, HEAD_DIM))
    return jnp.concatenate(rows, axis=0)


def _unpack(packed, like):
    out, r = [], 0
    for a in like:
        size = int(np.prod(a.shape))
        n = -(-size // (8 * HEAD_DIM)) * 8
        out.append(packed[r:r + n].reshape(-1)[:size].reshape(a.shape))
        r += n
    return out


def _pad_lanes(a):
    return jnp.pad(a, [(0, 0)] * (a.ndim - 1) + [(0, HEAD_DIM - a.shape[-1])])


def _as_shards(a):
    return a.reshape(N_CHIPS, a.shape[0] // N_CHIPS, a.shape[1])


def kernel(x, fox_w_in, fox_b_f, fox_q_gain, fox_k_gain, fox_w_out, dil_w_in, dil_q_gain, dil_k_gain, dil_w_out, mix_norm_g, mlp_norm_g, mlp_w_up, mlp_w_down, loss_target, m_fox_w_in, m_fox_b_f, m_fox_q_gain, m_fox_k_gain, m_fox_w_out, m_dil_w_in, m_dil_q_gain, m_dil_k_gain, m_dil_w_out, m_mix_norm_g, m_mlp_norm_g, m_mlp_w_up, m_mlp_w_down, v_fox_w_in, v_fox_b_f, v_fox_q_gain, v_fox_k_gain, v_fox_w_out, v_dil_w_in, v_dil_q_gain, v_dil_k_gain, v_dil_w_out, v_mix_norm_g, v_mlp_norm_g, v_mlp_w_up, v_mlp_w_down):
    wts = dict(fox_w_in=fox_w_in, fox_b_f=fox_b_f, fox_q_gain=fox_q_gain, fox_k_gain=fox_k_gain, fox_w_out=fox_w_out,
               dil_w_in=dil_w_in, dil_q_gain=dil_q_gain, dil_k_gain=dil_k_gain, dil_w_out=dil_w_out,
               mix_norm_g=mix_norm_g, mlp_norm_g=mlp_norm_g, mlp_w_up=mlp_w_up, mlp_w_down=mlp_w_down)
    mom1 = dict(fox_w_in=m_fox_w_in, fox_b_f=m_fox_b_f, fox_q_gain=m_fox_q_gain, fox_k_gain=m_fox_k_gain,
                fox_w_out=m_fox_w_out, dil_w_in=m_dil_w_in, dil_q_gain=m_dil_q_gain, dil_k_gain=m_dil_k_gain,
                dil_w_out=m_dil_w_out, mix_norm_g=m_mix_norm_g, mlp_norm_g=m_mlp_norm_g, mlp_w_up=m_mlp_w_up,
                mlp_w_down=m_mlp_w_down)
    mom2 = dict(fox_w_in=v_fox_w_in, fox_b_f=v_fox_b_f, fox_q_gain=v_fox_q_gain, fox_k_gain=v_fox_k_gain,
                fox_w_out=v_fox_w_out, dil_w_in=v_dil_w_in, dil_q_gain=v_dil_q_gain, dil_k_gain=v_dil_k_gain,
                dil_w_out=v_dil_w_out, mix_norm_g=v_mix_norm_g, mlp_norm_g=v_mlp_norm_g, mlp_w_up=v_mlp_w_up,
                mlp_w_down=v_mlp_w_down)
    T, D = x.shape[1], x.shape[2]
    H = D // HEAD_DIM
    cidx = lax.axis_index("c").astype(jnp.int32).reshape(1)

    def shards_of(d):
        return [d["fox_w_in"][0], d["fox_w_out"][0], d["dil_w_in"][0], d["dil_w_out"][0],
                d["mlp_w_up"][0], d["mlp_w_up"][1], d["mlp_w_down"][0], d["mlp_w_down"][1]]

    w_bf = [s.astype(BF16) for s in shards_of(wts)]
    first = _gather_weights(w_bf[:2])
    fox_in = jnp.moveaxis(first[0], 0, 1).reshape(D, -1)
    w = dict(
        fox_qkv=fox_in[:, :3 * D], fox_f=_pad_lanes(fox_in[:, 3 * D:]), fox_b=_pad_lanes(fox_b_f),
        fox_gains=jnp.stack([fox_q_gain, fox_k_gain]), fox_out=first[1].reshape(D, D),
        dil_gains=jnp.concatenate([dil_q_gain[0], dil_k_gain[0]])[:, None, :],
        mix_g=[mix_norm_g[0:1], mix_norm_g[1:2]], mlp_g=[mlp_norm_g[0:1], mlp_norm_g[1:2]])

    def late_weights(full):
        return dict(dil_in=full[0], dil_out=full[1].reshape(D, D), up=[full[2], full[3]],
                    down=[full[4].reshape(-1, D), full[5].reshape(-1, D)])

    def chip_sums(stacked, first, tag):
        got = _pair_exchange(stacked, f"grad_pair_exchange_{tag}")
        return [_pair_add(a, b, cidx, f"grad_pair_add_{first + t}") for t, (a, b) in enumerate(zip(stacked, got))]

    def early_reduce(g):
        stacked = [g["dil_in"], _as_shards(g["dil_out"]), g["up"][0], g["up"][1],
                   _as_shards(g["down"][0]), _as_shards(g["down"][1])]
        return _chip_scatter_rider(chip_sums(stacked, 2, "early"))

    loss, grad_x, g, early_parts = _local_step(x.reshape(T, D), loss_target.reshape(T, D), w, w_bf[2:], late_weights,
                                               early_reduce)
    loss = lax.psum(loss, ("x", "y", "c"))

    g_fox_in = jnp.concatenate([g["fox_qkv"], g["fox_f"][:, :H]], axis=1)
    g_fox_in = jnp.moveaxis(g_fox_in.reshape(D, N_CHIPS, -1), 1, 0)
    late_sums = chip_sums([g_fox_in, _as_shards(g["fox_out"])], 0, "late")
    parts = list(_run_rider(_chip_scatter_rider(late_sums), "grad_chip_scatter_late")) + early_parts
    halves = [_chip_sum(p, cidx, f"grad_chip_sum_{t}") for t, p in enumerate(parts)]
    totals = _half_exchange(halves)
    layers = dict(fox_w_in=[0], fox_w_out=[1], dil_w_in=[2], dil_w_out=[3], mlp_w_up=[4, 5], mlp_w_down=[6, 7])
    upd = {n: _adamw(wts[n], [totals[t] for t in ts], mom1[n], mom2[n], f"adamw_{n}") for n, ts in layers.items()}

    def large(k):
        return {n: upd[n][k] for n in LARGE}

    small_like = [wts[n] for n in SMALL]
    g_small = [g["fox_b"][:, :H], g["fox_gains"][0], g["fox_gains"][1], g["dil_gains"][:3, 0][None], g["dil_gains"][3:, 0][None],
               jnp.concatenate(g["mix_g"]), jnp.concatenate(g["mlp_g"])]
    packed = _small_update(_pack(g_small), _pack(small_like), _pack([mom1[n] for n in SMALL]), _pack([mom2[n] for n in SMALL]))
    small = [dict(zip(SMALL, _unpack(p, small_like))) for p in packed]

    outs = [loss, grad_x.reshape(x.shape)]
    for k in range(4):
        big = large(k)
        outs += [big[n] if n in big else small[k][n] for n in WEIGHTS]
    return tuple(outs)
```

```python
import functools
from typing import Callable, NamedTuple

import numpy as np
import jax
import jax.numpy as jnp
from jax import lax
from jax.experimental import pallas as pl
from jax.experimental.pallas import tpu as pltpu

F32 = jnp.float32
BF16 = jnp.bfloat16

HEAD_DIM = 128
DIL_PATTERNS = ((128, 1), (512, 4), (2048, 16))
DIL_SPAN = 128
ALIBI_MAX_EXP = 8.0
EPS = 1e-6
MASKED = -1e30

ADAM_LR = 0.001
ADAM_B1 = 0.9
ADAM_B2 = 0.999
ADAM_EPS = 1e-08
ADAM_WD = 0.01
ADAM_STEP = 10

N_CHIPS = 4
VMEM_LIMIT_BYTES = 56 * 1024 * 1024
MESH = pl.DeviceIdType.MESH
ANY = pl.BlockSpec(memory_space=pl.ANY)

NN = (((1,), (0,)), ((), ()))
NT = (((1,), (1,)), ((), ()))
TN = (((0,), (0,)), ((), ()))


def _params(*sem, **kw):
    return pltpu.CompilerParams(dimension_semantics=sem or None, vmem_limit_bytes=VMEM_LIMIT_BYTES, **kw)


def _dot(a, b, dims):
    return lax.dot_general(a, b, dims, preferred_element_type=F32)


def _tile(n, want):
    if n <= want:
        return n
    t = want - want % 128
    while n % t:
        t -= 128
    return t


def _mm(a, b, M, N, K, *, mode, name, out_dtypes, b_stack=0, out_stack=0, extras=(), epilogue=None,
        tm=1024, tn=1024, tk=2048, hosts=None):
    rider = hosts.rider(name) if hosts is not None else None
    per_b = per_o = None
    if b_stack:
        per_b = (K if mode == "nt" else N) // b_stack
    if out_stack:
        per_o = N // out_stack
    tm = _tile(M, tm)
    tn = _tile(min(x for x in (N, per_o, per_b if mode != "nt" else None) if x), tn)
    tk = _tile(min(x for x in (K, per_b if mode == "nt" else None) if x), tk)
    assert M % tm == 0 and N % tn == 0 and K % tk == 0, (name, M, N, K, tm, tn, tk)
    gk = K // tk
    if mode == "tn":
        a_spec = pl.BlockSpec((tk, tm), lambda i, j, k: (k, i))
    else:
        a_spec = pl.BlockSpec((tm, tk), lambda i, j, k: (i, k))
    if mode == "nt":
        if b_stack:
            npk = per_b // tk
            b_spec = pl.BlockSpec((None, tn, tk), lambda i, j, k: (k // npk, j, k % npk))
        else:
            b_spec = pl.BlockSpec((tn, tk), lambda i, j, k: (j, k))
    else:
        if b_stack:
            npj = per_b // tn
            b_spec = pl.BlockSpec((None, tk, tn), lambda i, j, k: (j // npj, k, j % npj))
        else:
            b_spec = pl.BlockSpec((tk, tn), lambda i, j, k: (k, j))
    if out_stack:
        npo = per_o // tn
        o_spec = pl.BlockSpec((None, tm, tn), lambda i, j, k: (j // npo, i, j % npo))
        o_shape = (out_stack, M, per_o)
    else:
        o_spec = pl.BlockSpec((tm, tn), lambda i, j, k: (i, j))
        o_shape = (M, N)
    e_spec = pl.BlockSpec((tm, tn), lambda i, j, k: (i, j))
    dims = {"nn": NN, "nt": NT, "tn": TN}[mode]
    ne, no = len(extras), len(out_dtypes)
    gi, gj = M // tm, N // tn
    r_in, r_out = (len(rider.operands), len(rider.out_shapes)) if rider else (0, 0)
    n_acc = 1 if gk > 1 else 0

    def body(a_ref, b_ref, *rest):
        ex, r_src = rest[:ne], rest[ne:ne + r_in]
        outs, r_dst = rest[ne + r_in:ne + r_in + no], rest[ne + r_in + no:ne + r_in + no + r_out]
        r_sems = rest[ne + r_in + no + r_out + n_acc:]
        i, j, k = pl.program_id(0), pl.program_id(1), pl.program_id(2)
        if rider:
            @pl.when((i == 0) & (j == 0) & (k == 0))
            def _():
                rider.start(r_src, r_dst, r_sems)

        def product():
            return _dot(a_ref[...].astype(BF16), b_ref[...].astype(BF16), dims)

        def finish(r):
            res = epilogue(r, *[e[...] for e in ex]) if epilogue is not None else (r,)
            for o, v in zip(outs, res):
                o[...] = v.astype(o.dtype)

        if gk == 1:
            finish(product())
        else:
            acc = rest[ne + r_in + no + r_out]

            @pl.when(k == 0)
            def _():
                acc[...] = product()

            @pl.when((k > 0) & (k < gk - 1))
            def _():
                acc[...] += product()

            @pl.when(k == gk - 1)
            def _():
                finish(acc[...] + product())

        if rider:
            @pl.when((i == gi - 1) & (j == gj - 1) & (k == gk - 1))
            def _():
                rider.finish(r_src, r_dst, r_sems)

    outs = pl.pallas_call(
        body, name=name,
        grid=(gi, gj, gk),
        in_specs=[a_spec, b_spec] + [e_spec] * ne + [ANY] * r_in,
        out_specs=[o_spec] * no + [ANY] * r_out,
        out_shape=[jax.ShapeDtypeStruct(o_shape, d) for d in out_dtypes] + (list(rider.out_shapes) if rider else []),
        scratch_shapes=([pltpu.VMEM((tm, tn), F32)] if gk > 1 else []) + (list(rider.scratch) if rider else []),
        input_output_aliases={2 + ne + s: no + d for s, d in rider.aliases.items()} if rider else {},
        compiler_params=(_params("arbitrary", "arbitrary", "arbitrary", has_side_effects=True) if rider
                         else _params("parallel", "parallel", "arbitrary")),
    )(a, b, *extras, *(rider.operands if rider else ()))
    if rider:
        hosts.carried[name] = list(outs[no:])
    return outs[0] if no == 1 else outs[:no]


def _rms_fwd(x, g, name, tb=512):
    T, D = x.shape
    tb = min(tb, T)

    def body(x_ref, g_ref, o_ref):
        xv = x_ref[...]
        r = lax.rsqrt(jnp.mean(xv * xv, axis=-1, keepdims=True) + EPS)
        o_ref[...] = (xv * r * g_ref[...]).astype(BF16)

    return pl.pallas_call(
        body, name=name, grid=(T // tb,),
        in_specs=[pl.BlockSpec((tb, D), lambda i: (i, 0)), pl.BlockSpec((1, D), lambda i: (0, 0))],
        out_specs=pl.BlockSpec((tb, D), lambda i: (i, 0)),
        out_shape=jax.ShapeDtypeStruct((T, D), BF16),
        compiler_params=_params("parallel"),
    )(x, g)


def _rms_bwd(dy, x, g, dres, name, tb=256):
    T, D = x.shape
    tb = min(tb, T)

    def body(dy_ref, x_ref, g_ref, dres_ref, dx_ref, dxb_ref, dg_ref):
        i = pl.program_id(0)
        xv, dyv = x_ref[...], dy_ref[...]
        r = lax.rsqrt(jnp.mean(xv * xv, axis=-1, keepdims=True) + EPS)
        gy = dyv * g_ref[...]
        dx = r * gy - xv * (r * r * r) * jnp.mean(gy * xv, axis=-1, keepdims=True)
        tot = dres_ref[...] + dx
        dx_ref[...] = tot
        dxb_ref[...] = tot.astype(BF16)
        part = jnp.sum(dyv * (xv * r), axis=0, keepdims=True)

        @pl.when(i == 0)
        def _():
            dg_ref[...] = part

        @pl.when(i > 0)
        def _():
            dg_ref[...] += part

    row = pl.BlockSpec((tb, D), lambda i: (i, 0))
    vec = pl.BlockSpec((1, D), lambda i: (0, 0))
    return pl.pallas_call(
        body, name=name, grid=(T // tb,),
        in_specs=[row, row, vec, row],
        out_specs=[row, row, vec],
        out_shape=[jax.ShapeDtypeStruct((T, D), F32), jax.ShapeDtypeStruct((T, D), BF16),
                   jax.ShapeDtypeStruct((1, D), F32)],
        compiler_params=_params("arbitrary"),
    )(dy, x, g, dres)


def _loss_head(y, tgt, name, tb=256):
    T, D = y.shape
    tb = min(tb, T)

    def body(y_ref, t_ref, dy_ref, dyb_ref, loss_ref):
        i = pl.program_id(0)
        e = y_ref[...] - t_ref[...]
        d = e * (1.0 / D)
        dy_ref[...] = d
        dyb_ref[...] = d.astype(BF16)
        part = 0.5 * jnp.sum(jnp.sum(e * e, axis=1, keepdims=True) * (1.0 / D), axis=0, keepdims=True)

        @pl.when(i == 0)
        def _():
            loss_ref[...] = part

        @pl.when(i > 0)
        def _():
            loss_ref[...] += part

    row = pl.BlockSpec((tb, D), lambda i: (i, 0))
    return pl.pallas_call(
        body, name=name, grid=(T // tb,),
        in_specs=[row, row],
        out_specs=[row, row, pl.BlockSpec((1, 1), lambda i: (0, 0))],
        out_shape=[jax.ShapeDtypeStruct((T, D), F32), jax.ShapeDtypeStruct((T, D), BF16),
                   jax.ShapeDtypeStruct((1, 1), F32)],
        compiler_params=_params("arbitrary"),
    )(y, tgt)


def _head_rms(xh, g):
    r = lax.rsqrt(jnp.mean(xh * xh, axis=-1, keepdims=True) + EPS)
    return xh * r * g


def _qkv_prep(proj, gains, n_norm, gain_row, ch, name, n_scaled=0, post_scale=1.0, tb=512):
    T, W = proj.shape
    tb = min(tb, T)
    nch = W // ch
    nh = ch // HEAD_DIM

    def body(p_ref, g_ref, o_ref):
        j = pl.program_id(0)

        @pl.when(j < n_norm)
        def _():
            g = g_ref[...]
            if n_scaled:
                g = g * jnp.where(j < n_scaled, post_scale, 1.0)
            for h in range(nh):
                sl = slice(h * HEAD_DIM, (h + 1) * HEAD_DIM)
                o_ref[:, sl] = _head_rms(p_ref[:, sl], g).astype(BF16)

        @pl.when(j >= n_norm)
        def _():
            o_ref[...] = p_ref[...].astype(BF16)

    return pl.pallas_call(
        body, name=name, grid=(nch, T // tb),
        in_specs=[pl.BlockSpec((tb, ch), lambda j, i: (i, j)),
                  pl.BlockSpec((None, 1, HEAD_DIM), lambda j, i: (gain_row(j), 0, 0))],
        out_specs=pl.BlockSpec((tb, ch), lambda j, i: (i, j)),
        out_shape=jax.ShapeDtypeStruct((T, W), BF16),
        compiler_params=_params("parallel", "parallel"),
    )(proj, gains)


def _into(body, name, grid, in_specs, out_spec, out_shape, extra_out_specs, extra_out_shapes, buf, operands, sem):
    if buf is None:
        def kernel(*refs):
            body(*refs)
        ins, alias, ops = in_specs, {}, operands
    else:
        def kernel(_, *refs):
            body(*refs)
        ins = [pl.BlockSpec(memory_space=pl.ANY)] + in_specs
        alias, ops = {0: 0}, (buf,) + tuple(operands)
    return pl.pallas_call(
        kernel, name=name, grid=grid, in_specs=ins,
        out_specs=[out_spec] + extra_out_specs,
        out_shape=[out_shape] + extra_out_shapes,
        input_output_aliases=alias,
        compiler_params=_params(*sem),
    )(*ops)


def _head_rms_bwd_into(buf, W, d, proj, gain, off, ch, name, in_scale=1.0, tb=256):
    T, wd = d.shape
    tb = min(tb, T)
    n = wd // ch
    nh = ch // HEAD_DIM

    def body(d_ref, p_ref, g_ref, o_ref, dg_ref):
        i = pl.program_id(1)
        g = g_ref[...]
        part = jnp.zeros((1, HEAD_DIM), F32)
        for h in range(nh):
            sl = slice(h * HEAD_DIM, (h + 1) * HEAD_DIM)
            xh, dy = p_ref[:, sl], d_ref[:, sl]
            if in_scale != 1.0:
                dy = dy * in_scale
            r = lax.rsqrt(jnp.mean(xh * xh, axis=-1, keepdims=True) + EPS)
            gy = dy * g
            dx = r * gy - xh * (r * r * r) * jnp.mean(gy * xh, axis=-1, keepdims=True)
            o_ref[:, sl] = dx.astype(BF16)
            part = part + jnp.sum(dy * (xh * r), axis=0, keepdims=True)

        @pl.when(i == 0)
        def _():
            dg_ref[...] = part

        @pl.when(i > 0)
        def _():
            dg_ref[...] += part

    return _into(
        body, name, (n, T // tb),
        [pl.BlockSpec((tb, ch), lambda j, i: (i, j)), pl.BlockSpec((tb, ch), lambda j, i: (i, off + j)),
         pl.BlockSpec((1, HEAD_DIM), lambda j, i: (0, 0))],
        pl.BlockSpec((tb, ch), lambda j, i: (i, off + j)), jax.ShapeDtypeStruct((T, W), BF16),
        [pl.BlockSpec((None, 1, HEAD_DIM), lambda j, i: (j, 0, 0))], [jax.ShapeDtypeStruct((n, 1, HEAD_DIM), F32)],
        buf, (d, proj, gain), ("parallel", "arbitrary"))


def _sum_cast_into(buf, W, srcs, off, ch, name, tb=256):
    T, wd = srcs[0].shape
    tb = min(tb, T)
    n = wd // ch
    ns = len(srcs)

    def body(*refs):
        o_ref = refs[ns]
        tot = refs[0][...]
        for s in refs[1:ns]:
            tot = tot + s[...]
        o_ref[...] = tot.astype(BF16)

    out = _into(
        body, name, (n, T // tb),
        [pl.BlockSpec((tb, ch), lambda j, i: (i, j))] * ns,
        pl.BlockSpec((tb, ch), lambda j, i: (i, off + j)), jax.ShapeDtypeStruct((T, W), BF16),
        [], [], buf, tuple(srcs), ("parallel", "parallel"))
    return out[0]


def _tri(n, lower):
    r = lax.broadcasted_iota(jnp.int32, (n, n), 0)
    c = lax.broadcasted_iota(jnp.int32, (n, n), 1)
    return jnp.where((c <= r) if lower else (c >= r), 1.0, 0.0).astype(F32)


def _dot_exact(a, b):
    return lax.dot_general(a, b, NN, precision=lax.Precision.HIGHEST, preferred_element_type=F32)


def _log_sigmoid(z):
    return jnp.minimum(z, 0.0) - jnp.log(1.0 + jnp.exp(-jnp.abs(z)))


def _gate_fwd(f_raw, b_pad, hp, out_scale, name, blk=256):
    T = f_raw.shape[0]
    blk = min(blk, T)

    def body(f_ref, b_ref, c_ref):
        tri = _tri(blk, True)
        carry = jnp.zeros((1, HEAD_DIM), F32)
        for j in range(T // blk):
            lf = _log_sigmoid(f_ref[j * blk:(j + 1) * blk, :] + b_ref[...])
            cb = _dot_exact(tri, lf) + carry
            carry = cb[blk - 1:blk, :]
            c_ref[:, j * blk:(j + 1) * blk] = cb.T[:hp, :] * out_scale

    return pl.pallas_call(
        body, name=name,
        in_specs=[pl.BlockSpec(memory_space=pltpu.VMEM)] * 2,
        out_specs=pl.BlockSpec(memory_space=pltpu.VMEM),
        out_shape=jax.ShapeDtypeStruct((hp, T), F32),
        compiler_params=_params(),
    )(f_raw, b_pad)


def _gate_bwd(dc_rows, dc_cols, f_raw, b_pad, n_heads, hp, name, blk=256):
    T = f_raw.shape[0]
    blk = min(blk, T)

    def body(dc_ref, dcc_ref, f_ref, b_ref, dz_ref, db_ref):
        tri = _tri(blk, False)
        lane = lax.broadcasted_iota(jnp.int32, (blk, HEAD_DIM), 1)
        carry = jnp.zeros((1, HEAD_DIM), F32)
        db = jnp.zeros((1, HEAD_DIM), F32)
        for j in reversed(range(T // blk)):
            rows = dc_ref[:, j * blk:(j + 1) * blk]
            if hp < HEAD_DIM:
                rows = jnp.concatenate([rows, jnp.zeros((HEAD_DIM - hp, blk), F32)], axis=0)
            dlf = _dot_exact(tri, rows.T + dcc_ref[j * blk:(j + 1) * blk, :]) + carry
            carry = dlf[0:1, :]
            z = f_ref[j * blk:(j + 1) * blk, :] + b_ref[...]
            dz = jnp.where(lane < n_heads, dlf / (1.0 + jnp.exp(z)), 0.0)
            dz_ref[j * blk:(j + 1) * blk, :] = dz.astype(BF16)
            db = db + jnp.sum(dz, axis=0, keepdims=True)
        db_ref[...] = db

    return pl.pallas_call(
        body, name=name,
        in_specs=[pl.BlockSpec(memory_space=pltpu.VMEM)] * 4,
        out_specs=[pl.BlockSpec(memory_space=pltpu.VMEM)] * 2,
        out_shape=[jax.ShapeDtypeStruct((T, HEAD_DIM), BF16), jax.ShapeDtypeStruct((1, HEAD_DIM), F32)],
        compiler_params=_params(),
    )(dc_rows, dc_cols, f_raw, b_pad)


def _pairs(nb, key_major):
    if key_major:
        pairs = [(qi, ki) for ki in range(nb) for qi in range(ki, nb)]
    else:
        pairs = [(qi, ki) for qi in range(nb) for ki in range(qi + 1)]
    return (jnp.asarray(np.array([p[0] for p in pairs], np.int32)),
            jnp.asarray(np.array([p[1] for p in pairs], np.int32)))


LOG2E = 1.4426950408889634
LN2 = 0.6931471805599453
FOX_Q_SCALE = HEAD_DIM ** -0.5 * LOG2E


def _fox_logits(q, k, ck_row, diagonal):
    s = _dot(q, k, NT) - ck_row
    if diagonal:
        row = lax.broadcasted_iota(jnp.int32, s.shape, 0)
        col = lax.broadcasted_iota(jnp.int32, s.shape, 1)
        s = jnp.where(col <= row, s, MASKED)
    return s


def _fox_fwd(qkv, ck, H, name, tb=1024, hs=4, hosts=None):
    rider = hosts.rider(name) if hosts is not None else None
    T = qkv.shape[0]
    tb = min(tb, T)
    nb = T // tb
    qt, kt = _pairs(nb, False)
    n_pairs = int(qt.shape[0])
    hb = H // hs
    r_in, r_out = (len(rider.operands), len(rider.out_shapes)) if rider else (0, 0)

    def body(qt_ref, kt_ref, q_ref, k_ref, v_ref, ck_ref, *rest):
        r_src, (o_ref, lse_ref), r_dst = rest[:r_in], rest[r_in:r_in + 2], rest[r_in + 2:r_in + 2 + r_out]
        m_sc, l_sc, acc_sc = rest[r_in + 2 + r_out:r_in + 5 + r_out]
        r_sems = rest[r_in + 5 + r_out:]
        p_ = pl.program_id(1)
        qi, ki = qt_ref[p_], kt_ref[p_]
        if rider:
            @pl.when((pl.program_id(0) == 0) & (p_ == 0))
            def _():
                rider.start(r_src, r_dst, r_sems)

        @pl.when(ki == 0)
        def _():
            m_sc[...] = jnp.full_like(m_sc, MASKED)
            l_sc[...] = jnp.zeros_like(l_sc)
            acc_sc[...] = jnp.zeros_like(acc_sc)

        heads = [(hh, slice(hh * HEAD_DIM, (hh + 1) * HEAD_DIM)) for hh in range(hs)]

        def tile(diagonal):
            for hh, sl in heads:
                s = _fox_logits(q_ref[:, sl], k_ref[:, sl], ck_ref[hh], diagonal)
                m_prev = m_sc[hh]
                m_new = jnp.maximum(m_prev, jnp.max(s, axis=1, keepdims=True))
                alpha = jnp.exp2(m_prev - m_new)
                p = jnp.exp2(s - m_new[:, :1])
                l_sc[hh] = alpha * l_sc[hh] + jnp.sum(p, axis=1, keepdims=True)
                acc_sc[hh] = alpha * acc_sc[hh] + _dot(p.astype(BF16), v_ref[:, sl], NN)
                m_sc[hh] = m_new

        @pl.when(ki < qi)
        def _():
            tile(False)

        @pl.when(ki == qi)
        def _():
            tile(True)
            for hh, sl in heads:
                o_ref[:, sl] = (acc_sc[hh] / l_sc[hh]).astype(BF16)
                lse_ref[:, sl] = m_sc[hh] + jnp.log(l_sc[hh]) * LOG2E

        if rider:
            @pl.when((pl.program_id(0) == hb - 1) & (p_ == n_pairs - 1))
            def _():
                rider.finish(r_src, r_dst, r_sems)

    blk = lambda f: pl.BlockSpec((tb, hs * HEAD_DIM), f)
    outs = pl.pallas_call(
        body, name=name,
        grid_spec=pltpu.PrefetchScalarGridSpec(
            num_scalar_prefetch=2, grid=(hb, n_pairs),
            in_specs=[blk(lambda h, p, qt, kt: (qt[p], h)),
                      blk(lambda h, p, qt, kt: (kt[p], hb + h)),
                      blk(lambda h, p, qt, kt: (kt[p], 2 * hb + h)),
                      pl.BlockSpec((hs, 1, tb), lambda h, p, qt, kt: (h, 0, kt[p]))] + [ANY] * r_in,
            out_specs=[blk(lambda h, p, qt, kt: (qt[p], h)), blk(lambda h, p, qt, kt: (qt[p], h))] + [ANY] * r_out,
            scratch_shapes=[pltpu.VMEM((hs, tb, HEAD_DIM), F32)] * 3 + (list(rider.scratch) if rider else [])),
        out_shape=[jax.ShapeDtypeStruct((T, H * HEAD_DIM), BF16), jax.ShapeDtypeStruct((T, H * HEAD_DIM), F32)]
        + (list(rider.out_shapes) if rider else []),
        compiler_params=_params("arbitrary", "arbitrary", has_side_effects=bool(rider)),
    )(qt, kt, qkv, qkv, qkv, ck, *(rider.operands if rider else ()))
    if rider:
        hosts.carried[name] = list(outs[2:])
    return outs[0], outs[1]


def _row_dot(do, o, nh, width, name, lane_per_head, tb=256):
    T = do.shape[0]
    tb = min(tb, T)
    wout = HEAD_DIM if lane_per_head else nh * HEAD_DIM

    def body(do_ref, o_ref, d_ref):
        lane = lax.broadcasted_iota(jnp.int32, (tb, HEAD_DIM), 1)
        tile = jnp.zeros((tb, HEAD_DIM), F32)
        for h in range(nh):
            sl = slice(h * width, (h + 1) * width)
            d = jnp.sum(do_ref[:, sl].astype(F32) * o_ref[:, sl].astype(F32), axis=1, keepdims=True)
            if lane_per_head:
                tile = jnp.where(lane == h, d, tile)
            else:
                d_ref[:, h * HEAD_DIM:(h + 1) * HEAD_DIM] = jnp.broadcast_to(d, (tb, HEAD_DIM))
        if lane_per_head:
            d_ref[...] = tile

    row = pl.BlockSpec((tb, nh * width), lambda i: (i, 0))
    return pl.pallas_call(
        body, name=name, grid=(T // tb,),
        in_specs=[row, row], out_specs=pl.BlockSpec((tb, wout), lambda i: (i, 0)),
        out_shape=jax.ShapeDtypeStruct((T, wout), F32),
        compiler_params=_params("parallel"),
    )(do, o)


def _fox_bwd(qkv, do, ck, lse, dd, H, hp, name, tb=1024, hs=2, hosts=None):
    rider = hosts.rider(name) if hosts is not None else None
    T = qkv.shape[0]
    tb = min(tb, T)
    nb = T // tb
    qt, kt = _pairs(nb, True)
    n_pairs = int(qt.shape[0])
    r_in, r_out = (len(rider.operands), len(rider.out_shapes)) if rider else (0, 0)
    hb = H // hs

    def body(qt_ref, kt_ref, q_ref, k_ref, v_ref, do_ref, ck_ref, lse_ref, dd_ref, *rest):
        r_src, r_dst, r_sems = rest[:r_in], rest[r_in + 5:r_in + 5 + r_out], rest[r_in + 5 + r_out:]
        dq_ref, dk_ref, dv_ref, dc_ref, dcq_ref = rest[r_in:r_in + 5]
        p_ = pl.program_id(1)
        qi, ki = qt_ref[p_], kt_ref[p_]
        if rider:
            @pl.when((pl.program_id(0) == 0) & (p_ == 0))
            def _():
                rider.start(r_src, r_dst, r_sems)

        @pl.when(p_ == 0)
        def _():
            dq_ref[...] = jnp.zeros_like(dq_ref)
            dcq_ref[...] = jnp.zeros_like(dcq_ref)

        @pl.when(qi == ki)
        def _():
            dk_ref[...] = jnp.zeros_like(dk_ref)
            dv_ref[...] = jnp.zeros_like(dv_ref)
            dc_ref[...] = jnp.zeros_like(dc_ref)

        rows = pl.ds(pl.multiple_of(qi * tb, tb), tb)

        def tile(diagonal):
            for hh in range(hs):
                sl = slice(hh * HEAD_DIM, (hh + 1) * HEAD_DIM)
                s = _fox_logits(q_ref[:, sl], k_ref[:, sl], ck_ref[hh], diagonal)
                p = jnp.exp2(s - lse_ref[:, hh * HEAD_DIM:hh * HEAD_DIM + 1])
                dv_ref[:, sl] += _dot(p.astype(BF16), do_ref[:, sl], TN)
                dp = _dot(do_ref[:, sl], v_ref[:, sl], NT)
                ds = p * (dp - dd_ref[:, hh * HEAD_DIM:hh * HEAD_DIM + 1])
                dc_ref[hh] -= jnp.sum(ds, axis=0, keepdims=True)
                dcq_ref[rows, sl] += jnp.sum(ds, axis=1, keepdims=True)
                dsb = ds.astype(BF16)
                dq_ref[rows, sl] += _dot(dsb, k_ref[:, sl], NN)
                dk_ref[:, sl] += _dot(dsb, q_ref[:, sl], TN)

        @pl.when(ki < qi)
        def _():
            tile(False)

        @pl.when(ki == qi)
        def _():
            tile(True)

        if rider:
            @pl.when((pl.program_id(0) == hb - 1) & (p_ == n_pairs - 1))
            def _():
                rider.finish(r_src, r_dst, r_sems)

    blk = lambda f: pl.BlockSpec((tb, hs * HEAD_DIM), f)
    at_q = lambda h, p, qt, kt: (qt[p], h)
    at_k = lambda h, p, qt, kt: (kt[p], h)
    crow = pl.BlockSpec((hs, 1, tb), lambda h, p, qt, kt: (h, 0, kt[p]))
    whole = pl.BlockSpec((T, hs * HEAD_DIM), lambda h, p, qt, kt: (0, h), pipeline_mode=pl.Buffered(1))
    wide = jax.ShapeDtypeStruct((T, H * HEAD_DIM), F32)
    outs = pl.pallas_call(
        body, name=name,
        grid_spec=pltpu.PrefetchScalarGridSpec(
            num_scalar_prefetch=2, grid=(hb, n_pairs),
            in_specs=[blk(at_q),
                      blk(lambda h, p, qt, kt: (kt[p], hb + h)),
                      blk(lambda h, p, qt, kt: (kt[p], 2 * hb + h)),
                      blk(at_q), crow, blk(at_q), blk(at_q)] + [ANY] * r_in,
            out_specs=[whole, blk(at_k), blk(at_k), crow, whole] + [ANY] * r_out,
            scratch_shapes=list(rider.scratch) if rider else []),
        out_shape=[wide, wide, wide, jax.ShapeDtypeStruct((hp, 1, T), F32), wide] + (list(rider.out_shapes) if rider else []),
        compiler_params=_params("arbitrary", "arbitrary", has_side_effects=bool(rider)),
    )(qt, kt, qkv, qkv, qkv, do, ck, lse, dd, *(rider.operands if rider else ()))
    if rider:
        hosts.carried[name] = list(outs[5:])
    return outs[:5]


def _lane_per_head(wide, H, name, tb=256):
    T = wide.shape[0]
    tb = min(tb, T)

    def body(w_ref, o_ref):
        lane = lax.broadcasted_iota(jnp.int32, (tb, HEAD_DIM), 1)
        tile = jnp.zeros((tb, HEAD_DIM), F32)
        for h in range(H):
            tile = jnp.where(lane == h, w_ref[:, h * HEAD_DIM:(h + 1) * HEAD_DIM], tile)
        o_ref[...] = tile

    return pl.pallas_call(
        body, name=name, grid=(T // tb,),
        in_specs=[pl.BlockSpec((tb, H * HEAD_DIM), lambda i: (i, 0))],
        out_specs=pl.BlockSpec((tb, HEAD_DIM), lambda i: (i, 0)),
        out_shape=jax.ShapeDtypeStruct((T, HEAD_DIM), F32),
        compiler_params=_params("parallel"),
    )(wide)


def _slopes(n_groups, nh):
    n = n_groups * nh
    s = np.exp2(-ALIBI_MAX_EXP * np.arange(1, n + 1, dtype=np.float32) / np.float32(n)).astype(np.float32)
    return s.reshape(n_groups, nh)


def _window_logits(qh, kh, slope_r, prev, has_prev):
    qi = lax.broadcasted_iota(jnp.int32, (DIL_SPAN, DIL_SPAN), 0)
    kl = lax.broadcasted_iota(jnp.int32, (DIL_SPAN, DIL_SPAN), 1)
    delta = qi - kl + (DIL_SPAN if prev else 0)
    s = _dot(qh, kh, NT) * (HEAD_DIM ** -0.5) - slope_r * delta.astype(F32)
    valid = ((kl >= qi) & has_prev) if prev else (kl <= qi)
    return jnp.where(valid, s, MASKED)


def _dil_views(T, r, G, nh, dv):
    L = T // r
    C, V = nh * HEAD_DIM, nh * dv
    return L, C, V, 2 * G * C + V


def _dil_fwd(qkv, g, r, G, nh, dv, slopes, name):
    T = qkv.shape[0]
    L, C, V, W = _dil_views(T, r, G, nh, dv)
    nblk = L // DIL_SPAN
    nc, nv = W // C, W // V
    view = qkv.reshape(L, r * W)

    def body(q_ref, kp_ref, kc_ref, vp_ref, vc_ref, num_ref, m_ref, den_ref):
        has_prev = pl.program_id(1) > 0
        lane = lax.broadcasted_iota(jnp.int32, (DIL_SPAN, HEAD_DIM), 1)
        m_tile = jnp.zeros((DIL_SPAN, HEAD_DIM), F32)
        den_tile = jnp.ones((DIL_SPAN, HEAD_DIM), F32)
        for h in range(nh):
            sl = slice(h * HEAD_DIM, (h + 1) * HEAD_DIM)
            vs = slice(h * dv, (h + 1) * dv)
            sr = float(slopes[h]) * r
            sc = _window_logits(q_ref[:, sl], kc_ref[:, sl], sr, False, has_prev)
            sp = _window_logits(q_ref[:, sl], kp_ref[:, sl], sr, True, has_prev)
            m = jnp.maximum(jnp.max(sc, axis=1, keepdims=True), jnp.max(sp, axis=1, keepdims=True))
            pc, pp = jnp.exp(sc - m), jnp.exp(sp - m)
            den = jnp.sum(pc, axis=1, keepdims=True) + jnp.sum(pp, axis=1, keepdims=True)
            num_ref[:, vs] = _dot(pc.astype(BF16), vc_ref[:, vs], NN) + _dot(pp.astype(BF16), vp_ref[:, vs], NN)
            m_tile = jnp.where(lane == h, m, m_tile)
            den_tile = jnp.where(lane == h, den, den_tile)
        m_ref[...] = m_tile
        den_ref[...] = den_tile

    prev = lambda i: jnp.maximum(i - 1, 0)
    stat = pl.BlockSpec((DIL_SPAN, HEAD_DIM), lambda b, i: (i, b))
    num, m, den = pl.pallas_call(
        body, name=name, grid=(r, nblk),
        in_specs=[pl.BlockSpec((DIL_SPAN, C), lambda b, i: (i, b * nc + g)),
                  pl.BlockSpec((DIL_SPAN, C), lambda b, i: (prev(i), b * nc + G + g)),
                  pl.BlockSpec((DIL_SPAN, C), lambda b, i: (i, b * nc + G + g)),
                  pl.BlockSpec((DIL_SPAN, V), lambda b, i: (prev(i), b * nv + nv - 1)),
                  pl.BlockSpec((DIL_SPAN, V), lambda b, i: (i, b * nv + nv - 1))],
        out_specs=[pl.BlockSpec((DIL_SPAN, V), lambda b, i: (i, b)), stat, stat],
        out_shape=[jax.ShapeDtypeStruct((L, r * V), F32), jax.ShapeDtypeStruct((L, r * HEAD_DIM), F32),
                   jax.ShapeDtypeStruct((L, r * HEAD_DIM), F32)],
        compiler_params=_params("parallel", "parallel"),
    )(view, view, view, view, view)
    return num.reshape(T, V), m.reshape(T, HEAD_DIM), den.reshape(T, HEAD_DIM)


def _dil_merge(nums, ms, dens, nh, dv, name, tb=256):
    T, V = nums[0].shape
    tb = min(tb, T)
    G = len(nums)

    def body(*refs):
        num_r, m_r, den_r = refs[:G], refs[G:2 * G], refs[2 * G:3 * G]
        o_ref, lse_ref = refs[3 * G], refs[3 * G + 1]
        mm = m_r[0][...]
        for g in range(1, G):
            mm = jnp.maximum(mm, m_r[g][...])
        w = [jnp.exp(m_r[g][...] - mm) for g in range(G)]
        den = w[0] * den_r[0][...]
        for g in range(1, G):
            den = den + w[g] * den_r[g][...]
        lse_ref[...] = mm + jnp.log(den)
        for h in range(nh):
            vs = slice(h * dv, (h + 1) * dv)
            num = w[0][:, h:h + 1] * num_r[0][:, vs]
            for g in range(1, G):
                num = num + w[g][:, h:h + 1] * num_r[g][:, vs]
            o_ref[:, vs] = (num / den[:, h:h + 1]).astype(BF16)

    wide = pl.BlockSpec((tb, V), lambda i: (i, 0))
    stat = pl.BlockSpec((tb, HEAD_DIM), lambda i: (i, 0))
    return pl.pallas_call(
        body, name=name, grid=(T // tb,),
        in_specs=[wide] * G + [stat] * (2 * G),
        out_specs=[wide, stat],
        out_shape=[jax.ShapeDtypeStruct((T, V), BF16), jax.ShapeDtypeStruct((T, HEAD_DIM), F32)],
        compiler_params=_params("parallel"),
    )(*nums, *ms, *dens)


def _dil_dq(qkv, do, lse, dd, g, r, G, nh, dv, slopes, name):
    T = qkv.shape[0]
    L, C, V, W = _dil_views(T, r, G, nh, dv)
    nblk = L // DIL_SPAN
    nc, nv = W // C, W // V
    view = qkv.reshape(L, r * W)
    scale = HEAD_DIM ** -0.5

    def body(q_ref, kp_ref, kc_ref, vp_ref, vc_ref, do_ref, lse_ref, dd_ref, dq_ref):
        has_prev = pl.program_id(1) > 0
        for h in range(nh):
            sl = slice(h * HEAD_DIM, (h + 1) * HEAD_DIM)
            vs = slice(h * dv, (h + 1) * dv)
            sr = float(slopes[h]) * r
            lse_h, dd_h = lse_ref[:, h:h + 1], dd_ref[:, h:h + 1]
            acc = jnp.zeros((DIL_SPAN, HEAD_DIM), F32)
            for k_ref, v_ref, is_prev in ((kc_ref, vc_ref, False), (kp_ref, vp_ref, True)):
                s = _window_logits(q_ref[:, sl], k_ref[:, sl], sr, is_prev, has_prev)
                p = jnp.exp(s - lse_h)
                dp = _dot(do_ref[:, vs], v_ref[:, vs], NT)
                ds = (p * (dp - dd_h)).astype(BF16)
                acc = acc + _dot(ds, k_ref[:, sl], NN)
            dq_ref[:, sl] = scale * acc

    prev = lambda i: jnp.maximum(i - 1, 0)
    stat = pl.BlockSpec((DIL_SPAN, HEAD_DIM), lambda b, i: (i, b))
    dq = pl.pallas_call(
        body, name=name, grid=(r, nblk),
        in_specs=[pl.BlockSpec((DIL_SPAN, C), lambda b, i: (i, b * nc + g)),
                  pl.BlockSpec((DIL_SPAN, C), lambda b, i: (prev(i), b * nc + G + g)),
                  pl.BlockSpec((DIL_SPAN, C), lambda b, i: (i, b * nc + G + g)),
                  pl.BlockSpec((DIL_SPAN, V), lambda b, i: (prev(i), b * nv + nv - 1)),
                  pl.BlockSpec((DIL_SPAN, V), lambda b, i: (i, b * nv + nv - 1)),
                  pl.BlockSpec((DIL_SPAN, V), lambda b, i: (i, b)), stat, stat],
        out_specs=pl.BlockSpec((DIL_SPAN, C), lambda b, i: (i, b)),
        out_shape=jax.ShapeDtypeStruct((L, r * C), F32),
        compiler_params=_params("parallel", "parallel"),
    )(view, view, view, view, view, do.reshape(L, r * V), lse.reshape(L, r * HEAD_DIM), dd.reshape(L, r * HEAD_DIM))
    return dq.reshape(T, C)


def _dil_dkv(qkv, do, lse, dd, g, r, G, nh, dv, slopes, name):
    T = qkv.shape[0]
    L, C, V, W = _dil_views(T, r, G, nh, dv)
    nblk = L // DIL_SPAN
    nc, nv = W // C, W // V
    view = qkv.reshape(L, r * W)
    scale = HEAD_DIM ** -0.5

    def body(k_ref, v_ref, qc_ref, qn_ref, doc_ref, don_ref, lsec_ref, lsen_ref, ddc_ref, ddn_ref, dk_ref, dv_ref):
        has_next = pl.program_id(1) < nblk - 1
        for h in range(nh):
            sl = slice(h * HEAD_DIM, (h + 1) * HEAD_DIM)
            vs = slice(h * dv, (h + 1) * dv)
            sr = float(slopes[h]) * r
            dk = jnp.zeros((DIL_SPAN, HEAD_DIM), F32)
            dvh = jnp.zeros((DIL_SPAN, dv), F32)
            for q_ref, do_ref, lse_ref, dd_ref, is_next in ((qc_ref, doc_ref, lsec_ref, ddc_ref, False),
                                                          (qn_ref, don_ref, lsen_ref, ddn_ref, True)):
                s = _window_logits(q_ref[:, sl], k_ref[:, sl], sr, is_next, has_next)
                p = jnp.exp(s - lse_ref[:, h:h + 1])
                dvh = dvh + _dot(p.astype(BF16), do_ref[:, vs], TN)
                dp = _dot(do_ref[:, vs], v_ref[:, vs], NT)
                ds = (p * (dp - dd_ref[:, h:h + 1])).astype(BF16)
                dk = dk + _dot(ds, q_ref[:, sl], TN)
            dk_ref[:, sl] = scale * dk
            dv_ref[:, vs] = dvh

    nxt = lambda i: jnp.minimum(i + 1, nblk - 1)
    stat_c = pl.BlockSpec((DIL_SPAN, HEAD_DIM), lambda b, i: (i, b))
    stat_n = pl.BlockSpec((DIL_SPAN, HEAD_DIM), lambda b, i: (nxt(i), b))
    do_v, lse_v, dd_v = do.reshape(L, r * V), lse.reshape(L, r * HEAD_DIM), dd.reshape(L, r * HEAD_DIM)
    dk, dvv = pl.pallas_call(
        body, name=name, grid=(r, nblk),
        in_specs=[pl.BlockSpec((DIL_SPAN, C), lambda b, i: (i, b * nc + G + g)),
                  pl.BlockSpec((DIL_SPAN, V), lambda b, i: (i, b * nv + nv - 1)),
                  pl.BlockSpec((DIL_SPAN, C), lambda b, i: (i, b * nc + g)),
                  pl.BlockSpec((DIL_SPAN, C), lambda b, i: (nxt(i), b * nc + g)),
                  pl.BlockSpec((DIL_SPAN, V), lambda b, i: (i, b)),
                  pl.BlockSpec((DIL_SPAN, V), lambda b, i: (nxt(i), b)),
                  stat_c, stat_n, stat_c, stat_n],
        out_specs=[pl.BlockSpec((DIL_SPAN, C), lambda b, i: (i, b)), pl.BlockSpec((DIL_SPAN, V), lambda b, i: (i, b))],
        out_shape=[jax.ShapeDtypeStruct((L, r * C), F32), jax.ShapeDtypeStruct((L, r * V), F32)],
        compiler_params=_params("parallel", "parallel"),
    )(view, view, view, view, do_v, do_v, lse_v, lse_v, dd_v, dd_v)
    return dk.reshape(T, C), dvv.reshape(T, V)


def _relu2(r):
    a = jnp.maximum(r, 0.0)
    return (a * a,)


def _mlp_fwd(x, g, w_up, w_down, tag, hosts=None):
    T, D = x.shape
    F = w_down.shape[0]
    h = _rms_fwd(x, g, f"{tag}_norm")
    a2 = _mm(h, w_up, T, F, D, mode="nn", name=f"{tag}_up", b_stack=N_CHIPS, out_dtypes=(BF16,), epilogue=_relu2,
             hosts=hosts)
    y = _mm(a2, w_down, T, D, F, mode="nn", name=f"{tag}_down", out_dtypes=(F32,), extras=(x,),
            epilogue=lambda r, res: (res + r,), hosts=hosts)
    return y, (x, h, a2)


def _mlp_bwd(dy, dyb, saved, g, w_up, w_down, tag, hosts=None):
    x, h, a2 = saved
    T, D = x.shape
    F = w_down.shape[0]
    d_down = _mm(a2, dyb, F, D, T, mode="tn", name=f"{tag}_dwdown", out_dtypes=(F32,), hosts=hosts)
    du = _mm(dyb, w_down, T, F, D, mode="nt", name=f"{tag}_da", out_dtypes=(BF16,), extras=(a2,),
             epilogue=lambda r, sq: (r * (2.0 * jnp.sqrt(sq.astype(F32))),))
    d_up = _mm(h, du, D, F, T, mode="tn", name=f"{tag}_dwup", out_stack=N_CHIPS, out_dtypes=(F32,))
    dh = _mm(du, w_up, T, D, F, mode="nt", name=f"{tag}_dh", b_stack=N_CHIPS, out_dtypes=(F32,))
    dx, dxb, dg = _rms_bwd(dh, x, g, dy, f"{tag}_dnorm")
    return dx, dxb, dg, d_up, d_down


def _fox_dims(D):
    H = D // HEAD_DIM
    return H, max(8, H), (H // 2) * HEAD_DIM


def _fox_layer_fwd(x, g, w_qkv, w_f, b_pad, gains, w_out, hosts=None):
    T, D = x.shape
    H, hp, ch = _fox_dims(D)
    h = _rms_fwd(x, g, "fox_norm")
    proj = _mm(h, w_qkv, T, 3 * D, D, mode="nn", name="fox_proj", out_dtypes=(F32,))
    f_raw = _mm(h, w_f, T, HEAD_DIM, D, mode="nn", name="fox_gate_proj", out_dtypes=(F32,))
    qkv = _qkv_prep(proj, gains, 4, lambda j: jnp.minimum(j // 2, 1), ch, "fox_qk_norm", n_scaled=2, post_scale=FOX_Q_SCALE)
    ck = _gate_fwd(f_raw, b_pad, hp, LOG2E, "fox_gate").reshape(hp, 1, T)
    o, lse = _fox_fwd(qkv, ck, H, "fox_attn", hosts=hosts)
    y = _mm(o, w_out, T, D, D, mode="nn", name="fox_out", out_dtypes=(F32,), extras=(x,),
            epilogue=lambda r, res: (res + r,), hosts=hosts)
    return y, (x, h, proj, f_raw, qkv, ck, o, lse)


def _fox_layer_bwd(dy, dyb, saved, g, w_qkv, w_f, b_pad, gains, w_out, hosts=None):
    x, h, proj, f_raw, qkv, ck, o, lse = saved
    T, D = x.shape
    H, hp, ch = _fox_dims(D)
    d_out = _mm(o, dyb, D, D, T, mode="tn", name="fox_dwout", out_dtypes=(F32,))
    do = _mm(dyb, w_out, T, D, D, mode="nt", name="fox_do", out_dtypes=(BF16,))
    dd = _row_dot(do, o, H, HEAD_DIM, "fox_rowdot", False)
    dq, dk, dv, dck, dcq = _fox_bwd(qkv, do, ck, lse, dd, H, hp, "fox_attn_bwd", hosts=hosts)
    dcq = _lane_per_head(dcq, H, "fox_dc_query")
    dproj, dgq = _head_rms_bwd_into(None, 3 * D, dq, proj, gains[0], 0, ch, "fox_dq_norm", in_scale=HEAD_DIM ** -0.5)
    dproj, dgk = _head_rms_bwd_into(dproj, 3 * D, dk, proj, gains[1], 2, ch, "fox_dk_norm", in_scale=LN2)
    dproj = _sum_cast_into(dproj, 3 * D, [dv], 4, ch, "fox_dv_cast")
    dz, db = _gate_bwd(dck.reshape(hp, T), dcq, f_raw, b_pad, H, hp, "fox_gate_bwd")
    d_qkv = _mm(h, dproj, D, 3 * D, T, mode="tn", name="fox_dwqkv", out_dtypes=(F32,))
    d_f = _mm(h, dz, D, HEAD_DIM, T, mode="tn", name="fox_dwgate", out_dtypes=(F32,))
    dh = _mm(dproj, w_qkv, T, D, 3 * D, mode="nt", name="fox_dh", out_dtypes=(F32,))
    dh = _mm(dz, w_f, T, D, HEAD_DIM, mode="nt", name="fox_dh_gate", out_dtypes=(F32,), extras=(dh,),
             epilogue=lambda r, e: (e + r,))
    dx, dxb, dg = _rms_bwd(dh, x, g, dy, "fox_dnorm")
    dgains = jnp.stack([dgq.sum(axis=0), dgk.sum(axis=0)])
    return dx, dxb, dg, d_qkv, d_f, db, dgains, d_out


def _dil_dims(D):
    nh = D // (2 * HEAD_DIM)
    return nh, D // nh, len(DIL_PATTERNS)


def _dil_layer_fwd(x, g, w_in, gains, w_out, hosts=None):
    T, D = x.shape
    nh, dv, G = _dil_dims(D)
    C = nh * HEAD_DIM
    W = 2 * G * C + nh * dv
    slopes = _slopes(G, nh)
    h = _rms_fwd(x, g, "dil_norm")
    proj = _mm(h, w_in, T, W, D, mode="nn", name="dil_proj", b_stack=N_CHIPS, out_dtypes=(F32,), hosts=hosts)
    qkv = _qkv_prep(proj, gains, 2 * G, lambda j: jnp.minimum(j, 2 * G - 1), C, "dil_qk_norm")
    parts = [_dil_fwd(qkv, gi, r, G, nh, dv, slopes[gi], f"dil_attn_g{gi}") for gi, (_, r) in enumerate(DIL_PATTERNS)]
    o, lse = _dil_merge([p[0] for p in parts], [p[1] for p in parts], [p[2] for p in parts], nh, dv, "dil_merge")
    y = _mm(o, w_out, T, D, D, mode="nn", name="dil_out", out_dtypes=(F32,), extras=(x,),
            epilogue=lambda r, res: (res + r,))
    return y, (x, h, proj, qkv, o, lse)


def _dil_layer_bwd(dy, dyb, saved, g, w_in, gains, w_out):
    x, h, proj, qkv, o, lse = saved
    T, D = x.shape
    nh, dv, G = _dil_dims(D)
    C = nh * HEAD_DIM
    W = 2 * G * C + nh * dv
    slopes = _slopes(G, nh)
    d_out = _mm(o, dyb, D, D, T, mode="tn", name="dil_dwout", out_dtypes=(F32,))
    do = _mm(dyb, w_out, T, D, D, mode="nt", name="dil_do", out_dtypes=(BF16,))
    dd = _row_dot(do, o, nh, dv, "dil_rowdot", True)
    dproj, dgs, dvs = None, [None] * (2 * G), []
    for gi, (_, r) in enumerate(DIL_PATTERNS):
        dq = _dil_dq(qkv, do, lse, dd, gi, r, G, nh, dv, slopes[gi], f"dil_dq_g{gi}")
        dk, dvg = _dil_dkv(qkv, do, lse, dd, gi, r, G, nh, dv, slopes[gi], f"dil_dkv_g{gi}")
        dvs.append(dvg)
        dproj, dgs[gi] = _head_rms_bwd_into(dproj, W, dq, proj, gains[gi], gi, C, f"dil_dq_norm_g{gi}")
        dproj, dgs[G + gi] = _head_rms_bwd_into(dproj, W, dk, proj, gains[G + gi], G + gi, C, f"dil_dk_norm_g{gi}")
    dproj = _sum_cast_into(dproj, W, dvs, 2 * G, C, "dil_dv_cast")
    d_in = _mm(h, dproj, D, W, T, mode="tn", name="dil_dwin", out_stack=N_CHIPS, out_dtypes=(F32,))
    dh = _mm(dproj, w_in, T, D, W, mode="nt", name="dil_dh", b_stack=N_CHIPS, out_dtypes=(F32,))
    dx, dxb, dg = _rms_bwd(dh, x, g, dy, "dil_dnorm")
    dgains = jnp.concatenate(dgs, axis=0)
    return dx, dxb, dg, d_in, dgains, d_out


class _Hosts:
    def __init__(self, riders, weights):
        self.riders, self.weights, self.carried, self.grads = dict(riders), weights, {}, {}

    def rider(self, name):
        make = self.riders.get(name)
        return make(self) if make is not None else None


def _local_step(x, tgt, w, hosts=None):
    g = hosts.grads if hosts is not None else {}
    y0, s_fox = _fox_layer_fwd(x, w["mix_g"][0], w["fox_qkv"], w["fox_f"], w["fox_b"], w["fox_gains"], w["fox_out"], hosts)
    if hosts is not None:
        w = {**w, **hosts.weights(hosts)}
    y1, s_mlp0 = _mlp_fwd(y0, w["mlp_g"][0], w["up0"], w["down0"], "mlp0", hosts)
    y2, s_dil = _dil_layer_fwd(y1, w["mix_g"][1], w["dil_in"], w["dil_gains"], w["dil_out"], hosts)
    if hosts is not None:
        w = {**w, **hosts.weights(hosts)}
    y3, s_mlp1 = _mlp_fwd(y2, w["mlp_g"][1], w["up1"], w["down1"], "mlp1")
    dy, dyb, loss = _loss_head(y3, tgt, "loss_head")
    dy, dyb, g_mlp1, g["up1"], g["down1"] = _mlp_bwd(dy, dyb, s_mlp1, w["mlp_g"][1], w["up1"], w["down1"], "mlp1")
    dy, dyb, g_mix1, g["dil_in"], g["dil_gains"], g["dil_out"] = _dil_layer_bwd(
        dy, dyb, s_dil, w["mix_g"][1], w["dil_in"], w["dil_gains"], w["dil_out"])
    dy, dyb, g_mlp0, g["up0"], g["down0"] = _mlp_bwd(dy, dyb, s_mlp0, w["mlp_g"][0], w["up0"], w["down0"], "mlp0", hosts)
    dy, dyb, g_mix0, g["fox_qkv"], g["fox_f"], g["fox_b"], g["fox_gains"], g["fox_out"] = _fox_layer_bwd(
        dy, dyb, s_fox, w["mix_g"][0], w["fox_qkv"], w["fox_f"], w["fox_b"], w["fox_gains"], w["fox_out"], hosts)
    g["mix_g"], g["mlp_g"] = (g_mix0, g_mix1), (g_mlp0, g_mlp1)
    return loss[0, 0], dy, g


def _place():
    x, y, c = lax.axis_index("x"), lax.axis_index("y"), lax.axis_index("c")
    chips = [(1 - x, y), (x, 1 - y), (1 - x, 1 - y)]
    return x, y, c, chips


def _remote(src, dst, send_sem, recv_sem, to):
    return pltpu.make_async_remote_copy(src_ref=src, dst_ref=dst, send_sem=send_sem, recv_sem=recv_sem,
                                        device_id=to, device_id_type=MESH)


def _gather_weights(shards):
    n = len(shards)

    def body(*refs):
        src, dst = refs[:n], refs[n:2 * n]
        send_sems, recv_sems, local_sems = refs[2 * n:]
        x, y, c, chips = _place()
        mine = 2 * x + y
        local = [pltpu.make_async_copy(src[t], dst[t].at[mine], local_sems.at[t]) for t in range(n)]
        for cp in local:
            cp.start()

        def half(t, slot, which):
            hr = shards[t].shape[0] // 2
            return dst[t].at[slot, pl.ds(which * hr, hr), :]

        def my_half(t):
            hr = shards[t].shape[0] // 2
            return src[t].at[pl.ds(c * hr, hr), :]

        sends = []
        for t in range(n):
            for j, (px, py) in enumerate(chips):
                cp = _remote(my_half(t), half(t, mine, c), send_sems.at[t, j], recv_sems.at[t, j], (px, py, c))
                cp.start()
                sends.append(cp)
        for j, (px, py) in enumerate(chips):
            for t in range(n):
                landed = half(t, 2 * px + py, c)
                _remote(landed, landed, send_sems.at[t, j], recv_sems.at[t, j], (px, py, c)).wait_recv()
                cp = _remote(landed, landed, send_sems.at[t, 3 + j], recv_sems.at[t, 3 + j], (x, y, 1 - c))
                cp.start()
                sends.append(cp)
        for j, (px, py) in enumerate(chips):
            for t in range(n):
                other = half(t, 2 * px + py, 1 - c)
                _remote(other, other, send_sems.at[t, 3 + j], recv_sems.at[t, 3 + j], (x, y, 1 - c)).wait_recv()
        for cp in sends:
            cp.wait_send()
        for cp in local:
            cp.wait()

    return pl.pallas_call(
        body, name="gather_weights",
        in_specs=[ANY] * n, out_specs=[ANY] * n,
        out_shape=[jax.ShapeDtypeStruct((N_CHIPS,) + s.shape, s.dtype) for s in shards],
        scratch_shapes=[pltpu.SemaphoreType.DMA((n, 6)), pltpu.SemaphoreType.DMA((n, 6)), pltpu.SemaphoreType.DMA((n,))],
        compiler_params=_params(has_side_effects=True),
    )(*shards)


class _Rider(NamedTuple):
    operands: tuple
    out_shapes: tuple
    scratch: tuple
    start: Callable
    finish: Callable
    aliases: dict = {}


def _gather_ici_rider(shards):
    n = len(shards)

    def copies(src, dst, sems):
        send_sems, recv_sems, local_sems = sems
        x, y, c, chips = _place()
        mine = 2 * x + y
        local, sends, recvs = [], [], []
        for t in range(n):
            hr = shards[t].shape[0] // 2
            local.append(pltpu.make_async_copy(src[t], dst[t].at[mine], local_sems.at[t]))
            for j, (px, py) in enumerate(chips):
                sends.append(_remote(src[t].at[pl.ds(c * hr, hr), :], dst[t].at[mine, pl.ds(c * hr, hr), :],
                                     send_sems.at[t, j], recv_sems.at[t, j], (px, py, c)))
                landed = dst[t].at[2 * px + py, pl.ds(c * hr, hr), :]
                recvs.append(_remote(landed, landed, send_sems.at[t, j], recv_sems.at[t, j], (px, py, c)))
        return local, sends, recvs

    def start(src, dst, sems):
        local, sends, _ = copies(src, dst, sems)
        for cp in local + sends:
            cp.start()

    def finish(src, dst, sems):
        local, sends, recvs = copies(src, dst, sems)
        for cp in recvs:
            cp.wait_recv()
        for cp in sends:
            cp.wait_send()
        for cp in local:
            cp.wait()

    return _Rider(tuple(shards), tuple(jax.ShapeDtypeStruct((N_CHIPS,) + s.shape, s.dtype) for s in shards),
                  (pltpu.SemaphoreType.DMA((n, 3)), pltpu.SemaphoreType.DMA((n, 3)), pltpu.SemaphoreType.DMA((n,))),
                  start, finish)


def _forward_rider(landed):
    n = len(landed)

    def copies(dst, sems):
        send_sems, recv_sems = sems
        x, y, c, chips = _place()
        sends, recvs = [], []
        for t in range(n):
            hr = landed[t].shape[1] // 2
            for j, (px, py) in enumerate(chips):
                got = dst[t].at[2 * px + py, pl.ds(c * hr, hr), :]
                other = dst[t].at[2 * px + py, pl.ds((1 - c) * hr, hr), :]
                sends.append(_remote(got, got, send_sems.at[t, j], recv_sems.at[t, j], (x, y, 1 - c)))
                recvs.append(_remote(other, other, send_sems.at[t, j], recv_sems.at[t, j], (x, y, 1 - c)))
        return sends, recvs

    def start(src, dst, sems):
        for cp in copies(dst, sems)[0]:
            cp.start()

    def finish(src, dst, sems):
        sends, recvs = copies(dst, sems)
        for cp in recvs:
            cp.wait_recv()
        for cp in sends:
            cp.wait_send()

    return _Rider(tuple(landed), tuple(jax.ShapeDtypeStruct(a.shape, a.dtype) for a in landed),
                  (pltpu.SemaphoreType.DMA((n, 3)), pltpu.SemaphoreType.DMA((n, 3))), start, finish,
                  {t: t for t in range(n)})


def _pair_exchange_rider(grads):
    n = len(grads)

    def copies(src, dst, sems):
        send_sems, recv_sems = sems
        x, y, c, _ = _place()
        return [_remote(src[t].at[:, pl.ds((1 - c) * (grads[t].shape[1] // 2), grads[t].shape[1] // 2), :], dst[t],
                        send_sems.at[t], recv_sems.at[t], (x, y, 1 - c)) for t in range(n)]

    def start(src, dst, sems):
        for cp in copies(src, dst, sems):
            cp.start()

    def finish(src, dst, sems):
        for cp in copies(src, dst, sems):
            cp.wait()

    return _Rider(tuple(grads),
                  tuple(jax.ShapeDtypeStruct((g.shape[0], g.shape[1] // 2, g.shape[2]), g.dtype) for g in grads),
                  (pltpu.SemaphoreType.DMA((n,)), pltpu.SemaphoreType.DMA((n,))), start, finish)


def _pair_add(g, got, cidx, name, tb=256):
    S, R, C = g.shape
    hr = R // 2
    tb = _rows_tile(hr, tb)
    nb = hr // tb

    def body(c_ref, a_ref, b_ref, o_ref):
        o_ref[...] = (a_ref[...] + b_ref[...]).astype(BF16)

    return pl.pallas_call(
        body, name=name,
        grid_spec=pltpu.PrefetchScalarGridSpec(
            num_scalar_prefetch=1, grid=(S, nb),
            in_specs=[pl.BlockSpec((None, tb, C), lambda s, i, c: (s, c[0] * nb + i, 0)),
                      pl.BlockSpec((None, tb, C), lambda s, i, c: (s, i, 0))],
            out_specs=pl.BlockSpec((None, tb, C), lambda s, i, c: (s, i, 0))),
        out_shape=jax.ShapeDtypeStruct((S, hr, C), BF16),
        compiler_params=_params("parallel", "parallel"),
    )(cidx, g, got)


def _rows_tile(n, want):
    t = min(n, want)
    while n % t or t % 8:
        t -= 8
    return t


def _chip_scatter_rider(sums):
    n = len(sums)

    def copies(src, dst, sems):
        send_sems, recv_sems, local_sems = sems
        x, y, c, chips = _place()
        mine = 2 * x + y
        local, sends, recvs = [], [], []
        for t in range(n):
            local.append(pltpu.make_async_copy(src[t].at[mine], dst[t].at[mine], local_sems.at[t]))
            for j, (px, py) in enumerate(chips):
                sends.append(_remote(src[t].at[2 * px + py], dst[t].at[mine], send_sems.at[t, j], recv_sems.at[t, j], (px, py, c)))
                slot = dst[t].at[2 * px + py]
                recvs.append(_remote(slot, slot, send_sems.at[t, j], recv_sems.at[t, j], (px, py, c)))
        return local, sends, recvs

    def start(src, dst, sems):
        local, sends, _ = copies(src, dst, sems)
        for cp in local + sends:
            cp.start()

    def finish(src, dst, sems):
        local, sends, recvs = copies(src, dst, sems)
        for cp in recvs:
            cp.wait_recv()
        for cp in sends:
            cp.wait_send()
        for cp in local:
            cp.wait()

    return _Rider(tuple(sums), tuple(jax.ShapeDtypeStruct(s.shape, s.dtype) for s in sums),
                  (pltpu.SemaphoreType.DMA((n, 3)), pltpu.SemaphoreType.DMA((n, 3)), pltpu.SemaphoreType.DMA((n,))),
                  start, finish)


def _run_rider(rider, name):
    r_in, r_out = len(rider.operands), len(rider.out_shapes)

    def body(*refs):
        src, dst, sems = refs[:r_in], refs[r_in:r_in + r_out], refs[r_in + r_out:]
        rider.start(src, dst, sems)
        rider.finish(src, dst, sems)

    return pl.pallas_call(
        body, name=name,
        in_specs=[ANY] * r_in, out_specs=[ANY] * r_out,
        out_shape=list(rider.out_shapes), scratch_shapes=list(rider.scratch),
        input_output_aliases=dict(rider.aliases),
        compiler_params=_params(has_side_effects=True),
    )(*rider.operands)


def _chip_sum(parts, cidx, name, tb=256):
    S, hr, C = parts.shape
    tb = _rows_tile(hr, tb)
    nb = hr // tb

    def body(c_ref, *refs):
        o_ref = refs[S]
        tot = refs[0][...].astype(F32)
        for s in range(1, S):
            tot = tot + refs[s][...].astype(F32)
        o_ref[...] = tot

    return pl.pallas_call(
        body, name=name,
        grid_spec=pltpu.PrefetchScalarGridSpec(
            num_scalar_prefetch=1, grid=(nb,),
            in_specs=[pl.BlockSpec((None, tb, C), functools.partial(lambda s, i, c: (s, i, 0), s)) for s in range(S)],
            out_specs=pl.BlockSpec((tb, C), lambda i, c: (c[0] * nb + i, 0))),
        out_shape=jax.ShapeDtypeStruct((2 * hr, C), F32),
        compiler_params=_params("parallel"),
    )(cidx, *([parts] * S))


def _half_exchange(halves):
    n = len(halves)

    def body(*refs):
        dst = refs[n:2 * n]
        send_sems, recv_sems = refs[2 * n:]
        x, y, c, _ = _place()
        cps = []
        for t in range(n):
            hr = halves[t].shape[0] // 2
            rows = dst[t].at[pl.ds(c * hr, hr), :]
            cp = _remote(rows, rows, send_sems.at[t], recv_sems.at[t], (x, y, 1 - c))
            cp.start()
            cps.append(cp)
        for t, cp in enumerate(cps):
            cp.wait_send()
            hr = halves[t].shape[0] // 2
            other = dst[t].at[pl.ds((1 - c) * hr, hr), :]
            _remote(other, other, send_sems.at[t], recv_sems.at[t], (x, y, 1 - c)).wait_recv()

    return pl.pallas_call(
        body, name="grad_half_exchange",
        in_specs=[ANY] * n, out_specs=[ANY] * n,
        out_shape=[jax.ShapeDtypeStruct(h.shape, h.dtype) for h in halves],
        input_output_aliases={t: t for t in range(n)},
        scratch_shapes=[pltpu.SemaphoreType.DMA((n,)), pltpu.SemaphoreType.DMA((n,))],
        compiler_params=_params(has_side_effects=True),
    )(*halves)


def _adamw_math(w, g, m, v):
    m = ADAM_B1 * m + (1.0 - ADAM_B1) * g
    v = ADAM_B2 * v + (1.0 - ADAM_B2) * (g * g)
    m_hat = m / (1.0 - ADAM_B1 ** ADAM_STEP)
    v_hat = v / (1.0 - ADAM_B2 ** ADAM_STEP)
    delta = -ADAM_LR * (m_hat / (jnp.sqrt(v_hat) + ADAM_EPS) + ADAM_WD * w)
    return delta, m, v


def _adamw(w, gs, m, v, name, tb=256):
    L, R, C = w.shape
    tb = _rows_tile(R, tb)
    nb = R // tb

    def body(w_ref, m_ref, v_ref, *rest):
        g_refs, (go_ref, d_ref, mo_ref, vo_ref) = rest[:L], rest[L:]
        layer = pl.program_id(0)
        for k in range(L):
            @pl.when(layer == k)
            def _(k=k):
                g = g_refs[k][...]
                d, mn, vn = _adamw_math(w_ref[...], g, m_ref[...], v_ref[...])
                go_ref[...] = g
                d_ref[...] = d
                mo_ref[...] = mn
                vo_ref[...] = vn

    def g_spec(k):
        return pl.BlockSpec((tb, C), lambda l, i: (jnp.where(l == k, i, jnp.where(l < k, 0, nb - 1)), 0))

    stacked = pl.BlockSpec((None, tb, C), lambda l, i: (l, i, 0))
    return pl.pallas_call(
        body, name=name, grid=(L, nb),
        in_specs=[stacked] * 3 + [g_spec(k) for k in range(L)], out_specs=[stacked] * 4,
        out_shape=[jax.ShapeDtypeStruct((L, R, C), F32)] * 4,
        compiler_params=_params("arbitrary", "arbitrary"),
    )(w, m, v, *gs)


N_DEV = 8


def _small_update(g, w, m, v):
    P = g.shape[0]

    def body(g_ref, w_ref, m_ref, v_ref, go_ref, d_ref, mo_ref, vo_ref, buf, send_sems, recv_sems):
        x, y, c, _ = _place()
        me = 4 * x + 2 * y + c
        buf[me] = g_ref[...]
        cps = []
        for k in range(1, N_DEV):
            fx, fy, fc = (k >> 2) & 1, (k >> 1) & 1, k & 1
            px = (1 - x) if fx else x
            py = (1 - y) if fy else y
            pc = (1 - c) if fc else c
            cp = _remote(g_ref, buf.at[me], send_sems.at[k - 1], recv_sems.at[k - 1], (px, py, pc))
            cp.start()
            cps.append((cp, 4 * px + 2 * py + pc))
        for k, (cp, peer) in enumerate(cps):
            _remote(g_ref, buf.at[peer], send_sems.at[k], recv_sems.at[k], (x, y, c)).wait_recv()
        for cp, _ in cps:
            cp.wait_send()
        tot = buf[0]
        for d in range(1, N_DEV):
            tot = tot + buf[d]
        go_ref[...] = tot
        dl, mn, vn = _adamw_math(w_ref[...], tot, m_ref[...], v_ref[...])
        d_ref[...] = dl
        mo_ref[...] = mn
        vo_ref[...] = vn

    vm = pl.BlockSpec(memory_space=pltpu.VMEM)
    return pl.pallas_call(
        body, name="small_params_update",
        in_specs=[vm] * 4, out_specs=[vm] * 4,
        out_shape=[jax.ShapeDtypeStruct((P, HEAD_DIM), F32)] * 4,
        scratch_shapes=[pltpu.VMEM((N_DEV, P, HEAD_DIM), F32), pltpu.SemaphoreType.DMA((N_DEV - 1,)),
                        pltpu.SemaphoreType.DMA((N_DEV - 1,))],
        compiler_params=_params(has_side_effects=True),
    )(g, w, m, v)


SMALL = ("fox_b_f", "fox_q_gain", "fox_k_gain", "dil_q_gain", "dil_k_gain", "mix_norm_g", "mlp_norm_g")
LARGE = ("fox_w_in", "fox_w_out", "dil_w_in", "dil_w_out", "mlp_w_up", "mlp_w_down")
WEIGHTS = ("fox_w_in", "fox_b_f", "fox_q_gain", "fox_k_gain", "fox_w_out", "dil_w_in", "dil_q_gain", "dil_k_gain",
           "dil_w_out", "mix_norm_g", "mlp_norm_g", "mlp_w_up", "mlp_w_down")


def _pack(parts):
    rows = []
    for a in parts:
        flat = a.reshape(-1)
        n = -(-flat.shape[0] // (8 * HEAD_DIM)) * (8 * HEAD_DIM)
        rows.append(jnp.pad(flat, (0, n - flat.shape[0])).reshape(-1, HEAD_DIM))
    return jnp.concatenate(rows, axis=0)


def _unpack(packed, like):
    out, r = [], 0
    for a in like:
        size = int(np.prod(a.shape))
        n = -(-size // (8 * HEAD_DIM)) * 8
        out.append(packed[r:r + n].reshape(-1)[:size].reshape(a.shape))
        r += n
    return out


def _pad_lanes(a):
    return jnp.pad(a, [(0, 0)] * (a.ndim - 1) + [(0, HEAD_DIM - a.shape[-1])])


def _as_shards(a):
    return a.reshape(N_CHIPS, a.shape[0] // N_CHIPS, a.shape[1])


def kernel(x, fox_w_in, fox_b_f, fox_q_gain, fox_k_gain, fox_w_out, dil_w_in, dil_q_gain, dil_k_gain, dil_w_out, mix_norm_g, mlp_norm_g, mlp_w_up, mlp_w_down, loss_target, m_fox_w_in, m_fox_b_f, m_fox_q_gain, m_fox_k_gain, m_fox_w_out, m_dil_w_in, m_dil_q_gain, m_dil_k_gain, m_dil_w_out, m_mix_norm_g, m_mlp_norm_g, m_mlp_w_up, m_mlp_w_down, v_fox_w_in, v_fox_b_f, v_fox_q_gain, v_fox_k_gain, v_fox_w_out, v_dil_w_in, v_dil_q_gain, v_dil_k_gain, v_dil_w_out, v_mix_norm_g, v_mlp_norm_g, v_mlp_w_up, v_mlp_w_down):
    wts = dict(fox_w_in=fox_w_in, fox_b_f=fox_b_f, fox_q_gain=fox_q_gain, fox_k_gain=fox_k_gain, fox_w_out=fox_w_out,
               dil_w_in=dil_w_in, dil_q_gain=dil_q_gain, dil_k_gain=dil_k_gain, dil_w_out=dil_w_out,
               mix_norm_g=mix_norm_g, mlp_norm_g=mlp_norm_g, mlp_w_up=mlp_w_up, mlp_w_down=mlp_w_down)
    mom1 = dict(fox_w_in=m_fox_w_in, fox_b_f=m_fox_b_f, fox_q_gain=m_fox_q_gain, fox_k_gain=m_fox_k_gain,
                fox_w_out=m_fox_w_out, dil_w_in=m_dil_w_in, dil_q_gain=m_dil_q_gain, dil_k_gain=m_dil_k_gain,
                dil_w_out=m_dil_w_out, mix_norm_g=m_mix_norm_g, mlp_norm_g=m_mlp_norm_g, mlp_w_up=m_mlp_w_up,
                mlp_w_down=m_mlp_w_down)
    mom2 = dict(fox_w_in=v_fox_w_in, fox_b_f=v_fox_b_f, fox_q_gain=v_fox_q_gain, fox_k_gain=v_fox_k_gain,
                fox_w_out=v_fox_w_out, dil_w_in=v_dil_w_in, dil_q_gain=v_dil_q_gain, dil_k_gain=v_dil_k_gain,
                dil_w_out=v_dil_w_out, mix_norm_g=v_mix_norm_g, mlp_norm_g=v_mlp_norm_g, mlp_w_up=v_mlp_w_up,
                mlp_w_down=v_mlp_w_down)
    T, D = x.shape[1], x.shape[2]
    H = D // HEAD_DIM
    cidx = lax.axis_index("c").astype(jnp.int32).reshape(1)

    def shards_of(d):
        return [d["fox_w_in"][0], d["fox_w_out"][0], d["dil_w_in"][0], d["dil_w_out"][0],
                d["mlp_w_up"][0], d["mlp_w_up"][1], d["mlp_w_down"][0], d["mlp_w_down"][1]]

    w_bf = [s.astype(BF16) for s in shards_of(wts)]
    first = _gather_weights(w_bf[:2])
    fox_in = jnp.moveaxis(first[0], 0, 1).reshape(D, -1)
    w = dict(
        fox_qkv=fox_in[:, :3 * D], fox_f=_pad_lanes(fox_in[:, 3 * D:]), fox_b=_pad_lanes(fox_b_f),
        fox_gains=jnp.stack([fox_q_gain, fox_k_gain]), fox_out=first[1].reshape(D, D),
        dil_gains=jnp.concatenate([dil_q_gain[0], dil_k_gain[0]])[:, None, :],
        mix_g=[mix_norm_g[0:1], mix_norm_g[1:2]], mlp_g=[mlp_norm_g[0:1], mlp_norm_g[1:2]])

    dil_in_sh, dil_out_sh, up0_sh, up1_sh, down0_sh, down1_sh = w_bf[2:]

    def gathered(hosts):
        got = {}
        if "fox_out" in hosts.carried:
            full = hosts.carried["fox_out"]
            got.update(dil_in=full[0], dil_out=full[1].reshape(D, D), up0=full[2], down0=full[3].reshape(-1, D))
        if "dil_proj" in hosts.carried:
            full = hosts.carried["dil_proj"]
            got.update(up1=full[0], down1=full[1].reshape(-1, D))
        return got

    def chip_sums(stacked, got, ts):
        return [_pair_add(a, b, cidx, f"grad_pair_add_{t}") for a, b, t in zip(stacked, got, ts)]

    def second_layer_grads(hosts):
        g = hosts.grads
        return [g["dil_in"], _as_shards(g["dil_out"]), g["up1"], _as_shards(g["down1"])]

    def early_scatter(hosts):
        g = hosts.grads
        mlp0 = [g["up0"], _as_shards(g["down0"])]
        got0 = _run_rider(_pair_exchange_rider(mlp0), "grad_pair_exchange_mlp0")
        sums1 = chip_sums(second_layer_grads(hosts), hosts.carried["mlp0_dwdown"], (2, 3, 5, 7))
        sums0 = chip_sums(mlp0, got0, (4, 6))
        return _chip_scatter_rider([sums1[0], sums1[1], sums0[0], sums1[2], sums0[1], sums1[3]])

    hosts = _Hosts({
        "fox_attn": lambda h: _gather_ici_rider([dil_in_sh, dil_out_sh, up0_sh, down0_sh]),
        "fox_out": lambda h: _forward_rider(h.carried["fox_attn"]),
        "mlp0_up": lambda h: _gather_ici_rider([up1_sh]),
        "mlp0_down": lambda h: _gather_ici_rider([down1_sh]),
        "dil_proj": lambda h: _forward_rider(h.carried["mlp0_up"] + h.carried["mlp0_down"]),
        "mlp0_dwdown": lambda h: _pair_exchange_rider(second_layer_grads(h)),
        "fox_attn_bwd": early_scatter,
    }, gathered)
    loss, grad_x, g = _local_step(x.reshape(T, D), loss_target.reshape(T, D), w, hosts)
    loss = lax.psum(loss, ("x", "y", "c"))

    g_fox_in = jnp.concatenate([g["fox_qkv"], g["fox_f"][:, :H]], axis=1)
    g_fox_in = jnp.moveaxis(g_fox_in.reshape(D, N_CHIPS, -1), 1, 0)
    late = [g_fox_in, _as_shards(g["fox_out"])]
    late_sums = chip_sums(late, _run_rider(_pair_exchange_rider(late), "grad_pair_exchange_late"), (0, 1))
    parts = list(_run_rider(_chip_scatter_rider(late_sums), "grad_chip_scatter_late")) + hosts.carried["fox_attn_bwd"]
    halves = [_chip_sum(p, cidx, f"grad_chip_sum_{t}") for t, p in enumerate(parts)]
    totals = _half_exchange(halves)
    layers = dict(fox_w_in=[0], fox_w_out=[1], dil_w_in=[2], dil_w_out=[3], mlp_w_up=[4, 5], mlp_w_down=[6, 7])
    upd = {n: _adamw(wts[n], [totals[t] for t in ts], mom1[n], mom2[n], f"adamw_{n}") for n, ts in layers.items()}

    def large(k):
        return {n: upd[n][k] for n in LARGE}

    small_like = [wts[n] for n in SMALL]
    g_small = [g["fox_b"][:, :H], g["fox_gains"][0], g["fox_gains"][1], g["dil_gains"][:3, 0][None], g["dil_gains"][3:, 0][None],
               jnp.concatenate(g["mix_g"]), jnp.concatenate(g["mlp_g"])]
    packed = _small_update(_pack(g_small), _pack(small_like), _pack([mom1[n] for n in SMALL]), _pack([mom2[n] for n in SMALL]))
    small = [dict(zip(SMALL, _unpack(p, small_like))) for p in packed]

    outs = [loss, grad_x.reshape(x.shape)]
    for k in range(4):
        big = large(k)
        outs += [big[n] if n in big else small[k][n] for n in WEIGHTS]
    return tuple(outs)
```

```python
import functools
from typing import Callable, NamedTuple

import numpy as np
import jax
import jax.numpy as jnp
from jax import lax
from jax.experimental import pallas as pl
from jax.experimental.pallas import tpu as pltpu

F32 = jnp.float32
BF16 = jnp.bfloat16

HEAD_DIM = 128
DIL_PATTERNS = ((128, 1), (512, 4), (2048, 16))
DIL_SPAN = 128
ALIBI_MAX_EXP = 8.0
EPS = 1e-6
MASKED = -1e30

ADAM_LR = 0.001
ADAM_B1 = 0.9
ADAM_B2 = 0.999
ADAM_EPS = 1e-08
ADAM_WD = 0.01
ADAM_STEP = 10

N_CHIPS = 4
VMEM_LIMIT_BYTES = 56 * 1024 * 1024
MESH = pl.DeviceIdType.MESH
ANY = pl.BlockSpec(memory_space=pl.ANY)

NN = (((1,), (0,)), ((), ()))
NT = (((1,), (1,)), ((), ()))
TN = (((0,), (0,)), ((), ()))


def _params(*sem, **kw):
    return pltpu.CompilerParams(dimension_semantics=sem or None, vmem_limit_bytes=VMEM_LIMIT_BYTES, **kw)


def _dot(a, b, dims):
    return lax.dot_general(a, b, dims, preferred_element_type=F32)


def _tile(n, want):
    if n <= want:
        return n
    t = want - want % 128
    while n % t:
        t -= 128
    return t


def _mm(a, b, M, N, K, *, mode, name, out_dtypes, b_stack=0, out_stack=0, extras=(), epilogue=None,
        tm=1024, tn=1024, tk=2048, hosts=None):
    rider = hosts.rider(name) if hosts is not None else None
    per_b = per_o = None
    if b_stack:
        per_b = (K if mode == "nt" else N) // b_stack
    if out_stack:
        per_o = N // out_stack
    tm = _tile(M, tm)
    tn = _tile(min(x for x in (N, per_o, per_b if mode != "nt" else None) if x), tn)
    tk = _tile(min(x for x in (K, per_b if mode == "nt" else None) if x), tk)
    assert M % tm == 0 and N % tn == 0 and K % tk == 0, (name, M, N, K, tm, tn, tk)
    gk = K // tk
    if mode == "tn":
        a_spec = pl.BlockSpec((tk, tm), lambda i, j, k: (k, i))
    else:
        a_spec = pl.BlockSpec((tm, tk), lambda i, j, k: (i, k))
    if mode == "nt":
        if b_stack:
            npk = per_b // tk
            b_spec = pl.BlockSpec((None, tn, tk), lambda i, j, k: (k // npk, j, k % npk))
        else:
            b_spec = pl.BlockSpec((tn, tk), lambda i, j, k: (j, k))
    else:
        if b_stack:
            npj = per_b // tn
            b_spec = pl.BlockSpec((None, tk, tn), lambda i, j, k: (j // npj, k, j % npj))
        else:
            b_spec = pl.BlockSpec((tk, tn), lambda i, j, k: (k, j))
    if out_stack:
        npo = per_o // tn
        o_spec = pl.BlockSpec((None, tm, tn), lambda i, j, k: (j // npo, i, j % npo))
        o_shape = (out_stack, M, per_o)
    else:
        o_spec = pl.BlockSpec((tm, tn), lambda i, j, k: (i, j))
        o_shape = (M, N)
    e_spec = pl.BlockSpec((tm, tn), lambda i, j, k: (i, j))
    dims = {"nn": NN, "nt": NT, "tn": TN}[mode]
    ne, no = len(extras), len(out_dtypes)
    gi, gj = M // tm, N // tn
    r_in, r_out = (len(rider.operands), len(rider.out_shapes)) if rider else (0, 0)
    n_acc = 1 if gk > 1 else 0

    def body(a_ref, b_ref, *rest):
        ex, r_src = rest[:ne], rest[ne:ne + r_in]
        outs, r_dst = rest[ne + r_in:ne + r_in + no], rest[ne + r_in + no:ne + r_in + no + r_out]
        r_sems = rest[ne + r_in + no + r_out + n_acc:]
        i, j, k = pl.program_id(0), pl.program_id(1), pl.program_id(2)
        if rider:
            @pl.when((i == 0) & (j == 0) & (k == 0))
            def _():
                rider.start(r_src, r_dst, r_sems)

        def product():
            return _dot(a_ref[...].astype(BF16), b_ref[...].astype(BF16), dims)

        def finish(r):
            res = epilogue(r, *[e[...] for e in ex]) if epilogue is not None else (r,)
            for o, v in zip(outs, res):
                o[...] = v.astype(o.dtype)

        if gk == 1:
            finish(product())
        else:
            acc = rest[ne + r_in + no + r_out]

            @pl.when(k == 0)
            def _():
                acc[...] = product()

            @pl.when((k > 0) & (k < gk - 1))
            def _():
                acc[...] += product()

            @pl.when(k == gk - 1)
            def _():
                finish(acc[...] + product())

        if rider:
            @pl.when((i == gi - 1) & (j == gj - 1) & (k == gk - 1))
            def _():
                rider.finish(r_src, r_dst, r_sems)

    outs = pl.pallas_call(
        body, name=name,
        grid=(gi, gj, gk),
        in_specs=[a_spec, b_spec] + [e_spec] * ne + [ANY] * r_in,
        out_specs=[o_spec] * no + [ANY] * r_out,
        out_shape=[jax.ShapeDtypeStruct(o_shape, d) for d in out_dtypes] + (list(rider.out_shapes) if rider else []),
        scratch_shapes=([pltpu.VMEM((tm, tn), F32)] if gk > 1 else []) + (list(rider.scratch) if rider else []),
        input_output_aliases={2 + ne + s: no + d for s, d in rider.aliases.items()} if rider else {},
        compiler_params=(_params("arbitrary", "arbitrary", "arbitrary", has_side_effects=True) if rider
                         else _params("parallel", "parallel", "arbitrary")),
    )(a, b, *extras, *(rider.operands if rider else ()))
    if rider:
        hosts.carried[name] = list(outs[no:])
    return outs[0] if no == 1 else outs[:no]


def _rms_fwd(x, g, name, tb=512):
    T, D = x.shape
    tb = min(tb, T)

    def body(x_ref, g_ref, o_ref):
        xv = x_ref[...]
        r = lax.rsqrt(jnp.mean(xv * xv, axis=-1, keepdims=True) + EPS)
        o_ref[...] = (xv * r * g_ref[...]).astype(BF16)

    return pl.pallas_call(
        body, name=name, grid=(T // tb,),
        in_specs=[pl.BlockSpec((tb, D), lambda i: (i, 0)), pl.BlockSpec((1, D), lambda i: (0, 0))],
        out_specs=pl.BlockSpec((tb, D), lambda i: (i, 0)),
        out_shape=jax.ShapeDtypeStruct((T, D), BF16),
        compiler_params=_params("parallel"),
    )(x, g)


def _rms_bwd(dy, x, g, dres, name, tb=256):
    T, D = x.shape
    tb = min(tb, T)

    def body(dy_ref, x_ref, g_ref, dres_ref, dx_ref, dxb_ref, dg_ref):
        i = pl.program_id(0)
        xv, dyv = x_ref[...], dy_ref[...]
        r = lax.rsqrt(jnp.mean(xv * xv, axis=-1, keepdims=True) + EPS)
        gy = dyv * g_ref[...]
        dx = r * gy - xv * (r * r * r) * jnp.mean(gy * xv, axis=-1, keepdims=True)
        tot = dres_ref[...] + dx
        dx_ref[...] = tot
        dxb_ref[...] = tot.astype(BF16)
        part = jnp.sum(dyv * (xv * r), axis=0, keepdims=True)

        @pl.when(i == 0)
        def _():
            dg_ref[...] = part

        @pl.when(i > 0)
        def _():
            dg_ref[...] += part

    row = pl.BlockSpec((tb, D), lambda i: (i, 0))
    vec = pl.BlockSpec((1, D), lambda i: (0, 0))
    return pl.pallas_call(
        body, name=name, grid=(T // tb,),
        in_specs=[row, row, vec, row],
        out_specs=[row, row, vec],
        out_shape=[jax.ShapeDtypeStruct((T, D), F32), jax.ShapeDtypeStruct((T, D), BF16),
                   jax.ShapeDtypeStruct((1, D), F32)],
        compiler_params=_params("arbitrary"),
    )(dy, x, g, dres)


def _loss_head(y, tgt, name, tb=256):
    T, D = y.shape
    tb = min(tb, T)

    def body(y_ref, t_ref, dy_ref, dyb_ref, loss_ref):
        i = pl.program_id(0)
        e = y_ref[...] - t_ref[...]
        d = e * (1.0 / D)
        dy_ref[...] = d
        dyb_ref[...] = d.astype(BF16)
        part = 0.5 * jnp.sum(jnp.sum(e * e, axis=1, keepdims=True) * (1.0 / D), axis=0, keepdims=True)

        @pl.when(i == 0)
        def _():
            loss_ref[...] = part

        @pl.when(i > 0)
        def _():
            loss_ref[...] += part

    row = pl.BlockSpec((tb, D), lambda i: (i, 0))
    return pl.pallas_call(
        body, name=name, grid=(T // tb,),
        in_specs=[row, row],
        out_specs=[row, row, pl.BlockSpec((1, 1), lambda i: (0, 0))],
        out_shape=[jax.ShapeDtypeStruct((T, D), F32), jax.ShapeDtypeStruct((T, D), BF16),
                   jax.ShapeDtypeStruct((1, 1), F32)],
        compiler_params=_params("arbitrary"),
    )(y, tgt)


def _head_rms(xh, g):
    r = lax.rsqrt(jnp.mean(xh * xh, axis=-1, keepdims=True) + EPS)
    return xh * r * g


def _qkv_prep(proj, gains, n_norm, gain_row, ch, name, n_scaled=0, post_scale=1.0, tb=512):
    T, W = proj.shape
    tb = min(tb, T)
    nch = W // ch
    nh = ch // HEAD_DIM

    def body(p_ref, g_ref, o_ref):
        j = pl.program_id(0)

        @pl.when(j < n_norm)
        def _():
            g = g_ref[...]
            if n_scaled:
                g = g * jnp.where(j < n_scaled, post_scale, 1.0)
            for h in range(nh):
                sl = slice(h * HEAD_DIM, (h + 1) * HEAD_DIM)
                o_ref[:, sl] = _head_rms(p_ref[:, sl], g).astype(BF16)

        @pl.when(j >= n_norm)
        def _():
            o_ref[...] = p_ref[...].astype(BF16)

    return pl.pallas_call(
        body, name=name, grid=(nch, T // tb),
        in_specs=[pl.BlockSpec((tb, ch), lambda j, i: (i, j)),
                  pl.BlockSpec((None, 1, HEAD_DIM), lambda j, i: (gain_row(j), 0, 0))],
        out_specs=pl.BlockSpec((tb, ch), lambda j, i: (i, j)),
        out_shape=jax.ShapeDtypeStruct((T, W), BF16),
        compiler_params=_params("parallel", "parallel"),
    )(proj, gains)


def _into(body, name, grid, in_specs, out_spec, out_shape, extra_out_specs, extra_out_shapes, buf, operands, sem):
    if buf is None:
        def kernel(*refs):
            body(*refs)
        ins, alias, ops = in_specs, {}, operands
    else:
        def kernel(_, *refs):
            body(*refs)
        ins = [pl.BlockSpec(memory_space=pl.ANY)] + in_specs
        alias, ops = {0: 0}, (buf,) + tuple(operands)
    return pl.pallas_call(
        kernel, name=name, grid=grid, in_specs=ins,
        out_specs=[out_spec] + extra_out_specs,
        out_shape=[out_shape] + extra_out_shapes,
        input_output_aliases=alias,
        compiler_params=_params(*sem),
    )(*ops)


def _head_rms_bwd_into(buf, W, d, proj, gain, off, ch, name, in_scale=1.0, tb=256):
    T, wd = d.shape
    tb = min(tb, T)
    n = wd // ch
    nh = ch // HEAD_DIM

    def body(d_ref, p_ref, g_ref, o_ref, dg_ref):
        i = pl.program_id(1)
        g = g_ref[...]
        part = jnp.zeros((1, HEAD_DIM), F32)
        for h in range(nh):
            sl = slice(h * HEAD_DIM, (h + 1) * HEAD_DIM)
            xh, dy = p_ref[:, sl], d_ref[:, sl]
            if in_scale != 1.0:
                dy = dy * in_scale
            r = lax.rsqrt(jnp.mean(xh * xh, axis=-1, keepdims=True) + EPS)
            gy = dy * g
            dx = r * gy - xh * (r * r * r) * jnp.mean(gy * xh, axis=-1, keepdims=True)
            o_ref[:, sl] = dx.astype(BF16)
            part = part + jnp.sum(dy * (xh * r), axis=0, keepdims=True)

        @pl.when(i == 0)
        def _():
            dg_ref[...] = part

        @pl.when(i > 0)
        def _():
            dg_ref[...] += part

    return _into(
        body, name, (n, T // tb),
        [pl.BlockSpec((tb, ch), lambda j, i: (i, j)), pl.BlockSpec((tb, ch), lambda j, i: (i, off + j)),
         pl.BlockSpec((1, HEAD_DIM), lambda j, i: (0, 0))],
        pl.BlockSpec((tb, ch), lambda j, i: (i, off + j)), jax.ShapeDtypeStruct((T, W), BF16),
        [pl.BlockSpec((None, 1, HEAD_DIM), lambda j, i: (j, 0, 0))], [jax.ShapeDtypeStruct((n, 1, HEAD_DIM), F32)],
        buf, (d, proj, gain), ("parallel", "arbitrary"))


def _sum_cast_into(buf, W, srcs, off, ch, name, tb=256):
    T, wd = srcs[0].shape
    tb = min(tb, T)
    n = wd // ch
    ns = len(srcs)

    def body(*refs):
        o_ref = refs[ns]
        tot = refs[0][...]
        for s in refs[1:ns]:
            tot = tot + s[...]
        o_ref[...] = tot.astype(BF16)

    out = _into(
        body, name, (n, T // tb),
        [pl.BlockSpec((tb, ch), lambda j, i: (i, j))] * ns,
        pl.BlockSpec((tb, ch), lambda j, i: (i, off + j)), jax.ShapeDtypeStruct((T, W), BF16),
        [], [], buf, tuple(srcs), ("parallel", "parallel"))
    return out[0]


def _tri(n, lower):
    r = lax.broadcasted_iota(jnp.int32, (n, n), 0)
    c = lax.broadcasted_iota(jnp.int32, (n, n), 1)
    return jnp.where((c <= r) if lower else (c >= r), 1.0, 0.0).astype(F32)


def _dot_exact(a, b):
    return lax.dot_general(a, b, NN, precision=lax.Precision.HIGHEST, preferred_element_type=F32)


def _log_sigmoid(z):
    return jnp.minimum(z, 0.0) - jnp.log(1.0 + jnp.exp(-jnp.abs(z)))


def _gate_fwd(f_raw, b_pad, hp, out_scale, name, blk=256):
    T = f_raw.shape[0]
    blk = min(blk, T)

    def body(f_ref, b_ref, c_ref):
        tri = _tri(blk, True)
        carry = jnp.zeros((1, HEAD_DIM), F32)
        for j in range(T // blk):
            lf = _log_sigmoid(f_ref[j * blk:(j + 1) * blk, :] + b_ref[...])
            cb = _dot_exact(tri, lf) + carry
            carry = cb[blk - 1:blk, :]
            c_ref[:, j * blk:(j + 1) * blk] = cb.T[:hp, :] * out_scale

    return pl.pallas_call(
        body, name=name,
        in_specs=[pl.BlockSpec(memory_space=pltpu.VMEM)] * 2,
        out_specs=pl.BlockSpec(memory_space=pltpu.VMEM),
        out_shape=jax.ShapeDtypeStruct((hp, T), F32),
        compiler_params=_params(),
    )(f_raw, b_pad)


def _gate_bwd(dc_rows, dc_cols, f_raw, b_pad, n_heads, hp, name, blk=256):
    T = f_raw.shape[0]
    blk = min(blk, T)

    def body(dc_ref, dcc_ref, f_ref, b_ref, dz_ref, db_ref):
        tri = _tri(blk, False)
        lane = lax.broadcasted_iota(jnp.int32, (blk, HEAD_DIM), 1)
        carry = jnp.zeros((1, HEAD_DIM), F32)
        db = jnp.zeros((1, HEAD_DIM), F32)
        for j in reversed(range(T // blk)):
            rows = dc_ref[:, j * blk:(j + 1) * blk]
            if hp < HEAD_DIM:
                rows = jnp.concatenate([rows, jnp.zeros((HEAD_DIM - hp, blk), F32)], axis=0)
            dlf = _dot_exact(tri, rows.T + dcc_ref[j * blk:(j + 1) * blk, :]) + carry
            carry = dlf[0:1, :]
            z = f_ref[j * blk:(j + 1) * blk, :] + b_ref[...]
            dz = jnp.where(lane < n_heads, dlf / (1.0 + jnp.exp(z)), 0.0)
            dz_ref[j * blk:(j + 1) * blk, :] = dz.astype(BF16)
            db = db + jnp.sum(dz, axis=0, keepdims=True)
        db_ref[...] = db

    return pl.pallas_call(
        body, name=name,
        in_specs=[pl.BlockSpec(memory_space=pltpu.VMEM)] * 4,
        out_specs=[pl.BlockSpec(memory_space=pltpu.VMEM)] * 2,
        out_shape=[jax.ShapeDtypeStruct((T, HEAD_DIM), BF16), jax.ShapeDtypeStruct((1, HEAD_DIM), F32)],
        compiler_params=_params(),
    )(dc_rows, dc_cols, f_raw, b_pad)


def _pairs(nb, key_major):
    if key_major:
        pairs = [(qi, ki) for ki in range(nb) for qi in range(ki, nb)]
    else:
        pairs = [(qi, ki) for qi in range(nb) for ki in range(qi + 1)]
    return (jnp.asarray(np.array([p[0] for p in pairs], np.int32)),
            jnp.asarray(np.array([p[1] for p in pairs], np.int32)))


LOG2E = 1.4426950408889634
LN2 = 0.6931471805599453
FOX_Q_SCALE = HEAD_DIM ** -0.5 * LOG2E


def _fox_logits(q, k, ck_row, diagonal):
    s = _dot(q, k, NT) - ck_row
    if diagonal:
        row = lax.broadcasted_iota(jnp.int32, s.shape, 0)
        col = lax.broadcasted_iota(jnp.int32, s.shape, 1)
        s = jnp.where(col <= row, s, MASKED)
    return s


def _fox_fwd(qkv, ck, H, name, tb=1024, hs=4, hosts=None):
    rider = hosts.rider(name) if hosts is not None else None
    T = qkv.shape[0]
    tb = min(tb, T)
    nb = T // tb
    qt, kt = _pairs(nb, False)
    n_pairs = int(qt.shape[0])
    hb = H // hs
    r_in, r_out = (len(rider.operands), len(rider.out_shapes)) if rider else (0, 0)

    def body(qt_ref, kt_ref, q_ref, k_ref, v_ref, ck_ref, *rest):
        r_src, (o_ref, lse_ref), r_dst = rest[:r_in], rest[r_in:r_in + 2], rest[r_in + 2:r_in + 2 + r_out]
        m_sc, l_sc, acc_sc = rest[r_in + 2 + r_out:r_in + 5 + r_out]
        r_sems = rest[r_in + 5 + r_out:]
        p_ = pl.program_id(1)
        qi, ki = qt_ref[p_], kt_ref[p_]
        if rider:
            @pl.when((pl.program_id(0) == 0) & (p_ == 0))
            def _():
                rider.start(r_src, r_dst, r_sems)

        @pl.when(ki == 0)
        def _():
            m_sc[...] = jnp.full_like(m_sc, MASKED)
            l_sc[...] = jnp.zeros_like(l_sc)
            acc_sc[...] = jnp.zeros_like(acc_sc)

        heads = [(hh, slice(hh * HEAD_DIM, (hh + 1) * HEAD_DIM)) for hh in range(hs)]

        def tile(diagonal):
            for hh, sl in heads:
                s = _fox_logits(q_ref[:, sl], k_ref[:, sl], ck_ref[hh], diagonal)
                m_prev = m_sc[hh]
                m_new = jnp.maximum(m_prev, jnp.max(s, axis=1, keepdims=True))
                alpha = jnp.exp2(m_prev - m_new)
                p = jnp.exp2(s - m_new[:, :1])
                l_sc[hh] = alpha * l_sc[hh] + jnp.sum(p, axis=1, keepdims=True)
                acc_sc[hh] = alpha * acc_sc[hh] + _dot(p.astype(BF16), v_ref[:, sl], NN)
                m_sc[hh] = m_new

        @pl.when(ki < qi)
        def _():
            tile(False)

        @pl.when(ki == qi)
        def _():
            tile(True)
            for hh, sl in heads:
                o_ref[:, sl] = (acc_sc[hh] / l_sc[hh]).astype(BF16)
                lse_ref[:, sl] = m_sc[hh] + jnp.log(l_sc[hh]) * LOG2E

        if rider:
            @pl.when((pl.program_id(0) == hb - 1) & (p_ == n_pairs - 1))
            def _():
                rider.finish(r_src, r_dst, r_sems)

    blk = lambda f: pl.BlockSpec((tb, hs * HEAD_DIM), f)
    outs = pl.pallas_call(
        body, name=name,
        grid_spec=pltpu.PrefetchScalarGridSpec(
            num_scalar_prefetch=2, grid=(hb, n_pairs),
            in_specs=[blk(lambda h, p, qt, kt: (qt[p], h)),
                      blk(lambda h, p, qt, kt: (kt[p], hb + h)),
                      blk(lambda h, p, qt, kt: (kt[p], 2 * hb + h)),
                      pl.BlockSpec((hs, 1, tb), lambda h, p, qt, kt: (h, 0, kt[p]))] + [ANY] * r_in,
            out_specs=[blk(lambda h, p, qt, kt: (qt[p], h)), blk(lambda h, p, qt, kt: (qt[p], h))] + [ANY] * r_out,
            scratch_shapes=[pltpu.VMEM((hs, tb, HEAD_DIM), F32)] * 3 + (list(rider.scratch) if rider else [])),
        out_shape=[jax.ShapeDtypeStruct((T, H * HEAD_DIM), BF16), jax.ShapeDtypeStruct((T, H * HEAD_DIM), F32)]
        + (list(rider.out_shapes) if rider else []),
        compiler_params=_params("arbitrary", "arbitrary", has_side_effects=bool(rider)),
    )(qt, kt, qkv, qkv, qkv, ck, *(rider.operands if rider else ()))
    if rider:
        hosts.carried[name] = list(outs[2:])
    return outs[0], outs[1]


def _row_dot(do, o, nh, width, name, lane_per_head, tb=256):
    T = do.shape[0]
    tb = min(tb, T)
    wout = HEAD_DIM if lane_per_head else nh * HEAD_DIM

    def body(do_ref, o_ref, d_ref):
        lane = lax.broadcasted_iota(jnp.int32, (tb, HEAD_DIM), 1)
        tile = jnp.zeros((tb, HEAD_DIM), F32)
        for h in range(nh):
            sl = slice(h * width, (h + 1) * width)
            d = jnp.sum(do_ref[:, sl].astype(F32) * o_ref[:, sl].astype(F32), axis=1, keepdims=True)
            if lane_per_head:
                tile = jnp.where(lane == h, d, tile)
            else:
                d_ref[:, h * HEAD_DIM:(h + 1) * HEAD_DIM] = jnp.broadcast_to(d, (tb, HEAD_DIM))
        if lane_per_head:
            d_ref[...] = tile

    row = pl.BlockSpec((tb, nh * width), lambda i: (i, 0))
    return pl.pallas_call(
        body, name=name, grid=(T // tb,),
        in_specs=[row, row], out_specs=pl.BlockSpec((tb, wout), lambda i: (i, 0)),
        out_shape=jax.ShapeDtypeStruct((T, wout), F32),
        compiler_params=_params("parallel"),
    )(do, o)


def _fox_bwd(qkv, do, ck, lse, dd, H, hp, name, tb=1024, hs=2, hosts=None):
    rider = hosts.rider(name) if hosts is not None else None
    T = qkv.shape[0]
    tb = min(tb, T)
    nb = T // tb
    qt, kt = _pairs(nb, True)
    n_pairs = int(qt.shape[0])
    r_in, r_out = (len(rider.operands), len(rider.out_shapes)) if rider else (0, 0)
    hb = H // hs

    def body(qt_ref, kt_ref, q_ref, k_ref, v_ref, do_ref, ck_ref, lse_ref, dd_ref, *rest):
        r_src, r_dst, r_sems = rest[:r_in], rest[r_in + 5:r_in + 5 + r_out], rest[r_in + 5 + r_out:]
        dq_ref, dk_ref, dv_ref, dc_ref, dcq_ref = rest[r_in:r_in + 5]
        p_ = pl.program_id(1)
        qi, ki = qt_ref[p_], kt_ref[p_]
        if rider:
            @pl.when((pl.program_id(0) == 0) & (p_ == 0))
            def _():
                rider.start(r_src, r_dst, r_sems)

        @pl.when(p_ == 0)
        def _():
            dq_ref[...] = jnp.zeros_like(dq_ref)
            dcq_ref[...] = jnp.zeros_like(dcq_ref)

        @pl.when(qi == ki)
        def _():
            dk_ref[...] = jnp.zeros_like(dk_ref)
            dv_ref[...] = jnp.zeros_like(dv_ref)
            dc_ref[...] = jnp.zeros_like(dc_ref)

        rows = pl.ds(pl.multiple_of(qi * tb, tb), tb)

        def tile(diagonal):
            for hh in range(hs):
                sl = slice(hh * HEAD_DIM, (hh + 1) * HEAD_DIM)
                s = _fox_logits(q_ref[:, sl], k_ref[:, sl], ck_ref[hh], diagonal)
                p = jnp.exp2(s - lse_ref[:, hh * HEAD_DIM:hh * HEAD_DIM + 1])
                dv_ref[:, sl] += _dot(p.astype(BF16), do_ref[:, sl], TN)
                dp = _dot(do_ref[:, sl], v_ref[:, sl], NT)
                ds = p * (dp - dd_ref[:, hh * HEAD_DIM:hh * HEAD_DIM + 1])
                dc_ref[hh] -= jnp.sum(ds, axis=0, keepdims=True)
                dcq_ref[rows, sl] += jnp.sum(ds, axis=1, keepdims=True)
                dsb = ds.astype(BF16)
                dq_ref[rows, sl] += _dot(dsb, k_ref[:, sl], NN)
                dk_ref[:, sl] += _dot(dsb, q_ref[:, sl], TN)

        @pl.when(ki < qi)
        def _():
            tile(False)

        @pl.when(ki == qi)
        def _():
            tile(True)

        if rider:
            @pl.when((pl.program_id(0) == hb - 1) & (p_ == n_pairs - 1))
            def _():
                rider.finish(r_src, r_dst, r_sems)

    blk = lambda f: pl.BlockSpec((tb, hs * HEAD_DIM), f)
    at_q = lambda h, p, qt, kt: (qt[p], h)
    at_k = lambda h, p, qt, kt: (kt[p], h)
    crow = pl.BlockSpec((hs, 1, tb), lambda h, p, qt, kt: (h, 0, kt[p]))
    whole = pl.BlockSpec((T, hs * HEAD_DIM), lambda h, p, qt, kt: (0, h), pipeline_mode=pl.Buffered(1))
    wide = jax.ShapeDtypeStruct((T, H * HEAD_DIM), F32)
    outs = pl.pallas_call(
        body, name=name,
        grid_spec=pltpu.PrefetchScalarGridSpec(
            num_scalar_prefetch=2, grid=(hb, n_pairs),
            in_specs=[blk(at_q),
                      blk(lambda h, p, qt, kt: (kt[p], hb + h)),
                      blk(lambda h, p, qt, kt: (kt[p], 2 * hb + h)),
                      blk(at_q), crow, blk(at_q), blk(at_q)] + [ANY] * r_in,
            out_specs=[whole, blk(at_k), blk(at_k), crow, whole] + [ANY] * r_out,
            scratch_shapes=list(rider.scratch) if rider else []),
        out_shape=[wide, wide, wide, jax.ShapeDtypeStruct((hp, 1, T), F32), wide] + (list(rider.out_shapes) if rider else []),
        compiler_params=_params("arbitrary", "arbitrary", has_side_effects=bool(rider)),
    )(qt, kt, qkv, qkv, qkv, do, ck, lse, dd, *(rider.operands if rider else ()))
    if rider:
        hosts.carried[name] = list(outs[5:])
    return outs[:5]


def _lane_per_head(wide, H, name, tb=256):
    T = wide.shape[0]
    tb = min(tb, T)

    def body(w_ref, o_ref):
        lane = lax.broadcasted_iota(jnp.int32, (tb, HEAD_DIM), 1)
        tile = jnp.zeros((tb, HEAD_DIM), F32)
        for h in range(H):
            tile = jnp.where(lane == h, w_ref[:, h * HEAD_DIM:(h + 1) * HEAD_DIM], tile)
        o_ref[...] = tile

    return pl.pallas_call(
        body, name=name, grid=(T // tb,),
        in_specs=[pl.BlockSpec((tb, H * HEAD_DIM), lambda i: (i, 0))],
        out_specs=pl.BlockSpec((tb, HEAD_DIM), lambda i: (i, 0)),
        out_shape=jax.ShapeDtypeStruct((T, HEAD_DIM), F32),
        compiler_params=_params("parallel"),
    )(wide)


def _slopes(n_groups, nh):
    n = n_groups * nh
    s = np.exp2(-ALIBI_MAX_EXP * np.arange(1, n + 1, dtype=np.float32) / np.float32(n)).astype(np.float32)
    return s.reshape(n_groups, nh)


def _window_logits(qh, kh, slope_r, prev, has_prev):
    qi = lax.broadcasted_iota(jnp.int32, (DIL_SPAN, DIL_SPAN), 0)
    kl = lax.broadcasted_iota(jnp.int32, (DIL_SPAN, DIL_SPAN), 1)
    delta = qi - kl + (DIL_SPAN if prev else 0)
    s = _dot(qh, kh, NT) * (HEAD_DIM ** -0.5) - slope_r * delta.astype(F32)
    valid = ((kl >= qi) & has_prev) if prev else (kl <= qi)
    return jnp.where(valid, s, MASKED)


def _dil_views(T, r, G, nh, dv):
    L = T // r
    C, V = nh * HEAD_DIM, nh * dv
    return L, C, V, 2 * G * C + V


def _dil_fwd(qkv, g, r, G, nh, dv, slopes, name):
    T = qkv.shape[0]
    L, C, V, W = _dil_views(T, r, G, nh, dv)
    nblk = L // DIL_SPAN
    nc, nv = W // C, W // V
    view = qkv.reshape(L, r * W)

    def body(q_ref, kp_ref, kc_ref, vp_ref, vc_ref, num_ref, m_ref, den_ref):
        has_prev = pl.program_id(1) > 0
        lane = lax.broadcasted_iota(jnp.int32, (DIL_SPAN, HEAD_DIM), 1)
        m_tile = jnp.zeros((DIL_SPAN, HEAD_DIM), F32)
        den_tile = jnp.ones((DIL_SPAN, HEAD_DIM), F32)
        for h in range(nh):
            sl = slice(h * HEAD_DIM, (h + 1) * HEAD_DIM)
            vs = slice(h * dv, (h + 1) * dv)
            sr = float(slopes[h]) * r
            sc = _window_logits(q_ref[:, sl], kc_ref[:, sl], sr, False, has_prev)
            sp = _window_logits(q_ref[:, sl], kp_ref[:, sl], sr, True, has_prev)
            m = jnp.maximum(jnp.max(sc, axis=1, keepdims=True), jnp.max(sp, axis=1, keepdims=True))
            pc, pp = jnp.exp(sc - m), jnp.exp(sp - m)
            den = jnp.sum(pc, axis=1, keepdims=True) + jnp.sum(pp, axis=1, keepdims=True)
            num_ref[:, vs] = _dot(pc.astype(BF16), vc_ref[:, vs], NN) + _dot(pp.astype(BF16), vp_ref[:, vs], NN)
            m_tile = jnp.where(lane == h, m, m_tile)
            den_tile = jnp.where(lane == h, den, den_tile)
        m_ref[...] = m_tile
        den_ref[...] = den_tile

    prev = lambda i: jnp.maximum(i - 1, 0)
    stat = pl.BlockSpec((DIL_SPAN, HEAD_DIM), lambda b, i: (i, b))
    num, m, den = pl.pallas_call(
        body, name=name, grid=(r, nblk),
        in_specs=[pl.BlockSpec((DIL_SPAN, C), lambda b, i: (i, b * nc + g)),
                  pl.BlockSpec((DIL_SPAN, C), lambda b, i: (prev(i), b * nc + G + g)),
                  pl.BlockSpec((DIL_SPAN, C), lambda b, i: (i, b * nc + G + g)),
                  pl.BlockSpec((DIL_SPAN, V), lambda b, i: (prev(i), b * nv + nv - 1)),
                  pl.BlockSpec((DIL_SPAN, V), lambda b, i: (i, b * nv + nv - 1))],
        out_specs=[pl.BlockSpec((DIL_SPAN, V), lambda b, i: (i, b)), stat, stat],
        out_shape=[jax.ShapeDtypeStruct((L, r * V), F32), jax.ShapeDtypeStruct((L, r * HEAD_DIM), F32),
                   jax.ShapeDtypeStruct((L, r * HEAD_DIM), F32)],
        compiler_params=_params("parallel", "parallel"),
    )(view, view, view, view, view)
    return num.reshape(T, V), m.reshape(T, HEAD_DIM), den.reshape(T, HEAD_DIM)


def _dil_merge(nums, ms, dens, nh, dv, name, tb=256):
    T, V = nums[0].shape
    tb = min(tb, T)
    G = len(nums)

    def body(*refs):
        num_r, m_r, den_r = refs[:G], refs[G:2 * G], refs[2 * G:3 * G]
        o_ref, lse_ref = refs[3 * G], refs[3 * G + 1]
        mm = m_r[0][...]
        for g in range(1, G):
            mm = jnp.maximum(mm, m_r[g][...])
        w = [jnp.exp(m_r[g][...] - mm) for g in range(G)]
        den = w[0] * den_r[0][...]
        for g in range(1, G):
            den = den + w[g] * den_r[g][...]
        lse_ref[...] = mm + jnp.log(den)
        for h in range(nh):
            vs = slice(h * dv, (h + 1) * dv)
            num = w[0][:, h:h + 1] * num_r[0][:, vs]
            for g in range(1, G):
                num = num + w[g][:, h:h + 1] * num_r[g][:, vs]
            o_ref[:, vs] = (num / den[:, h:h + 1]).astype(BF16)

    wide = pl.BlockSpec((tb, V), lambda i: (i, 0))
    stat = pl.BlockSpec((tb, HEAD_DIM), lambda i: (i, 0))
    return pl.pallas_call(
        body, name=name, grid=(T // tb,),
        in_specs=[wide] * G + [stat] * (2 * G),
        out_specs=[wide, stat],
        out_shape=[jax.ShapeDtypeStruct((T, V), BF16), jax.ShapeDtypeStruct((T, HEAD_DIM), F32)],
        compiler_params=_params("parallel"),
    )(*nums, *ms, *dens)


def _dil_dq(qkv, do, lse, dd, g, r, G, nh, dv, slopes, name):
    T = qkv.shape[0]
    L, C, V, W = _dil_views(T, r, G, nh, dv)
    nblk = L // DIL_SPAN
    nc, nv = W // C, W // V
    view = qkv.reshape(L, r * W)
    scale = HEAD_DIM ** -0.5

    def body(q_ref, kp_ref, kc_ref, vp_ref, vc_ref, do_ref, lse_ref, dd_ref, dq_ref):
        has_prev = pl.program_id(1) > 0
        for h in range(nh):
            sl = slice(h * HEAD_DIM, (h + 1) * HEAD_DIM)
            vs = slice(h * dv, (h + 1) * dv)
            sr = float(slopes[h]) * r
            lse_h, dd_h = lse_ref[:, h:h + 1], dd_ref[:, h:h + 1]
            acc = jnp.zeros((DIL_SPAN, HEAD_DIM), F32)
            for k_ref, v_ref, is_prev in ((kc_ref, vc_ref, False), (kp_ref, vp_ref, True)):
                s = _window_logits(q_ref[:, sl], k_ref[:, sl], sr, is_prev, has_prev)
                p = jnp.exp(s - lse_h)
                dp = _dot(do_ref[:, vs], v_ref[:, vs], NT)
                ds = (p * (dp - dd_h)).astype(BF16)
                acc = acc + _dot(ds, k_ref[:, sl], NN)
            dq_ref[:, sl] = scale * acc

    prev = lambda i: jnp.maximum(i - 1, 0)
    stat = pl.BlockSpec((DIL_SPAN, HEAD_DIM), lambda b, i: (i, b))
    dq = pl.pallas_call(
        body, name=name, grid=(r, nblk),
        in_specs=[pl.BlockSpec((DIL_SPAN, C), lambda b, i: (i, b * nc + g)),
                  pl.BlockSpec((DIL_SPAN, C), lambda b, i: (prev(i), b * nc + G + g)),
                  pl.BlockSpec((DIL_SPAN, C), lambda b, i: (i, b * nc + G + g)),
                  pl.BlockSpec((DIL_SPAN, V), lambda b, i: (prev(i), b * nv + nv - 1)),
                  pl.BlockSpec((DIL_SPAN, V), lambda b, i: (i, b * nv + nv - 1)),
                  pl.BlockSpec((DIL_SPAN, V), lambda b, i: (i, b)), stat, stat],
        out_specs=pl.BlockSpec((DIL_SPAN, C), lambda b, i: (i, b)),
        out_shape=jax.ShapeDtypeStruct((L, r * C), F32),
        compiler_params=_params("parallel", "parallel"),
    )(view, view, view, view, view, do.reshape(L, r * V), lse.reshape(L, r * HEAD_DIM), dd.reshape(L, r * HEAD_DIM))
    return dq.reshape(T, C)


def _dil_dkv(qkv, do, lse, dd, g, r, G, nh, dv, slopes, name):
    T = qkv.shape[0]
    L, C, V, W = _dil_views(T, r, G, nh, dv)
    nblk = L // DIL_SPAN
    nc, nv = W // C, W // V
    view = qkv.reshape(L, r * W)
    scale = HEAD_DIM ** -0.5

    def body(k_ref, v_ref, qc_ref, qn_ref, doc_ref, don_ref, lsec_ref, lsen_ref, ddc_ref, ddn_ref, dk_ref, dv_ref):
        has_next = pl.program_id(1) < nblk - 1
        for h in range(nh):
            sl = slice(h * HEAD_DIM, (h + 1) * HEAD_DIM)
            vs = slice(h * dv, (h + 1) * dv)
            sr = float(slopes[h]) * r
            dk = jnp.zeros((DIL_SPAN, HEAD_DIM), F32)
            dvh = jnp.zeros((DIL_SPAN, dv), F32)
            for q_ref, do_ref, lse_ref, dd_ref, is_next in ((qc_ref, doc_ref, lsec_ref, ddc_ref, False),
                                                          (qn_ref, don_ref, lsen_ref, ddn_ref, True)):
                s = _window_logits(q_ref[:, sl], k_ref[:, sl], sr, is_next, has_next)
                p = jnp.exp(s - lse_ref[:, h:h + 1])
                dvh = dvh + _dot(p.astype(BF16), do_ref[:, vs], TN)
                dp = _dot(do_ref[:, vs], v_ref[:, vs], NT)
                ds = (p * (dp - dd_ref[:, h:h + 1])).astype(BF16)
                dk = dk + _dot(ds, q_ref[:, sl], TN)
            dk_ref[:, sl] = scale * dk
            dv_ref[:, vs] = dvh

    nxt = lambda i: jnp.minimum(i + 1, nblk - 1)
    stat_c = pl.BlockSpec((DIL_SPAN, HEAD_DIM), lambda b, i: (i, b))
    stat_n = pl.BlockSpec((DIL_SPAN, HEAD_DIM), lambda b, i: (nxt(i), b))
    do_v, lse_v, dd_v = do.reshape(L, r * V), lse.reshape(L, r * HEAD_DIM), dd.reshape(L, r * HEAD_DIM)
    dk, dvv = pl.pallas_call(
        body, name=name, grid=(r, nblk),
        in_specs=[pl.BlockSpec((DIL_SPAN, C), lambda b, i: (i, b * nc + G + g)),
                  pl.BlockSpec((DIL_SPAN, V), lambda b, i: (i, b * nv + nv - 1)),
                  pl.BlockSpec((DIL_SPAN, C), lambda b, i: (i, b * nc + g)),
                  pl.BlockSpec((DIL_SPAN, C), lambda b, i: (nxt(i), b * nc + g)),
                  pl.BlockSpec((DIL_SPAN, V), lambda b, i: (i, b)),
                  pl.BlockSpec((DIL_SPAN, V), lambda b, i: (nxt(i), b)),
                  stat_c, stat_n, stat_c, stat_n],
        out_specs=[pl.BlockSpec((DIL_SPAN, C), lambda b, i: (i, b)), pl.BlockSpec((DIL_SPAN, V), lambda b, i: (i, b))],
        out_shape=[jax.ShapeDtypeStruct((L, r * C), F32), jax.ShapeDtypeStruct((L, r * V), F32)],
        compiler_params=_params("parallel", "parallel"),
    )(view, view, view, view, do_v, do_v, lse_v, lse_v, dd_v, dd_v)
    return dk.reshape(T, C), dvv.reshape(T, V)


def _relu2(r):
    a = jnp.maximum(r, 0.0)
    return (a * a,)


def _mlp_fwd(x, g, w_up, w_down, tag, hosts=None):
    T, D = x.shape
    F = w_down.shape[0]
    h = _rms_fwd(x, g, f"{tag}_norm")
    a2 = _mm(h, w_up, T, F, D, mode="nn", name=f"{tag}_up", b_stack=N_CHIPS, out_dtypes=(BF16,), epilogue=_relu2,
             hosts=hosts)
    y = _mm(a2, w_down, T, D, F, mode="nn", name=f"{tag}_down", out_dtypes=(F32,), extras=(x,),
            epilogue=lambda r, res: (res + r,), hosts=hosts)
    return y, (x, h, a2)


def _mlp_bwd(dy, dyb, saved, g, w_up, w_down, tag, hosts=None):
    x, h, a2 = saved
    T, D = x.shape
    F = w_down.shape[0]
    d_down = _mm(a2, dyb, F, D, T, mode="tn", name=f"{tag}_dwdown", out_dtypes=(F32,), hosts=hosts)
    du = _mm(dyb, w_down, T, F, D, mode="nt", name=f"{tag}_da", out_dtypes=(BF16,), extras=(a2,),
             epilogue=lambda r, sq: (r * (2.0 * jnp.sqrt(sq.astype(F32))),))
    d_up = _mm(h, du, D, F, T, mode="tn", name=f"{tag}_dwup", out_stack=N_CHIPS, out_dtypes=(F32,))
    dh = _mm(du, w_up, T, D, F, mode="nt", name=f"{tag}_dh", b_stack=N_CHIPS, out_dtypes=(F32,))
    dx, dxb, dg = _rms_bwd(dh, x, g, dy, f"{tag}_dnorm")
    return dx, dxb, dg, d_up, d_down


def _fox_dims(D):
    H = D // HEAD_DIM
    return H, max(8, H), (H // 2) * HEAD_DIM


def _fox_layer_fwd(x, g, w_qkv, w_f, b_pad, gains, w_out, hosts=None):
    T, D = x.shape
    H, hp, ch = _fox_dims(D)
    h = _rms_fwd(x, g, "fox_norm")
    proj = _mm(h, w_qkv, T, 3 * D, D, mode="nn", name="fox_proj", out_dtypes=(F32,))
    f_raw = _mm(h, w_f, T, HEAD_DIM, D, mode="nn", name="fox_gate_proj", out_dtypes=(F32,))
    qkv = _qkv_prep(proj, gains, 4, lambda j: jnp.minimum(j // 2, 1), ch, "fox_qk_norm", n_scaled=2, post_scale=FOX_Q_SCALE)
    ck = _gate_fwd(f_raw, b_pad, hp, LOG2E, "fox_gate").reshape(hp, 1, T)
    o, lse = _fox_fwd(qkv, ck, H, "fox_attn", hosts=hosts)
    y = _mm(o, w_out, T, D, D, mode="nn", name="fox_out", out_dtypes=(F32,), extras=(x,),
            epilogue=lambda r, res: (res + r,), hosts=hosts)
    return y, (x, h, proj, f_raw, qkv, ck, o, lse)


def _fox_layer_bwd(dy, dyb, saved, g, w_qkv, w_f, b_pad, gains, w_out, hosts=None):
    x, h, proj, f_raw, qkv, ck, o, lse = saved
    T, D = x.shape
    H, hp, ch = _fox_dims(D)
    d_out = _mm(o, dyb, D, D, T, mode="tn", name="fox_dwout", out_dtypes=(F32,), hosts=hosts)
    do = _mm(dyb, w_out, T, D, D, mode="nt", name="fox_do", out_dtypes=(BF16,))
    dd = _row_dot(do, o, H, HEAD_DIM, "fox_rowdot", False)
    dq, dk, dv, dck, dcq = _fox_bwd(qkv, do, ck, lse, dd, H, hp, "fox_attn_bwd", hosts=hosts)
    dcq = _lane_per_head(dcq, H, "fox_dc_query")
    dproj, dgq = _head_rms_bwd_into(None, 3 * D, dq, proj, gains[0], 0, ch, "fox_dq_norm", in_scale=HEAD_DIM ** -0.5)
    dproj, dgk = _head_rms_bwd_into(dproj, 3 * D, dk, proj, gains[1], 2, ch, "fox_dk_norm", in_scale=LN2)
    dproj = _sum_cast_into(dproj, 3 * D, [dv], 4, ch, "fox_dv_cast")
    dz, db = _gate_bwd(dck.reshape(hp, T), dcq, f_raw, b_pad, H, hp, "fox_gate_bwd")
    d_qkv = _mm(h, dproj, D, 3 * D, T, mode="tn", name="fox_dwqkv", out_dtypes=(F32,))
    d_f = _mm(h, dz, D, HEAD_DIM, T, mode="tn", name="fox_dwgate", out_dtypes=(F32,))
    if hosts is not None:
        hosts.grads.update(fox_qkv=d_qkv, fox_f=d_f, fox_out=d_out)
    dh = _mm(dproj, w_qkv, T, D, 3 * D, mode="nt", name="fox_dh", out_dtypes=(F32,), hosts=hosts)
    dh = _mm(dz, w_f, T, D, HEAD_DIM, mode="nt", name="fox_dh_gate", out_dtypes=(F32,), extras=(dh,),
             epilogue=lambda r, e: (e + r,))
    dx, dxb, dg = _rms_bwd(dh, x, g, dy, "fox_dnorm")
    dgains = jnp.stack([dgq.sum(axis=0), dgk.sum(axis=0)])
    return dx, dxb, dg, d_qkv, d_f, db, dgains, d_out


def _dil_dims(D):
    nh = D // (2 * HEAD_DIM)
    return nh, D // nh, len(DIL_PATTERNS)


def _dil_layer_fwd(x, g, w_in, gains, w_out, hosts=None):
    T, D = x.shape
    nh, dv, G = _dil_dims(D)
    C = nh * HEAD_DIM
    W = 2 * G * C + nh * dv
    slopes = _slopes(G, nh)
    h = _rms_fwd(x, g, "dil_norm")
    proj = _mm(h, w_in, T, W, D, mode="nn", name="dil_proj", b_stack=N_CHIPS, out_dtypes=(F32,), hosts=hosts)
    qkv = _qkv_prep(proj, gains, 2 * G, lambda j: jnp.minimum(j, 2 * G - 1), C, "dil_qk_norm")
    parts = [_dil_fwd(qkv, gi, r, G, nh, dv, slopes[gi], f"dil_attn_g{gi}") for gi, (_, r) in enumerate(DIL_PATTERNS)]
    o, lse = _dil_merge([p[0] for p in parts], [p[1] for p in parts], [p[2] for p in parts], nh, dv, "dil_merge")
    y = _mm(o, w_out, T, D, D, mode="nn", name="dil_out", out_dtypes=(F32,), extras=(x,),
            epilogue=lambda r, res: (res + r,))
    return y, (x, h, proj, qkv, o, lse)


def _dil_layer_bwd(dy, dyb, saved, g, w_in, gains, w_out):
    x, h, proj, qkv, o, lse = saved
    T, D = x.shape
    nh, dv, G = _dil_dims(D)
    C = nh * HEAD_DIM
    W = 2 * G * C + nh * dv
    slopes = _slopes(G, nh)
    d_out = _mm(o, dyb, D, D, T, mode="tn", name="dil_dwout", out_dtypes=(F32,))
    do = _mm(dyb, w_out, T, D, D, mode="nt", name="dil_do", out_dtypes=(BF16,))
    dd = _row_dot(do, o, nh, dv, "dil_rowdot", True)
    dproj, dgs, dvs = None, [None] * (2 * G), []
    for gi, (_, r) in enumerate(DIL_PATTERNS):
        dq = _dil_dq(qkv, do, lse, dd, gi, r, G, nh, dv, slopes[gi], f"dil_dq_g{gi}")
        dk, dvg = _dil_dkv(qkv, do, lse, dd, gi, r, G, nh, dv, slopes[gi], f"dil_dkv_g{gi}")
        dvs.append(dvg)
        dproj, dgs[gi] = _head_rms_bwd_into(dproj, W, dq, proj, gains[gi], gi, C, f"dil_dq_norm_g{gi}")
        dproj, dgs[G + gi] = _head_rms_bwd_into(dproj, W, dk, proj, gains[G + gi], G + gi, C, f"dil_dk_norm_g{gi}")
    dproj = _sum_cast_into(dproj, W, dvs, 2 * G, C, "dil_dv_cast")
    d_in = _mm(h, dproj, D, W, T, mode="tn", name="dil_dwin", out_stack=N_CHIPS, out_dtypes=(F32,))
    dh = _mm(dproj, w_in, T, D, W, mode="nt", name="dil_dh", b_stack=N_CHIPS, out_dtypes=(F32,))
    dx, dxb, dg = _rms_bwd(dh, x, g, dy, "dil_dnorm")
    dgains = jnp.concatenate(dgs, axis=0)
    return dx, dxb, dg, d_in, dgains, d_out


class _Hosts:
    def __init__(self, riders, weights):
        self.riders, self.weights, self.carried, self.grads = dict(riders), weights, {}, {}

    def rider(self, name):
        make = self.riders.get(name)
        return make(self) if make is not None else None


def _local_step(x, tgt, w, hosts=None):
    g = hosts.grads if hosts is not None else {}
    y0, s_fox = _fox_layer_fwd(x, w["mix_g"][0], w["fox_qkv"], w["fox_f"], w["fox_b"], w["fox_gains"], w["fox_out"], hosts)
    if hosts is not None:
        w = {**w, **hosts.weights(hosts)}
    y1, s_mlp0 = _mlp_fwd(y0, w["mlp_g"][0], w["up0"], w["down0"], "mlp0", hosts)
    y2, s_dil = _dil_layer_fwd(y1, w["mix_g"][1], w["dil_in"], w["dil_gains"], w["dil_out"], hosts)
    if hosts is not None:
        w = {**w, **hosts.weights(hosts)}
    y3, s_mlp1 = _mlp_fwd(y2, w["mlp_g"][1], w["up1"], w["down1"], "mlp1")
    dy, dyb, loss = _loss_head(y3, tgt, "loss_head")
    dy, dyb, g_mlp1, g["up1"], g["down1"] = _mlp_bwd(dy, dyb, s_mlp1, w["mlp_g"][1], w["up1"], w["down1"], "mlp1")
    dy, dyb, g_mix1, g["dil_in"], g["dil_gains"], g["dil_out"] = _dil_layer_bwd(
        dy, dyb, s_dil, w["mix_g"][1], w["dil_in"], w["dil_gains"], w["dil_out"])
    dy, dyb, g_mlp0, g["up0"], g["down0"] = _mlp_bwd(dy, dyb, s_mlp0, w["mlp_g"][0], w["up0"], w["down0"], "mlp0", hosts)
    dy, dyb, g_mix0, g["fox_qkv"], g["fox_f"], g["fox_b"], g["fox_gains"], g["fox_out"] = _fox_layer_bwd(
        dy, dyb, s_fox, w["mix_g"][0], w["fox_qkv"], w["fox_f"], w["fox_b"], w["fox_gains"], w["fox_out"], hosts)
    g["mix_g"], g["mlp_g"] = (g_mix0, g_mix1), (g_mlp0, g_mlp1)
    return loss[0, 0], dy, g


def _place():
    x, y, c = lax.axis_index("x"), lax.axis_index("y"), lax.axis_index("c")
    chips = [(1 - x, y), (x, 1 - y), (1 - x, 1 - y)]
    return x, y, c, chips


def _remote(src, dst, send_sem, recv_sem, to):
    return pltpu.make_async_remote_copy(src_ref=src, dst_ref=dst, send_sem=send_sem, recv_sem=recv_sem,
                                        device_id=to, device_id_type=MESH)


def _gather_weights(shards):
    n = len(shards)

    def body(*refs):
        src, dst = refs[:n], refs[n:2 * n]
        send_sems, recv_sems, local_sems = refs[2 * n:]
        x, y, c, chips = _place()
        mine = 2 * x + y
        local = [pltpu.make_async_copy(src[t], dst[t].at[mine], local_sems.at[t]) for t in range(n)]
        for cp in local:
            cp.start()

        def half(t, slot, which):
            hr = shards[t].shape[0] // 2
            return dst[t].at[slot, pl.ds(which * hr, hr), :]

        def my_half(t):
            hr = shards[t].shape[0] // 2
            return src[t].at[pl.ds(c * hr, hr), :]

        sends = []
        for t in range(n):
            for j, (px, py) in enumerate(chips):
                cp = _remote(my_half(t), half(t, mine, c), send_sems.at[t, j], recv_sems.at[t, j], (px, py, c))
                cp.start()
                sends.append(cp)
        for j, (px, py) in enumerate(chips):
            for t in range(n):
                landed = half(t, 2 * px + py, c)
                _remote(landed, landed, send_sems.at[t, j], recv_sems.at[t, j], (px, py, c)).wait_recv()
                cp = _remote(landed, landed, send_sems.at[t, 3 + j], recv_sems.at[t, 3 + j], (x, y, 1 - c))
                cp.start()
                sends.append(cp)
        for j, (px, py) in enumerate(chips):
            for t in range(n):
                other = half(t, 2 * px + py, 1 - c)
                _remote(other, other, send_sems.at[t, 3 + j], recv_sems.at[t, 3 + j], (x, y, 1 - c)).wait_recv()
        for cp in sends:
            cp.wait_send()
        for cp in local:
            cp.wait()

    return pl.pallas_call(
        body, name="gather_weights",
        in_specs=[ANY] * n, out_specs=[ANY] * n,
        out_shape=[jax.ShapeDtypeStruct((N_CHIPS,) + s.shape, s.dtype) for s in shards],
        scratch_shapes=[pltpu.SemaphoreType.DMA((n, 6)), pltpu.SemaphoreType.DMA((n, 6)), pltpu.SemaphoreType.DMA((n,))],
        compiler_params=_params(has_side_effects=True),
    )(*shards)


class _Rider(NamedTuple):
    operands: tuple
    out_shapes: tuple
    scratch: tuple
    start: Callable
    finish: Callable
    aliases: dict = {}


def _gather_ici_rider(shards):
    n = len(shards)

    def copies(src, dst, sems):
        send_sems, recv_sems, local_sems = sems
        x, y, c, chips = _place()
        mine = 2 * x + y
        local, sends, recvs = [], [], []
        for t in range(n):
            hr = shards[t].shape[0] // 2
            local.append(pltpu.make_async_copy(src[t], dst[t].at[mine], local_sems.at[t]))
            for j, (px, py) in enumerate(chips):
                sends.append(_remote(src[t].at[pl.ds(c * hr, hr), :], dst[t].at[mine, pl.ds(c * hr, hr), :],
                                     send_sems.at[t, j], recv_sems.at[t, j], (px, py, c)))
                landed = dst[t].at[2 * px + py, pl.ds(c * hr, hr), :]
                recvs.append(_remote(landed, landed, send_sems.at[t, j], recv_sems.at[t, j], (px, py, c)))
        return local, sends, recvs

    def start(src, dst, sems):
        local, sends, _ = copies(src, dst, sems)
        for cp in local + sends:
            cp.start()

    def finish(src, dst, sems):
        local, sends, recvs = copies(src, dst, sems)
        for cp in recvs:
            cp.wait_recv()
        for cp in sends:
            cp.wait_send()
        for cp in local:
            cp.wait()

    return _Rider(tuple(shards), tuple(jax.ShapeDtypeStruct((N_CHIPS,) + s.shape, s.dtype) for s in shards),
                  (pltpu.SemaphoreType.DMA((n, 3)), pltpu.SemaphoreType.DMA((n, 3)), pltpu.SemaphoreType.DMA((n,))),
                  start, finish)


def _forward_rider(landed):
    n = len(landed)

    def copies(dst, sems):
        send_sems, recv_sems = sems
        x, y, c, chips = _place()
        sends, recvs = [], []
        for t in range(n):
            hr = landed[t].shape[1] // 2
            for j, (px, py) in enumerate(chips):
                got = dst[t].at[2 * px + py, pl.ds(c * hr, hr), :]
                other = dst[t].at[2 * px + py, pl.ds((1 - c) * hr, hr), :]
                sends.append(_remote(got, got, send_sems.at[t, j], recv_sems.at[t, j], (x, y, 1 - c)))
                recvs.append(_remote(other, other, send_sems.at[t, j], recv_sems.at[t, j], (x, y, 1 - c)))
        return sends, recvs

    def start(src, dst, sems):
        for cp in copies(dst, sems)[0]:
            cp.start()

    def finish(src, dst, sems):
        sends, recvs = copies(dst, sems)
        for cp in recvs:
            cp.wait_recv()
        for cp in sends:
            cp.wait_send()

    return _Rider(tuple(landed), tuple(jax.ShapeDtypeStruct(a.shape, a.dtype) for a in landed),
                  (pltpu.SemaphoreType.DMA((n, 3)), pltpu.SemaphoreType.DMA((n, 3))), start, finish,
                  {t: t for t in range(n)})


def _pair_exchange_rider(grads):
    n = len(grads)

    def copies(src, dst, sems):
        send_sems, recv_sems = sems
        x, y, c, _ = _place()
        return [_remote(src[t].at[:, pl.ds((1 - c) * (grads[t].shape[1] // 2), grads[t].shape[1] // 2), :], dst[t],
                        send_sems.at[t], recv_sems.at[t], (x, y, 1 - c)) for t in range(n)]

    def start(src, dst, sems):
        for cp in copies(src, dst, sems):
            cp.start()

    def finish(src, dst, sems):
        for cp in copies(src, dst, sems):
            cp.wait()

    return _Rider(tuple(grads),
                  tuple(jax.ShapeDtypeStruct((g.shape[0], g.shape[1] // 2, g.shape[2]), g.dtype) for g in grads),
                  (pltpu.SemaphoreType.DMA((n,)), pltpu.SemaphoreType.DMA((n,))), start, finish)


def _pair_add(g, got, cidx, name, tb=256):
    S, R, C = g.shape
    hr = R // 2
    tb = _rows_tile(hr, tb)
    nb = hr // tb

    def body(c_ref, a_ref, b_ref, o_ref):
        o_ref[...] = (a_ref[...] + b_ref[...]).astype(BF16)

    return pl.pallas_call(
        body, name=name,
        grid_spec=pltpu.PrefetchScalarGridSpec(
            num_scalar_prefetch=1, grid=(S, nb),
            in_specs=[pl.BlockSpec((None, tb, C), lambda s, i, c: (s, c[0] * nb + i, 0)),
                      pl.BlockSpec((None, tb, C), lambda s, i, c: (s, i, 0))],
            out_specs=pl.BlockSpec((None, tb, C), lambda s, i, c: (s, i, 0))),
        out_shape=jax.ShapeDtypeStruct((S, hr, C), BF16),
        compiler_params=_params("parallel", "parallel"),
    )(cidx, g, got)


def _rows_tile(n, want):
    t = min(n, want)
    while n % t or t % 8:
        t -= 8
    return t


def _chip_scatter_rider(sums):
    n = len(sums)

    def copies(src, dst, sems):
        send_sems, recv_sems, local_sems = sems
        x, y, c, chips = _place()
        mine = 2 * x + y
        local, sends, recvs = [], [], []
        for t in range(n):
            local.append(pltpu.make_async_copy(src[t].at[mine], dst[t].at[mine], local_sems.at[t]))
            for j, (px, py) in enumerate(chips):
                sends.append(_remote(src[t].at[2 * px + py], dst[t].at[mine], send_sems.at[t, j], recv_sems.at[t, j], (px, py, c)))
                slot = dst[t].at[2 * px + py]
                recvs.append(_remote(slot, slot, send_sems.at[t, j], recv_sems.at[t, j], (px, py, c)))
        return local, sends, recvs

    def start(src, dst, sems):
        local, sends, _ = copies(src, dst, sems)
        for cp in local + sends:
            cp.start()

    def finish(src, dst, sems):
        local, sends, recvs = copies(src, dst, sems)
        for cp in recvs:
            cp.wait_recv()
        for cp in sends:
            cp.wait_send()
        for cp in local:
            cp.wait()

    return _Rider(tuple(sums), tuple(jax.ShapeDtypeStruct(s.shape, s.dtype) for s in sums),
                  (pltpu.SemaphoreType.DMA((n, 3)), pltpu.SemaphoreType.DMA((n, 3)), pltpu.SemaphoreType.DMA((n,))),
                  start, finish)


def _run_rider(rider, name):
    r_in, r_out = len(rider.operands), len(rider.out_shapes)

    def body(*refs):
        src, dst, sems = refs[:r_in], refs[r_in:r_in + r_out], refs[r_in + r_out:]
        rider.start(src, dst, sems)
        rider.finish(src, dst, sems)

    return pl.pallas_call(
        body, name=name,
        in_specs=[ANY] * r_in, out_specs=[ANY] * r_out,
        out_shape=list(rider.out_shapes), scratch_shapes=list(rider.scratch),
        input_output_aliases=dict(rider.aliases),
        compiler_params=_params(has_side_effects=True),
    )(*rider.operands)


def _chip_sum(parts, cidx, name, tb=256):
    S, hr, C = parts.shape
    tb = _rows_tile(hr, tb)
    nb = hr // tb

    def body(c_ref, *refs):
        o_ref = refs[S]
        tot = refs[0][...].astype(F32)
        for s in range(1, S):
            tot = tot + refs[s][...].astype(F32)
        o_ref[...] = tot

    return pl.pallas_call(
        body, name=name,
        grid_spec=pltpu.PrefetchScalarGridSpec(
            num_scalar_prefetch=1, grid=(nb,),
            in_specs=[pl.BlockSpec((None, tb, C), functools.partial(lambda s, i, c: (s, i, 0), s)) for s in range(S)],
            out_specs=pl.BlockSpec((tb, C), lambda i, c: (c[0] * nb + i, 0))),
        out_shape=jax.ShapeDtypeStruct((2 * hr, C), F32),
        compiler_params=_params("parallel"),
    )(cidx, *([parts] * S))


def _half_exchange(halves):
    n = len(halves)

    def body(*refs):
        dst = refs[n:2 * n]
        send_sems, recv_sems = refs[2 * n:]
        x, y, c, _ = _place()
        cps = []
        for t in range(n):
            hr = halves[t].shape[0] // 2
            rows = dst[t].at[pl.ds(c * hr, hr), :]
            cp = _remote(rows, rows, send_sems.at[t], recv_sems.at[t], (x, y, 1 - c))
            cp.start()
            cps.append(cp)
        for t, cp in enumerate(cps):
            cp.wait_send()
            hr = halves[t].shape[0] // 2
            other = dst[t].at[pl.ds((1 - c) * hr, hr), :]
            _remote(other, other, send_sems.at[t], recv_sems.at[t], (x, y, 1 - c)).wait_recv()

    return pl.pallas_call(
        body, name="grad_half_exchange",
        in_specs=[ANY] * n, out_specs=[ANY] * n,
        out_shape=[jax.ShapeDtypeStruct(h.shape, h.dtype) for h in halves],
        input_output_aliases={t: t for t in range(n)},
        scratch_shapes=[pltpu.SemaphoreType.DMA((n,)), pltpu.SemaphoreType.DMA((n,))],
        compiler_params=_params(has_side_effects=True),
    )(*halves)


def _adamw_math(w, g, m, v):
    m = ADAM_B1 * m + (1.0 - ADAM_B1) * g
    v = ADAM_B2 * v + (1.0 - ADAM_B2) * (g * g)
    m_hat = m / (1.0 - ADAM_B1 ** ADAM_STEP)
    v_hat = v / (1.0 - ADAM_B2 ** ADAM_STEP)
    delta = -ADAM_LR * (m_hat / (jnp.sqrt(v_hat) + ADAM_EPS) + ADAM_WD * w)
    return delta, m, v


def _adamw(w, gs, m, v, name, tb=256):
    L, R, C = w.shape
    tb = _rows_tile(R, tb)
    nb = R // tb

    def body(w_ref, m_ref, v_ref, *rest):
        g_refs, (go_ref, d_ref, mo_ref, vo_ref) = rest[:L], rest[L:]
        layer = pl.program_id(0)
        for k in range(L):
            @pl.when(layer == k)
            def _(k=k):
                g = g_refs[k][...]
                d, mn, vn = _adamw_math(w_ref[...], g, m_ref[...], v_ref[...])
                go_ref[...] = g
                d_ref[...] = d
                mo_ref[...] = mn
                vo_ref[...] = vn

    def g_spec(k):
        return pl.BlockSpec((tb, C), lambda l, i: (jnp.where(l == k, i, jnp.where(l < k, 0, nb - 1)), 0))

    stacked = pl.BlockSpec((None, tb, C), lambda l, i: (l, i, 0))
    return pl.pallas_call(
        body, name=name, grid=(L, nb),
        in_specs=[stacked] * 3 + [g_spec(k) for k in range(L)], out_specs=[stacked] * 4,
        out_shape=[jax.ShapeDtypeStruct((L, R, C), F32)] * 4,
        compiler_params=_params("arbitrary", "arbitrary"),
    )(w, m, v, *gs)


N_DEV = 8


def _small_update(g, w, m, v):
    P = g.shape[0]

    def body(g_ref, w_ref, m_ref, v_ref, go_ref, d_ref, mo_ref, vo_ref, buf, send_sems, recv_sems):
        x, y, c, _ = _place()
        me = 4 * x + 2 * y + c
        buf[me] = g_ref[...]
        cps = []
        for k in range(1, N_DEV):
            fx, fy, fc = (k >> 2) & 1, (k >> 1) & 1, k & 1
            px = (1 - x) if fx else x
            py = (1 - y) if fy else y
            pc = (1 - c) if fc else c
            cp = _remote(g_ref, buf.at[me], send_sems.at[k - 1], recv_sems.at[k - 1], (px, py, pc))
            cp.start()
            cps.append((cp, 4 * px + 2 * py + pc))
        for k, (cp, peer) in enumerate(cps):
            _remote(g_ref, buf.at[peer], send_sems.at[k], recv_sems.at[k], (x, y, c)).wait_recv()
        for cp, _ in cps:
            cp.wait_send()
        tot = buf[0]
        for d in range(1, N_DEV):
            tot = tot + buf[d]
        go_ref[...] = tot
        dl, mn, vn = _adamw_math(w_ref[...], tot, m_ref[...], v_ref[...])
        d_ref[...] = dl
        mo_ref[...] = mn
        vo_ref[...] = vn

    vm = pl.BlockSpec(memory_space=pltpu.VMEM)
    return pl.pallas_call(
        body, name="small_params_update",
        in_specs=[vm] * 4, out_specs=[vm] * 4,
        out_shape=[jax.ShapeDtypeStruct((P, HEAD_DIM), F32)] * 4,
        scratch_shapes=[pltpu.VMEM((N_DEV, P, HEAD_DIM), F32), pltpu.SemaphoreType.DMA((N_DEV - 1,)),
                        pltpu.SemaphoreType.DMA((N_DEV - 1,))],
        compiler_params=_params(has_side_effects=True),
    )(g, w, m, v)


SMALL = ("fox_b_f", "fox_q_gain", "fox_k_gain", "dil_q_gain", "dil_k_gain", "mix_norm_g", "mlp_norm_g")
LARGE = ("fox_w_in", "fox_w_out", "dil_w_in", "dil_w_out", "mlp_w_up", "mlp_w_down")
WEIGHTS = ("fox_w_in", "fox_b_f", "fox_q_gain", "fox_k_gain", "fox_w_out", "dil_w_in", "dil_q_gain", "dil_k_gain",
           "dil_w_out", "mix_norm_g", "mlp_norm_g", "mlp_w_up", "mlp_w_down")


def _pack(parts):
    rows = []
    for a in parts:
        flat = a.reshape(-1)
        n = -(-flat.shape[0] // (8 * HEAD_DIM)) * (8 * HEAD_DIM)
        rows.append(jnp.pad(flat, (0, n - flat.shape[0])).reshape(-1, HEAD_DIM))
    return jnp.concatenate(rows, axis=0)


def _unpack(packed, like):
    out, r = [], 0
    for a in like:
        size = int(np.prod(a.shape))
        n = -(-size // (8 * HEAD_DIM)) * 8
        out.append(packed[r:r + n].reshape(-1)[:size].reshape(a.shape))
        r += n
    return out


def _pad_lanes(a):
    return jnp.pad(a, [(0, 0)] * (a.ndim - 1) + [(0, HEAD_DIM - a.shape[-1])])


def _as_shards(a):
    return a.reshape(N_CHIPS, a.shape[0] // N_CHIPS, a.shape[1])


def kernel(x, fox_w_in, fox_b_f, fox_q_gain, fox_k_gain, fox_w_out, dil_w_in, dil_q_gain, dil_k_gain, dil_w_out, mix_norm_g, mlp_norm_g, mlp_w_up, mlp_w_down, loss_target, m_fox_w_in, m_fox_b_f, m_fox_q_gain, m_fox_k_gain, m_fox_w_out, m_dil_w_in, m_dil_q_gain, m_dil_k_gain, m_dil_w_out, m_mix_norm_g, m_mlp_norm_g, m_mlp_w_up, m_mlp_w_down, v_fox_w_in, v_fox_b_f, v_fox_q_gain, v_fox_k_gain, v_fox_w_out, v_dil_w_in, v_dil_q_gain, v_dil_k_gain, v_dil_w_out, v_mix_norm_g, v_mlp_norm_g, v_mlp_w_up, v_mlp_w_down):
    wts = dict(fox_w_in=fox_w_in, fox_b_f=fox_b_f, fox_q_gain=fox_q_gain, fox_k_gain=fox_k_gain, fox_w_out=fox_w_out,
               dil_w_in=dil_w_in, dil_q_gain=dil_q_gain, dil_k_gain=dil_k_gain, dil_w_out=dil_w_out,
               mix_norm_g=mix_norm_g, mlp_norm_g=mlp_norm_g, mlp_w_up=mlp_w_up, mlp_w_down=mlp_w_down)
    mom1 = dict(fox_w_in=m_fox_w_in, fox_b_f=m_fox_b_f, fox_q_gain=m_fox_q_gain, fox_k_gain=m_fox_k_gain,
                fox_w_out=m_fox_w_out, dil_w_in=m_dil_w_in, dil_q_gain=m_dil_q_gain, dil_k_gain=m_dil_k_gain,
                dil_w_out=m_dil_w_out, mix_norm_g=m_mix_norm_g, mlp_norm_g=m_mlp_norm_g, mlp_w_up=m_mlp_w_up,
                mlp_w_down=m_mlp_w_down)
    mom2 = dict(fox_w_in=v_fox_w_in, fox_b_f=v_fox_b_f, fox_q_gain=v_fox_q_gain, fox_k_gain=v_fox_k_gain,
                fox_w_out=v_fox_w_out, dil_w_in=v_dil_w_in, dil_q_gain=v_dil_q_gain, dil_k_gain=v_dil_k_gain,
                dil_w_out=v_dil_w_out, mix_norm_g=v_mix_norm_g, mlp_norm_g=v_mlp_norm_g, mlp_w_up=v_mlp_w_up,
                mlp_w_down=v_mlp_w_down)
    T, D = x.shape[1], x.shape[2]
    H = D // HEAD_DIM
    cidx = lax.axis_index("c").astype(jnp.int32).reshape(1)

    def shards_of(d):
        return [d["fox_w_in"][0], d["fox_w_out"][0], d["dil_w_in"][0], d["dil_w_out"][0],
                d["mlp_w_up"][0], d["mlp_w_up"][1], d["mlp_w_down"][0], d["mlp_w_down"][1]]

    w_bf = [s.astype(BF16) for s in shards_of(wts)]
    first = _gather_weights(w_bf[:2])
    fox_in = jnp.moveaxis(first[0], 0, 1).reshape(D, -1)
    w = dict(
        fox_qkv=fox_in[:, :3 * D], fox_f=_pad_lanes(fox_in[:, 3 * D:]), fox_b=_pad_lanes(fox_b_f),
        fox_gains=jnp.stack([fox_q_gain, fox_k_gain]), fox_out=first[1].reshape(D, D),
        dil_gains=jnp.concatenate([dil_q_gain[0], dil_k_gain[0]])[:, None, :],
        mix_g=[mix_norm_g[0:1], mix_norm_g[1:2]], mlp_g=[mlp_norm_g[0:1], mlp_norm_g[1:2]])

    dil_in_sh, dil_out_sh, up0_sh, up1_sh, down0_sh, down1_sh = w_bf[2:]

    def gathered(hosts):
        got = {}
        if "fox_out" in hosts.carried:
            full = hosts.carried["fox_out"]
            got.update(dil_in=full[0], dil_out=full[1].reshape(D, D), up0=full[2], down0=full[3].reshape(-1, D))
        if "dil_proj" in hosts.carried:
            full = hosts.carried["dil_proj"]
            got.update(up1=full[0], down1=full[1].reshape(-1, D))
        return got

    def chip_sums(stacked, got, ts):
        return [_pair_add(a, b, cidx, f"grad_pair_add_{t}") for a, b, t in zip(stacked, got, ts)]

    def second_layer_grads(hosts):
        g = hosts.grads
        return [g["dil_in"], _as_shards(g["dil_out"]), g["up1"], _as_shards(g["down1"])]

    def first_mlp_grads(hosts):
        g = hosts.grads
        return [g["up0"], _as_shards(g["down0"])]

    def first_mixer_grads(hosts):
        if "first_mixer_grads" not in hosts.carried:
            g = hosts.grads
            g_in = jnp.concatenate([g["fox_qkv"], g["fox_f"][:, :H]], axis=1)
            hosts.carried["first_mixer_grads"] = [jnp.moveaxis(g_in.reshape(D, N_CHIPS, -1), 1, 0), _as_shards(g["fox_out"])]
        return hosts.carried["first_mixer_grads"]

    def early_scatter(hosts):
        sums1 = chip_sums(second_layer_grads(hosts), hosts.carried["mlp0_dwdown"], (2, 3, 5, 7))
        sums0 = chip_sums(first_mlp_grads(hosts), hosts.carried["fox_dwout"], (4, 6))
        return _chip_scatter_rider([sums1[0], sums1[1], sums0[0], sums1[2], sums0[1], sums1[3]])

    hosts = _Hosts({
        "fox_attn": lambda h: _gather_ici_rider([dil_in_sh, dil_out_sh, up0_sh, down0_sh]),
        "fox_out": lambda h: _forward_rider(h.carried["fox_attn"]),
        "mlp0_up": lambda h: _gather_ici_rider([up1_sh]),
        "mlp0_down": lambda h: _gather_ici_rider([down1_sh]),
        "dil_proj": lambda h: _forward_rider(h.carried["mlp0_up"] + h.carried["mlp0_down"]),
        "mlp0_dwdown": lambda h: _pair_exchange_rider(second_layer_grads(h)),
        "fox_dwout": lambda h: _pair_exchange_rider(first_mlp_grads(h)),
        "fox_attn_bwd": early_scatter,
        "fox_dh": lambda h: _pair_exchange_rider(first_mixer_grads(h)),
    }, gathered)
    loss, grad_x, g = _local_step(x.reshape(T, D), loss_target.reshape(T, D), w, hosts)
    loss = lax.psum(loss, ("x", "y", "c"))

    late_sums = chip_sums(first_mixer_grads(hosts), hosts.carried["fox_dh"], (0, 1))
    parts = list(_run_rider(_chip_scatter_rider(late_sums), "grad_chip_scatter_late")) + hosts.carried["fox_attn_bwd"]
    halves = [_chip_sum(p, cidx, f"grad_chip_sum_{t}") for t, p in enumerate(parts)]
    totals = _half_exchange(halves)
    layers = dict(fox_w_in=[0], fox_w_out=[1], dil_w_in=[2], dil_w_out=[3], mlp_w_up=[4, 5], mlp_w_down=[6, 7])
    upd = {n: _adamw(wts[n], [totals[t] for t in ts], mom1[n], mom2[n], f"adamw_{n}") for n, ts in layers.items()}

    def large(k):
        return {n: upd[n][k] for n in LARGE}

    small_like = [wts[n] for n in SMALL]
    g_small = [g["fox_b"][:, :H], g["fox_gains"][0], g["fox_gains"][1], g["dil_gains"][:3, 0][None], g["dil_gains"][3:, 0][None],
               jnp.concatenate(g["mix_g"]), jnp.concatenate(g["mlp_g"])]
    packed = _small_update(_pack(g_small), _pack(small_like), _pack([mom1[n] for n in SMALL]), _pack([mom2[n] for n in SMALL]))
    small = [dict(zip(SMALL, _unpack(p, small_like))) for p in packed]

    outs = [loss, grad_x.reshape(x.shape)]
    for k in range(4):
        big = large(k)
        outs += [big[n] if n in big else small[k][n] for n in WEIGHTS]
    return tuple(outs)
```

```python
import functools
from typing import Callable, NamedTuple

import numpy as np
import jax
import jax.numpy as jnp
from jax import lax
from jax.experimental import pallas as pl
from jax.experimental.pallas import tpu as pltpu

F32 = jnp.float32
BF16 = jnp.bfloat16

HEAD_DIM = 128
DIL_PATTERNS = ((128, 1), (512, 4), (2048, 16))
DIL_SPAN = 128
ALIBI_MAX_EXP = 8.0
EPS = 1e-6
MASKED = -1e30

ADAM_LR = 0.001
ADAM_B1 = 0.9
ADAM_B2 = 0.999
ADAM_EPS = 1e-08
ADAM_WD = 0.01
ADAM_STEP = 10

N_CHIPS = 4
VMEM_LIMIT_BYTES = 56 * 1024 * 1024
MESH = pl.DeviceIdType.MESH
ANY = pl.BlockSpec(memory_space=pl.ANY)

NN = (((1,), (0,)), ((), ()))
NT = (((1,), (1,)), ((), ()))
TN = (((0,), (0,)), ((), ()))


def _params(*sem, **kw):
    return pltpu.CompilerParams(dimension_semantics=sem or None, vmem_limit_bytes=VMEM_LIMIT_BYTES, **kw)


def _dot(a, b, dims):
    return lax.dot_general(a, b, dims, preferred_element_type=F32)


def _tile(n, want):
    if n <= want:
        return n
    t = want - want % 128
    while n % t:
        t -= 128
    return t


def _mm(a, b, M, N, K, *, mode, name, out_dtypes, b_stack=0, out_stack=0, extras=(), epilogue=None,
        tm=1024, tn=1024, tk=2048, hosts=None):
    rider = hosts.rider(name) if hosts is not None else None
    per_b = per_o = None
    if b_stack:
        per_b = (K if mode == "nt" else N) // b_stack
    if out_stack:
        per_o = N // out_stack
    tm = _tile(M, tm)
    tn = _tile(min(x for x in (N, per_o, per_b if mode != "nt" else None) if x), tn)
    tk = _tile(min(x for x in (K, per_b if mode == "nt" else None) if x), tk)
    assert M % tm == 0 and N % tn == 0 and K % tk == 0, (name, M, N, K, tm, tn, tk)
    gk = K // tk
    if mode == "tn":
        a_spec = pl.BlockSpec((tk, tm), lambda i, j, k: (k, i))
    else:
        a_spec = pl.BlockSpec((tm, tk), lambda i, j, k: (i, k))
    if mode == "nt":
        if b_stack:
            npk = per_b // tk
            b_spec = pl.BlockSpec((None, tn, tk), lambda i, j, k: (k // npk, j, k % npk))
        else:
            b_spec = pl.BlockSpec((tn, tk), lambda i, j, k: (j, k))
    else:
        if b_stack:
            npj = per_b // tn
            b_spec = pl.BlockSpec((None, tk, tn), lambda i, j, k: (j // npj, k, j % npj))
        else:
            b_spec = pl.BlockSpec((tk, tn), lambda i, j, k: (k, j))
    if out_stack:
        npo = per_o // tn
        o_spec = pl.BlockSpec((None, tm, tn), lambda i, j, k: (j // npo, i, j % npo))
        o_shape = (out_stack, M, per_o)
    else:
        o_spec = pl.BlockSpec((tm, tn), lambda i, j, k: (i, j))
        o_shape = (M, N)
    e_spec = pl.BlockSpec((tm, tn), lambda i, j, k: (i, j))
    dims = {"nn": NN, "nt": NT, "tn": TN}[mode]
    ne, no = len(extras), len(out_dtypes)
    gi, gj = M // tm, N // tn
    r_in, r_out = (len(rider.operands), len(rider.out_shapes)) if rider else (0, 0)
    n_acc = 1 if gk > 1 else 0

    def body(a_ref, b_ref, *rest):
        ex, r_src = rest[:ne], rest[ne:ne + r_in]
        outs, r_dst = rest[ne + r_in:ne + r_in + no], rest[ne + r_in + no:ne + r_in + no + r_out]
        r_sems = rest[ne + r_in + no + r_out + n_acc:]
        i, j, k = pl.program_id(0), pl.program_id(1), pl.program_id(2)
        if rider:
            @pl.when((i == 0) & (j == 0) & (k == 0))
            def _():
                rider.start(r_src, r_dst, r_sems)

        def product():
            return _dot(a_ref[...].astype(BF16), b_ref[...].astype(BF16), dims)

        def finish(r):
            res = epilogue(r, *[e[...] for e in ex]) if epilogue is not None else (r,)
            for o, v in zip(outs, res):
                o[...] = v.astype(o.dtype)

        if gk == 1:
            finish(product())
        else:
            acc = rest[ne + r_in + no + r_out]

            @pl.when(k == 0)
            def _():
                acc[...] = product()

            @pl.when((k > 0) & (k < gk - 1))
            def _():
                acc[...] += product()

            @pl.when(k == gk - 1)
            def _():
                finish(acc[...] + product())

        if rider:
            @pl.when((i == gi - 1) & (j == gj - 1) & (k == gk - 1))
            def _():
                rider.finish(r_src, r_dst, r_sems)

    outs = pl.pallas_call(
        body, name=name,
        grid=(gi, gj, gk),
        in_specs=[a_spec, b_spec] + [e_spec] * ne + [ANY] * r_in,
        out_specs=[o_spec] * no + [ANY] * r_out,
        out_shape=[jax.ShapeDtypeStruct(o_shape, d) for d in out_dtypes] + (list(rider.out_shapes) if rider else []),
        scratch_shapes=([pltpu.VMEM((tm, tn), F32)] if gk > 1 else []) + (list(rider.scratch) if rider else []),
        input_output_aliases={2 + ne + s: no + d for s, d in rider.aliases.items()} if rider else {},
        compiler_params=(_params("arbitrary", "arbitrary", "arbitrary", has_side_effects=True) if rider
                         else _params("parallel", "parallel", "arbitrary")),
    )(a, b, *extras, *(rider.operands if rider else ()))
    if rider:
        hosts.carried[name] = list(outs[no:])
    return outs[0] if no == 1 else outs[:no]


def _rms_fwd(x, g, name, tb=512):
    T, D = x.shape
    tb = min(tb, T)

    def body(x_ref, g_ref, o_ref):
        xv = x_ref[...]
        r = lax.rsqrt(jnp.mean(xv * xv, axis=-1, keepdims=True) + EPS)
        o_ref[...] = (xv * r * g_ref[...]).astype(BF16)

    return pl.pallas_call(
        body, name=name, grid=(T // tb,),
        in_specs=[pl.BlockSpec((tb, D), lambda i: (i, 0)), pl.BlockSpec((1, D), lambda i: (0, 0))],
        out_specs=pl.BlockSpec((tb, D), lambda i: (i, 0)),
        out_shape=jax.ShapeDtypeStruct((T, D), BF16),
        compiler_params=_params("parallel"),
    )(x, g)


def _rms_bwd(dy, x, g, dres, name, tb=256, hosts=None):
    rider = hosts.rider(name) if hosts is not None else None
    r_in, r_out = (len(rider.operands), len(rider.out_shapes)) if rider else (0, 0)
    T, D = x.shape
    tb = min(tb, T)
    steps = T // tb

    def body(dy_ref, x_ref, g_ref, dres_ref, *rest):
        r_src, (dx_ref, dxb_ref, dg_ref) = rest[:r_in], rest[r_in:r_in + 3]
        r_dst, r_sems = rest[r_in + 3:r_in + 3 + r_out], rest[r_in + 3 + r_out:]
        i = pl.program_id(0)
        if rider:
            @pl.when(i == 0)
            def _():
                rider.start(r_src, r_dst, r_sems)

        xv, dyv = x_ref[...], dy_ref[...]
        r = lax.rsqrt(jnp.mean(xv * xv, axis=-1, keepdims=True) + EPS)
        gy = dyv * g_ref[...]
        dx = r * gy - xv * (r * r * r) * jnp.mean(gy * xv, axis=-1, keepdims=True)
        tot = dres_ref[...] + dx
        dx_ref[...] = tot
        dxb_ref[...] = tot.astype(BF16)
        part = jnp.sum(dyv * (xv * r), axis=0, keepdims=True)

        @pl.when(i == 0)
        def _():
            dg_ref[...] = part

        @pl.when(i > 0)
        def _():
            dg_ref[...] += part

        if rider:
            @pl.when(i == steps - 1)
            def _():
                rider.finish(r_src, r_dst, r_sems)

    row = pl.BlockSpec((tb, D), lambda i: (i, 0))
    vec = pl.BlockSpec((1, D), lambda i: (0, 0))
    outs = pl.pallas_call(
        body, name=name, grid=(steps,),
        in_specs=[row, row, vec, row] + [ANY] * r_in,
        out_specs=[row, row, vec] + [ANY] * r_out,
        out_shape=[jax.ShapeDtypeStruct((T, D), F32), jax.ShapeDtypeStruct((T, D), BF16),
                   jax.ShapeDtypeStruct((1, D), F32)] + (list(rider.out_shapes) if rider else []),
        scratch_shapes=list(rider.scratch) if rider else [],
        compiler_params=_params("arbitrary", has_side_effects=bool(rider)),
    )(dy, x, g, dres, *(rider.operands if rider else ()))
    if rider:
        hosts.carried[name] = list(outs[3:])
    return outs[:3]


def _loss_head(y, tgt, name, tb=256):
    T, D = y.shape
    tb = min(tb, T)

    def body(y_ref, t_ref, dy_ref, dyb_ref, loss_ref):
        i = pl.program_id(0)
        e = y_ref[...] - t_ref[...]
        d = e * (1.0 / D)
        dy_ref[...] = d
        dyb_ref[...] = d.astype(BF16)
        part = 0.5 * jnp.sum(jnp.sum(e * e, axis=1, keepdims=True) * (1.0 / D), axis=0, keepdims=True)

        @pl.when(i == 0)
        def _():
            loss_ref[...] = part

        @pl.when(i > 0)
        def _():
            loss_ref[...] += part

    row = pl.BlockSpec((tb, D), lambda i: (i, 0))
    return pl.pallas_call(
        body, name=name, grid=(T // tb,),
        in_specs=[row, row],
        out_specs=[row, row, pl.BlockSpec((1, 1), lambda i: (0, 0))],
        out_shape=[jax.ShapeDtypeStruct((T, D), F32), jax.ShapeDtypeStruct((T, D), BF16),
                   jax.ShapeDtypeStruct((1, 1), F32)],
        compiler_params=_params("arbitrary"),
    )(y, tgt)


def _head_rms(xh, g):
    r = lax.rsqrt(jnp.mean(xh * xh, axis=-1, keepdims=True) + EPS)
    return xh * r * g


def _qkv_prep(proj, gains, n_norm, gain_row, ch, name, n_scaled=0, post_scale=1.0, tb=512):
    T, W = proj.shape
    tb = min(tb, T)
    nch = W // ch
    nh = ch // HEAD_DIM

    def body(p_ref, g_ref, o_ref):
        j = pl.program_id(0)

        @pl.when(j < n_norm)
        def _():
            g = g_ref[...]
            if n_scaled:
                g = g * jnp.where(j < n_scaled, post_scale, 1.0)
            for h in range(nh):
                sl = slice(h * HEAD_DIM, (h + 1) * HEAD_DIM)
                o_ref[:, sl] = _head_rms(p_ref[:, sl], g).astype(BF16)

        @pl.when(j >= n_norm)
        def _():
            o_ref[...] = p_ref[...].astype(BF16)

    return pl.pallas_call(
        body, name=name, grid=(nch, T // tb),
        in_specs=[pl.BlockSpec((tb, ch), lambda j, i: (i, j)),
                  pl.BlockSpec((None, 1, HEAD_DIM), lambda j, i: (gain_row(j), 0, 0))],
        out_specs=pl.BlockSpec((tb, ch), lambda j, i: (i, j)),
        out_shape=jax.ShapeDtypeStruct((T, W), BF16),
        compiler_params=_params("parallel", "parallel"),
    )(proj, gains)


def _into(body, name, grid, in_specs, out_spec, out_shape, extra_out_specs, extra_out_shapes, buf, operands, sem):
    if buf is None:
        def kernel(*refs):
            body(*refs)
        ins, alias, ops = in_specs, {}, operands
    else:
        def kernel(_, *refs):
            body(*refs)
        ins = [pl.BlockSpec(memory_space=pl.ANY)] + in_specs
        alias, ops = {0: 0}, (buf,) + tuple(operands)
    return pl.pallas_call(
        kernel, name=name, grid=grid, in_specs=ins,
        out_specs=[out_spec] + extra_out_specs,
        out_shape=[out_shape] + extra_out_shapes,
        input_output_aliases=alias,
        compiler_params=_params(*sem),
    )(*ops)


def _head_rms_bwd_into(buf, W, d, proj, gain, off, ch, name, in_scale=1.0, tb=256):
    T, wd = d.shape
    tb = min(tb, T)
    n = wd // ch
    nh = ch // HEAD_DIM

    def body(d_ref, p_ref, g_ref, o_ref, dg_ref):
        i = pl.program_id(1)
        g = g_ref[...]
        part = jnp.zeros((1, HEAD_DIM), F32)
        for h in range(nh):
            sl = slice(h * HEAD_DIM, (h + 1) * HEAD_DIM)
            xh, dy = p_ref[:, sl], d_ref[:, sl]
            if in_scale != 1.0:
                dy = dy * in_scale
            r = lax.rsqrt(jnp.mean(xh * xh, axis=-1, keepdims=True) + EPS)
            gy = dy * g
            dx = r * gy - xh * (r * r * r) * jnp.mean(gy * xh, axis=-1, keepdims=True)
            o_ref[:, sl] = dx.astype(BF16)
            part = part + jnp.sum(dy * (xh * r), axis=0, keepdims=True)

        @pl.when(i == 0)
        def _():
            dg_ref[...] = part

        @pl.when(i > 0)
        def _():
            dg_ref[...] += part

    return _into(
        body, name, (n, T // tb),
        [pl.BlockSpec((tb, ch), lambda j, i: (i, j)), pl.BlockSpec((tb, ch), lambda j, i: (i, off + j)),
         pl.BlockSpec((1, HEAD_DIM), lambda j, i: (0, 0))],
        pl.BlockSpec((tb, ch), lambda j, i: (i, off + j)), jax.ShapeDtypeStruct((T, W), BF16),
        [pl.BlockSpec((None, 1, HEAD_DIM), lambda j, i: (j, 0, 0))], [jax.ShapeDtypeStruct((n, 1, HEAD_DIM), F32)],
        buf, (d, proj, gain), ("parallel", "arbitrary"))


def _sum_cast_into(buf, W, srcs, off, ch, name, tb=256):
    T, wd = srcs[0].shape
    tb = min(tb, T)
    n = wd // ch
    ns = len(srcs)

    def body(*refs):
        o_ref = refs[ns]
        tot = refs[0][...]
        for s in refs[1:ns]:
            tot = tot + s[...]
        o_ref[...] = tot.astype(BF16)

    out = _into(
        body, name, (n, T // tb),
        [pl.BlockSpec((tb, ch), lambda j, i: (i, j))] * ns,
        pl.BlockSpec((tb, ch), lambda j, i: (i, off + j)), jax.ShapeDtypeStruct((T, W), BF16),
        [], [], buf, tuple(srcs), ("parallel", "parallel"))
    return out[0]


def _tri(n, lower):
    r = lax.broadcasted_iota(jnp.int32, (n, n), 0)
    c = lax.broadcasted_iota(jnp.int32, (n, n), 1)
    return jnp.where((c <= r) if lower else (c >= r), 1.0, 0.0).astype(F32)


def _dot_exact(a, b):
    return lax.dot_general(a, b, NN, precision=lax.Precision.HIGHEST, preferred_element_type=F32)


def _log_sigmoid(z):
    return jnp.minimum(z, 0.0) - jnp.log(1.0 + jnp.exp(-jnp.abs(z)))


def _gate_fwd(f_raw, b_pad, hp, out_scale, name, blk=256):
    T = f_raw.shape[0]
    blk = min(blk, T)

    def body(f_ref, b_ref, c_ref):
        tri = _tri(blk, True)
        carry = jnp.zeros((1, HEAD_DIM), F32)
        for j in range(T // blk):
            lf = _log_sigmoid(f_ref[j * blk:(j + 1) * blk, :] + b_ref[...])
            cb = _dot_exact(tri, lf) + carry
            carry = cb[blk - 1:blk, :]
            c_ref[:, j * blk:(j + 1) * blk] = cb.T[:hp, :] * out_scale

    return pl.pallas_call(
        body, name=name,
        in_specs=[pl.BlockSpec(memory_space=pltpu.VMEM)] * 2,
        out_specs=pl.BlockSpec(memory_space=pltpu.VMEM),
        out_shape=jax.ShapeDtypeStruct((hp, T), F32),
        compiler_params=_params(),
    )(f_raw, b_pad)


def _gate_bwd(dc_rows, dc_cols, f_raw, b_pad, n_heads, hp, name, blk=256):
    T = f_raw.shape[0]
    blk = min(blk, T)

    def body(dc_ref, dcc_ref, f_ref, b_ref, dz_ref, db_ref):
        tri = _tri(blk, False)
        lane = lax.broadcasted_iota(jnp.int32, (blk, HEAD_DIM), 1)
        carry = jnp.zeros((1, HEAD_DIM), F32)
        db = jnp.zeros((1, HEAD_DIM), F32)
        for j in reversed(range(T // blk)):
            rows = dc_ref[:, j * blk:(j + 1) * blk]
            if hp < HEAD_DIM:
                rows = jnp.concatenate([rows, jnp.zeros((HEAD_DIM - hp, blk), F32)], axis=0)
            dlf = _dot_exact(tri, rows.T + dcc_ref[j * blk:(j + 1) * blk, :]) + carry
            carry = dlf[0:1, :]
            z = f_ref[j * blk:(j + 1) * blk, :] + b_ref[...]
            dz = jnp.where(lane < n_heads, dlf / (1.0 + jnp.exp(z)), 0.0)
            dz_ref[j * blk:(j + 1) * blk, :] = dz.astype(BF16)
            db = db + jnp.sum(dz, axis=0, keepdims=True)
        db_ref[...] = db

    return pl.pallas_call(
        body, name=name,
        in_specs=[pl.BlockSpec(memory_space=pltpu.VMEM)] * 4,
        out_specs=[pl.BlockSpec(memory_space=pltpu.VMEM)] * 2,
        out_shape=[jax.ShapeDtypeStruct((T, HEAD_DIM), BF16), jax.ShapeDtypeStruct((1, HEAD_DIM), F32)],
        compiler_params=_params(),
    )(dc_rows, dc_cols, f_raw, b_pad)


def _pairs(nb, key_major):
    if key_major:
        pairs = [(qi, ki) for ki in range(nb) for qi in range(ki, nb)]
    else:
        pairs = [(qi, ki) for qi in range(nb) for ki in range(qi + 1)]
    return (jnp.asarray(np.array([p[0] for p in pairs], np.int32)),
            jnp.asarray(np.array([p[1] for p in pairs], np.int32)))


LOG2E = 1.4426950408889634
LN2 = 0.6931471805599453
FOX_Q_SCALE = HEAD_DIM ** -0.5 * LOG2E


def _fox_logits(q, k, ck_row, diagonal):
    s = _dot(q, k, NT) - ck_row
    if diagonal:
        row = lax.broadcasted_iota(jnp.int32, s.shape, 0)
        col = lax.broadcasted_iota(jnp.int32, s.shape, 1)
        s = jnp.where(col <= row, s, MASKED)
    return s


def _fox_fwd(qkv, ck, H, name, tb=1024, hs=4, hosts=None):
    rider = hosts.rider(name) if hosts is not None else None
    T = qkv.shape[0]
    tb = min(tb, T)
    nb = T // tb
    qt, kt = _pairs(nb, False)
    n_pairs = int(qt.shape[0])
    hb = H // hs
    r_in, r_out = (len(rider.operands), len(rider.out_shapes)) if rider else (0, 0)

    def body(qt_ref, kt_ref, q_ref, k_ref, v_ref, ck_ref, *rest):
        r_src, (o_ref, lse_ref), r_dst = rest[:r_in], rest[r_in:r_in + 2], rest[r_in + 2:r_in + 2 + r_out]
        m_sc, l_sc, acc_sc = rest[r_in + 2 + r_out:r_in + 5 + r_out]
        r_sems = rest[r_in + 5 + r_out:]
        p_ = pl.program_id(1)
        qi, ki = qt_ref[p_], kt_ref[p_]
        if rider:
            @pl.when((pl.program_id(0) == 0) & (p_ == 0))
            def _():
                rider.start(r_src, r_dst, r_sems)

        @pl.when(ki == 0)
        def _():
            m_sc[...] = jnp.full_like(m_sc, MASKED)
            l_sc[...] = jnp.zeros_like(l_sc)
            acc_sc[...] = jnp.zeros_like(acc_sc)

        heads = [(hh, slice(hh * HEAD_DIM, (hh + 1) * HEAD_DIM)) for hh in range(hs)]

        def tile(diagonal):
            for hh, sl in heads:
                s = _fox_logits(q_ref[:, sl], k_ref[:, sl], ck_ref[hh], diagonal)
                m_prev = m_sc[hh]
                m_new = jnp.maximum(m_prev, jnp.max(s, axis=1, keepdims=True))
                alpha = jnp.exp2(m_prev - m_new)
                p = jnp.exp2(s - m_new[:, :1])
                l_sc[hh] = alpha * l_sc[hh] + jnp.sum(p, axis=1, keepdims=True)
                acc_sc[hh] = alpha * acc_sc[hh] + _dot(p.astype(BF16), v_ref[:, sl], NN)
                m_sc[hh] = m_new

        @pl.when(ki < qi)
        def _():
            tile(False)

        @pl.when(ki == qi)
        def _():
            tile(True)
            for hh, sl in heads:
                o_ref[:, sl] = (acc_sc[hh] / l_sc[hh]).astype(BF16)
                lse_ref[:, sl] = m_sc[hh] + jnp.log(l_sc[hh]) * LOG2E

        if rider:
            @pl.when((pl.program_id(0) == hb - 1) & (p_ == n_pairs - 1))
            def _():
                rider.finish(r_src, r_dst, r_sems)

    blk = lambda f: pl.BlockSpec((tb, hs * HEAD_DIM), f)
    outs = pl.pallas_call(
        body, name=name,
        grid_spec=pltpu.PrefetchScalarGridSpec(
            num_scalar_prefetch=2, grid=(hb, n_pairs),
            in_specs=[blk(lambda h, p, qt, kt: (qt[p], h)),
                      blk(lambda h, p, qt, kt: (kt[p], hb + h)),
                      blk(lambda h, p, qt, kt: (kt[p], 2 * hb + h)),
                      pl.BlockSpec((hs, 1, tb), lambda h, p, qt, kt: (h, 0, kt[p]))] + [ANY] * r_in,
            out_specs=[blk(lambda h, p, qt, kt: (qt[p], h)), blk(lambda h, p, qt, kt: (qt[p], h))] + [ANY] * r_out,
            scratch_shapes=[pltpu.VMEM((hs, tb, HEAD_DIM), F32)] * 3 + (list(rider.scratch) if rider else [])),
        out_shape=[jax.ShapeDtypeStruct((T, H * HEAD_DIM), BF16), jax.ShapeDtypeStruct((T, H * HEAD_DIM), F32)]
        + (list(rider.out_shapes) if rider else []),
        compiler_params=_params("arbitrary", "arbitrary", has_side_effects=bool(rider)),
    )(qt, kt, qkv, qkv, qkv, ck, *(rider.operands if rider else ()))
    if rider:
        hosts.carried[name] = list(outs[2:])
    return outs[0], outs[1]


def _row_dot(do, o, nh, width, name, lane_per_head, tb=256):
    T = do.shape[0]
    tb = min(tb, T)
    wout = HEAD_DIM if lane_per_head else nh * HEAD_DIM

    def body(do_ref, o_ref, d_ref):
        lane = lax.broadcasted_iota(jnp.int32, (tb, HEAD_DIM), 1)
        tile = jnp.zeros((tb, HEAD_DIM), F32)
        for h in range(nh):
            sl = slice(h * width, (h + 1) * width)
            d = jnp.sum(do_ref[:, sl].astype(F32) * o_ref[:, sl].astype(F32), axis=1, keepdims=True)
            if lane_per_head:
                tile = jnp.where(lane == h, d, tile)
            else:
                d_ref[:, h * HEAD_DIM:(h + 1) * HEAD_DIM] = jnp.broadcast_to(d, (tb, HEAD_DIM))
        if lane_per_head:
            d_ref[...] = tile

    row = pl.BlockSpec((tb, nh * width), lambda i: (i, 0))
    return pl.pallas_call(
        body, name=name, grid=(T // tb,),
        in_specs=[row, row], out_specs=pl.BlockSpec((tb, wout), lambda i: (i, 0)),
        out_shape=jax.ShapeDtypeStruct((T, wout), F32),
        compiler_params=_params("parallel"),
    )(do, o)


def _fox_bwd(qkv, do, ck, lse, dd, H, hp, name, tb=1024, hs=2, hosts=None):
    rider = hosts.rider(name) if hosts is not None else None
    T = qkv.shape[0]
    tb = min(tb, T)
    nb = T // tb
    qt, kt = _pairs(nb, True)
    n_pairs = int(qt.shape[0])
    r_in, r_out = (len(rider.operands), len(rider.out_shapes)) if rider else (0, 0)
    hb = H // hs

    def body(qt_ref, kt_ref, q_ref, k_ref, v_ref, do_ref, ck_ref, lse_ref, dd_ref, *rest):
        r_src, r_dst, r_sems = rest[:r_in], rest[r_in + 5:r_in + 5 + r_out], rest[r_in + 5 + r_out:]
        dq_ref, dk_ref, dv_ref, dc_ref, dcq_ref = rest[r_in:r_in + 5]
        p_ = pl.program_id(1)
        qi, ki = qt_ref[p_], kt_ref[p_]
        if rider:
            @pl.when((pl.program_id(0) == 0) & (p_ == 0))
            def _():
                rider.start(r_src, r_dst, r_sems)

        @pl.when(p_ == 0)
        def _():
            dq_ref[...] = jnp.zeros_like(dq_ref)
            dcq_ref[...] = jnp.zeros_like(dcq_ref)

        @pl.when(qi == ki)
        def _():
            dk_ref[...] = jnp.zeros_like(dk_ref)
            dv_ref[...] = jnp.zeros_like(dv_ref)
            dc_ref[...] = jnp.zeros_like(dc_ref)

        rows = pl.ds(pl.multiple_of(qi * tb, tb), tb)

        def tile(diagonal):
            for hh in range(hs):
                sl = slice(hh * HEAD_DIM, (hh + 1) * HEAD_DIM)
                s = _fox_logits(q_ref[:, sl], k_ref[:, sl], ck_ref[hh], diagonal)
                p = jnp.exp2(s - lse_ref[:, hh * HEAD_DIM:hh * HEAD_DIM + 1])
                dv_ref[:, sl] += _dot(p.astype(BF16), do_ref[:, sl], TN)
                dp = _dot(do_ref[:, sl], v_ref[:, sl], NT)
                ds = p * (dp - dd_ref[:, hh * HEAD_DIM:hh * HEAD_DIM + 1])
                dc_ref[hh] -= jnp.sum(ds, axis=0, keepdims=True)
                dcq_ref[rows, sl] += jnp.sum(ds, axis=1, keepdims=True)
                dsb = ds.astype(BF16)
                dq_ref[rows, sl] += _dot(dsb, k_ref[:, sl], NN)
                dk_ref[:, sl] += _dot(dsb, q_ref[:, sl], TN)

        @pl.when(ki < qi)
        def _():
            tile(False)

        @pl.when(ki == qi)
        def _():
            tile(True)

        if rider:
            @pl.when((pl.program_id(0) == hb - 1) & (p_ == n_pairs - 1))
            def _():
                rider.finish(r_src, r_dst, r_sems)

    blk = lambda f: pl.BlockSpec((tb, hs * HEAD_DIM), f)
    at_q = lambda h, p, qt, kt: (qt[p], h)
    at_k = lambda h, p, qt, kt: (kt[p], h)
    crow = pl.BlockSpec((hs, 1, tb), lambda h, p, qt, kt: (h, 0, kt[p]))
    whole = pl.BlockSpec((T, hs * HEAD_DIM), lambda h, p, qt, kt: (0, h), pipeline_mode=pl.Buffered(1))
    wide = jax.ShapeDtypeStruct((T, H * HEAD_DIM), F32)
    outs = pl.pallas_call(
        body, name=name,
        grid_spec=pltpu.PrefetchScalarGridSpec(
            num_scalar_prefetch=2, grid=(hb, n_pairs),
            in_specs=[blk(at_q),
                      blk(lambda h, p, qt, kt: (kt[p], hb + h)),
                      blk(lambda h, p, qt, kt: (kt[p], 2 * hb + h)),
                      blk(at_q), crow, blk(at_q), blk(at_q)] + [ANY] * r_in,
            out_specs=[whole, blk(at_k), blk(at_k), crow, whole] + [ANY] * r_out,
            scratch_shapes=list(rider.scratch) if rider else []),
        out_shape=[wide, wide, wide, jax.ShapeDtypeStruct((hp, 1, T), F32), wide] + (list(rider.out_shapes) if rider else []),
        compiler_params=_params("arbitrary", "arbitrary", has_side_effects=bool(rider)),
    )(qt, kt, qkv, qkv, qkv, do, ck, lse, dd, *(rider.operands if rider else ()))
    if rider:
        hosts.carried[name] = list(outs[5:])
    return outs[:5]


def _lane_per_head(wide, H, name, tb=256):
    T = wide.shape[0]
    tb = min(tb, T)

    def body(w_ref, o_ref):
        lane = lax.broadcasted_iota(jnp.int32, (tb, HEAD_DIM), 1)
        tile = jnp.zeros((tb, HEAD_DIM), F32)
        for h in range(H):
            tile = jnp.where(lane == h, w_ref[:, h * HEAD_DIM:(h + 1) * HEAD_DIM], tile)
        o_ref[...] = tile

    return pl.pallas_call(
        body, name=name, grid=(T // tb,),
        in_specs=[pl.BlockSpec((tb, H * HEAD_DIM), lambda i: (i, 0))],
        out_specs=pl.BlockSpec((tb, HEAD_DIM), lambda i: (i, 0)),
        out_shape=jax.ShapeDtypeStruct((T, HEAD_DIM), F32),
        compiler_params=_params("parallel"),
    )(wide)


def _slopes(n_groups, nh):
    n = n_groups * nh
    s = np.exp2(-ALIBI_MAX_EXP * np.arange(1, n + 1, dtype=np.float32) / np.float32(n)).astype(np.float32)
    return s.reshape(n_groups, nh)


def _window_logits(qh, kh, slope_r, prev, has_prev):
    qi = lax.broadcasted_iota(jnp.int32, (DIL_SPAN, DIL_SPAN), 0)
    kl = lax.broadcasted_iota(jnp.int32, (DIL_SPAN, DIL_SPAN), 1)
    delta = qi - kl + (DIL_SPAN if prev else 0)
    s = _dot(qh, kh, NT) * (HEAD_DIM ** -0.5) - slope_r * delta.astype(F32)
    valid = ((kl >= qi) & has_prev) if prev else (kl <= qi)
    return jnp.where(valid, s, MASKED)


def _dil_views(T, r, G, nh, dv):
    L = T // r
    C, V = nh * HEAD_DIM, nh * dv
    return L, C, V, 2 * G * C + V


def _dil_fwd(qkv, g, r, G, nh, dv, slopes, name):
    T = qkv.shape[0]
    L, C, V, W = _dil_views(T, r, G, nh, dv)
    nblk = L // DIL_SPAN
    nc, nv = W // C, W // V
    view = qkv.reshape(L, r * W)

    def body(q_ref, kp_ref, kc_ref, vp_ref, vc_ref, num_ref, m_ref, den_ref):
        has_prev = pl.program_id(1) > 0
        lane = lax.broadcasted_iota(jnp.int32, (DIL_SPAN, HEAD_DIM), 1)
        m_tile = jnp.zeros((DIL_SPAN, HEAD_DIM), F32)
        den_tile = jnp.ones((DIL_SPAN, HEAD_DIM), F32)
        for h in range(nh):
            sl = slice(h * HEAD_DIM, (h + 1) * HEAD_DIM)
            vs = slice(h * dv, (h + 1) * dv)
            sr = float(slopes[h]) * r
            sc = _window_logits(q_ref[:, sl], kc_ref[:, sl], sr, False, has_prev)
            sp = _window_logits(q_ref[:, sl], kp_ref[:, sl], sr, True, has_prev)
            m = jnp.maximum(jnp.max(sc, axis=1, keepdims=True), jnp.max(sp, axis=1, keepdims=True))
            pc, pp = jnp.exp(sc - m), jnp.exp(sp - m)
            den = jnp.sum(pc, axis=1, keepdims=True) + jnp.sum(pp, axis=1, keepdims=True)
            num_ref[:, vs] = _dot(pc.astype(BF16), vc_ref[:, vs], NN) + _dot(pp.astype(BF16), vp_ref[:, vs], NN)
            m_tile = jnp.where(lane == h, m, m_tile)
            den_tile = jnp.where(lane == h, den, den_tile)
        m_ref[...] = m_tile
        den_ref[...] = den_tile

    prev = lambda i: jnp.maximum(i - 1, 0)
    stat = pl.BlockSpec((DIL_SPAN, HEAD_DIM), lambda b, i: (i, b))
    num, m, den = pl.pallas_call(
        body, name=name, grid=(r, nblk),
        in_specs=[pl.BlockSpec((DIL_SPAN, C), lambda b, i: (i, b * nc + g)),
                  pl.BlockSpec((DIL_SPAN, C), lambda b, i: (prev(i), b * nc + G + g)),
                  pl.BlockSpec((DIL_SPAN, C), lambda b, i: (i, b * nc + G + g)),
                  pl.BlockSpec((DIL_SPAN, V), lambda b, i: (prev(i), b * nv + nv - 1)),
                  pl.BlockSpec((DIL_SPAN, V), lambda b, i: (i, b * nv + nv - 1))],
        out_specs=[pl.BlockSpec((DIL_SPAN, V), lambda b, i: (i, b)), stat, stat],
        out_shape=[jax.ShapeDtypeStruct((L, r * V), F32), jax.ShapeDtypeStruct((L, r * HEAD_DIM), F32),
                   jax.ShapeDtypeStruct((L, r * HEAD_DIM), F32)],
        compiler_params=_params("parallel", "parallel"),
    )(view, view, view, view, view)
    return num.reshape(T, V), m.reshape(T, HEAD_DIM), den.reshape(T, HEAD_DIM)


def _dil_merge(nums, ms, dens, nh, dv, name, tb=256):
    T, V = nums[0].shape
    tb = min(tb, T)
    G = len(nums)

    def body(*refs):
        num_r, m_r, den_r = refs[:G], refs[G:2 * G], refs[2 * G:3 * G]
        o_ref, lse_ref = refs[3 * G], refs[3 * G + 1]
        mm = m_r[0][...]
        for g in range(1, G):
            mm = jnp.maximum(mm, m_r[g][...])
        w = [jnp.exp(m_r[g][...] - mm) for g in range(G)]
        den = w[0] * den_r[0][...]
        for g in range(1, G):
            den = den + w[g] * den_r[g][...]
        lse_ref[...] = mm + jnp.log(den)
        for h in range(nh):
            vs = slice(h * dv, (h + 1) * dv)
            num = w[0][:, h:h + 1] * num_r[0][:, vs]
            for g in range(1, G):
                num = num + w[g][:, h:h + 1] * num_r[g][:, vs]
            o_ref[:, vs] = (num / den[:, h:h + 1]).astype(BF16)

    wide = pl.BlockSpec((tb, V), lambda i: (i, 0))
    stat = pl.BlockSpec((tb, HEAD_DIM), lambda i: (i, 0))
    return pl.pallas_call(
        body, name=name, grid=(T // tb,),
        in_specs=[wide] * G + [stat] * (2 * G),
        out_specs=[wide, stat],
        out_shape=[jax.ShapeDtypeStruct((T, V), BF16), jax.ShapeDtypeStruct((T, HEAD_DIM), F32)],
        compiler_params=_params("parallel"),
    )(*nums, *ms, *dens)


def _dil_dq(qkv, do, lse, dd, g, r, G, nh, dv, slopes, name):
    T = qkv.shape[0]
    L, C, V, W = _dil_views(T, r, G, nh, dv)
    nblk = L // DIL_SPAN
    nc, nv = W // C, W // V
    view = qkv.reshape(L, r * W)
    scale = HEAD_DIM ** -0.5

    def body(q_ref, kp_ref, kc_ref, vp_ref, vc_ref, do_ref, lse_ref, dd_ref, dq_ref):
        has_prev = pl.program_id(1) > 0
        for h in range(nh):
            sl = slice(h * HEAD_DIM, (h + 1) * HEAD_DIM)
            vs = slice(h * dv, (h + 1) * dv)
            sr = float(slopes[h]) * r
            lse_h, dd_h = lse_ref[:, h:h + 1], dd_ref[:, h:h + 1]
            acc = jnp.zeros((DIL_SPAN, HEAD_DIM), F32)
            for k_ref, v_ref, is_prev in ((kc_ref, vc_ref, False), (kp_ref, vp_ref, True)):
                s = _window_logits(q_ref[:, sl], k_ref[:, sl], sr, is_prev, has_prev)
                p = jnp.exp(s - lse_h)
                dp = _dot(do_ref[:, vs], v_ref[:, vs], NT)
                ds = (p * (dp - dd_h)).astype(BF16)
                acc = acc + _dot(ds, k_ref[:, sl], NN)
            dq_ref[:, sl] = scale * acc

    prev = lambda i: jnp.maximum(i - 1, 0)
    stat = pl.BlockSpec((DIL_SPAN, HEAD_DIM), lambda b, i: (i, b))
    dq = pl.pallas_call(
        body, name=name, grid=(r, nblk),
        in_specs=[pl.BlockSpec((DIL_SPAN, C), lambda b, i: (i, b * nc + g)),
                  pl.BlockSpec((DIL_SPAN, C), lambda b, i: (prev(i), b * nc + G + g)),
                  pl.BlockSpec((DIL_SPAN, C), lambda b, i: (i, b * nc + G + g)),
                  pl.BlockSpec((DIL_SPAN, V), lambda b, i: (prev(i), b * nv + nv - 1)),
                  pl.BlockSpec((DIL_SPAN, V), lambda b, i: (i, b * nv + nv - 1)),
                  pl.BlockSpec((DIL_SPAN, V), lambda b, i: (i, b)), stat, stat],
        out_specs=pl.BlockSpec((DIL_SPAN, C), lambda b, i: (i, b)),
        out_shape=jax.ShapeDtypeStruct((L, r * C), F32),
        compiler_params=_params("parallel", "parallel"),
    )(view, view, view, view, view, do.reshape(L, r * V), lse.reshape(L, r * HEAD_DIM), dd.reshape(L, r * HEAD_DIM))
    return dq.reshape(T, C)


def _dil_dkv(qkv, do, lse, dd, g, r, G, nh, dv, slopes, name):
    T = qkv.shape[0]
    L, C, V, W = _dil_views(T, r, G, nh, dv)
    nblk = L // DIL_SPAN
    nc, nv = W // C, W // V
    view = qkv.reshape(L, r * W)
    scale = HEAD_DIM ** -0.5

    def body(k_ref, v_ref, qc_ref, qn_ref, doc_ref, don_ref, lsec_ref, lsen_ref, ddc_ref, ddn_ref, dk_ref, dv_ref):
        has_next = pl.program_id(1) < nblk - 1
        for h in range(nh):
            sl = slice(h * HEAD_DIM, (h + 1) * HEAD_DIM)
            vs = slice(h * dv, (h + 1) * dv)
            sr = float(slopes[h]) * r
            dk = jnp.zeros((DIL_SPAN, HEAD_DIM), F32)
            dvh = jnp.zeros((DIL_SPAN, dv), F32)
            for q_ref, do_ref, lse_ref, dd_ref, is_next in ((qc_ref, doc_ref, lsec_ref, ddc_ref, False),
                                                          (qn_ref, don_ref, lsen_ref, ddn_ref, True)):
                s = _window_logits(q_ref[:, sl], k_ref[:, sl], sr, is_next, has_next)
                p = jnp.exp(s - lse_ref[:, h:h + 1])
                dvh = dvh + _dot(p.astype(BF16), do_ref[:, vs], TN)
                dp = _dot(do_ref[:, vs], v_ref[:, vs], NT)
                ds = (p * (dp - dd_ref[:, h:h + 1])).astype(BF16)
                dk = dk + _dot(ds, q_ref[:, sl], TN)
            dk_ref[:, sl] = scale * dk
            dv_ref[:, vs] = dvh

    nxt = lambda i: jnp.minimum(i + 1, nblk - 1)
    stat_c = pl.BlockSpec((DIL_SPAN, HEAD_DIM), lambda b, i: (i, b))
    stat_n = pl.BlockSpec((DIL_SPAN, HEAD_DIM), lambda b, i: (nxt(i), b))
    do_v, lse_v, dd_v = do.reshape(L, r * V), lse.reshape(L, r * HEAD_DIM), dd.reshape(L, r * HEAD_DIM)
    dk, dvv = pl.pallas_call(
        body, name=name, grid=(r, nblk),
        in_specs=[pl.BlockSpec((DIL_SPAN, C), lambda b, i: (i, b * nc + G + g)),
                  pl.BlockSpec((DIL_SPAN, V), lambda b, i: (i, b * nv + nv - 1)),
                  pl.BlockSpec((DIL_SPAN, C), lambda b, i: (i, b * nc + g)),
                  pl.BlockSpec((DIL_SPAN, C), lambda b, i: (nxt(i), b * nc + g)),
                  pl.BlockSpec((DIL_SPAN, V), lambda b, i: (i, b)),
                  pl.BlockSpec((DIL_SPAN, V), lambda b, i: (nxt(i), b)),
                  stat_c, stat_n, stat_c, stat_n],
        out_specs=[pl.BlockSpec((DIL_SPAN, C), lambda b, i: (i, b)), pl.BlockSpec((DIL_SPAN, V), lambda b, i: (i, b))],
        out_shape=[jax.ShapeDtypeStruct((L, r * C), F32), jax.ShapeDtypeStruct((L, r * V), F32)],
        compiler_params=_params("parallel", "parallel"),
    )(view, view, view, view, do_v, do_v, lse_v, lse_v, dd_v, dd_v)
    return dk.reshape(T, C), dvv.reshape(T, V)


def _relu2(r):
    a = jnp.maximum(r, 0.0)
    return (a * a,)


def _mlp_fwd(x, g, w_up, w_down, tag, hosts=None):
    T, D = x.shape
    F = w_down.shape[0]
    h = _rms_fwd(x, g, f"{tag}_norm")
    a2 = _mm(h, w_up, T, F, D, mode="nn", name=f"{tag}_up", b_stack=N_CHIPS, out_dtypes=(BF16,), epilogue=_relu2,
             hosts=hosts)
    y = _mm(a2, w_down, T, D, F, mode="nn", name=f"{tag}_down", out_dtypes=(F32,), extras=(x,),
            epilogue=lambda r, res: (res + r,), hosts=hosts)
    return y, (x, h, a2)


def _mlp_bwd(dy, dyb, saved, g, w_up, w_down, tag, hosts=None):
    x, h, a2 = saved
    T, D = x.shape
    F = w_down.shape[0]
    d_down = _mm(a2, dyb, F, D, T, mode="tn", name=f"{tag}_dwdown", out_dtypes=(F32,), hosts=hosts)
    du = _mm(dyb, w_down, T, F, D, mode="nt", name=f"{tag}_da", out_dtypes=(BF16,), extras=(a2,),
             epilogue=lambda r, sq: (r * (2.0 * jnp.sqrt(sq.astype(F32))),))
    d_up = _mm(h, du, D, F, T, mode="tn", name=f"{tag}_dwup", out_stack=N_CHIPS, out_dtypes=(F32,))
    dh = _mm(du, w_up, T, D, F, mode="nt", name=f"{tag}_dh", b_stack=N_CHIPS, out_dtypes=(F32,))
    dx, dxb, dg = _rms_bwd(dh, x, g, dy, f"{tag}_dnorm")
    return dx, dxb, dg, d_up, d_down


def _fox_dims(D):
    H = D // HEAD_DIM
    return H, max(8, H), (H // 2) * HEAD_DIM


def _fox_layer_fwd(x, g, w_qkv, w_f, b_pad, gains, w_out, hosts=None):
    T, D = x.shape
    H, hp, ch = _fox_dims(D)
    h = _rms_fwd(x, g, "fox_norm")
    proj = _mm(h, w_qkv, T, 3 * D, D, mode="nn", name="fox_proj", out_dtypes=(F32,))
    f_raw = _mm(h, w_f, T, HEAD_DIM, D, mode="nn", name="fox_gate_proj", out_dtypes=(F32,))
    qkv = _qkv_prep(proj, gains, 4, lambda j: jnp.minimum(j // 2, 1), ch, "fox_qk_norm", n_scaled=2, post_scale=FOX_Q_SCALE)
    ck = _gate_fwd(f_raw, b_pad, hp, LOG2E, "fox_gate").reshape(hp, 1, T)
    o, lse = _fox_fwd(qkv, ck, H, "fox_attn", hosts=hosts)
    y = _mm(o, w_out, T, D, D, mode="nn", name="fox_out", out_dtypes=(F32,), extras=(x,),
            epilogue=lambda r, res: (res + r,), hosts=hosts)
    return y, (x, h, proj, f_raw, qkv, ck, o, lse)


def _fox_layer_bwd(dy, dyb, saved, g, w_qkv, w_f, b_pad, gains, w_out, hosts=None):
    x, h, proj, f_raw, qkv, ck, o, lse = saved
    T, D = x.shape
    H, hp, ch = _fox_dims(D)
    d_out = _mm(o, dyb, D, D, T, mode="tn", name="fox_dwout", out_dtypes=(F32,), hosts=hosts)
    do = _mm(dyb, w_out, T, D, D, mode="nt", name="fox_do", out_dtypes=(BF16,))
    dd = _row_dot(do, o, H, HEAD_DIM, "fox_rowdot", False)
    dq, dk, dv, dck, dcq = _fox_bwd(qkv, do, ck, lse, dd, H, hp, "fox_attn_bwd", hosts=hosts)
    dcq = _lane_per_head(dcq, H, "fox_dc_query")
    dproj, dgq = _head_rms_bwd_into(None, 3 * D, dq, proj, gains[0], 0, ch, "fox_dq_norm", in_scale=HEAD_DIM ** -0.5)
    dproj, dgk = _head_rms_bwd_into(dproj, 3 * D, dk, proj, gains[1], 2, ch, "fox_dk_norm", in_scale=LN2)
    dproj = _sum_cast_into(dproj, 3 * D, [dv], 4, ch, "fox_dv_cast")
    dz, db = _gate_bwd(dck.reshape(hp, T), dcq, f_raw, b_pad, H, hp, "fox_gate_bwd")
    d_qkv = _mm(h, dproj, D, 3 * D, T, mode="tn", name="fox_dwqkv", out_dtypes=(F32,))
    d_f = _mm(h, dz, D, HEAD_DIM, T, mode="tn", name="fox_dwgate", out_dtypes=(F32,))
    if hosts is not None:
        hosts.grads.update(fox_qkv=d_qkv, fox_f=d_f, fox_out=d_out)
    dh = _mm(dproj, w_qkv, T, D, 3 * D, mode="nt", name="fox_dh", out_dtypes=(F32,), hosts=hosts)
    dh = _mm(dz, w_f, T, D, HEAD_DIM, mode="nt", name="fox_dh_gate", out_dtypes=(F32,), extras=(dh,),
             epilogue=lambda r, e: (e + r,))
    dx, dxb, dg = _rms_bwd(dh, x, g, dy, "fox_dnorm", hosts=hosts)
    dgains = jnp.stack([dgq.sum(axis=0), dgk.sum(axis=0)])
    return dx, dxb, dg, d_qkv, d_f, db, dgains, d_out


def _dil_dims(D):
    nh = D // (2 * HEAD_DIM)
    return nh, D // nh, len(DIL_PATTERNS)


def _dil_layer_fwd(x, g, w_in, gains, w_out, hosts=None):
    T, D = x.shape
    nh, dv, G = _dil_dims(D)
    C = nh * HEAD_DIM
    W = 2 * G * C + nh * dv
    slopes = _slopes(G, nh)
    h = _rms_fwd(x, g, "dil_norm")
    proj = _mm(h, w_in, T, W, D, mode="nn", name="dil_proj", b_stack=N_CHIPS, out_dtypes=(F32,), hosts=hosts)
    qkv = _qkv_prep(proj, gains, 2 * G, lambda j: jnp.minimum(j, 2 * G - 1), C, "dil_qk_norm")
    parts = [_dil_fwd(qkv, gi, r, G, nh, dv, slopes[gi], f"dil_attn_g{gi}") for gi, (_, r) in enumerate(DIL_PATTERNS)]
    o, lse = _dil_merge([p[0] for p in parts], [p[1] for p in parts], [p[2] for p in parts], nh, dv, "dil_merge")
    y = _mm(o, w_out, T, D, D, mode="nn", name="dil_out", out_dtypes=(F32,), extras=(x,),
            epilogue=lambda r, res: (res + r,))
    return y, (x, h, proj, qkv, o, lse)


def _dil_layer_bwd(dy, dyb, saved, g, w_in, gains, w_out):
    x, h, proj, qkv, o, lse = saved
    T, D = x.shape
    nh, dv, G = _dil_dims(D)
    C = nh * HEAD_DIM
    W = 2 * G * C + nh * dv
    slopes = _slopes(G, nh)
    d_out = _mm(o, dyb, D, D, T, mode="tn", name="dil_dwout", out_dtypes=(F32,))
    do = _mm(dyb, w_out, T, D, D, mode="nt", name="dil_do", out_dtypes=(BF16,))
    dd = _row_dot(do, o, nh, dv, "dil_rowdot", True)
    dproj, dgs, dvs = None, [None] * (2 * G), []
    for gi, (_, r) in enumerate(DIL_PATTERNS):
        dq = _dil_dq(qkv, do, lse, dd, gi, r, G, nh, dv, slopes[gi], f"dil_dq_g{gi}")
        dk, dvg = _dil_dkv(qkv, do, lse, dd, gi, r, G, nh, dv, slopes[gi], f"dil_dkv_g{gi}")
        dvs.append(dvg)
        dproj, dgs[gi] = _head_rms_bwd_into(dproj, W, dq, proj, gains[gi], gi, C, f"dil_dq_norm_g{gi}")
        dproj, dgs[G + gi] = _head_rms_bwd_into(dproj, W, dk, proj, gains[G + gi], G + gi, C, f"dil_dk_norm_g{gi}")
    dproj = _sum_cast_into(dproj, W, dvs, 2 * G, C, "dil_dv_cast")
    d_in = _mm(h, dproj, D, W, T, mode="tn", name="dil_dwin", out_stack=N_CHIPS, out_dtypes=(F32,))
    dh = _mm(dproj, w_in, T, D, W, mode="nt", name="dil_dh", b_stack=N_CHIPS, out_dtypes=(F32,))
    dx, dxb, dg = _rms_bwd(dh, x, g, dy, "dil_dnorm")
    dgains = jnp.concatenate(dgs, axis=0)
    return dx, dxb, dg, d_in, dgains, d_out


class _Hosts:
    def __init__(self, riders, weights):
        self.riders, self.weights, self.carried, self.grads = dict(riders), weights, {}, {}

    def rider(self, name):
        make = self.riders.get(name)
        return make(self) if make is not None else None


def _local_step(x, tgt, w, hosts=None):
    g = hosts.grads if hosts is not None else {}
    y0, s_fox = _fox_layer_fwd(x, w["mix_g"][0], w["fox_qkv"], w["fox_f"], w["fox_b"], w["fox_gains"], w["fox_out"], hosts)
    if hosts is not None:
        w = {**w, **hosts.weights(hosts)}
    y1, s_mlp0 = _mlp_fwd(y0, w["mlp_g"][0], w["up0"], w["down0"], "mlp0", hosts)
    y2, s_dil = _dil_layer_fwd(y1, w["mix_g"][1], w["dil_in"], w["dil_gains"], w["dil_out"], hosts)
    if hosts is not None:
        w = {**w, **hosts.weights(hosts)}
    y3, s_mlp1 = _mlp_fwd(y2, w["mlp_g"][1], w["up1"], w["down1"], "mlp1")
    dy, dyb, loss = _loss_head(y3, tgt, "loss_head")
    dy, dyb, g_mlp1, g["up1"], g["down1"] = _mlp_bwd(dy, dyb, s_mlp1, w["mlp_g"][1], w["up1"], w["down1"], "mlp1")
    dy, dyb, g_mix1, g["dil_in"], g["dil_gains"], g["dil_out"] = _dil_layer_bwd(
        dy, dyb, s_dil, w["mix_g"][1], w["dil_in"], w["dil_gains"], w["dil_out"])
    dy, dyb, g_mlp0, g["up0"], g["down0"] = _mlp_bwd(dy, dyb, s_mlp0, w["mlp_g"][0], w["up0"], w["down0"], "mlp0", hosts)
    dy, dyb, g_mix0, g["fox_qkv"], g["fox_f"], g["fox_b"], g["fox_gains"], g["fox_out"] = _fox_layer_bwd(
        dy, dyb, s_fox, w["mix_g"][0], w["fox_qkv"], w["fox_f"], w["fox_b"], w["fox_gains"], w["fox_out"], hosts)
    g["mix_g"], g["mlp_g"] = (g_mix0, g_mix1), (g_mlp0, g_mlp1)
    return loss[0, 0], dy, g


def _place():
    x, y, c = lax.axis_index("x"), lax.axis_index("y"), lax.axis_index("c")
    chips = [(1 - x, y), (x, 1 - y), (1 - x, 1 - y)]
    return x, y, c, chips


def _remote(src, dst, send_sem, recv_sem, to):
    return pltpu.make_async_remote_copy(src_ref=src, dst_ref=dst, send_sem=send_sem, recv_sem=recv_sem,
                                        device_id=to, device_id_type=MESH)


def _gather_weights(shards):
    n = len(shards)

    def body(*refs):
        src, dst = refs[:n], refs[n:2 * n]
        send_sems, recv_sems, local_sems = refs[2 * n:]
        x, y, c, chips = _place()
        mine = 2 * x + y
        local = [pltpu.make_async_copy(src[t], dst[t].at[mine], local_sems.at[t]) for t in range(n)]
        for cp in local:
            cp.start()

        def half(t, slot, which):
            hr = shards[t].shape[0] // 2
            return dst[t].at[slot, pl.ds(which * hr, hr), :]

        def my_half(t):
            hr = shards[t].shape[0] // 2
            return src[t].at[pl.ds(c * hr, hr), :]

        sends = []
        for t in range(n):
            for j, (px, py) in enumerate(chips):
                cp = _remote(my_half(t), half(t, mine, c), send_sems.at[t, j], recv_sems.at[t, j], (px, py, c))
                cp.start()
                sends.append(cp)
        for j, (px, py) in enumerate(chips):
            for t in range(n):
                landed = half(t, 2 * px + py, c)
                _remote(landed, landed, send_sems.at[t, j], recv_sems.at[t, j], (px, py, c)).wait_recv()
                cp = _remote(landed, landed, send_sems.at[t, 3 + j], recv_sems.at[t, 3 + j], (x, y, 1 - c))
                cp.start()
                sends.append(cp)
        for j, (px, py) in enumerate(chips):
            for t in range(n):
                other = half(t, 2 * px + py, 1 - c)
                _remote(other, other, send_sems.at[t, 3 + j], recv_sems.at[t, 3 + j], (x, y, 1 - c)).wait_recv()
        for cp in sends:
            cp.wait_send()
        for cp in local:
            cp.wait()

    return pl.pallas_call(
        body, name="gather_weights",
        in_specs=[ANY] * n, out_specs=[ANY] * n,
        out_shape=[jax.ShapeDtypeStruct((N_CHIPS,) + s.shape, s.dtype) for s in shards],
        scratch_shapes=[pltpu.SemaphoreType.DMA((n, 6)), pltpu.SemaphoreType.DMA((n, 6)), pltpu.SemaphoreType.DMA((n,))],
        compiler_params=_params(has_side_effects=True),
    )(*shards)


class _Rider(NamedTuple):
    operands: tuple
    out_shapes: tuple
    scratch: tuple
    start: Callable
    finish: Callable
    aliases: dict = {}


def _gather_ici_rider(shards):
    n = len(shards)

    def copies(src, dst, sems):
        send_sems, recv_sems, local_sems = sems
        x, y, c, chips = _place()
        mine = 2 * x + y
        local, sends, recvs = [], [], []
        for t in range(n):
            hr = shards[t].shape[0] // 2
            local.append(pltpu.make_async_copy(src[t], dst[t].at[mine], local_sems.at[t]))
            for j, (px, py) in enumerate(chips):
                sends.append(_remote(src[t].at[pl.ds(c * hr, hr), :], dst[t].at[mine, pl.ds(c * hr, hr), :],
                                     send_sems.at[t, j], recv_sems.at[t, j], (px, py, c)))
                landed = dst[t].at[2 * px + py, pl.ds(c * hr, hr), :]
                recvs.append(_remote(landed, landed, send_sems.at[t, j], recv_sems.at[t, j], (px, py, c)))
        return local, sends, recvs

    def start(src, dst, sems):
        local, sends, _ = copies(src, dst, sems)
        for cp in local + sends:
            cp.start()

    def finish(src, dst, sems):
        local, sends, recvs = copies(src, dst, sems)
        for cp in recvs:
            cp.wait_recv()
        for cp in sends:
            cp.wait_send()
        for cp in local:
            cp.wait()

    return _Rider(tuple(shards), tuple(jax.ShapeDtypeStruct((N_CHIPS,) + s.shape, s.dtype) for s in shards),
                  (pltpu.SemaphoreType.DMA((n, 3)), pltpu.SemaphoreType.DMA((n, 3)), pltpu.SemaphoreType.DMA((n,))),
                  start, finish)


def _forward_rider(landed):
    n = len(landed)

    def copies(dst, sems):
        send_sems, recv_sems = sems
        x, y, c, chips = _place()
        sends, recvs = [], []
        for t in range(n):
            hr = landed[t].shape[1] // 2
            for j, (px, py) in enumerate(chips):
                got = dst[t].at[2 * px + py, pl.ds(c * hr, hr), :]
                other = dst[t].at[2 * px + py, pl.ds((1 - c) * hr, hr), :]
                sends.append(_remote(got, got, send_sems.at[t, j], recv_sems.at[t, j], (x, y, 1 - c)))
                recvs.append(_remote(other, other, send_sems.at[t, j], recv_sems.at[t, j], (x, y, 1 - c)))
        return sends, recvs

    def start(src, dst, sems):
        for cp in copies(dst, sems)[0]:
            cp.start()

    def finish(src, dst, sems):
        sends, recvs = copies(dst, sems)
        for cp in recvs:
            cp.wait_recv()
        for cp in sends:
            cp.wait_send()

    return _Rider(tuple(landed), tuple(jax.ShapeDtypeStruct(a.shape, a.dtype) for a in landed),
                  (pltpu.SemaphoreType.DMA((n, 3)), pltpu.SemaphoreType.DMA((n, 3))), start, finish,
                  {t: t for t in range(n)})


def _pair_exchange_rider(grads):
    n = len(grads)

    def copies(src, dst, sems):
        send_sems, recv_sems = sems
        x, y, c, _ = _place()
        return [_remote(src[t].at[:, pl.ds((1 - c) * (grads[t].shape[1] // 2), grads[t].shape[1] // 2), :], dst[t],
                        send_sems.at[t], recv_sems.at[t], (x, y, 1 - c)) for t in range(n)]

    def start(src, dst, sems):
        for cp in copies(src, dst, sems):
            cp.start()

    def finish(src, dst, sems):
        for cp in copies(src, dst, sems):
            cp.wait()

    return _Rider(tuple(grads),
                  tuple(jax.ShapeDtypeStruct((g.shape[0], g.shape[1] // 2, g.shape[2]), g.dtype) for g in grads),
                  (pltpu.SemaphoreType.DMA((n,)), pltpu.SemaphoreType.DMA((n,))), start, finish)


def _pair_add(g, got, cidx, name, tb=256):
    S, R, C = g.shape
    hr = R // 2
    tb = _rows_tile(hr, tb)
    nb = hr // tb

    def body(c_ref, a_ref, b_ref, o_ref):
        o_ref[...] = (a_ref[...] + b_ref[...]).astype(BF16)

    return pl.pallas_call(
        body, name=name,
        grid_spec=pltpu.PrefetchScalarGridSpec(
            num_scalar_prefetch=1, grid=(S, nb),
            in_specs=[pl.BlockSpec((None, tb, C), lambda s, i, c: (s, c[0] * nb + i, 0)),
                      pl.BlockSpec((None, tb, C), lambda s, i, c: (s, i, 0))],
            out_specs=pl.BlockSpec((None, tb, C), lambda s, i, c: (s, i, 0))),
        out_shape=jax.ShapeDtypeStruct((S, hr, C), BF16),
        compiler_params=_params("parallel", "parallel"),
    )(cidx, g, got)


def _rows_tile(n, want):
    t = min(n, want)
    while n % t or t % 8:
        t -= 8
    return t


def _chip_scatter_rider(sums):
    n = len(sums)

    def copies(src, dst, sems):
        send_sems, recv_sems, local_sems = sems
        x, y, c, chips = _place()
        mine = 2 * x + y
        local, sends, recvs = [], [], []
        for t in range(n):
            local.append(pltpu.make_async_copy(src[t].at[mine], dst[t].at[mine], local_sems.at[t]))
            for j, (px, py) in enumerate(chips):
                sends.append(_remote(src[t].at[2 * px + py], dst[t].at[mine], send_sems.at[t, j], recv_sems.at[t, j], (px, py, c)))
                slot = dst[t].at[2 * px + py]
                recvs.append(_remote(slot, slot, send_sems.at[t, j], recv_sems.at[t, j], (px, py, c)))
        return local, sends, recvs

    def start(src, dst, sems):
        local, sends, _ = copies(src, dst, sems)
        for cp in local + sends:
            cp.start()

    def finish(src, dst, sems):
        local, sends, recvs = copies(src, dst, sems)
        for cp in recvs:
            cp.wait_recv()
        for cp in sends:
            cp.wait_send()
        for cp in local:
            cp.wait()

    return _Rider(tuple(sums), tuple(jax.ShapeDtypeStruct(s.shape, s.dtype) for s in sums),
                  (pltpu.SemaphoreType.DMA((n, 3)), pltpu.SemaphoreType.DMA((n, 3)), pltpu.SemaphoreType.DMA((n,))),
                  start, finish)


def _chip_sum(parts, cidx, name, tb=256):
    S, hr, C = parts.shape
    tb = _rows_tile(hr, tb)
    nb = hr // tb

    def body(c_ref, *refs):
        o_ref = refs[S]
        tot = refs[0][...].astype(F32)
        for s in range(1, S):
            tot = tot + refs[s][...].astype(F32)
        o_ref[...] = tot

    return pl.pallas_call(
        body, name=name,
        grid_spec=pltpu.PrefetchScalarGridSpec(
            num_scalar_prefetch=1, grid=(nb,),
            in_specs=[pl.BlockSpec((None, tb, C), functools.partial(lambda s, i, c: (s, i, 0), s)) for s in range(S)],
            out_specs=pl.BlockSpec((tb, C), lambda i, c: (c[0] * nb + i, 0))),
        out_shape=jax.ShapeDtypeStruct((2 * hr, C), F32),
        compiler_params=_params("parallel"),
    )(cidx, *([parts] * S))


def _half_exchange(halves):
    n = len(halves)

    def body(*refs):
        dst = refs[n:2 * n]
        send_sems, recv_sems = refs[2 * n:]
        x, y, c, _ = _place()
        cps = []
        for t in range(n):
            hr = halves[t].shape[0] // 2
            rows = dst[t].at[pl.ds(c * hr, hr), :]
            cp = _remote(rows, rows, send_sems.at[t], recv_sems.at[t], (x, y, 1 - c))
            cp.start()
            cps.append(cp)
        for t, cp in enumerate(cps):
            cp.wait_send()
            hr = halves[t].shape[0] // 2
            other = dst[t].at[pl.ds((1 - c) * hr, hr), :]
            _remote(other, other, send_sems.at[t], recv_sems.at[t], (x, y, 1 - c)).wait_recv()

    return pl.pallas_call(
        body, name="grad_half_exchange",
        in_specs=[ANY] * n, out_specs=[ANY] * n,
        out_shape=[jax.ShapeDtypeStruct(h.shape, h.dtype) for h in halves],
        input_output_aliases={t: t for t in range(n)},
        scratch_shapes=[pltpu.SemaphoreType.DMA((n,)), pltpu.SemaphoreType.DMA((n,))],
        compiler_params=_params(has_side_effects=True),
    )(*halves)


def _adamw_math(w, g, m, v):
    m = ADAM_B1 * m + (1.0 - ADAM_B1) * g
    v = ADAM_B2 * v + (1.0 - ADAM_B2) * (g * g)
    m_hat = m / (1.0 - ADAM_B1 ** ADAM_STEP)
    v_hat = v / (1.0 - ADAM_B2 ** ADAM_STEP)
    delta = -ADAM_LR * (m_hat / (jnp.sqrt(v_hat) + ADAM_EPS) + ADAM_WD * w)
    return delta, m, v


def _adamw(w, gs, m, v, name, tb=256):
    L, R, C = w.shape
    tb = _rows_tile(R, tb)
    nb = R // tb

    def body(w_ref, m_ref, v_ref, *rest):
        g_refs, (go_ref, d_ref, mo_ref, vo_ref) = rest[:L], rest[L:]
        layer = pl.program_id(0)
        for k in range(L):
            @pl.when(layer == k)
            def _(k=k):
                g = g_refs[k][...]
                d, mn, vn = _adamw_math(w_ref[...], g, m_ref[...], v_ref[...])
                go_ref[...] = g
                d_ref[...] = d
                mo_ref[...] = mn
                vo_ref[...] = vn

    def g_spec(k):
        return pl.BlockSpec((tb, C), lambda l, i: (jnp.where(l == k, i, jnp.where(l < k, 0, nb - 1)), 0))

    stacked = pl.BlockSpec((None, tb, C), lambda l, i: (l, i, 0))
    return pl.pallas_call(
        body, name=name, grid=(L, nb),
        in_specs=[stacked] * 3 + [g_spec(k) for k in range(L)], out_specs=[stacked] * 4,
        out_shape=[jax.ShapeDtypeStruct((L, R, C), F32)] * 4,
        compiler_params=_params("arbitrary", "arbitrary"),
    )(w, m, v, *gs)


N_DEV = 8


def _small_update(g, w, m, v):
    P = g.shape[0]

    def body(g_ref, w_ref, m_ref, v_ref, go_ref, d_ref, mo_ref, vo_ref, buf, send_sems, recv_sems):
        x, y, c, _ = _place()
        me = 4 * x + 2 * y + c
        buf[me] = g_ref[...]
        cps = []
        for k in range(1, N_DEV):
            fx, fy, fc = (k >> 2) & 1, (k >> 1) & 1, k & 1
            px = (1 - x) if fx else x
            py = (1 - y) if fy else y
            pc = (1 - c) if fc else c
            cp = _remote(g_ref, buf.at[me], send_sems.at[k - 1], recv_sems.at[k - 1], (px, py, pc))
            cp.start()
            cps.append((cp, 4 * px + 2 * py + pc))
        for k, (cp, peer) in enumerate(cps):
            _remote(g_ref, buf.at[peer], send_sems.at[k], recv_sems.at[k], (x, y, c)).wait_recv()
        for cp, _ in cps:
            cp.wait_send()
        tot = buf[0]
        for d in range(1, N_DEV):
            tot = tot + buf[d]
        go_ref[...] = tot
        dl, mn, vn = _adamw_math(w_ref[...], tot, m_ref[...], v_ref[...])
        d_ref[...] = dl
        mo_ref[...] = mn
        vo_ref[...] = vn

    vm = pl.BlockSpec(memory_space=pltpu.VMEM)
    return pl.pallas_call(
        body, name="small_params_update",
        in_specs=[vm] * 4, out_specs=[vm] * 4,
        out_shape=[jax.ShapeDtypeStruct((P, HEAD_DIM), F32)] * 4,
        scratch_shapes=[pltpu.VMEM((N_DEV, P, HEAD_DIM), F32), pltpu.SemaphoreType.DMA((N_DEV - 1,)),
                        pltpu.SemaphoreType.DMA((N_DEV - 1,))],
        compiler_params=_params(has_side_effects=True),
    )(g, w, m, v)


SMALL = ("fox_b_f", "fox_q_gain", "fox_k_gain", "dil_q_gain", "dil_k_gain", "mix_norm_g", "mlp_norm_g")
LARGE = ("fox_w_in", "fox_w_out", "dil_w_in", "dil_w_out", "mlp_w_up", "mlp_w_down")
WEIGHTS = ("fox_w_in", "fox_b_f", "fox_q_gain", "fox_k_gain", "fox_w_out", "dil_w_in", "dil_q_gain", "dil_k_gain",
           "dil_w_out", "mix_norm_g", "mlp_norm_g", "mlp_w_up", "mlp_w_down")


def _pack(parts):
    rows = []
    for a in parts:
        flat = a.reshape(-1)
        n = -(-flat.shape[0] // (8 * HEAD_DIM)) * (8 * HEAD_DIM)
        rows.append(jnp.pad(flat, (0, n - flat.shape[0])).reshape(-1, HEAD_DIM))
    return jnp.concatenate(rows, axis=0)


def _unpack(packed, like):
    out, r = [], 0
    for a in like:
        size = int(np.prod(a.shape))
        n = -(-size // (8 * HEAD_DIM)) * 8
        out.append(packed[r:r + n].reshape(-1)[:size].reshape(a.shape))
        r += n
    return out


def _pad_lanes(a):
    return jnp.pad(a, [(0, 0)] * (a.ndim - 1) + [(0, HEAD_DIM - a.shape[-1])])


def _as_shards(a):
    return a.reshape(N_CHIPS, a.shape[0] // N_CHIPS, a.shape[1])


def kernel(x, fox_w_in, fox_b_f, fox_q_gain, fox_k_gain, fox_w_out, dil_w_in, dil_q_gain, dil_k_gain, dil_w_out, mix_norm_g, mlp_norm_g, mlp_w_up, mlp_w_down, loss_target, m_fox_w_in, m_fox_b_f, m_fox_q_gain, m_fox_k_gain, m_fox_w_out, m_dil_w_in, m_dil_q_gain, m_dil_k_gain, m_dil_w_out, m_mix_norm_g, m_mlp_norm_g, m_mlp_w_up, m_mlp_w_down, v_fox_w_in, v_fox_b_f, v_fox_q_gain, v_fox_k_gain, v_fox_w_out, v_dil_w_in, v_dil_q_gain, v_dil_k_gain, v_dil_w_out, v_mix_norm_g, v_mlp_norm_g, v_mlp_w_up, v_mlp_w_down):
    wts = dict(fox_w_in=fox_w_in, fox_b_f=fox_b_f, fox_q_gain=fox_q_gain, fox_k_gain=fox_k_gain, fox_w_out=fox_w_out,
               dil_w_in=dil_w_in, dil_q_gain=dil_q_gain, dil_k_gain=dil_k_gain, dil_w_out=dil_w_out,
               mix_norm_g=mix_norm_g, mlp_norm_g=mlp_norm_g, mlp_w_up=mlp_w_up, mlp_w_down=mlp_w_down)
    mom1 = dict(fox_w_in=m_fox_w_in, fox_b_f=m_fox_b_f, fox_q_gain=m_fox_q_gain, fox_k_gain=m_fox_k_gain,
                fox_w_out=m_fox_w_out, dil_w_in=m_dil_w_in, dil_q_gain=m_dil_q_gain, dil_k_gain=m_dil_k_gain,
                dil_w_out=m_dil_w_out, mix_norm_g=m_mix_norm_g, mlp_norm_g=m_mlp_norm_g, mlp_w_up=m_mlp_w_up,
                mlp_w_down=m_mlp_w_down)
    mom2 = dict(fox_w_in=v_fox_w_in, fox_b_f=v_fox_b_f, fox_q_gain=v_fox_q_gain, fox_k_gain=v_fox_k_gain,
                fox_w_out=v_fox_w_out, dil_w_in=v_dil_w_in, dil_q_gain=v_dil_q_gain, dil_k_gain=v_dil_k_gain,
                dil_w_out=v_dil_w_out, mix_norm_g=v_mix_norm_g, mlp_norm_g=v_mlp_norm_g, mlp_w_up=v_mlp_w_up,
                mlp_w_down=v_mlp_w_down)
    T, D = x.shape[1], x.shape[2]
    H = D // HEAD_DIM
    cidx = lax.axis_index("c").astype(jnp.int32).reshape(1)

    def shards_of(d):
        return [d["fox_w_in"][0], d["fox_w_out"][0], d["dil_w_in"][0], d["dil_w_out"][0],
                d["mlp_w_up"][0], d["mlp_w_up"][1], d["mlp_w_down"][0], d["mlp_w_down"][1]]

    w_bf = [s.astype(BF16) for s in shards_of(wts)]
    first = _gather_weights(w_bf[:2])
    fox_in = jnp.moveaxis(first[0], 0, 1).reshape(D, -1)
    w = dict(
        fox_qkv=fox_in[:, :3 * D], fox_f=_pad_lanes(fox_in[:, 3 * D:]), fox_b=_pad_lanes(fox_b_f),
        fox_gains=jnp.stack([fox_q_gain, fox_k_gain]), fox_out=first[1].reshape(D, D),
        dil_gains=jnp.concatenate([dil_q_gain[0], dil_k_gain[0]])[:, None, :],
        mix_g=[mix_norm_g[0:1], mix_norm_g[1:2]], mlp_g=[mlp_norm_g[0:1], mlp_norm_g[1:2]])

    dil_in_sh, dil_out_sh, up0_sh, up1_sh, down0_sh, down1_sh = w_bf[2:]

    def gathered(hosts):
        got = {}
        if "fox_out" in hosts.carried:
            full = hosts.carried["fox_out"]
            got.update(dil_in=full[0], dil_out=full[1].reshape(D, D), up0=full[2], down0=full[3].reshape(-1, D))
        if "dil_proj" in hosts.carried:
            full = hosts.carried["dil_proj"]
            got.update(up1=full[0], down1=full[1].reshape(-1, D))
        return got

    def chip_sums(stacked, got, ts):
        return [_pair_add(a, b, cidx, f"grad_pair_add_{t}") for a, b, t in zip(stacked, got, ts)]

    def second_layer_grads(hosts):
        g = hosts.grads
        return [g["dil_in"], _as_shards(g["dil_out"]), g["up1"], _as_shards(g["down1"])]

    def first_mlp_grads(hosts):
        g = hosts.grads
        return [g["up0"], _as_shards(g["down0"])]

    def first_mixer_grads(hosts):
        if "first_mixer_grads" not in hosts.carried:
            g = hosts.grads
            g_in = jnp.concatenate([g["fox_qkv"], g["fox_f"][:, :H]], axis=1)
            hosts.carried["first_mixer_grads"] = [jnp.moveaxis(g_in.reshape(D, N_CHIPS, -1), 1, 0), _as_shards(g["fox_out"])]
        return hosts.carried["first_mixer_grads"]

    def early_scatter(hosts):
        sums1 = chip_sums(second_layer_grads(hosts), hosts.carried["mlp0_dwdown"], (2, 3, 5, 7))
        sums0 = chip_sums(first_mlp_grads(hosts), hosts.carried["fox_dwout"], (4, 6))
        return _chip_scatter_rider([sums1[0], sums1[1], sums0[0], sums1[2], sums0[1], sums1[3]])

    hosts = _Hosts({
        "fox_attn": lambda h: _gather_ici_rider([dil_in_sh, dil_out_sh, up0_sh, down0_sh]),
        "fox_out": lambda h: _forward_rider(h.carried["fox_attn"]),
        "mlp0_up": lambda h: _gather_ici_rider([up1_sh]),
        "mlp0_down": lambda h: _gather_ici_rider([down1_sh]),
        "dil_proj": lambda h: _forward_rider(h.carried["mlp0_up"] + h.carried["mlp0_down"]),
        "mlp0_dwdown": lambda h: _pair_exchange_rider(second_layer_grads(h)),
        "fox_dwout": lambda h: _pair_exchange_rider(first_mlp_grads(h)),
        "fox_attn_bwd": early_scatter,
        "fox_dh": lambda h: _pair_exchange_rider(first_mixer_grads(h)),
        "fox_dnorm": lambda h: _chip_scatter_rider(chip_sums(first_mixer_grads(h), h.carried["fox_dh"], (0, 1))),
    }, gathered)
    loss, grad_x, g = _local_step(x.reshape(T, D), loss_target.reshape(T, D), w, hosts)
    loss = lax.psum(loss, ("x", "y", "c"))

    parts = hosts.carried["fox_dnorm"] + hosts.carried["fox_attn_bwd"]
    halves = [_chip_sum(p, cidx, f"grad_chip_sum_{t}") for t, p in enumerate(parts)]
    totals = _half_exchange(halves)
    layers = dict(fox_w_in=[0], fox_w_out=[1], dil_w_in=[2], dil_w_out=[3], mlp_w_up=[4, 5], mlp_w_down=[6, 7])
    upd = {n: _adamw(wts[n], [totals[t] for t in ts], mom1[n], mom2[n], f"adamw_{n}") for n, ts in layers.items()}

    def large(k):
        return {n: upd[n][k] for n in LARGE}

    small_like = [wts[n] for n in SMALL]
    g_small = [g["fox_b"][:, :H], g["fox_gains"][0], g["fox_gains"][1], g["dil_gains"][:3, 0][None], g["dil_gains"][3:, 0][None],
               jnp.concatenate(g["mix_g"]), jnp.concatenate(g["mlp_g"])]
    packed = _small_update(_pack(g_small), _pack(small_like), _pack([mom1[n] for n in SMALL]), _pack([mom2[n] for n in SMALL]))
    small = [dict(zip(SMALL, _unpack(p, small_like))) for p in packed]

    outs = [loss, grad_x.reshape(x.shape)]
    for k in range(4):
        big = large(k)
        outs += [big[n] if n in big else small[k][n] for n in WEIGHTS]
    return tuple(outs)
```
